```python
import jax, jax.numpy as jnp
from jax import lax
import numpy as np

D_MODEL = 2048
BATCH = 2
SEQ = 4096
DEPTH = 1
DEC_BATCH = 32
DEC_SEQ = 1
PAST_LEN = 16384
PAGE_SIZE = 128

N_HEADS_A = 8
HEAD_DIM = 128
D_ATTN = N_HEADS_A * HEAD_DIM
ROPE_DIM = HEAD_DIM // 4
ROPE_THETA = 500000.0
DILATED_GROUPS = ((128, 1), (512, 4), (2048, 16))
MAX_WINDOW = 2048
QBLOCK = 128
ATTN_SCALE = HEAD_DIM ** -0.5
D_POOL = D_MODEL - D_ATTN
POOL_WINDOWS = (2, 4, 8, 16)
POOL_GROUP_DIM = D_POOL // len(POOL_WINDOWS)
POOL_STATE = max(POOL_WINDOWS) - 1
D_MIX = D_ATTN + D_POOL
D_IN = 3 * D_ATTN + D_POOL
N_EXPERTS = 64
TOP_K = 8
N_EXPERT_GROUPS = 8
TOPK_GROUPS = 4
D_EXPERT = 512
D_SHARED = 512
ROUTED_SCALE = 2.5
MOE_TOKEN_BLOCK = 256
EPS = 1e-6

kernel_name = 'dilated_pool_moe_hybrid_step'


def rms_norm(x, g):
    xf = x.astype(jnp.float32)
    y = xf * lax.rsqrt(jnp.mean(xf * xf, axis=-1, keepdims=True) + EPS)
    return (y * g.astype(jnp.float32)).astype(x.dtype)


def modulate(x, g, shift, scale):
    return rms_norm(x, g) * (1 + scale) + shift


def ada_modulation(c, w_ada, b_ada):
    a = jax.nn.silu(c) @ w_ada + b_ada
    return jnp.split(a[:, None, :], 6, axis=-1)


def rope_partial(x, pos):
    half = ROPE_DIM // 2
    inv_freq = jnp.float32(ROPE_THETA) ** (-jnp.arange(half, dtype=jnp.float32) / half)
    ang = pos.astype(jnp.float32)[:, None] * inv_freq[None, :]
    cos = jnp.cos(ang)[None, :, None, :]
    sin = jnp.sin(ang)[None, :, None, :]
    xr = x[..., :ROPE_DIM].astype(jnp.float32)
    x1, x2 = xr[..., :half], xr[..., half:]
    rot = jnp.concatenate([x1 * cos - x2 * sin, x2 * cos + x1 * sin], axis=-1)
    return jnp.concatenate([rot.astype(x.dtype), x[..., ROPE_DIM:]], axis=-1)


def mixer_inputs(x, c, pos, w_ada, b_ada, g_mix, w_in, q_norm, k_norm):
    mods = ada_modulation(c, w_ada, b_ada)
    h = modulate(x, g_mix, mods[0], mods[1])
    z = h @ w_in
    B, T, _ = z.shape
    q, k, v, u = jnp.split(z, [D_ATTN, 2 * D_ATTN, 3 * D_ATTN], axis=-1)
    heads = lambda t: t.reshape(B, T, N_HEADS_A, HEAD_DIM)
    q = rope_partial(rms_norm(heads(q), q_norm), pos)
    k = rope_partial(rms_norm(heads(k), k_norm), pos)
    return mods, q, k, heads(v), u


def dilated_group_prompt(q, k, v, dilation, band):
    B, S, H, E = q.shape
    L = S // dilation
    nb = -(-L // QBLOCK)
    Lp = nb * QBLOCK

    def split(t):
        t = t.reshape(B, L, dilation, H, E)
        t = jnp.pad(t, ((0, 0), (0, Lp - L), (0, 0), (0, 0), (0, 0)))
        return t.reshape(B, nb, QBLOCK, dilation, H, E)

    def with_prev(t):
        prev = jnp.pad(t[:, :-1], ((0, 0), (1, 0), (0, 0), (0, 0), (0, 0), (0, 0)))
        return jnp.concatenate([prev, t], axis=2)

    qb = split(q)
    kk = with_prev(split(k))
    vv = with_prev(split(v))
    s = jnp.einsum('bnqrhe,bnkrhe->bnrhqk', qb, kk, preferred_element_type=jnp.float32) * ATTN_SCALE
    qi = jnp.arange(QBLOCK)[:, None]
    kj = jnp.arange(2 * QBLOCK)[None, :]
    blk = jnp.arange(nb)[:, None, None]
    dist = qi + QBLOCK - kj
    key_sub = blk * QBLOCK - QBLOCK + kj
    valid = (dist >= 0) & (dist <= band) & (key_sub >= 0)
    s = jnp.where(valid[None, :, None, None], s, -jnp.inf)
    m = jnp.max(s, axis=-1)
    p = jnp.exp(s - m[..., None])
    l = jnp.sum(p, axis=-1)
    acc = jnp.einsum('bnrhqk,bnkrhe->bnqrhe', p.astype(v.dtype), vv, preferred_element_type=jnp.float32)

    def merge(t):
        t = t.reshape((B, Lp, dilation) + t.shape[4:])[:, :L]
        return t.reshape((B, S) + t.shape[3:])

    return merge(m.transpose(0, 1, 4, 2, 3)), merge(l.transpose(0, 1, 4, 2, 3)), merge(acc)


def dilated_group_decode(q, k_ext, v_ext, n_buf, dilation, band):
    T = q.shape[1]
    idx = n_buf + jnp.arange(T)[:, None] - dilation * jnp.arange(band + 1)[None, :]
    valid = idx >= 0
    idx = jnp.maximum(idx, 0)
    kg = k_ext[:, idx]
    vg = v_ext[:, idx]
    s = jnp.einsum('bthe,btkhe->bthk', q, kg, preferred_element_type=jnp.float32) * ATTN_SCALE
    s = jnp.where(valid[None, :, None, :], s, -jnp.inf)
    m = jnp.max(s, axis=-1)
    p = jnp.exp(s - m[..., None])
    l = jnp.sum(p, axis=-1)
    acc = jnp.einsum('bthk,btkhe->bthe', p.astype(v_ext.dtype), vg, preferred_element_type=jnp.float32)
    return m, l, acc


def combine_groups(parts):
    m_all = parts[0][0]
    for m, _, _ in parts[1:]:
        m_all = jnp.maximum(m_all, m)
    num = jnp.zeros_like(parts[0][2])
    den = jnp.zeros_like(parts[0][1])
    for m, l, acc in parts:
        w = jnp.exp(m - m_all)
        num = num + w[..., None] * acc
        den = den + w * l
    return num / den[..., None]


def multiscale_pool(u_ext, pos, w_pool, pool_scale):
    P = POOL_STATE
    T = u_ext.shape[1] - P
    uf = u_ext.astype(jnp.float32)
    cs = jnp.concatenate([jnp.zeros_like(uf[:, :1]), jnp.cumsum(uf, axis=1)], axis=1)
    end = cs[:, P + 1:]
    u_new = uf[:, P:]
    outs = []
    for g, w in enumerate(POOL_WINDOWS):
        sl = slice(g * POOL_GROUP_DIM, (g + 1) * POOL_GROUP_DIM)
        win_sum = end[..., sl] - cs[:, P + 1 - w:P + 1 - w + T, sl]
        count = jnp.minimum(w, pos + 1).astype(jnp.float32)[None, :, None]
        d = win_sum / count - u_new[..., sl]
        outs.append(jnp.einsum('btc,cf->btf', d.astype(u_ext.dtype), w_pool[g]))
    return jnp.concatenate(outs, axis=-1) * pool_scale


def route(xt, w_router, b_router):
    N = xt.shape[0]
    scores = jax.nn.sigmoid((xt @ w_router).astype(jnp.float32))
    biased = scores + b_router.astype(jnp.float32)
    grp = biased.reshape(N, N_EXPERT_GROUPS, N_EXPERTS // N_EXPERT_GROUPS)
    grp_score = jnp.sum(lax.top_k(grp, 2)[0], axis=-1)
    _, gidx = lax.top_k(grp_score, TOPK_GROUPS)
    gmask = jnp.sum(jax.nn.one_hot(gidx, N_EXPERT_GROUPS, dtype=jnp.float32), axis=1)
    emask = jnp.repeat(gmask, N_EXPERTS // N_EXPERT_GROUPS, axis=1)
    _, eidx = lax.top_k(jnp.where(emask > 0, biased, -jnp.inf), TOP_K)
    sel = jnp.take_along_axis(scores, eidx, axis=-1)
    wts = sel / jnp.sum(sel, axis=-1, keepdims=True) * ROUTED_SCALE
    return jnp.sum(jax.nn.one_hot(eidx, N_EXPERTS, dtype=jnp.float32) * wts[..., None], axis=1)


def moe_ffn(x, w_router, b_router, w_gate, w_up, w_down, ws_gate, ws_up, ws_down):
    B, T, D = x.shape
    xt = x.reshape(-1, D)
    N = xt.shape[0]
    gates = route(xt, w_router, b_router)
    blk = min(MOE_TOKEN_BLOCK, N)
    nblk = -(-N // blk)
    pad = nblk * blk - N
    xb = jnp.pad(xt, ((0, pad), (0, 0))).reshape(nblk, blk, D)
    gb = jnp.pad(gates, ((0, pad), (0, 0))).reshape(nblk, blk, N_EXPERTS)

    def token_block(args):
        xs, gs = args
        hg = jnp.einsum('td,edf->tef', xs, w_gate)
        hu = jnp.einsum('td,edf->tef', xs, w_up)
        h = jax.nn.silu(hg) * hu * gs[..., None].astype(xs.dtype)
        return jnp.einsum('tef,efd->td', h, w_down)

    routed = lax.map(token_block, (xb, gb)).reshape(-1, D)[:N]
    shared = (jax.nn.silu(xt @ ws_gate) * (xt @ ws_up)) @ ws_down
    return (routed + shared).reshape(B, T, D)


def finish_layer(x, attn, pool, mods, w_out, g_ffn, w_router, b_router, w_gate, w_up, w_down, ws_gate, ws_up, ws_down):
    B, T, _ = x.shape
    mix = jnp.concatenate([attn.reshape(B, T, D_ATTN).astype(x.dtype), pool], axis=-1)
    x = x + mods[2] * (mix @ w_out)
    h = modulate(x, g_ffn, mods[3], mods[4])
    return x + mods[5] * moe_ffn(h, w_router, b_router, w_gate, w_up, w_down, ws_gate, ws_up, ws_down)


def setup_inputs(seed: int = 0) -> dict:
    key = jax.random.key(seed)
    ks = jax.random.split(key, 32)
    nrm = lambda k, shape, scale: jax.random.normal(k, shape, jnp.float32) * scale
    n_buf = min(MAX_WINDOW, PAST_LEN)
    return {
        'x_prompt': nrm(ks[0], (BATCH, SEQ, D_MODEL), 1.0),
        'x_sample': nrm(ks[1], (DEC_BATCH, DEC_SEQ, D_MODEL), 1.0),
        'cache_k_win': nrm(ks[2], (DEPTH, DEC_BATCH, n_buf, N_HEADS_A, HEAD_DIM), 1.0),
        'cache_v_win': nrm(ks[3], (DEPTH, DEC_BATCH, n_buf, N_HEADS_A, HEAD_DIM), 1.0),
        'state_pool': nrm(ks[4], (DEPTH, DEC_BATCH, POOL_STATE, D_POOL), 1.0),
        'c_prompt': nrm(ks[5], (BATCH, D_MODEL), 1.0),
        'c_sample': nrm(ks[6], (DEC_BATCH, D_MODEL), 1.0),
        'w_ada': nrm(ks[7], (DEPTH, D_MODEL, 6 * D_MODEL), 0.5 * D_MODEL ** -0.5),
        'b_ada': nrm(ks[8], (DEPTH, 6 * D_MODEL), 0.02),
        'g_mix': 1.0 + nrm(ks[9], (DEPTH, D_MODEL), 0.02),
        'w_in': nrm(ks[10], (DEPTH, D_MODEL, D_IN), D_MODEL ** -0.5),
        'q_norm': 1.0 + nrm(ks[11], (DEPTH, HEAD_DIM), 0.02),
        'k_norm': 1.0 + nrm(ks[12], (DEPTH, HEAD_DIM), 0.02),
        'w_pool': nrm(ks[13], (DEPTH, len(POOL_WINDOWS), POOL_GROUP_DIM, POOL_GROUP_DIM), POOL_GROUP_DIM ** -0.5),
        'pool_scale': 1.0 + nrm(ks[14], (DEPTH, D_POOL), 0.1),
        'w_out': nrm(ks[15], (DEPTH, D_MIX, D_MODEL), D_MIX ** -0.5),
        'g_ffn': 1.0 + nrm(ks[16], (DEPTH, D_MODEL), 0.02),
        'w_router': nrm(ks[17], (DEPTH, D_MODEL, N_EXPERTS), D_MODEL ** -0.5),
        'b_router': nrm(ks[18], (DEPTH, N_EXPERTS), 0.01),
        'w_gate': nrm(ks[19], (DEPTH, N_EXPERTS, D_MODEL, D_EXPERT), D_MODEL ** -0.5),
        'w_up': nrm(ks[20], (DEPTH, N_EXPERTS, D_MODEL, D_EXPERT), D_MODEL ** -0.5),
        'w_down': nrm(ks[21], (DEPTH, N_EXPERTS, D_EXPERT, D_MODEL), D_EXPERT ** -0.5),
        'ws_gate': nrm(ks[22], (DEPTH, D_MODEL, D_SHARED), D_MODEL ** -0.5),
        'ws_up': nrm(ks[23], (DEPTH, D_MODEL, D_SHARED), D_MODEL ** -0.5),
        'ws_down': nrm(ks[24], (DEPTH, D_SHARED, D_MODEL), D_SHARED ** -0.5),
    }


def reference(x_prompt, x_sample, cache_k_win, cache_v_win, state_pool, c_prompt, c_sample,
              w_ada, b_ada, g_mix, w_in, q_norm, k_norm, w_pool, pool_scale, w_out, g_ffn,
              w_router, b_router, w_gate, w_up, w_down, ws_gate, ws_up, ws_down):
    S = x_prompt.shape[1]
    T = x_sample.shape[1]
    pos_p = jnp.arange(S, dtype=jnp.int32)
    pos_s = PAST_LEN + jnp.arange(T, dtype=jnp.int32)
    n_buf = cache_k_win.shape[2]
    n_keep = min(MAX_WINDOW, S)
    xp, xs = x_prompt, x_sample
    kp_l, vp_l, pp_l, ks_l, vs_l, ps_l = [], [], [], [], [], []
    for l in range(DEPTH):
        ffn_w = (w_router[l], b_router[l], w_gate[l], w_up[l], w_down[l], ws_gate[l], ws_up[l], ws_down[l])
        mods, q, k, v, u = mixer_inputs(xp, c_prompt, pos_p, w_ada[l], b_ada[l], g_mix[l], w_in[l], q_norm[l], k_norm[l])
        attn = combine_groups([dilated_group_prompt(q, k, v, d, w // d) for (w, d) in DILATED_GROUPS])
        u_ext = jnp.pad(u, ((0, 0), (POOL_STATE, 0), (0, 0)))
        pool = multiscale_pool(u_ext, pos_p, w_pool[l], pool_scale[l])
        xp_next = finish_layer(xp, attn, pool, mods, w_out[l], g_ffn[l], *ffn_w)
        kp_l.append(k[:, S - n_keep:])
        vp_l.append(v[:, S - n_keep:])
        pp_l.append(u_ext[:, -POOL_STATE:])
        mods, q, k, v, u = mixer_inputs(xs, c_sample, pos_s, w_ada[l], b_ada[l], g_mix[l], w_in[l], q_norm[l], k_norm[l])
        k_ext = jnp.concatenate([cache_k_win[l].astype(k.dtype), k], axis=1)
        v_ext = jnp.concatenate([cache_v_win[l].astype(v.dtype), v], axis=1)
        attn = combine_groups([dilated_group_decode(q, k_ext, v_ext, n_buf, d, w // d) for (w, d) in DILATED_GROUPS])
        u_ext = jnp.concatenate([state_pool[l].astype(u.dtype), u], axis=1)
        pool = multiscale_pool(u_ext, pos_s, w_pool[l], pool_scale[l])
        xs = finish_layer(xs, attn, pool, mods, w_out[l], g_ffn[l], *ffn_w)
        ks_l.append(k_ext[:, -n_buf:])
        vs_l.append(v_ext[:, -n_buf:])
        ps_l.append(u_ext[:, -POOL_STATE:])
        xp = xp_next
    return (xp, xs, jnp.stack(kp_l), jnp.stack(vp_l), jnp.stack(pp_l), jnp.stack(ks_l), jnp.stack(vs_l), jnp.stack(ps_l))
```

```python
import functools

import jax
import jax.numpy as jnp
from jax import lax
from jax.experimental import pallas as pl
from jax.experimental.pallas import tpu as pltpu

F32 = jnp.float32
BF16 = jnp.bfloat16
I32 = jnp.int32

D_MODEL = 2048
N_HEADS = 8
HEAD_DIM = 128
D_ATTN = N_HEADS * HEAD_DIM
D_POOL = D_MODEL - D_ATTN
D_IN = 3 * D_ATTN + D_POOL
ROPE_DIM = HEAD_DIM // 4
ROPE_HALF = ROPE_DIM // 2
ROPE_THETA = 500000.0
DILATED_GROUPS = ((128, 1), (512, 4), (2048, 16))
QBLOCK = 128
ATTN_SCALE = HEAD_DIM ** -0.5
POOL_WINDOWS = (2, 4, 8, 16)
POOL_GROUP_DIM = D_POOL // len(POOL_WINDOWS)
POOL_STATE = max(POOL_WINDOWS) - 1
POOL_HALO = 16
N_EXPERTS = 64
TOP_K = 8
N_EXPERT_GROUPS = 8
GROUP_SIZE = N_EXPERTS // N_EXPERT_GROUPS
TOPK_GROUPS = 4
D_EXPERT = 512
D_SHARED = 512
ROUTED_SCALE = 2.5
EPS = 1e-6
PAST_LEN = 16384
MAX_WINDOW = 2048

LANES = 128
ROW_CHUNKS = D_MODEL // LANES
NEG_BIG = -1e30

TOKEN_TILE = 256
EXPERT_TILE = 256
COMBINE_TILE = 128
ROUTE_TILE = 256
VMEM_LIMIT = 60 * 1024 * 1024


def _cparams(sem, vmem=VMEM_LIMIT):
    return pltpu.CompilerParams(dimension_semantics=sem, vmem_limit_bytes=vmem)


def _ada_kernel(c_ref, w_ref, b_ref, o_ref):
    c = c_ref[...]
    s = (c * jax.nn.sigmoid(c)).astype(BF16)
    o_ref[...] = jnp.dot(s, w_ref[...].astype(BF16), preferred_element_type=F32) + b_ref[...]


def _ada(c_all, w_ada, b_ada):
    rows = c_all.shape[0]
    n = w_ada.shape[1]
    tn = 1024
    return pl.pallas_call(
        _ada_kernel,
        grid=(n // tn,),
        in_specs=[
            pl.BlockSpec((rows, D_MODEL), lambda j: (0, 0)),
            pl.BlockSpec((D_MODEL, tn), lambda j: (0, j)),
            pl.BlockSpec((1, tn), lambda j: (0, j)),
        ],
        out_specs=pl.BlockSpec((rows, tn), lambda j: (0, j)),
        out_shape=jax.ShapeDtypeStruct((rows, n), F32),
        compiler_params=_cparams(("parallel",)),
        name="ada_modulation",
    )(c_all, w_ada, b_ada.reshape(1, n))


def _rms(x):
    return x * lax.rsqrt(jnp.mean(x * x, axis=-1, keepdims=True) + EPS)


def _in_kernel(x_ref, sh_ref, sc_ref, g_ref, w_ref, qn_ref, kn_ref, c_ref, s1_ref, s2_ref,
               q_ref, k_ref, v_ref, u_ref):
    x = x_ref[...]
    h = _rms(x) * g_ref[...] * (1.0 + sc_ref[0]) + sh_ref[0]
    hb = h.astype(BF16)
    cos = c_ref[...]
    s1 = s1_ref[...]
    s2 = s2_ref[...]

    def qk(sec, nrm, out_ref):
        z = jnp.dot(hb, w_ref[:, sec * D_ATTN:(sec + 1) * D_ATTN], preferred_element_type=F32)
        for hd in range(N_HEADS):
            sl = slice(hd * HEAD_DIM, (hd + 1) * HEAD_DIM)
            r = _rms(z[:, sl]) * nrm
            out_ref[:, sl] = (r * cos + pltpu.roll(r, HEAD_DIM - ROPE_HALF, 1) * s1
                              + pltpu.roll(r, ROPE_HALF, 1) * s2)

    qk(0, qn_ref[...], q_ref)
    qk(1, kn_ref[...], k_ref)
    v_ref[...] = jnp.dot(hb, w_ref[:, 2 * D_ATTN:3 * D_ATTN], preferred_element_type=F32)
    u_ref[...] = jnp.dot(hb, w_ref[:, 3 * D_ATTN:], preferred_element_type=F32)


def _in_proj(x2d, shift, scale, g_mix, w_in_bf, q_norm, k_norm, rope, tm, rows_per_mod, rope_tiles):
    m = x2d.shape[0]
    r = shift.shape[1]
    mod_spec = pl.BlockSpec((1, r, D_MODEL), lambda i: (i // rows_per_mod, 0, 0))
    rope_spec = pl.BlockSpec((tm, HEAD_DIM), lambda i: (i % rope_tiles, 0))
    out_spec = pl.BlockSpec((tm, D_ATTN), lambda i: (i, 0))
    out_sd = jax.ShapeDtypeStruct((m, D_ATTN), F32)
    return pl.pallas_call(
        _in_kernel,
        grid=(m // tm,),
        in_specs=[
            pl.BlockSpec((tm, D_MODEL), lambda i: (i, 0)),
            mod_spec, mod_spec,
            pl.BlockSpec((1, D_MODEL), lambda i: (0, 0)),
            pl.BlockSpec((D_MODEL, D_IN), lambda i: (0, 0)),
            pl.BlockSpec((1, HEAD_DIM), lambda i: (0, 0)),
            pl.BlockSpec((1, HEAD_DIM), lambda i: (0, 0)),
            rope_spec, rope_spec, rope_spec,
        ],
        out_specs=[out_spec] * 4,
        out_shape=[out_sd] * 4,
        compiler_params=_cparams(("parallel",)),
        name="in_proj",
    )(x2d, shift, scale, g_mix, w_in_bf, q_norm, k_norm, *rope)


def _rope_tables(pos):
    inv_freq = jnp.float32(ROPE_THETA) ** (-jnp.arange(ROPE_HALF, dtype=F32) / ROPE_HALF)
    ang = pos.astype(F32)[:, None] * inv_freq[None, :]
    cos, sin = jnp.cos(ang), jnp.sin(ang)
    t = pos.shape[0]
    rest = HEAD_DIM - ROPE_DIM
    c = jnp.concatenate([cos, cos, jnp.ones((t, rest), F32)], axis=1)
    s1 = jnp.concatenate([-sin, jnp.zeros((t, ROPE_HALF + rest), F32)], axis=1)
    s2 = jnp.concatenate([jnp.zeros((t, ROPE_HALF), F32), sin, jnp.zeros((t, rest), F32)], axis=1)
    return c, s1, s2


def _attn_prompt_kernel(q_ref, k_ref, v_ref, o_ref, m_scr, l_scr, acc_scr):
    seq = q_ref.shape[1]
    m_scr[...] = jnp.full(m_scr.shape, NEG_BIG, F32)
    l_scr[...] = jnp.zeros(l_scr.shape, F32)
    acc_scr[...] = jnp.zeros(acc_scr.shape, F32)
    qi = lax.broadcasted_iota(I32, (QBLOCK, QBLOCK), 0)
    kj = lax.broadcasted_iota(I32, (QBLOCK, QBLOCK), 1)
    mask_cur = kj <= qi
    mask_prev = kj >= qi

    for window, dil in DILATED_GROUPS:
        assert window // dil == QBLOCK
        nblk = seq // dil // QBLOCK

        def rows(start, dil=dil):
            return pl.ds(start, QBLOCK) if dil == 1 else pl.ds(start, QBLOCK, stride=dil)

        def body(it, carry, dil=dil, nblk=nblk, rows=rows):
            res = it // nblk
            blk = it % nblk
            start = blk * (QBLOCK * dil) + res
            prev = jnp.maximum(start - QBLOCK * dil, res)
            q = q_ref[0, rows(start), :].astype(BF16)
            k_cur = k_ref[0, rows(start), :].astype(BF16)
            k_prev = k_ref[0, rows(prev), :].astype(BF16)
            v_cur = v_ref[0, rows(start), :].astype(BF16)
            v_prev = v_ref[0, rows(prev), :].astype(BF16)
            dims = (((1,), (1,)), ((), ()))
            s_cur = lax.dot_general(q, k_cur, dims, preferred_element_type=F32) * ATTN_SCALE
            s_prev = lax.dot_general(q, k_prev, dims, preferred_element_type=F32) * ATTN_SCALE
            s_cur = jnp.where(mask_cur, s_cur, NEG_BIG)
            s_prev = jnp.where(jnp.logical_and(mask_prev, blk > 0), s_prev, NEG_BIG)
            m_old = m_scr[rows(start), :]
            l_old = l_scr[rows(start), :]
            a_old = acc_scr[rows(start), :]
            m_blk = jnp.maximum(jnp.max(s_cur, axis=1, keepdims=True),
                                jnp.max(s_prev, axis=1, keepdims=True))
            m_new = jnp.maximum(m_old, m_blk)
            alpha = jnp.exp(m_old - m_new)
            p_cur = jnp.exp(s_cur - m_new[:, :1])
            p_prev = jnp.exp(s_prev - m_new[:, :1])
            l_new = alpha * l_old + (jnp.sum(p_cur, axis=1, keepdims=True)
                                     + jnp.sum(p_prev, axis=1, keepdims=True))
            pv = (jnp.dot(p_cur.astype(BF16), v_cur, preferred_element_type=F32)
                  + jnp.dot(p_prev.astype(BF16), v_prev, preferred_element_type=F32))
            m_scr[rows(start), :] = m_new
            l_scr[rows(start), :] = l_new
            acc_scr[rows(start), :] = alpha * a_old + pv
            return carry

        lax.fori_loop(0, dil * nblk, body, 0)

    o_ref[0] = acc_scr[...] / l_scr[...]


def _attn_prompt(q, k, v):
    b, s, _ = q.shape
    spec = pl.BlockSpec((1, s, HEAD_DIM), lambda bi, hi: (bi, 0, hi))
    return pl.pallas_call(
        _attn_prompt_kernel,
        grid=(b, N_HEADS),
        in_specs=[spec, spec, spec],
        out_specs=spec,
        out_shape=jax.ShapeDtypeStruct((b, s, D_ATTN), F32),
        scratch_shapes=[pltpu.VMEM((s, HEAD_DIM), F32)] * 3,
        compiler_params=_cparams(("parallel", "parallel")),
        name="attn_prompt",
    )(q, k, v)


def _attn_decode_kernel(q_ref, kn_ref, vn_ref, k1_ref, k2_ref, k3_ref, v1_ref, v2_ref, v3_ref, o_ref):
    q = q_ref[0]
    k_new = kn_ref[0]
    v_new = vn_ref[0]
    n_grp = len(DILATED_GROUPS)

    def scores(kb):
        return jnp.sum(kb * q[None], axis=-1, keepdims=True) * ATTN_SCALE

    kcs = [k1_ref[0], k2_ref[0][:, 0], k3_ref[0][:, 0]]
    vcs = [v1_ref[0], v2_ref[0][:, 0], v3_ref[0][:, 0]]
    s_new = jnp.sum(k_new * q, axis=-1, keepdims=True) * ATTN_SCALE
    s_grp = [scores(kb) for kb in kcs]
    m = s_new
    for s in s_grp:
        m = jnp.maximum(m, jnp.max(s, axis=0))
    p_new = jnp.exp(s_new - m)
    den = n_grp * p_new
    num = n_grp * p_new * v_new
    for s, vb in zip(s_grp, vcs):
        p = jnp.exp(s - m[None])
        den = den + jnp.sum(p, axis=0)
        num = num + jnp.sum(p * vb, axis=0)
    o_ref[0] = num / den


def _attn_decode(q, k_new, v_new, cache_k, cache_v):
    bd, n_buf = cache_k.shape[0], cache_k.shape[1]
    assert n_buf == MAX_WINDOW

    def views(cache):
        out, specs = [], []
        for window, dil in DILATED_GROUPS:
            band = window // dil
            assert band == QBLOCK and n_buf % dil == 0 and n_buf // dil >= band
            if dil == 1:
                out.append(cache)
                specs.append(pl.BlockSpec((1, band, N_HEADS, HEAD_DIM),
                                          lambda b, nb=n_buf // band: (b, nb - 1, 0, 0)))
            else:
                out.append(cache.reshape(bd, n_buf // dil, dil, N_HEADS, HEAD_DIM))
                specs.append(pl.BlockSpec((1, band, 1, N_HEADS, HEAD_DIM),
                                          lambda b, nb=n_buf // dil // band: (b, nb - 1, 0, 0, 0)))
        return out, specs

    kv, kspecs = views(cache_k)
    vv, vspecs = views(cache_v)
    tok = pl.BlockSpec((1, N_HEADS, HEAD_DIM), lambda b: (b, 0, 0))
    return pl.pallas_call(
        _attn_decode_kernel,
        grid=(bd,),
        in_specs=[tok, tok, tok] + kspecs + vspecs,
        out_specs=tok,
        out_shape=jax.ShapeDtypeStruct((bd, N_HEADS, HEAD_DIM), F32),
        compiler_params=_cparams(("parallel",)),
        name="attn_decode",
    )(q, k_new, v_new, *kv, *vv)


def _window_update_kernel(ck_ref, cv_ref, kn_ref, vn_ref, ok_ref, ov_ref, sem):
    bd, n_buf = ck_ref.shape[0], ck_ref.shape[1]
    copies = []
    for b in range(bd):
        for src, dst in ((ck_ref, ok_ref), (cv_ref, ov_ref)):
            copies.append(pltpu.make_async_copy(src.at[b, pl.ds(1, n_buf - 1)],
                                                dst.at[b, pl.ds(0, n_buf - 1)], sem.at[0]))
    for src, dst in ((kn_ref, ok_ref), (vn_ref, ov_ref)):
        copies.append(pltpu.make_async_copy(src, dst.at[:, n_buf - 1], sem.at[1]))
    for c in copies:
        c.start()
    for c in copies:
        c.wait()


def _window_update(cache_k, cache_v, k_new, v_new):
    any_spec = pl.BlockSpec(memory_space=pl.ANY)
    sd = jax.ShapeDtypeStruct(cache_k.shape, cache_k.dtype)
    return pl.pallas_call(
        _window_update_kernel,
        in_specs=[any_spec] * 4,
        out_specs=[any_spec] * 2,
        out_shape=[sd, sd],
        scratch_shapes=[pltpu.SemaphoreType.DMA((2,))],
        name="window_update",
    )(cache_k, cache_v, k_new, v_new)


def _pool_project(d_groups, wp_ref, ps_ref, out_ref):
    for g, d in enumerate(d_groups):
        sl = slice(g * POOL_GROUP_DIM, (g + 1) * POOL_GROUP_DIM)
        out_ref[:, sl] = jnp.dot(d.astype(BF16), wp_ref[g], preferred_element_type=F32) * ps_ref[:, sl]


def _pool_prompt_kernel(u_ref, prev_ref, wp_ref, ps_ref, o_ref):
    i = pl.program_id(1)
    tp = u_ref.shape[1]
    u = u_ref[0]
    prev = jnp.where(i > 0, prev_ref[0], 0.0)
    ext = jnp.concatenate([prev, u], axis=0)
    pos = i * tp + lax.broadcasted_iota(I32, (tp, 1), 0)
    d_groups = []
    for g, w in enumerate(POOL_WINDOWS):
        sl = slice(g * POOL_GROUP_DIM, (g + 1) * POOL_GROUP_DIM)
        a = ext[:, sl]
        span = 1
        while span < w:
            a = a + pltpu.roll(a, span, 0)
            span *= 2
        win = a[POOL_HALO:, :]
        cnt = jnp.minimum(w, pos + 1).astype(F32)
        d_groups.append(win / cnt - u[:, sl])
    _pool_project(d_groups, wp_ref, ps_ref, o_ref.at[0])


def _pool_prompt(u, w_pool_bf, pool_scale):
    b, s, _ = u.shape
    tp = 512
    halo_blocks = tp // POOL_HALO
    return pl.pallas_call(
        _pool_prompt_kernel,
        grid=(b, s // tp),
        in_specs=[
            pl.BlockSpec((1, tp, D_POOL), lambda bi, i: (bi, i, 0)),
            pl.BlockSpec((1, POOL_HALO, D_POOL), lambda bi, i: (bi, jnp.maximum(i * halo_blocks - 1, 0), 0)),
            pl.BlockSpec(w_pool_bf.shape, lambda bi, i: (0, 0, 0)),
            pl.BlockSpec((1, D_POOL), lambda bi, i: (0, 0)),
        ],
        out_specs=pl.BlockSpec((1, tp, D_POOL), lambda bi, i: (bi, i, 0)),
        out_shape=jax.ShapeDtypeStruct((b, s, D_POOL), F32),
        compiler_params=_cparams(("parallel", "parallel")),
        name="pool_prompt",
    )(u, u, w_pool_bf, pool_scale)


def _pool_decode_kernel(u_ref, st_ref, wp_ref, ps_ref, o_ref):
    u = u_ref[...]
    d_groups = []
    for g, w in enumerate(POOL_WINDOWS):
        sl = slice(g * POOL_GROUP_DIM, (g + 1) * POOL_GROUP_DIM)
        win = u[:, sl]
        for j in range(1, w):
            win = win + st_ref[POOL_STATE - j][:, sl]
        d_groups.append(win / float(w) - u[:, sl])
    _pool_project(d_groups, wp_ref, ps_ref, o_ref)


def _pool_decode(u, state_t, w_pool_bf, pool_scale):
    assert PAST_LEN + 1 >= max(POOL_WINDOWS)
    return pl.pallas_call(
        _pool_decode_kernel,
        out_shape=jax.ShapeDtypeStruct(u.shape, F32),
        compiler_params=_cparams(None),
        name="pool_decode",
    )(u, state_t, w_pool_bf, pool_scale)


def _split_bf16(x):
    hi = x.astype(BF16)
    lo = (x - hi.astype(F32)).astype(BF16)
    return hi, lo


def _out_kernel(x_ref, a_ref, p_ref, g1_ref, sh_ref, sc_ref, g2_ref, wo_ref, gf_ref, wr_ref,
                wsg_ref, wsu_ref, wsd_ref, base_ref, h2_ref, lg_ref):
    tm = x_ref.shape[0]
    mix = (jnp.dot(a_ref[...].astype(BF16), wo_ref[:D_ATTN, :], preferred_element_type=F32)
           + jnp.dot(p_ref[...].astype(BF16), wo_ref[D_ATTN:, :], preferred_element_type=F32))
    x1 = x_ref[...] + g1_ref[0] * mix
    h2 = _rms(x1) * gf_ref[...] * (1.0 + sc_ref[0]) + sh_ref[0]
    for c in range(ROW_CHUNKS):
        h2_ref[pl.ds(c, tm, stride=ROW_CHUNKS), :] = h2[:, c * LANES:(c + 1) * LANES]
    h_hi, h_lo = _split_bf16(h2)
    w_hi, w_lo = _split_bf16(wr_ref[...])
    dims = (((1,), (1,)), ((), ()))
    lg_ref[...] = (lax.dot_general(w_hi, h_hi, dims, preferred_element_type=F32)
                   + lax.dot_general(w_hi, h_lo, dims, preferred_element_type=F32)
                   + lax.dot_general(w_lo, h_hi, dims, preferred_element_type=F32))
    sg = jnp.dot(h_hi, wsg_ref[...], preferred_element_type=F32)
    su = jnp.dot(h_hi, wsu_ref[...], preferred_element_type=F32)
    hs = (sg * jax.nn.sigmoid(sg) * su).astype(BF16)
    base_ref[...] = x1 + g2_ref[0] * jnp.dot(hs, wsd_ref[...], preferred_element_type=F32)


def _out_proj(x2d, attn, pool, gate1, shift2, scale2, gate2, w_out_bf, g_ffn, w_router_t,
              ws_gate_bf, ws_up_bf, ws_down_bf, tm, rows_per_mod, h2_tokens, h2_all, h2_block0):
    m = x2d.shape[0]
    r = gate1.shape[1]
    n_main = m // tm
    h2_rows = h2_tokens * ROW_CHUNKS
    n_fill = 0 if h2_all is not None else pl.cdiv(h2_rows - m * ROW_CHUNKS, tm * ROW_CHUNKS)
    last = n_main - 1
    row = lambda i: jnp.minimum(i, last)
    mod_spec = pl.BlockSpec((1, r, D_MODEL), lambda i: (row(i) // rows_per_mod, 0, 0))
    const = lambda shape: pl.BlockSpec(shape, lambda i: (0,) * len(shape))
    in_specs = [
        pl.BlockSpec((tm, D_MODEL), lambda i: (row(i), 0)),
        pl.BlockSpec((tm, D_ATTN), lambda i: (row(i), 0)),
        pl.BlockSpec((tm, D_POOL), lambda i: (row(i), 0)),
        mod_spec, mod_spec, mod_spec, mod_spec,
        const((D_MODEL, D_MODEL)),
        const((1, D_MODEL)),
        const((N_EXPERTS, D_MODEL)),
        const((D_MODEL, D_SHARED)), const((D_MODEL, D_SHARED)), const((D_SHARED, D_MODEL)),
    ]
    args = [x2d, attn, pool, gate1, shift2, scale2, gate2, w_out_bf, g_ffn, w_router_t,
            ws_gate_bf, ws_up_bf, ws_down_bf]
    aliases = {}
    n_in = len(args)
    if h2_all is not None:
        in_specs.append(pl.BlockSpec(memory_space=pl.ANY))
        args.append(h2_all)
        aliases = {n_in: 1}

    def kernel(*refs):
        refs = refs[:n_in] + refs[len(args):]
        if n_fill == 0:
            _out_kernel(*refs)
            return
        step = pl.program_id(0)
        pl.when(step < n_main)(lambda: _out_kernel(*refs))

        @pl.when(step >= n_main)
        def _():
            h2_ref = refs[n_in + 1]
            h2_ref[...] = jnp.zeros(h2_ref.shape, F32)

    return pl.pallas_call(
        kernel,
        grid=(n_main + n_fill,),
        in_specs=in_specs,
        out_specs=[
            pl.BlockSpec((tm, D_MODEL), lambda i: (row(i), 0)),
            pl.BlockSpec((tm * ROW_CHUNKS, LANES), lambda i: (i + h2_block0, 0)),
            pl.BlockSpec((N_EXPERTS, tm), lambda i: (0, row(i))),
        ],
        out_shape=[
            jax.ShapeDtypeStruct((m, D_MODEL), F32),
            jax.ShapeDtypeStruct((h2_rows, LANES), F32),
            jax.ShapeDtypeStruct((N_EXPERTS, m), F32),
        ],
        input_output_aliases=aliases,
        compiler_params=_cparams(("arbitrary",)),
        name="out_proj",
    )(*args)


def _first_index_of_max(v, iota, size, axis):
    m = jnp.max(v, axis=axis, keepdims=True)
    idx = jnp.min(jnp.where(v == m, iota, size), axis=axis, keepdims=True)
    return m, idx


def _route_kernel(n_valid, lg_ref, b_ref, eidx_ref, wts_ref, rank_ref, cnt_ref, carry):
    i = pl.program_id(0)
    tn = lg_ref.shape[1]

    @pl.when(i == 0)
    def _():
        carry[...] = jnp.zeros(carry.shape, F32)

    scores = jax.nn.sigmoid(lg_ref[...])
    biased = scores + b_ref[...]
    grp = biased.reshape(N_EXPERT_GROUPS, GROUP_SIZE, tn)
    io_g = lax.broadcasted_iota(I32, grp.shape, 1)
    m1, i1 = _first_index_of_max(grp, io_g, GROUP_SIZE, 1)
    m2 = jnp.max(jnp.where(io_g == i1, -jnp.inf, grp), axis=1, keepdims=True)
    gscore = (m1 + m2)[:, 0, :]
    io_n = lax.broadcasted_iota(I32, gscore.shape, 0)
    gsel = jnp.zeros(gscore.shape, jnp.bool_)
    for _ in range(TOPK_GROUPS):
        _, gi = _first_index_of_max(gscore, io_n, N_EXPERT_GROUPS, 0)
        hit = io_n == gi
        gsel = jnp.logical_or(gsel, hit)
        gscore = jnp.where(hit, -jnp.inf, gscore)
    emask = jnp.broadcast_to(gsel[:, None, :], grp.shape).reshape(N_EXPERTS, tn)
    cand = jnp.where(emask, biased, -jnp.inf)
    io_e = lax.broadcasted_iota(I32, cand.shape, 0)
    picked = jnp.zeros(cand.shape, jnp.bool_)
    eidx, sel = [], []
    for _ in range(TOP_K):
        _, ei = _first_index_of_max(cand, io_e, N_EXPERTS, 0)
        hit = io_e == ei
        eidx.append(ei)
        sel.append(jnp.sum(jnp.where(hit, scores, 0.0), axis=0, keepdims=True))
        picked = jnp.logical_or(picked, hit)
        cand = jnp.where(hit, -jnp.inf, cand)
    sel = jnp.concatenate(sel, axis=0)
    eidx = jnp.concatenate(eidx, axis=0)
    wts_ref[...] = sel / jnp.sum(sel, axis=0, keepdims=True) * ROUTED_SCALE
    eidx_ref[...] = eidx

    tok = i * tn + lax.broadcasted_iota(I32, cand.shape, 1)
    mask = jnp.logical_and(picked, tok < n_valid).astype(F32)
    tri = (lax.broadcasted_iota(I32, (tn, tn), 0) <= lax.broadcasted_iota(I32, (tn, tn), 1)).astype(BF16)
    pos = jnp.dot(mask.astype(BF16), tri, preferred_element_type=F32) + carry[:, :1]
    ranks = [jnp.sum(jnp.where(io_e == eidx[k:k + 1], pos, 0.0), axis=0, keepdims=True) for k in range(TOP_K)]
    rank_ref[...] = jnp.concatenate(ranks, axis=0).astype(I32) - 1
    total = carry[...] + jnp.sum(mask, axis=1, keepdims=True)
    carry[...] = total
    cnt_ref[...] = total.astype(I32)


def _route(logits_t, b_router, n_valid):
    mp = logits_t.shape[1]
    tn = ROUTE_TILE
    tok_spec = pl.BlockSpec((TOP_K, tn), lambda i: (0, i))
    return pl.pallas_call(
        functools.partial(_route_kernel, n_valid),
        grid=(mp // tn,),
        in_specs=[
            pl.BlockSpec((N_EXPERTS, tn), lambda i: (0, i)),
            pl.BlockSpec((N_EXPERTS, 1), lambda i: (0, 0)),
        ],
        out_specs=[tok_spec, tok_spec, tok_spec, pl.BlockSpec((N_EXPERTS, LANES), lambda i: (0, 0))],
        out_shape=[
            jax.ShapeDtypeStruct((TOP_K, mp), I32),
            jax.ShapeDtypeStruct((TOP_K, mp), F32),
            jax.ShapeDtypeStruct((TOP_K, mp), I32),
            jax.ShapeDtypeStruct((N_EXPERTS, LANES), I32),
        ],
        scratch_shapes=[pltpu.VMEM((N_EXPERTS, LANES), F32)],
        compiler_params=_cparams(("arbitrary",)),
        name="route",
    )(logits_t, b_router.reshape(N_EXPERTS, 1))


TILE_USED = 1
TILE_FIRST = 2


def _experts_kernel(te_ref, ts_ref, tf_ref, tok_ref, h_hbm, wg_ref, wu_ref, wd_ref, y_ref,
                    xbuf, sem, wg_bf, wu_bf, wd_bf):
    j = pl.program_id(0)
    n_steps = pl.num_programs(0)
    tm = EXPERT_TILE
    slot = j % 2
    unroll = 8

    def gather_tile(tile, dst_slot):
        start = ts_ref[tile]

        def body(g, carry):
            for r in range(unroll):
                i = g * unroll + r
                tok = tok_ref[start + i]
                pltpu.make_async_copy(
                    h_hbm.at[pl.ds(pl.multiple_of(tok * ROW_CHUNKS, ROW_CHUNKS), ROW_CHUNKS), :],
                    xbuf.at[dst_slot, pl.ds(pl.multiple_of(i * ROW_CHUNKS, ROW_CHUNKS), ROW_CHUNKS), :],
                    sem.at[dst_slot]).start()
            return carry

        lax.fori_loop(0, tm // unroll, body, 0)

    used = (tf_ref[j] & TILE_USED) != 0

    @pl.when(jnp.logical_and(j == 0, used))
    def _():
        gather_tile(0, 0)

    nxt = jnp.minimum(j + 1, n_steps - 1)

    @pl.when(jnp.logical_and(j + 1 < n_steps, (tf_ref[nxt] & TILE_USED) != 0))
    def _():
        gather_tile(j + 1, 1 - slot)

    @pl.when((tf_ref[j] & TILE_FIRST) != 0)
    def _():
        wg_bf[...] = wg_ref[0].astype(BF16)
        wu_bf[...] = wu_ref[0].astype(BF16)
        wd_bf[...] = wd_ref[0].astype(BF16)

    @pl.when(used)
    def _():
        pltpu.make_async_copy(h_hbm.at[pl.ds(0, tm * ROW_CHUNKS), :], xbuf.at[slot], sem.at[slot]).wait()
        x = jnp.concatenate(
            [xbuf[slot, pl.ds(c, tm, stride=ROW_CHUNKS), :] for c in range(ROW_CHUNKS)], axis=1).astype(BF16)
        hg = jnp.dot(x, wg_bf[...], preferred_element_type=F32)
        hu = jnp.dot(x, wu_bf[...], preferred_element_type=F32)
        h = (hg * jax.nn.sigmoid(hg) * hu).astype(BF16)
        y = jnp.dot(h, wd_bf[...], preferred_element_type=F32)
        for c in range(ROW_CHUNKS):
            y_ref[pl.ds(c, tm, stride=ROW_CHUNKS), :] = y[:, c * LANES:(c + 1) * LANES]

    @pl.when(jnp.logical_not(used))
    def _():
        y_ref[...] = jnp.zeros(y_ref.shape, F32)


def _experts(tile_expert, tile_start, tile_flags, sorted_tok, h2_all, w_gate, w_up, w_down):
    n_tiles = tile_expert.shape[0]
    tm = EXPERT_TILE
    grid_spec = pltpu.PrefetchScalarGridSpec(
        num_scalar_prefetch=4,
        grid=(n_tiles,),
        in_specs=[
            pl.BlockSpec(memory_space=pl.ANY),
            pl.BlockSpec((1, D_MODEL, D_EXPERT), lambda j, te, ts, tf, tok: (te[j], 0, 0)),
            pl.BlockSpec((1, D_MODEL, D_EXPERT), lambda j, te, ts, tf, tok: (te[j], 0, 0)),
            pl.BlockSpec((1, D_EXPERT, D_MODEL), lambda j, te, ts, tf, tok: (te[j], 0, 0)),
        ],
        out_specs=pl.BlockSpec((tm * ROW_CHUNKS, LANES), lambda j, te, ts, tf, tok: (j, 0)),
        scratch_shapes=[
            pltpu.VMEM((2, tm * ROW_CHUNKS, LANES), F32),
            pltpu.SemaphoreType.DMA((2,)),
            pltpu.VMEM((D_MODEL, D_EXPERT), BF16),
            pltpu.VMEM((D_MODEL, D_EXPERT), BF16),
            pltpu.VMEM((D_EXPERT, D_MODEL), BF16),
        ],
    )
    return pl.pallas_call(
        _experts_kernel,
        grid_spec=grid_spec,
        out_shape=jax.ShapeDtypeStruct((n_tiles * tm * ROW_CHUNKS, LANES), F32),
        compiler_params=_cparams(("arbitrary",)),
        name="experts",
    )(tile_expert, tile_start, tile_flags, sorted_tok, h2_all, w_gate, w_up, w_down)


def _combine_kernel(slot_ref, y_hbm, w_ref, base_ref, g2_ref, o_ref, buf, sem):
    i = pl.program_id(0)
    tc = base_ref.shape[0]
    n = tc * TOP_K
    unroll = 8

    def body(g, carry):
        for r in range(unroll):
            a = g * unroll + r
            s = slot_ref[i * n + a]
            pltpu.make_async_copy(
                y_hbm.at[pl.ds(pl.multiple_of(s * ROW_CHUNKS, ROW_CHUNKS), ROW_CHUNKS), :],
                buf.at[pl.ds(pl.multiple_of(a * ROW_CHUNKS, ROW_CHUNKS), ROW_CHUNKS), :],
                sem.at[0]).start()
        return carry

    lax.fori_loop(0, n // unroll, body, 0)
    pltpu.make_async_copy(y_hbm.at[pl.ds(0, n * ROW_CHUNKS), :], buf, sem.at[0]).wait()
    w = w_ref[...]
    g2 = g2_ref[0]
    for c in range(ROW_CHUNKS):
        acc = jnp.zeros((tc, LANES), F32)
        for k in range(TOP_K):
            acc = acc + w[:, k:k + 1] * buf[pl.ds(k * tc * ROW_CHUNKS + c, tc, stride=ROW_CHUNKS), :]
        sl = slice(c * LANES, (c + 1) * LANES)
        o_ref[:, sl] = base_ref[:, sl] + g2[:, sl] * acc


def _combine(slot_flat, y_all, wts, base, gate2, tc, rows_per_mod):
    m = base.shape[0]
    r = gate2.shape[1]
    grid_spec = pltpu.PrefetchScalarGridSpec(
        num_scalar_prefetch=1,
        grid=(m // tc,),
        in_specs=[
            pl.BlockSpec(memory_space=pl.ANY),
            pl.BlockSpec((tc, TOP_K), lambda i, s: (i, 0)),
            pl.BlockSpec((tc, D_MODEL), lambda i, s: (i, 0)),
            pl.BlockSpec((1, r, D_MODEL), lambda i, s: (i // rows_per_mod, 0, 0)),
        ],
        out_specs=pl.BlockSpec((tc, D_MODEL), lambda i, s: (i, 0)),
        scratch_shapes=[
            pltpu.VMEM((tc * TOP_K * ROW_CHUNKS, LANES), F32),
            pltpu.SemaphoreType.DMA((1,)),
        ],
    )
    return pl.pallas_call(
        _combine_kernel,
        grid_spec=grid_spec,
        out_shape=jax.ShapeDtypeStruct((m, D_MODEL), F32),
        compiler_params=_cparams(("arbitrary",)),
        name="combine",
    )(slot_flat, y_all, wts, base, gate2)


def _tile_major_slots(slots, tc):
    k, m = slots.shape
    return slots.reshape(k, m // tc, tc).transpose(1, 0, 2).reshape(-1)


def _moe_routed(h2_all, logits_t, b_router, w_gate, w_up, w_down):
    n_tok = logits_t.shape[1]
    mp = -(-n_tok // ROUTE_TILE) * ROUTE_TILE
    logits_t = jnp.pad(logits_t, ((0, 0), (0, mp - n_tok)))
    eidx, wts, rank, counts = _route(logits_t, b_router, n_tok)
    eidx, wts, rank, counts = eidx[:, :n_tok], wts[:, :n_tok], rank[:, :n_tok], counts[:, 0]

    tm = EXPERT_TILE
    n_tiles = n_tok * TOP_K // tm + N_EXPERTS
    tiles_e = (counts + tm - 1) // tm
    tile_end = jnp.cumsum(tiles_e)
    tile_begin = tile_end - tiles_e
    dense_begin = jnp.cumsum(counts) - counts
    slots = (tile_begin * tm)[eidx] + rank
    keys = (eidx * n_tok + jnp.arange(n_tok, dtype=I32)[None, :]).reshape(-1)
    sorted_tok = jnp.concatenate([jnp.sort(keys) % n_tok, jnp.zeros((tm,), I32)])
    tj = jnp.arange(n_tiles, dtype=I32)
    total_tiles = tile_end[-1]
    used = tj < total_tiles
    tj_used = jnp.minimum(tj, total_tiles - 1)
    te = jnp.sum((tile_end[None, :] <= tj_used[:, None]).astype(I32), axis=1)
    local = tj - tile_begin[te]
    ts = jnp.where(used, dense_begin[te] + local * tm, 0).astype(I32)
    tf = used.astype(I32) * TILE_USED + jnp.logical_and(used, local == 0).astype(I32) * TILE_FIRST
    y_all = _experts(te, ts, tf, sorted_tok, h2_all, w_gate, w_up, w_down)
    return y_all, slots, wts.T


def kernel(x_prompt, x_sample, cache_k_win, cache_v_win, state_pool, c_prompt, c_sample, w_ada, b_ada, g_mix, w_in, q_norm, k_norm, w_pool, pool_scale, w_out, g_ffn, w_router, b_router, w_gate, w_up, w_down, ws_gate, ws_up, ws_down):
    depth = w_ada.shape[0]
    assert depth == 1
    l = 0
    nb, seq, _ = x_prompt.shape
    bd, dec_seq, _ = x_sample.shape
    assert dec_seq == 1 and seq % (TOKEN_TILE * 16) == 0
    n_prompt = nb * seq
    n_tok = n_prompt + bd
    n_keep = min(MAX_WINDOW, seq)
    tiles_per_batch = seq // TOKEN_TILE

    n_mod_rows = -(-(nb + bd) // 8) * 8
    c_all = jnp.concatenate([c_prompt, c_sample, jnp.zeros((n_mod_rows - nb - bd, D_MODEL), F32)], axis=0)
    mods = _ada(c_all, w_ada[l], b_ada[l])
    mods_p = [mods[:nb, j * D_MODEL:(j + 1) * D_MODEL].reshape(nb, 1, D_MODEL) for j in range(6)]
    mods_s = [mods[nb:nb + bd, j * D_MODEL:(j + 1) * D_MODEL].reshape(1, bd, D_MODEL) for j in range(6)]

    g_mix_l = g_mix[l].reshape(1, D_MODEL)
    g_ffn_l = g_ffn[l].reshape(1, D_MODEL)
    qn = q_norm[l].reshape(1, HEAD_DIM)
    kn = k_norm[l].reshape(1, HEAD_DIM)
    w_in_bf = w_in[l].astype(BF16)
    w_out_bf = w_out[l].astype(BF16)
    w_pool_bf = w_pool[l].astype(BF16)
    ps = pool_scale[l].reshape(1, D_POOL)
    w_router_t = w_router[l].T
    wsg_bf, wsu_bf, wsd_bf = ws_gate[l].astype(BF16), ws_up[l].astype(BF16), ws_down[l].astype(BF16)

    rope_p = _rope_tables(jnp.arange(seq, dtype=I32))
    rope_s = _rope_tables(jnp.full((bd,), PAST_LEN, I32))
    xp2 = x_prompt.reshape(n_prompt, D_MODEL)
    xs2 = x_sample.reshape(bd, D_MODEL)
    qp, kp, vp, up = _in_proj(xp2, mods_p[0], mods_p[1], g_mix_l, w_in_bf, qn, kn, rope_p,
                              TOKEN_TILE, tiles_per_batch, tiles_per_batch)
    qs, ks, vs, us = _in_proj(xs2, mods_s[0], mods_s[1], g_mix_l, w_in_bf, qn, kn, rope_s, bd, 1, 1)

    to_seq = lambda t: t.reshape(nb, seq, -1)
    attn_p = _attn_prompt(to_seq(qp), to_seq(kp), to_seq(vp)).reshape(n_prompt, D_ATTN)
    pool_p = _pool_prompt(to_seq(up), w_pool_bf, ps).reshape(n_prompt, D_POOL)
    heads = lambda t: t.reshape(bd, N_HEADS, HEAD_DIM)
    ck, cv = cache_k_win[l], cache_v_win[l]
    attn_s = _attn_decode(heads(qs), heads(ks), heads(vs), ck, cv).reshape(bd, D_ATTN)
    pool_s = _pool_decode(us, jnp.swapaxes(state_pool[l], 0, 1), w_pool_bf, ps)
    k_win_s, v_win_s = _window_update(ck, cv, heads(ks), heads(vs))

    base_p, h2_all, lg_p = _out_proj(xp2, attn_p, pool_p, mods_p[2], mods_p[3], mods_p[4], mods_p[5],
                                     w_out_bf, g_ffn_l, w_router_t, wsg_bf, wsu_bf, wsd_bf,
                                     TOKEN_TILE, tiles_per_batch, n_tok, None, 0)
    base_s, h2_all, lg_s = _out_proj(xs2, attn_s, pool_s, mods_s[2], mods_s[3], mods_s[4], mods_s[5],
                                     w_out_bf, g_ffn_l, w_router_t, wsg_bf, wsu_bf, wsd_bf,
                                     bd, 1, n_tok, h2_all, n_prompt // bd)

    y_all, slots, wts_t = _moe_routed(h2_all, jnp.concatenate([lg_p, lg_s], axis=1), b_router[l],
                                      w_gate[l], w_up[l], w_down[l])

    yp = _combine(_tile_major_slots(slots[:, :n_prompt], COMBINE_TILE), y_all, wts_t[:n_prompt], base_p,
                  mods_p[5], COMBINE_TILE, seq // COMBINE_TILE)
    ys = _combine(_tile_major_slots(slots[:, n_prompt:n_tok], bd), y_all, wts_t[n_prompt:n_tok], base_s,
                  mods_s[5], bd, 1)

    y_prompt = yp.reshape(nb, seq, D_MODEL)
    y_sample = ys.reshape(bd, dec_seq, D_MODEL)
    win = lambda t: to_seq(t)[:, seq - n_keep:].reshape(1, nb, n_keep, N_HEADS, HEAD_DIM)
    pool_p_state = to_seq(up)[:, seq - POOL_STATE:][None]
    pool_s_state = jnp.concatenate([state_pool[l][:, 1:], us[:, None, :]], axis=1)[None]
    return (y_prompt, y_sample, win(kp), win(vp), pool_p_state,
            k_win_s[None], v_win_s[None], pool_s_state)
```

```python
import functools

import jax
import jax.numpy as jnp
from jax import lax
from jax.experimental import pallas as pl
from jax.experimental.pallas import tpu as pltpu

F32 = jnp.float32
BF16 = jnp.bfloat16
I32 = jnp.int32

D_MODEL = 2048
N_HEADS = 8
HEAD_DIM = 128
D_ATTN = N_HEADS * HEAD_DIM
D_POOL = D_MODEL - D_ATTN
D_IN = 3 * D_ATTN + D_POOL
ROPE_DIM = HEAD_DIM // 4
ROPE_HALF = ROPE_DIM // 2
ROPE_THETA = 500000.0
DILATED_GROUPS = ((128, 1), (512, 4), (2048, 16))
QBLOCK = 128
ATTN_SCALE = HEAD_DIM ** -0.5
POOL_WINDOWS = (2, 4, 8, 16)
POOL_GROUP_DIM = D_POOL // len(POOL_WINDOWS)
POOL_STATE = max(POOL_WINDOWS) - 1
POOL_HALO = 16
N_EXPERTS = 64
TOP_K = 8
N_EXPERT_GROUPS = 8
GROUP_SIZE = N_EXPERTS // N_EXPERT_GROUPS
TOPK_GROUPS = 4
D_EXPERT = 512
D_SHARED = 512
ROUTED_SCALE = 2.5
EPS = 1e-6
PAST_LEN = 16384
MAX_WINDOW = 2048

LANES = 128
ROW_CHUNKS = D_MODEL // LANES
NEG_BIG = -1e30

TOKEN_TILE = 256
EXPERT_TILE = 256
COMBINE_TILE = 128
ROUTE_TILE = 256
VMEM_LIMIT = 60 * 1024 * 1024


def _cparams(sem, vmem=VMEM_LIMIT):
    return pltpu.CompilerParams(dimension_semantics=sem, vmem_limit_bytes=vmem)


def _ada_kernel(c_ref, w_ref, b_ref, o_ref):
    c = c_ref[...]
    s = (c * jax.nn.sigmoid(c)).astype(BF16)
    o_ref[...] = jnp.dot(s, w_ref[...].astype(BF16), preferred_element_type=F32) + b_ref[...]


def _ada(c_all, w_ada, b_ada):
    rows = c_all.shape[0]
    n = w_ada.shape[1]
    tn = 1024
    return pl.pallas_call(
        _ada_kernel,
        grid=(n // tn,),
        in_specs=[
            pl.BlockSpec((rows, D_MODEL), lambda j: (0, 0)),
            pl.BlockSpec((D_MODEL, tn), lambda j: (0, j)),
            pl.BlockSpec((1, tn), lambda j: (0, j)),
        ],
        out_specs=pl.BlockSpec((rows, tn), lambda j: (0, j)),
        out_shape=jax.ShapeDtypeStruct((rows, n), F32),
        compiler_params=_cparams(("parallel",)),
        name="ada_modulation",
    )(c_all, w_ada, b_ada.reshape(1, n))


def _rms(x):
    return x * lax.rsqrt(jnp.mean(x * x, axis=-1, keepdims=True) + EPS)


def _in_kernel(x_ref, sh_ref, sc_ref, g_ref, w_ref, qn_ref, kn_ref, c_ref, s1_ref, s2_ref,
               q_ref, k_ref, v_ref, u_ref):
    x = x_ref[...]
    h = _rms(x) * g_ref[...] * (1.0 + sc_ref[0]) + sh_ref[0]
    hb = h.astype(BF16)
    cos = c_ref[...]
    s1 = s1_ref[...]
    s2 = s2_ref[...]

    def qk(sec, nrm, out_ref):
        z = jnp.dot(hb, w_ref[:, sec * D_ATTN:(sec + 1) * D_ATTN], preferred_element_type=F32)
        for hd in range(N_HEADS):
            sl = slice(hd * HEAD_DIM, (hd + 1) * HEAD_DIM)
            r = _rms(z[:, sl]) * nrm
            out_ref[:, sl] = (r * cos + pltpu.roll(r, HEAD_DIM - ROPE_HALF, 1) * s1
                              + pltpu.roll(r, ROPE_HALF, 1) * s2)

    qk(0, qn_ref[...], q_ref)
    qk(1, kn_ref[...], k_ref)
    v_ref[...] = jnp.dot(hb, w_ref[:, 2 * D_ATTN:3 * D_ATTN], preferred_element_type=F32)
    u_ref[...] = jnp.dot(hb, w_ref[:, 3 * D_ATTN:], preferred_element_type=F32)


def _in_proj(x2d, shift, scale, g_mix, w_in_bf, q_norm, k_norm, rope, tm, rows_per_mod, rope_tiles):
    m = x2d.shape[0]
    r = shift.shape[1]
    mod_spec = pl.BlockSpec((1, r, D_MODEL), lambda i: (i // rows_per_mod, 0, 0))
    rope_spec = pl.BlockSpec((tm, HEAD_DIM), lambda i: (i % rope_tiles, 0))
    out_spec = pl.BlockSpec((tm, D_ATTN), lambda i: (i, 0))
    out_sd = jax.ShapeDtypeStruct((m, D_ATTN), F32)
    return pl.pallas_call(
        _in_kernel,
        grid=(m // tm,),
        in_specs=[
            pl.BlockSpec((tm, D_MODEL), lambda i: (i, 0)),
            mod_spec, mod_spec,
            pl.BlockSpec((1, D_MODEL), lambda i: (0, 0)),
            pl.BlockSpec((D_MODEL, D_IN), lambda i: (0, 0)),
            pl.BlockSpec((1, HEAD_DIM), lambda i: (0, 0)),
            pl.BlockSpec((1, HEAD_DIM), lambda i: (0, 0)),
            rope_spec, rope_spec, rope_spec,
        ],
        out_specs=[out_spec] * 4,
        out_shape=[out_sd] * 4,
        compiler_params=_cparams(("parallel",)),
        name="in_proj",
    )(x2d, shift, scale, g_mix, w_in_bf, q_norm, k_norm, *rope)


def _rope_tables(pos):
    inv_freq = jnp.float32(ROPE_THETA) ** (-jnp.arange(ROPE_HALF, dtype=F32) / ROPE_HALF)
    ang = pos.astype(F32)[:, None] * inv_freq[None, :]
    cos, sin = jnp.cos(ang), jnp.sin(ang)
    t = pos.shape[0]
    rest = HEAD_DIM - ROPE_DIM
    c = jnp.concatenate([cos, cos, jnp.ones((t, rest), F32)], axis=1)
    s1 = jnp.concatenate([-sin, jnp.zeros((t, ROPE_HALF + rest), F32)], axis=1)
    s2 = jnp.concatenate([jnp.zeros((t, ROPE_HALF), F32), sin, jnp.zeros((t, rest), F32)], axis=1)
    return c, s1, s2


ATTN_INTERLEAVE = 4


def _attn_prompt_kernel(q_ref, k_ref, v_ref, o_ref, m_scr, l_scr, acc_scr):
    seq = q_ref.shape[1]
    m_scr[...] = jnp.full(m_scr.shape, NEG_BIG, F32)
    l_scr[...] = jnp.zeros(l_scr.shape, F32)
    acc_scr[...] = jnp.zeros(acc_scr.shape, F32)
    qi = lax.broadcasted_iota(I32, (QBLOCK, 2 * QBLOCK), 0)
    kj = lax.broadcasted_iota(I32, (QBLOCK, 2 * QBLOCK), 1)
    dist = qi + QBLOCK - kj
    band_mask = jnp.logical_and(dist >= 0, dist <= QBLOCK)
    in_cur = kj >= QBLOCK
    ones = jnp.ones((2 * QBLOCK, HEAD_DIM), BF16)
    dims = (((1,), (1,)), ((), ()))

    for window, dil in DILATED_GROUPS:
        assert window // dil == QBLOCK
        n_units = seq // QBLOCK
        assert n_units % ATTN_INTERLEAVE == 0 and (dil % ATTN_INTERLEAVE == 0 or dil == 1)

        def rows(start, dil=dil):
            return pl.ds(start, QBLOCK) if dil == 1 else pl.ds(start, QBLOCK, stride=dil)

        def body(it, carry, dil=dil, rows=rows):
            loaded = []
            for u in range(ATTN_INTERLEAVE):
                f = it * ATTN_INTERLEAVE + u
                blk, res = f // dil, f % dil
                start = blk * (QBLOCK * dil) + res
                prev = jnp.maximum(start - QBLOCK * dil, res)
                q = q_ref[0, rows(start), :].astype(BF16)
                k = jnp.concatenate([k_ref[0, rows(prev), :], k_ref[0, rows(start), :]], axis=0).astype(BF16)
                v = jnp.concatenate([v_ref[0, rows(prev), :], v_ref[0, rows(start), :]], axis=0).astype(BF16)
                state = (m_scr[rows(start), :], l_scr[rows(start), :], acc_scr[rows(start), :])
                loaded.append((start, blk, q, k, v, state))
            updated = []
            for start, blk, q, k, v, (m_old, l_old, a_old) in loaded:
                s = lax.dot_general(q, k, dims, preferred_element_type=F32) * ATTN_SCALE
                valid = jnp.logical_and(band_mask, jnp.logical_or(in_cur, blk > 0))
                s = jnp.where(valid, s, NEG_BIG)
                m_new = jnp.maximum(m_old, jnp.max(s, axis=1, keepdims=True))
                alpha = jnp.exp(m_old - m_new)
                p = jnp.exp(s - jnp.concatenate([m_new, m_new], axis=1)).astype(BF16)
                l_new = alpha * l_old + jnp.dot(p, ones, preferred_element_type=F32)
                a_new = alpha * a_old + jnp.dot(p, v, preferred_element_type=F32)
                updated.append((start, m_new, l_new, a_new))
            for start, m_new, l_new, a_new in updated:
                m_scr[rows(start), :] = m_new
                l_scr[rows(start), :] = l_new
                acc_scr[rows(start), :] = a_new
            return carry

        lax.fori_loop(0, n_units // ATTN_INTERLEAVE, body, 0)

    o_ref[0] = acc_scr[...] / l_scr[...]


def _attn_prompt(q, k, v):
    b, s, _ = q.shape
    spec = pl.BlockSpec((1, s, HEAD_DIM), lambda bi, hi: (bi, 0, hi))
    return pl.pallas_call(
        _attn_prompt_kernel,
        grid=(b, N_HEADS),
        in_specs=[spec, spec, spec],
        out_specs=spec,
        out_shape=jax.ShapeDtypeStruct((b, s, D_ATTN), F32),
        scratch_shapes=[pltpu.VMEM((s, HEAD_DIM), F32)] * 3,
        compiler_params=_cparams(("parallel", "parallel")),
        name="attn_prompt",
    )(q, k, v)


def _attn_decode_kernel(q_ref, kn_ref, vn_ref, k1_ref, k2_ref, k3_ref, v1_ref, v2_ref, v3_ref, o_ref):
    q = q_ref[0]
    k_new = kn_ref[0]
    v_new = vn_ref[0]
    n_grp = len(DILATED_GROUPS)

    def scores(kb):
        return jnp.sum(kb * q[None], axis=-1, keepdims=True) * ATTN_SCALE

    kcs = [k1_ref[0], k2_ref[0][:, 0], k3_ref[0][:, 0]]
    vcs = [v1_ref[0], v2_ref[0][:, 0], v3_ref[0][:, 0]]
    s_new = jnp.sum(k_new * q, axis=-1, keepdims=True) * ATTN_SCALE
    s_grp = [scores(kb) for kb in kcs]
    m = s_new
    for s in s_grp:
        m = jnp.maximum(m, jnp.max(s, axis=0))
    p_new = jnp.exp(s_new - m)
    den = n_grp * p_new
    num = n_grp * p_new * v_new
    for s, vb in zip(s_grp, vcs):
        p = jnp.exp(s - m[None])
        den = den + jnp.sum(p, axis=0)
        num = num + jnp.sum(p * vb, axis=0)
    o_ref[0] = num / den


def _attn_decode(q, k_new, v_new, cache_k, cache_v):
    bd, n_buf = cache_k.shape[0], cache_k.shape[1]
    assert n_buf == MAX_WINDOW

    def views(cache):
        out, specs = [], []
        for window, dil in DILATED_GROUPS:
            band = window // dil
            assert band == QBLOCK and n_buf % dil == 0 and n_buf // dil >= band
            if dil == 1:
                out.append(cache)
                specs.append(pl.BlockSpec((1, band, N_HEADS, HEAD_DIM),
                                          lambda b, nb=n_buf // band: (b, nb - 1, 0, 0)))
            else:
                out.append(cache.reshape(bd, n_buf // dil, dil, N_HEADS, HEAD_DIM))
                specs.append(pl.BlockSpec((1, band, 1, N_HEADS, HEAD_DIM),
                                          lambda b, nb=n_buf // dil // band: (b, nb - 1, 0, 0, 0)))
        return out, specs

    kv, kspecs = views(cache_k)
    vv, vspecs = views(cache_v)
    tok = pl.BlockSpec((1, N_HEADS, HEAD_DIM), lambda b: (b, 0, 0))
    return pl.pallas_call(
        _attn_decode_kernel,
        grid=(bd,),
        in_specs=[tok, tok, tok] + kspecs + vspecs,
        out_specs=tok,
        out_shape=jax.ShapeDtypeStruct((bd, N_HEADS, HEAD_DIM), F32),
        compiler_params=_cparams(("parallel",)),
        name="attn_decode",
    )(q, k_new, v_new, *kv, *vv)


WINDOW_CHUNK = 1024


def _window_update_kernel(ck_ref, cv_ref, nk_ref, nv_ref, kn_ref, vn_ref, ok_ref, ov_ref):
    rows = ck_ref.shape[1]
    last = pl.program_id(1) == pl.num_programs(1) - 1
    for cur, nxt, new, out in ((ck_ref, nk_ref, kn_ref, ok_ref), (cv_ref, nv_ref, vn_ref, ov_ref)):
        out[0, pl.ds(0, rows - 1)] = cur[0, pl.ds(1, rows - 1)]
        out[0, rows - 1] = jnp.where(last, new[0], nxt[0, 0])


def _window_update(cache_k, cache_v, k_new, v_new):
    bd, n_buf = cache_k.shape[0], cache_k.shape[1]
    rows = WINDOW_CHUNK
    assert n_buf % rows == 0
    tail = cache_k.shape[2:]
    chunk = pl.BlockSpec((1, rows) + tail, lambda b, c: (b, c, 0, 0))
    nxt = pl.BlockSpec((1, 1) + tail, lambda b, c: (b, jnp.minimum((c + 1) * rows, n_buf - 1), 0, 0))
    new = pl.BlockSpec((1,) + tail, lambda b, c: (b, 0, 0))
    sd = jax.ShapeDtypeStruct(cache_k.shape, cache_k.dtype)
    return pl.pallas_call(
        _window_update_kernel,
        grid=(bd, n_buf // rows),
        in_specs=[chunk, chunk, nxt, nxt, new, new],
        out_specs=[chunk, chunk],
        out_shape=[sd, sd],
        compiler_params=_cparams(("parallel", "parallel")),
        name="window_update",
    )(cache_k, cache_v, cache_k, cache_v, k_new, v_new)


def _pool_project(d_groups, wp_ref, ps_ref, out_ref):
    for g, d in enumerate(d_groups):
        sl = slice(g * POOL_GROUP_DIM, (g + 1) * POOL_GROUP_DIM)
        out_ref[:, sl] = jnp.dot(d.astype(BF16), wp_ref[g], preferred_element_type=F32) * ps_ref[:, sl]


def _pool_prompt_kernel(u_ref, prev_ref, wp_ref, ps_ref, o_ref):
    i = pl.program_id(1)
    tp = u_ref.shape[1]
    u = u_ref[0]
    prev = jnp.where(i > 0, prev_ref[0], 0.0)
    ext = jnp.concatenate([prev, u], axis=0)
    pos = i * tp + lax.broadcasted_iota(I32, (tp, 1), 0)
    d_groups = []
    for g, w in enumerate(POOL_WINDOWS):
        sl = slice(g * POOL_GROUP_DIM, (g + 1) * POOL_GROUP_DIM)
        a = ext[:, sl]
        span = 1
        while span < w:
            a = a + pltpu.roll(a, span, 0)
            span *= 2
        win = a[POOL_HALO:, :]
        cnt = jnp.minimum(w, pos + 1).astype(F32)
        d_groups.append(win / cnt - u[:, sl])
    _pool_project(d_groups, wp_ref, ps_ref, o_ref.at[0])


def _pool_prompt(u, w_pool_bf, pool_scale):
    b, s, _ = u.shape
    tp = 512
    halo_blocks = tp // POOL_HALO
    return pl.pallas_call(
        _pool_prompt_kernel,
        grid=(b, s // tp),
        in_specs=[
            pl.BlockSpec((1, tp, D_POOL), lambda bi, i: (bi, i, 0)),
            pl.BlockSpec((1, POOL_HALO, D_POOL), lambda bi, i: (bi, jnp.maximum(i * halo_blocks - 1, 0), 0)),
            pl.BlockSpec(w_pool_bf.shape, lambda bi, i: (0, 0, 0)),
            pl.BlockSpec((1, D_POOL), lambda bi, i: (0, 0)),
        ],
        out_specs=pl.BlockSpec((1, tp, D_POOL), lambda bi, i: (bi, i, 0)),
        out_shape=jax.ShapeDtypeStruct((b, s, D_POOL), F32),
        compiler_params=_cparams(("parallel", "parallel")),
        name="pool_prompt",
    )(u, u, w_pool_bf, pool_scale)


def _pool_decode_kernel(u_ref, st_ref, wp_ref, ps_ref, o_ref):
    u = u_ref[...]
    d_groups = []
    for g, w in enumerate(POOL_WINDOWS):
        sl = slice(g * POOL_GROUP_DIM, (g + 1) * POOL_GROUP_DIM)
        win = u[:, sl]
        for j in range(1, w):
            win = win + st_ref[POOL_STATE - j][:, sl]
        d_groups.append(win / float(w) - u[:, sl])
    _pool_project(d_groups, wp_ref, ps_ref, o_ref)


def _pool_decode(u, state_t, w_pool_bf, pool_scale):
    assert PAST_LEN + 1 >= max(POOL_WINDOWS)
    return pl.pallas_call(
        _pool_decode_kernel,
        out_shape=jax.ShapeDtypeStruct(u.shape, F32),
        compiler_params=_cparams(None),
        name="pool_decode",
    )(u, state_t, w_pool_bf, pool_scale)


def _split_bf16(x):
    hi = x.astype(BF16)
    lo = (x - hi.astype(F32)).astype(BF16)
    return hi, lo


def _out_kernel(x_ref, a_ref, p_ref, g1_ref, sh_ref, sc_ref, g2_ref, wo_ref, gf_ref, wr_ref,
                wsg_ref, wsu_ref, wsd_ref, base_ref, h2_ref, lg_ref):
    tm = x_ref.shape[0]
    mix = (jnp.dot(a_ref[...].astype(BF16), wo_ref[:D_ATTN, :], preferred_element_type=F32)
           + jnp.dot(p_ref[...].astype(BF16), wo_ref[D_ATTN:, :], preferred_element_type=F32))
    x1 = x_ref[...] + g1_ref[0] * mix
    h2 = _rms(x1) * gf_ref[...] * (1.0 + sc_ref[0]) + sh_ref[0]
    for c in range(ROW_CHUNKS):
        h2_ref[pl.ds(c, tm, stride=ROW_CHUNKS), :] = h2[:, c * LANES:(c + 1) * LANES]
    h_hi, h_lo = _split_bf16(h2)
    w_hi, w_lo = _split_bf16(wr_ref[...])
    dims = (((1,), (1,)), ((), ()))
    lg_ref[...] = (lax.dot_general(w_hi, h_hi, dims, preferred_element_type=F32)
                   + lax.dot_general(w_hi, h_lo, dims, preferred_element_type=F32)
                   + lax.dot_general(w_lo, h_hi, dims, preferred_element_type=F32))
    sg = jnp.dot(h_hi, wsg_ref[...], preferred_element_type=F32)
    su = jnp.dot(h_hi, wsu_ref[...], preferred_element_type=F32)
    hs = (sg * jax.nn.sigmoid(sg) * su).astype(BF16)
    base_ref[...] = x1 + g2_ref[0] * jnp.dot(hs, wsd_ref[...], preferred_element_type=F32)


def _out_proj(x2d, attn, pool, gate1, shift2, scale2, gate2, w_out_bf, g_ffn, w_router_t,
              ws_gate_bf, ws_up_bf, ws_down_bf, tm, rows_per_mod, h2_tokens, h2_all, h2_block0):
    m = x2d.shape[0]
    r = gate1.shape[1]
    n_main = m // tm
    h2_rows = h2_tokens * ROW_CHUNKS
    n_fill = 0 if h2_all is not None else pl.cdiv(h2_rows - m * ROW_CHUNKS, tm * ROW_CHUNKS)
    last = n_main - 1
    row = lambda i: jnp.minimum(i, last)
    mod_spec = pl.BlockSpec((1, r, D_MODEL), lambda i: (row(i) // rows_per_mod, 0, 0))
    const = lambda shape: pl.BlockSpec(shape, lambda i: (0,) * len(shape))
    in_specs = [
        pl.BlockSpec((tm, D_MODEL), lambda i: (row(i), 0)),
        pl.BlockSpec((tm, D_ATTN), lambda i: (row(i), 0)),
        pl.BlockSpec((tm, D_POOL), lambda i: (row(i), 0)),
        mod_spec, mod_spec, mod_spec, mod_spec,
        const((D_MODEL, D_MODEL)),
        const((1, D_MODEL)),
        const((N_EXPERTS, D_MODEL)),
        const((D_MODEL, D_SHARED)), const((D_MODEL, D_SHARED)), const((D_SHARED, D_MODEL)),
    ]
    args = [x2d, attn, pool, gate1, shift2, scale2, gate2, w_out_bf, g_ffn, w_router_t,
            ws_gate_bf, ws_up_bf, ws_down_bf]
    aliases = {}
    n_in = len(args)
    if h2_all is not None:
        in_specs.append(pl.BlockSpec(memory_space=pl.ANY))
        args.append(h2_all)
        aliases = {n_in: 1}

    def kernel(*refs):
        refs = refs[:n_in] + refs[len(args):]
        if n_fill == 0:
            _out_kernel(*refs)
            return
        step = pl.program_id(0)
        pl.when(step < n_main)(lambda: _out_kernel(*refs))

        @pl.when(step >= n_main)
        def _():
            h2_ref = refs[n_in + 1]
            h2_ref[...] = jnp.zeros(h2_ref.shape, F32)

    return pl.pallas_call(
        kernel,
        grid=(n_main + n_fill,),
        in_specs=in_specs,
        out_specs=[
            pl.BlockSpec((tm, D_MODEL), lambda i: (row(i), 0)),
            pl.BlockSpec((tm * ROW_CHUNKS, LANES), lambda i: (i + h2_block0, 0)),
            pl.BlockSpec((N_EXPERTS, tm), lambda i: (0, row(i))),
        ],
        out_shape=[
            jax.ShapeDtypeStruct((m, D_MODEL), F32),
            jax.ShapeDtypeStruct((h2_rows, LANES), F32),
            jax.ShapeDtypeStruct((N_EXPERTS, m), F32),
        ],
        input_output_aliases=aliases,
        compiler_params=_cparams(("arbitrary",)),
        name="out_proj",
    )(*args)


def _first_index_of_max(v, iota, size, axis):
    m = jnp.max(v, axis=axis, keepdims=True)
    idx = jnp.min(jnp.where(v == m, iota, size), axis=axis, keepdims=True)
    return m, idx


def _route_kernel(n_valid, lg_ref, b_ref, eidx_ref, wts_ref, rank_ref, cnt_ref, carry):
    i = pl.program_id(0)
    tn = lg_ref.shape[1]

    @pl.when(i == 0)
    def _():
        carry[...] = jnp.zeros(carry.shape, F32)

    scores = jax.nn.sigmoid(lg_ref[...])
    biased = scores + b_ref[...]
    grp = biased.reshape(N_EXPERT_GROUPS, GROUP_SIZE, tn)
    io_g = lax.broadcasted_iota(I32, grp.shape, 1)
    m1, i1 = _first_index_of_max(grp, io_g, GROUP_SIZE, 1)
    m2 = jnp.max(jnp.where(io_g == i1, -jnp.inf, grp), axis=1, keepdims=True)
    gscore = (m1 + m2)[:, 0, :]
    io_n = lax.broadcasted_iota(I32, gscore.shape, 0)
    gsel = jnp.zeros(gscore.shape, jnp.bool_)
    for _ in range(TOPK_GROUPS):
        _, gi = _first_index_of_max(gscore, io_n, N_EXPERT_GROUPS, 0)
        hit = io_n == gi
        gsel = jnp.logical_or(gsel, hit)
        gscore = jnp.where(hit, -jnp.inf, gscore)
    emask = jnp.broadcast_to(gsel[:, None, :], grp.shape).reshape(N_EXPERTS, tn)
    cand = jnp.where(emask, biased, -jnp.inf)
    io_e = lax.broadcasted_iota(I32, cand.shape, 0)
    picked = jnp.zeros(cand.shape, jnp.bool_)
    eidx, sel = [], []
    for _ in range(TOP_K):
        _, ei = _first_index_of_max(cand, io_e, N_EXPERTS, 0)
        hit = io_e == ei
        eidx.append(ei)
        sel.append(jnp.sum(jnp.where(hit, scores, 0.0), axis=0, keepdims=True))
        picked = jnp.logical_or(picked, hit)
        cand = jnp.where(hit, -jnp.inf, cand)
    sel = jnp.concatenate(sel, axis=0)
    eidx = jnp.concatenate(eidx, axis=0)
    wts_ref[...] = sel / jnp.sum(sel, axis=0, keepdims=True) * ROUTED_SCALE
    eidx_ref[...] = eidx

    tok = i * tn + lax.broadcasted_iota(I32, cand.shape, 1)
    mask = jnp.logical_and(picked, tok < n_valid).astype(F32)
    tri = (lax.broadcasted_iota(I32, (tn, tn), 0) <= lax.broadcasted_iota(I32, (tn, tn), 1)).astype(BF16)
    pos = jnp.dot(mask.astype(BF16), tri, preferred_element_type=F32) + carry[:, :1]
    ranks = [jnp.sum(jnp.where(io_e == eidx[k:k + 1], pos, 0.0), axis=0, keepdims=True) for k in range(TOP_K)]
    rank_ref[...] = jnp.concatenate(ranks, axis=0).astype(I32) - 1
    total = carry[...] + jnp.sum(mask, axis=1, keepdims=True)
    carry[...] = total
    cnt_ref[...] = total.astype(I32)


def _route(logits_t, b_router, n_valid):
    mp = logits_t.shape[1]
    tn = ROUTE_TILE
    tok_spec = pl.BlockSpec((TOP_K, tn), lambda i: (0, i))
    return pl.pallas_call(
        functools.partial(_route_kernel, n_valid),
        grid=(mp // tn,),
        in_specs=[
            pl.BlockSpec((N_EXPERTS, tn), lambda i: (0, i)),
            pl.BlockSpec((N_EXPERTS, 1), lambda i: (0, 0)),
        ],
        out_specs=[tok_spec, tok_spec, tok_spec, pl.BlockSpec((N_EXPERTS, LANES), lambda i: (0, 0))],
        out_shape=[
            jax.ShapeDtypeStruct((TOP_K, mp), I32),
            jax.ShapeDtypeStruct((TOP_K, mp), F32),
            jax.ShapeDtypeStruct((TOP_K, mp), I32),
            jax.ShapeDtypeStruct((N_EXPERTS, LANES), I32),
        ],
        scratch_shapes=[pltpu.VMEM((N_EXPERTS, LANES), F32)],
        compiler_params=_cparams(("arbitrary",)),
        name="route",
    )(logits_t, b_router.reshape(N_EXPERTS, 1))


TILE_USED = 1
TILE_FIRST = 2


def _experts_kernel(te_ref, ts_ref, tf_ref, tok_ref, h_hbm, wg_ref, wu_ref, wd_ref, y_ref,
                    xbuf, sem, wg_bf, wu_bf, wd_bf):
    j = pl.program_id(0)
    n_steps = pl.num_programs(0)
    tm = EXPERT_TILE
    slot = j % 2
    unroll = 8

    def gather_tile(tile, dst_slot):
        start = ts_ref[tile]

        def body(g, carry):
            for r in range(unroll):
                i = g * unroll + r
                tok = tok_ref[start + i]
                pltpu.make_async_copy(
                    h_hbm.at[pl.ds(pl.multiple_of(tok * ROW_CHUNKS, ROW_CHUNKS), ROW_CHUNKS), :],
                    xbuf.at[dst_slot, pl.ds(pl.multiple_of(i * ROW_CHUNKS, ROW_CHUNKS), ROW_CHUNKS), :],
                    sem.at[dst_slot]).start()
            return carry

        lax.fori_loop(0, tm // unroll, body, 0)

    used = (tf_ref[j] & TILE_USED) != 0

    @pl.when(jnp.logical_and(j == 0, used))
    def _():
        gather_tile(0, 0)

    nxt = jnp.minimum(j + 1, n_steps - 1)

    @pl.when(jnp.logical_and(j + 1 < n_steps, (tf_ref[nxt] & TILE_USED) != 0))
    def _():
        gather_tile(j + 1, 1 - slot)

    @pl.when((tf_ref[j] & TILE_FIRST) != 0)
    def _():
        wg_bf[...] = wg_ref[0].astype(BF16)
        wu_bf[...] = wu_ref[0].astype(BF16)
        wd_bf[...] = wd_ref[0].astype(BF16)

    @pl.when(used)
    def _():
        pltpu.make_async_copy(h_hbm.at[pl.ds(0, tm * ROW_CHUNKS), :], xbuf.at[slot], sem.at[slot]).wait()
        x = jnp.concatenate(
            [xbuf[slot, pl.ds(c, tm, stride=ROW_CHUNKS), :] for c in range(ROW_CHUNKS)], axis=1).astype(BF16)
        hg = jnp.dot(x, wg_bf[...], preferred_element_type=F32)
        hu = jnp.dot(x, wu_bf[...], preferred_element_type=F32)
        h = (hg * jax.nn.sigmoid(hg) * hu).astype(BF16)
        y = jnp.dot(h, wd_bf[...], preferred_element_type=F32)
        for c in range(ROW_CHUNKS):
            y_ref[pl.ds(c, tm, stride=ROW_CHUNKS), :] = y[:, c * LANES:(c + 1) * LANES]

    @pl.when(jnp.logical_not(used))
    def _():
        y_ref[...] = jnp.zeros(y_ref.shape, F32)


def _experts(tile_expert, tile_start, tile_flags, sorted_tok, h2_all, w_gate, w_up, w_down):
    n_tiles = tile_expert.shape[0]
    tm = EXPERT_TILE
    grid_spec = pltpu.PrefetchScalarGridSpec(
        num_scalar_prefetch=4,
        grid=(n_tiles,),
        in_specs=[
            pl.BlockSpec(memory_space=pl.ANY),
            pl.BlockSpec((1, D_MODEL, D_EXPERT), lambda j, te, ts, tf, tok: (te[j], 0, 0)),
            pl.BlockSpec((1, D_MODEL, D_EXPERT), lambda j, te, ts, tf, tok: (te[j], 0, 0)),
            pl.BlockSpec((1, D_EXPERT, D_MODEL), lambda j, te, ts, tf, tok: (te[j], 0, 0)),
        ],
        out_specs=pl.BlockSpec((tm * ROW_CHUNKS, LANES), lambda j, te, ts, tf, tok: (j, 0)),
        scratch_shapes=[
            pltpu.VMEM((2, tm * ROW_CHUNKS, LANES), F32),
            pltpu.SemaphoreType.DMA((2,)),
            pltpu.VMEM((D_MODEL, D_EXPERT), BF16),
            pltpu.VMEM((D_MODEL, D_EXPERT), BF16),
            pltpu.VMEM((D_EXPERT, D_MODEL), BF16),
        ],
    )
    return pl.pallas_call(
        _experts_kernel,
        grid_spec=grid_spec,
        out_shape=jax.ShapeDtypeStruct((n_tiles * tm * ROW_CHUNKS, LANES), F32),
        compiler_params=_cparams(("arbitrary",)),
        name="experts",
    )(tile_expert, tile_start, tile_flags, sorted_tok, h2_all, w_gate, w_up, w_down)


def _combine_kernel(slot_ref, y_hbm, w_ref, base_ref, g2_ref, o_ref, buf, sem):
    i = pl.program_id(0)
    n_steps = pl.num_programs(0)
    tc = base_ref.shape[0]
    n = tc * TOP_K
    unroll = 8
    slot = i % 2

    def gather_tile(tile, dst):
        def body(g, carry):
            for r in range(unroll):
                a = g * unroll + r
                s = slot_ref[tile * n + a]
                pltpu.make_async_copy(
                    y_hbm.at[pl.ds(pl.multiple_of(s * ROW_CHUNKS, ROW_CHUNKS), ROW_CHUNKS), :],
                    buf.at[dst, pl.ds(pl.multiple_of(a * ROW_CHUNKS, ROW_CHUNKS), ROW_CHUNKS), :],
                    sem.at[dst]).start()
            return carry

        lax.fori_loop(0, n // unroll, body, 0)

    @pl.when(i == 0)
    def _():
        gather_tile(0, 0)

    @pl.when(i + 1 < n_steps)
    def _():
        gather_tile(i + 1, 1 - slot)

    pltpu.make_async_copy(y_hbm.at[pl.ds(0, n * ROW_CHUNKS), :], buf.at[slot], sem.at[slot]).wait()
    w = w_ref[...]
    g2 = g2_ref[0]
    for c in range(ROW_CHUNKS):
        acc = jnp.zeros((tc, LANES), F32)
        for k in range(TOP_K):
            acc = acc + w[:, k:k + 1] * buf[slot, pl.ds(k * tc * ROW_CHUNKS + c, tc, stride=ROW_CHUNKS), :]
        sl = slice(c * LANES, (c + 1) * LANES)
        o_ref[:, sl] = base_ref[:, sl] + g2[:, sl] * acc


def _combine(slot_flat, y_all, wts, base, gate2, tc, rows_per_mod):
    m = base.shape[0]
    r = gate2.shape[1]
    grid_spec = pltpu.PrefetchScalarGridSpec(
        num_scalar_prefetch=1,
        grid=(m // tc,),
        in_specs=[
            pl.BlockSpec(memory_space=pl.ANY),
            pl.BlockSpec((tc, TOP_K), lambda i, s: (i, 0)),
            pl.BlockSpec((tc, D_MODEL), lambda i, s: (i, 0)),
            pl.BlockSpec((1, r, D_MODEL), lambda i, s: (i // rows_per_mod, 0, 0)),
        ],
        out_specs=pl.BlockSpec((tc, D_MODEL), lambda i, s: (i, 0)),
        scratch_shapes=[
            pltpu.VMEM((2, tc * TOP_K * ROW_CHUNKS, LANES), F32),
            pltpu.SemaphoreType.DMA((2,)),
        ],
    )
    return pl.pallas_call(
        _combine_kernel,
        grid_spec=grid_spec,
        out_shape=jax.ShapeDtypeStruct((m, D_MODEL), F32),
        compiler_params=_cparams(("arbitrary",)),
        name="combine",
    )(slot_flat, y_all, wts, base, gate2)


def _tile_major_slots(slots, tc):
    k, m = slots.shape
    return slots.reshape(k, m // tc, tc).transpose(1, 0, 2).reshape(-1)


def _moe_routed(h2_all, logits_t, b_router, w_gate, w_up, w_down):
    n_tok = logits_t.shape[1]
    mp = -(-n_tok // ROUTE_TILE) * ROUTE_TILE
    logits_t = jnp.pad(logits_t, ((0, 0), (0, mp - n_tok)))
    eidx, wts, rank, counts = _route(logits_t, b_router, n_tok)
    eidx, wts, rank, counts = eidx[:, :n_tok], wts[:, :n_tok], rank[:, :n_tok], counts[:, 0]

    tm = EXPERT_TILE
    n_tiles = n_tok * TOP_K // tm + N_EXPERTS
    tiles_e = (counts + tm - 1) // tm
    tile_end = jnp.cumsum(tiles_e)
    tile_begin = tile_end - tiles_e
    dense_begin = jnp.cumsum(counts) - counts
    e_ids = jnp.arange(N_EXPERTS, dtype=I32)
    pad_begin = jnp.sum(jnp.where(eidx[..., None] == e_ids, tile_begin * tm, 0), axis=-1)
    slots = pad_begin + rank
    keys = (eidx * n_tok + jnp.arange(n_tok, dtype=I32)[None, :]).reshape(-1)
    sorted_tok = jnp.concatenate([jnp.sort(keys) % n_tok, jnp.zeros((tm,), I32)])
    tj = jnp.arange(n_tiles, dtype=I32)
    total_tiles = tile_end[-1]
    used = tj < total_tiles
    tj_used = jnp.minimum(tj, total_tiles - 1)
    te = jnp.sum((tile_end[None, :] <= tj_used[:, None]).astype(I32), axis=1)
    local = tj - tile_begin[te]
    ts = jnp.where(used, dense_begin[te] + local * tm, 0).astype(I32)
    tf = used.astype(I32) * TILE_USED + jnp.logical_and(used, local == 0).astype(I32) * TILE_FIRST
    y_all = _experts(te, ts, tf, sorted_tok, h2_all, w_gate, w_up, w_down)
    return y_all, slots, wts.T


def kernel(x_prompt, x_sample, cache_k_win, cache_v_win, state_pool, c_prompt, c_sample, w_ada, b_ada, g_mix, w_in, q_norm, k_norm, w_pool, pool_scale, w_out, g_ffn, w_router, b_router, w_gate, w_up, w_down, ws_gate, ws_up, ws_down):
    depth = w_ada.shape[0]
    assert depth == 1
    l = 0
    nb, seq, _ = x_prompt.shape
    bd, dec_seq, _ = x_sample.shape
    assert dec_seq == 1 and seq % (TOKEN_TILE * 16) == 0
    n_prompt = nb * seq
    n_tok = n_prompt + bd
    n_keep = min(MAX_WINDOW, seq)
    tiles_per_batch = seq // TOKEN_TILE

    n_mod_rows = -(-(nb + bd) // 8) * 8
    c_all = jnp.concatenate([c_prompt, c_sample, jnp.zeros((n_mod_rows - nb - bd, D_MODEL), F32)], axis=0)
    mods = _ada(c_all, w_ada[l], b_ada[l])
    mods_p = [mods[:nb, j * D_MODEL:(j + 1) * D_MODEL].reshape(nb, 1, D_MODEL) for j in range(6)]
    mods_s = [mods[nb:nb + bd, j * D_MODEL:(j + 1) * D_MODEL].reshape(1, bd, D_MODEL) for j in range(6)]

    g_mix_l = g_mix[l].reshape(1, D_MODEL)
    g_ffn_l = g_ffn[l].reshape(1, D_MODEL)
    qn = q_norm[l].reshape(1, HEAD_DIM)
    kn = k_norm[l].reshape(1, HEAD_DIM)
    w_in_bf = w_in[l].astype(BF16)
    w_out_bf = w_out[l].astype(BF16)
    w_pool_bf = w_pool[l].astype(BF16)
    ps = pool_scale[l].reshape(1, D_POOL)
    w_router_t = w_router[l].T
    wsg_bf, wsu_bf, wsd_bf = ws_gate[l].astype(BF16), ws_up[l].astype(BF16), ws_down[l].astype(BF16)

    rope_p = _rope_tables(jnp.arange(seq, dtype=I32))
    rope_s = _rope_tables(jnp.full((bd,), PAST_LEN, I32))
    xp2 = x_prompt.reshape(n_prompt, D_MODEL)
    xs2 = x_sample.reshape(bd, D_MODEL)
    qp, kp, vp, up = _in_proj(xp2, mods_p[0], mods_p[1], g_mix_l, w_in_bf, qn, kn, rope_p,
                              TOKEN_TILE, tiles_per_batch, tiles_per_batch)
    qs, ks, vs, us = _in_proj(xs2, mods_s[0], mods_s[1], g_mix_l, w_in_bf, qn, kn, rope_s, bd, 1, 1)

    to_seq = lambda t: t.reshape(nb, seq, -1)
    attn_p = _attn_prompt(to_seq(qp), to_seq(kp), to_seq(vp)).reshape(n_prompt, D_ATTN)
    pool_p = _pool_prompt(to_seq(up), w_pool_bf, ps).reshape(n_prompt, D_POOL)
    heads = lambda t: t.reshape(bd, N_HEADS, HEAD_DIM)
    ck, cv = cache_k_win[l], cache_v_win[l]
    attn_s = _attn_decode(heads(qs), heads(ks), heads(vs), ck, cv).reshape(bd, D_ATTN)
    pool_s = _pool_decode(us, jnp.swapaxes(state_pool[l], 0, 1), w_pool_bf, ps)
    k_win_s, v_win_s = _window_update(ck, cv, heads(ks), heads(vs))

    base_p, h2_all, lg_p = _out_proj(xp2, attn_p, pool_p, mods_p[2], mods_p[3], mods_p[4], mods_p[5],
                                     w_out_bf, g_ffn_l, w_router_t, wsg_bf, wsu_bf, wsd_bf,
                                     TOKEN_TILE, tiles_per_batch, n_tok, None, 0)
    base_s, h2_all, lg_s = _out_proj(xs2, attn_s, pool_s, mods_s[2], mods_s[3], mods_s[4], mods_s[5],
                                     w_out_bf, g_ffn_l, w_router_t, wsg_bf, wsu_bf, wsd_bf,
                                     bd, 1, n_tok, h2_all, n_prompt // bd)

    y_all, slots, wts_t = _moe_routed(h2_all, jnp.concatenate([lg_p, lg_s], axis=1), b_router[l],
                                      w_gate[l], w_up[l], w_down[l])

    yp = _combine(_tile_major_slots(slots[:, :n_prompt], COMBINE_TILE), y_all, wts_t[:n_prompt], base_p,
                  mods_p[5], COMBINE_TILE, seq // COMBINE_TILE)
    ys = _combine(_tile_major_slots(slots[:, n_prompt:n_tok], bd), y_all, wts_t[n_prompt:n_tok], base_s,
                  mods_s[5], bd, 1)

    y_prompt = yp.reshape(nb, seq, D_MODEL)
    y_sample = ys.reshape(bd, dec_seq, D_MODEL)
    win = lambda t: to_seq(t)[:, seq - n_keep:].reshape(1, nb, n_keep, N_HEADS, HEAD_DIM)
    pool_p_state = to_seq(up)[:, seq - POOL_STATE:][None]
    pool_s_state = jnp.concatenate([state_pool[l][:, 1:], us[:, None, :]], axis=1)[None]
    return (y_prompt, y_sample, win(kp), win(vp), pool_p_state,
            k_win_s[None], v_win_s[None], pool_s_state)
```

```python
import functools

import jax
import jax.numpy as jnp
from jax import lax
from jax.experimental import pallas as pl
from jax.experimental.pallas import tpu as pltpu

F32 = jnp.float32
BF16 = jnp.bfloat16
I32 = jnp.int32

D_MODEL = 2048
N_HEADS = 8
HEAD_DIM = 128
D_ATTN = N_HEADS * HEAD_DIM
D_POOL = D_MODEL - D_ATTN
D_IN = 3 * D_ATTN + D_POOL
ROPE_DIM = HEAD_DIM // 4
ROPE_HALF = ROPE_DIM // 2
ROPE_THETA = 500000.0
DILATED_GROUPS = ((128, 1), (512, 4), (2048, 16))
QBLOCK = 128
ATTN_SCALE = HEAD_DIM ** -0.5
POOL_WINDOWS = (2, 4, 8, 16)
POOL_GROUP_DIM = D_POOL // len(POOL_WINDOWS)
POOL_STATE = max(POOL_WINDOWS) - 1
POOL_HALO = 16
N_EXPERTS = 64
TOP_K = 8
N_EXPERT_GROUPS = 8
GROUP_SIZE = N_EXPERTS // N_EXPERT_GROUPS
TOPK_GROUPS = 4
D_EXPERT = 512
D_SHARED = 512
ROUTED_SCALE = 2.5
EPS = 1e-6
PAST_LEN = 16384
MAX_WINDOW = 2048

LANES = 128
ROW_CHUNKS = D_MODEL // LANES
NEG_BIG = -1e30
LOG2_E = 1.4426950408889634

TOKEN_TILE = 256
EXPERT_TILE = 256
COMBINE_TILE = 128
ROUTE_TILE = 256
VMEM_LIMIT = 60 * 1024 * 1024


def _cparams(sem, vmem=VMEM_LIMIT):
    return pltpu.CompilerParams(dimension_semantics=sem, vmem_limit_bytes=vmem)


def _ada_kernel(c_ref, w_ref, b_ref, o_ref):
    c = c_ref[...]
    s = (c * jax.nn.sigmoid(c)).astype(BF16)
    o_ref[...] = jnp.dot(s, w_ref[...].astype(BF16), preferred_element_type=F32) + b_ref[...]


def _ada(c_all, w_ada, b_ada):
    rows = c_all.shape[0]
    n = w_ada.shape[1]
    tn = 1024
    return pl.pallas_call(
        _ada_kernel,
        grid=(n // tn,),
        in_specs=[
            pl.BlockSpec((rows, D_MODEL), lambda j: (0, 0)),
            pl.BlockSpec((D_MODEL, tn), lambda j: (0, j)),
            pl.BlockSpec((1, tn), lambda j: (0, j)),
        ],
        out_specs=pl.BlockSpec((rows, tn), lambda j: (0, j)),
        out_shape=jax.ShapeDtypeStruct((rows, n), F32),
        compiler_params=_cparams(("parallel",)),
        name="ada_modulation",
    )(c_all, w_ada, b_ada.reshape(1, n))


def _rms(x):
    return x * lax.rsqrt(jnp.mean(x * x, axis=-1, keepdims=True) + EPS)


def _in_kernel(x_ref, sh_ref, sc_ref, g_ref, w_ref, qn_ref, kn_ref, c_ref, s1_ref, s2_ref,
               q_ref, k_ref, v_ref, u_ref, kw_ref, vw_ref):
    tm = x_ref.shape[0]
    x = x_ref[...]
    h = _rms(x) * g_ref[...] * (1.0 + sc_ref[0]) + sh_ref[0]
    hb = h.astype(BF16)
    cos = c_ref[...]
    s1 = s1_ref[...]
    s2 = s2_ref[...]

    def head_rows(win_ref, hd, val):
        win_ref[pl.ds(hd, tm, stride=N_HEADS), :] = val

    def qk(sec, nrm, out_ref, win_ref):
        z = jnp.dot(hb, w_ref[:, sec * D_ATTN:(sec + 1) * D_ATTN], preferred_element_type=F32)
        for hd in range(N_HEADS):
            sl = slice(hd * HEAD_DIM, (hd + 1) * HEAD_DIM)
            r = _rms(z[:, sl]) * nrm
            r = r * cos + pltpu.roll(r, HEAD_DIM - ROPE_HALF, 1) * s1 + pltpu.roll(r, ROPE_HALF, 1) * s2
            out_ref[:, sl] = r
            if win_ref is not None:
                head_rows(win_ref, hd, r)

    qk(0, qn_ref[...], q_ref, None)
    qk(1, kn_ref[...], k_ref, kw_ref)
    v = jnp.dot(hb, w_ref[:, 2 * D_ATTN:3 * D_ATTN], preferred_element_type=F32)
    v_ref[...] = v
    for hd in range(N_HEADS):
        head_rows(vw_ref, hd, v[:, hd * HEAD_DIM:(hd + 1) * HEAD_DIM])
    u_ref[...] = jnp.dot(hb, w_ref[:, 3 * D_ATTN:], preferred_element_type=F32)


def _in_proj(x2d, shift, scale, g_mix, w_in_bf, q_norm, k_norm, rope, tm, rows_per_mod, rope_tiles, keep_tiles):
    m = x2d.shape[0]
    r = shift.shape[1]
    n_groups = m // tm // rows_per_mod
    skip = rows_per_mod - keep_tiles
    mod_spec = pl.BlockSpec((1, r, D_MODEL), lambda i: (i // rows_per_mod, 0, 0))
    rope_spec = pl.BlockSpec((tm, HEAD_DIM), lambda i: (i % rope_tiles, 0))
    out_spec = pl.BlockSpec((tm, D_ATTN), lambda i: (i, 0))
    out_sd = jax.ShapeDtypeStruct((m, D_ATTN), F32)
    win_spec = pl.BlockSpec(
        (tm * N_HEADS, HEAD_DIM),
        lambda i: ((i // rows_per_mod) * keep_tiles + jnp.maximum(i % rows_per_mod - skip, 0), 0))
    win_sd = jax.ShapeDtypeStruct((n_groups * keep_tiles * tm * N_HEADS, HEAD_DIM), F32)
    return pl.pallas_call(
        _in_kernel,
        grid=(m // tm,),
        in_specs=[
            pl.BlockSpec((tm, D_MODEL), lambda i: (i, 0)),
            mod_spec, mod_spec,
            pl.BlockSpec((1, D_MODEL), lambda i: (0, 0)),
            pl.BlockSpec((D_MODEL, D_IN), lambda i: (0, 0)),
            pl.BlockSpec((1, HEAD_DIM), lambda i: (0, 0)),
            pl.BlockSpec((1, HEAD_DIM), lambda i: (0, 0)),
            rope_spec, rope_spec, rope_spec,
        ],
        out_specs=[out_spec] * 4 + [win_spec] * 2,
        out_shape=[out_sd] * 4 + [win_sd] * 2,
        compiler_params=_cparams(("arbitrary",)),
        name="in_proj",
    )(x2d, shift, scale, g_mix, w_in_bf, q_norm, k_norm, *rope)


def _rope_tables(pos):
    inv_freq = jnp.float32(ROPE_THETA) ** (-jnp.arange(ROPE_HALF, dtype=F32) / ROPE_HALF)
    ang = pos.astype(F32)[:, None] * inv_freq[None, :]
    cos, sin = jnp.cos(ang), jnp.sin(ang)
    t = pos.shape[0]
    rest = HEAD_DIM - ROPE_DIM
    c = jnp.concatenate([cos, cos, jnp.ones((t, rest), F32)], axis=1)
    s1 = jnp.concatenate([-sin, jnp.zeros((t, ROPE_HALF + rest), F32)], axis=1)
    s2 = jnp.concatenate([jnp.zeros((t, ROPE_HALF), F32), sin, jnp.zeros((t, rest), F32)], axis=1)
    return c, s1, s2


ATTN_INTERLEAVE = 4


def _attn_prompt_kernel(q_ref, k_ref, v_ref, o_ref, m_scr, l_scr, acc_scr):
    seq = q_ref.shape[1]
    m_scr[...] = jnp.full(m_scr.shape, NEG_BIG, F32)
    l_scr[...] = jnp.zeros(l_scr.shape, F32)
    acc_scr[...] = jnp.zeros(acc_scr.shape, F32)
    qi = lax.broadcasted_iota(I32, (QBLOCK, 2 * QBLOCK), 0)
    kj = lax.broadcasted_iota(I32, (QBLOCK, 2 * QBLOCK), 1)
    dist = qi + QBLOCK - kj
    band_mask = jnp.logical_and(dist >= 0, dist <= QBLOCK)
    bias_band = jnp.where(band_mask, 0.0, NEG_BIG)
    bias_first = jnp.where(jnp.logical_and(band_mask, kj >= QBLOCK), 0.0, NEG_BIG)
    ones = jnp.ones((2 * QBLOCK, HEAD_DIM), BF16)
    dims = (((1,), (1,)), ((), ()))
    q_scale = ATTN_SCALE * LOG2_E

    for window, dil in DILATED_GROUPS:
        assert window // dil == QBLOCK
        n_units = seq // QBLOCK
        assert n_units % ATTN_INTERLEAVE == 0 and (dil % ATTN_INTERLEAVE == 0 or dil == 1)

        def rows(start, dil=dil):
            return pl.ds(start, QBLOCK) if dil == 1 else pl.ds(start, QBLOCK, stride=dil)

        def body(it, carry, dil=dil, rows=rows):
            loaded = []
            for u in range(ATTN_INTERLEAVE):
                f = it * ATTN_INTERLEAVE + u
                blk, res = f // dil, f % dil
                start = blk * (QBLOCK * dil) + res
                prev = jnp.maximum(start - QBLOCK * dil, res)
                q = (q_ref[0, rows(start), :] * q_scale).astype(BF16)
                k = jnp.concatenate([k_ref[0, rows(prev), :], k_ref[0, rows(start), :]], axis=0).astype(BF16)
                v = jnp.concatenate([v_ref[0, rows(prev), :], v_ref[0, rows(start), :]], axis=0).astype(BF16)
                v_ones = jnp.concatenate([v, ones], axis=1)
                state = (m_scr[rows(start), :], l_scr[rows(start), :], acc_scr[rows(start), :])
                loaded.append((start, blk, q, k, v_ones, state))
            scores = [lax.dot_general(q, k, dims, preferred_element_type=F32)
                      + jnp.where(blk > 0, bias_band, bias_first)
                      for _, blk, q, k, _, _ in loaded]
            probs = []
            for s, (_, _, _, _, _, (m_old, _, _)) in zip(scores, loaded):
                m_new = jnp.maximum(m_old, jnp.max(s, axis=1, keepdims=True))
                p = jnp.exp2(s - jnp.concatenate([m_new, m_new], axis=1)).astype(BF16)
                probs.append((m_new, jnp.exp2(m_old - m_new), p))
            updated = []
            for (m_new, alpha, p), (start, _, _, _, v_ones, (_, l_old, a_old)) in zip(probs, loaded):
                pv = jnp.dot(p, v_ones, preferred_element_type=F32)
                updated.append((start, m_new, alpha * l_old + pv[:, HEAD_DIM:], alpha * a_old + pv[:, :HEAD_DIM]))
            for start, m_new, l_new, a_new in updated:
                m_scr[rows(start), :] = m_new
                l_scr[rows(start), :] = l_new
                acc_scr[rows(start), :] = a_new
            return carry

        lax.fori_loop(0, n_units // ATTN_INTERLEAVE, body, 0)

    o_ref[0] = acc_scr[...] / l_scr[...]


def _attn_prompt(q, k, v):
    b, s, _ = q.shape
    spec = pl.BlockSpec((1, s, HEAD_DIM), lambda bi, hi: (bi, 0, hi))
    return pl.pallas_call(
        _attn_prompt_kernel,
        grid=(b, N_HEADS),
        in_specs=[spec, spec, spec],
        out_specs=spec,
        out_shape=jax.ShapeDtypeStruct((b, s, D_ATTN), F32),
        scratch_shapes=[pltpu.VMEM((s, HEAD_DIM), F32)] * 3,
        compiler_params=_cparams(("parallel", "parallel")),
        name="attn_prompt",
    )(q, k, v)


def _attn_decode_kernel(q_ref, kn_ref, vn_ref, k1_ref, k2_ref, k3_ref, v1_ref, v2_ref, v3_ref, o_ref):
    q = q_ref[0]
    k_new = kn_ref[0]
    v_new = vn_ref[0]
    n_grp = len(DILATED_GROUPS)

    def scores(kb):
        return jnp.sum(kb * q[None], axis=-1, keepdims=True) * ATTN_SCALE

    kcs = [k1_ref[0], k2_ref[0][:, 0], k3_ref[0][:, 0]]
    vcs = [v1_ref[0], v2_ref[0][:, 0], v3_ref[0][:, 0]]
    s_new = jnp.sum(k_new * q, axis=-1, keepdims=True) * ATTN_SCALE
    s_grp = [scores(kb) for kb in kcs]
    m = s_new
    for s in s_grp:
        m = jnp.maximum(m, jnp.max(s, axis=0))
    p_new = jnp.exp(s_new - m)
    den = n_grp * p_new
    num = n_grp * p_new * v_new
    for s, vb in zip(s_grp, vcs):
        p = jnp.exp(s - m[None])
        den = den + jnp.sum(p, axis=0)
        num = num + jnp.sum(p * vb, axis=0)
    o_ref[0] = num / den


def _attn_decode(q, k_new, v_new, cache_k, cache_v):
    bd, n_buf = cache_k.shape[0], cache_k.shape[1]
    assert n_buf == MAX_WINDOW

    def views(cache):
        out, specs = [], []
        for window, dil in DILATED_GROUPS:
            band = window // dil
            assert band == QBLOCK and n_buf % dil == 0 and n_buf // dil >= band
            if dil == 1:
                out.append(cache)
                specs.append(pl.BlockSpec((1, band, N_HEADS, HEAD_DIM),
                                          lambda b, nb=n_buf // band: (b, nb - 1, 0, 0)))
            else:
                out.append(cache.reshape(bd, n_buf // dil, dil, N_HEADS, HEAD_DIM))
                specs.append(pl.BlockSpec((1, band, 1, N_HEADS, HEAD_DIM),
                                          lambda b, nb=n_buf // dil // band: (b, nb - 1, 0, 0, 0)))
        return out, specs

    kv, kspecs = views(cache_k)
    vv, vspecs = views(cache_v)
    tok = pl.BlockSpec((1, N_HEADS, HEAD_DIM), lambda b: (b, 0, 0))
    return pl.pallas_call(
        _attn_decode_kernel,
        grid=(bd,),
        in_specs=[tok, tok, tok] + kspecs + vspecs,
        out_specs=tok,
        out_shape=jax.ShapeDtypeStruct((bd, N_HEADS, HEAD_DIM), F32),
        compiler_params=_cparams(("parallel",)),
        name="attn_decode",
    )(q, k_new, v_new, *kv, *vv)


WINDOW_CHUNK = 1024


def _window_update_kernel(ck_ref, cv_ref, nk_ref, nv_ref, kn_ref, vn_ref, ok_ref, ov_ref):
    rows = ck_ref.shape[1]
    last = pl.program_id(1) == pl.num_programs(1) - 1
    for cur, nxt, new, out in ((ck_ref, nk_ref, kn_ref, ok_ref), (cv_ref, nv_ref, vn_ref, ov_ref)):
        out[0, pl.ds(0, rows - 1)] = cur[0, pl.ds(1, rows - 1)]
        out[0, rows - 1] = jnp.where(last, new[0], nxt[0, 0])


def _window_update(cache_k, cache_v, k_new, v_new):
    bd, n_buf = cache_k.shape[0], cache_k.shape[1]
    rows = WINDOW_CHUNK
    assert n_buf % rows == 0
    tail = cache_k.shape[2:]
    chunk = pl.BlockSpec((1, rows) + tail, lambda b, c: (b, c, 0, 0))
    nxt = pl.BlockSpec((1, 1) + tail, lambda b, c: (b, jnp.minimum((c + 1) * rows, n_buf - 1), 0, 0))
    new = pl.BlockSpec((1,) + tail, lambda b, c: (b, 0, 0))
    sd = jax.ShapeDtypeStruct(cache_k.shape, cache_k.dtype)
    return pl.pallas_call(
        _window_update_kernel,
        grid=(bd, n_buf // rows),
        in_specs=[chunk, chunk, nxt, nxt, new, new],
        out_specs=[chunk, chunk],
        out_shape=[sd, sd],
        compiler_params=_cparams(("parallel", "parallel")),
        name="window_update",
    )(cache_k, cache_v, cache_k, cache_v, k_new, v_new)


def _pool_project(d_groups, wp_ref, ps_ref, out_ref):
    for g, d in enumerate(d_groups):
        sl = slice(g * POOL_GROUP_DIM, (g + 1) * POOL_GROUP_DIM)
        out_ref[:, sl] = jnp.dot(d.astype(BF16), wp_ref[g], preferred_element_type=F32) * ps_ref[:, sl]


def _pool_prompt_kernel(u_ref, prev_ref, wp_ref, ps_ref, o_ref):
    i = pl.program_id(1)
    tp = u_ref.shape[1]
    u = u_ref[0]
    prev = jnp.where(i > 0, prev_ref[0], 0.0)
    ext = jnp.concatenate([prev, u], axis=0)
    pos = i * tp + lax.broadcasted_iota(I32, (tp, 1), 0)
    d_groups = []
    for g, w in enumerate(POOL_WINDOWS):
        sl = slice(g * POOL_GROUP_DIM, (g + 1) * POOL_GROUP_DIM)
        a = ext[:, sl]
        span = 1
        while span < w:
            a = a + pltpu.roll(a, span, 0)
            span *= 2
        win = a[POOL_HALO:, :]
        cnt = jnp.minimum(w, pos + 1).astype(F32)
        d_groups.append(win / cnt - u[:, sl])
    _pool_project(d_groups, wp_ref, ps_ref, o_ref.at[0])


def _pool_prompt(u, w_pool_bf, pool_scale):
    b, s, _ = u.shape
    tp = 512
    halo_blocks = tp // POOL_HALO
    return pl.pallas_call(
        _pool_prompt_kernel,
        grid=(b, s // tp),
        in_specs=[
            pl.BlockSpec((1, tp, D_POOL), lambda bi, i: (bi, i, 0)),
            pl.BlockSpec((1, POOL_HALO, D_POOL), lambda bi, i: (bi, jnp.maximum(i * halo_blocks - 1, 0), 0)),
            pl.BlockSpec(w_pool_bf.shape, lambda bi, i: (0, 0, 0)),
            pl.BlockSpec((1, D_POOL), lambda bi, i: (0, 0)),
        ],
        out_specs=pl.BlockSpec((1, tp, D_POOL), lambda bi, i: (bi, i, 0)),
        out_shape=jax.ShapeDtypeStruct((b, s, D_POOL), F32),
        compiler_params=_cparams(("parallel", "parallel")),
        name="pool_prompt",
    )(u, u, w_pool_bf, pool_scale)


def _pool_decode_kernel(u_ref, st_ref, wp_ref, ps_ref, o_ref):
    u = u_ref[...]
    d_groups = []
    for g, w in enumerate(POOL_WINDOWS):
        sl = slice(g * POOL_GROUP_DIM, (g + 1) * POOL_GROUP_DIM)
        win = u[:, sl]
        for j in range(1, w):
            win = win + st_ref[POOL_STATE - j][:, sl]
        d_groups.append(win / float(w) - u[:, sl])
    _pool_project(d_groups, wp_ref, ps_ref, o_ref)


def _pool_decode(u, state_t, w_pool_bf, pool_scale):
    assert PAST_LEN + 1 >= max(POOL_WINDOWS)
    return pl.pallas_call(
        _pool_decode_kernel,
        out_shape=jax.ShapeDtypeStruct(u.shape, F32),
        compiler_params=_cparams(None),
        name="pool_decode",
    )(u, state_t, w_pool_bf, pool_scale)


def _split_bf16(x):
    hi = x.astype(BF16)
    lo = (x - hi.astype(F32)).astype(BF16)
    return hi, lo


def _out_kernel(x_ref, a_ref, p_ref, g1_ref, sh_ref, sc_ref, g2_ref, wo_ref, gf_ref, wr_ref,
                wsg_ref, wsu_ref, wsd_ref, base_ref, h2_ref, lg_ref):
    tm = x_ref.shape[0]
    mix = (jnp.dot(a_ref[...].astype(BF16), wo_ref[:D_ATTN, :], preferred_element_type=F32)
           + jnp.dot(p_ref[...].astype(BF16), wo_ref[D_ATTN:, :], preferred_element_type=F32))
    x1 = x_ref[...] + g1_ref[0] * mix
    h2 = _rms(x1) * gf_ref[...] * (1.0 + sc_ref[0]) + sh_ref[0]
    for c in range(ROW_CHUNKS):
        h2_ref[pl.ds(c, tm, stride=ROW_CHUNKS), :] = h2[:, c * LANES:(c + 1) * LANES]
    h_hi, h_lo = _split_bf16(h2)
    w_hi, w_lo = _split_bf16(wr_ref[...])
    dims = (((1,), (1,)), ((), ()))
    lg_ref[...] = (lax.dot_general(w_hi, h_hi, dims, preferred_element_type=F32)
                   + lax.dot_general(w_hi, h_lo, dims, preferred_element_type=F32)
                   + lax.dot_general(w_lo, h_hi, dims, preferred_element_type=F32))
    sg = jnp.dot(h_hi, wsg_ref[...], preferred_element_type=F32)
    su = jnp.dot(h_hi, wsu_ref[...], preferred_element_type=F32)
    hs = (sg * jax.nn.sigmoid(sg) * su).astype(BF16)
    base_ref[...] = x1 + g2_ref[0] * jnp.dot(hs, wsd_ref[...], preferred_element_type=F32)


def _out_proj(x2d, attn, pool, gate1, shift2, scale2, gate2, w_out_bf, g_ffn, w_router_t,
              ws_gate_bf, ws_up_bf, ws_down_bf, tm, rows_per_mod, h2_tokens, h2_all, h2_block0):
    m = x2d.shape[0]
    r = gate1.shape[1]
    n_main = m // tm
    h2_rows = h2_tokens * ROW_CHUNKS
    n_fill = 0 if h2_all is not None else pl.cdiv(h2_rows - m * ROW_CHUNKS, tm * ROW_CHUNKS)
    last = n_main - 1
    row = lambda i: jnp.minimum(i, last)
    mod_spec = pl.BlockSpec((1, r, D_MODEL), lambda i: (row(i) // rows_per_mod, 0, 0))
    const = lambda shape: pl.BlockSpec(shape, lambda i: (0,) * len(shape))
    in_specs = [
        pl.BlockSpec((tm, D_MODEL), lambda i: (row(i), 0)),
        pl.BlockSpec((tm, D_ATTN), lambda i: (row(i), 0)),
        pl.BlockSpec((tm, D_POOL), lambda i: (row(i), 0)),
        mod_spec, mod_spec, mod_spec, mod_spec,
        const((D_MODEL, D_MODEL)),
        const((1, D_MODEL)),
        const((N_EXPERTS, D_MODEL)),
        const((D_MODEL, D_SHARED)), const((D_MODEL, D_SHARED)), const((D_SHARED, D_MODEL)),
    ]
    args = [x2d, attn, pool, gate1, shift2, scale2, gate2, w_out_bf, g_ffn, w_router_t,
            ws_gate_bf, ws_up_bf, ws_down_bf]
    aliases = {}
    n_in = len(args)
    if h2_all is not None:
        in_specs.append(pl.BlockSpec(memory_space=pl.ANY))
        args.append(h2_all)
        aliases = {n_in: 1}

    def kernel(*refs):
        refs = refs[:n_in] + refs[len(args):]
        if n_fill == 0:
            _out_kernel(*refs)
            return
        step = pl.program_id(0)
        pl.when(step < n_main)(lambda: _out_kernel(*refs))

        @pl.when(step >= n_main)
        def _():
            h2_ref = refs[n_in + 1]
            h2_ref[...] = jnp.zeros(h2_ref.shape, F32)

    return pl.pallas_call(
        kernel,
        grid=(n_main + n_fill,),
        in_specs=in_specs,
        out_specs=[
            pl.BlockSpec((tm, D_MODEL), lambda i: (row(i), 0)),
            pl.BlockSpec((tm * ROW_CHUNKS, LANES), lambda i: (i + h2_block0, 0)),
            pl.BlockSpec((N_EXPERTS, tm), lambda i: (0, row(i))),
        ],
        out_shape=[
            jax.ShapeDtypeStruct((m, D_MODEL), F32),
            jax.ShapeDtypeStruct((h2_rows, LANES), F32),
            jax.ShapeDtypeStruct((N_EXPERTS, m), F32),
        ],
        input_output_aliases=aliases,
        compiler_params=_cparams(("arbitrary",)),
        name="out_proj",
    )(*args)


def _first_index_of_max(v, iota, size, axis):
    m = jnp.max(v, axis=axis, keepdims=True)
    idx = jnp.min(jnp.where(v == m, iota, size), axis=axis, keepdims=True)
    return m, idx


def _route_kernel(n_valid, lg_ref, b_ref, eidx_ref, wts_ref, rank_ref, cnt_ref, carry):
    i = pl.program_id(0)
    tn = lg_ref.shape[1]

    @pl.when(i == 0)
    def _():
        carry[...] = jnp.zeros(carry.shape, F32)

    scores = jax.nn.sigmoid(lg_ref[...])
    biased = scores + b_ref[...]
    grp = biased.reshape(N_EXPERT_GROUPS, GROUP_SIZE, tn)
    io_g = lax.broadcasted_iota(I32, grp.shape, 1)
    m1, i1 = _first_index_of_max(grp, io_g, GROUP_SIZE, 1)
    m2 = jnp.max(jnp.where(io_g == i1, -jnp.inf, grp), axis=1, keepdims=True)
    gscore = (m1 + m2)[:, 0, :]
    io_n = lax.broadcasted_iota(I32, gscore.shape, 0)
    gsel = jnp.zeros(gscore.shape, jnp.bool_)
    for _ in range(TOPK_GROUPS):
        _, gi = _first_index_of_max(gscore, io_n, N_EXPERT_GROUPS, 0)
        hit = io_n == gi
        gsel = jnp.logical_or(gsel, hit)
        gscore = jnp.where(hit, -jnp.inf, gscore)
    emask = jnp.broadcast_to(gsel[:, None, :], grp.shape).reshape(N_EXPERTS, tn)
    cand = jnp.where(emask, biased, -jnp.inf)
    io_e = lax.broadcasted_iota(I32, cand.shape, 0)
    picked = jnp.zeros(cand.shape, jnp.bool_)
    eidx, sel = [], []
    for _ in range(TOP_K):
        _, ei = _first_index_of_max(cand, io_e, N_EXPERTS, 0)
        hit = io_e == ei
        eidx.append(ei)
        sel.append(jnp.sum(jnp.where(hit, scores, 0.0), axis=0, keepdims=True))
        picked = jnp.logical_or(picked, hit)
        cand = jnp.where(hit, -jnp.inf, cand)
    sel = jnp.concatenate(sel, axis=0)
    eidx = jnp.concatenate(eidx, axis=0)
    wts_ref[...] = sel / jnp.sum(sel, axis=0, keepdims=True) * ROUTED_SCALE
    eidx_ref[...] = eidx

    tok = i * tn + lax.broadcasted_iota(I32, cand.shape, 1)
    mask = jnp.logical_and(picked, tok < n_valid).astype(F32)
    tri = (lax.broadcasted_iota(I32, (tn, tn), 0) <= lax.broadcasted_iota(I32, (tn, tn), 1)).astype(BF16)
    pos = jnp.dot(mask.astype(BF16), tri, preferred_element_type=F32) + carry[:, :1]
    ranks = [jnp.sum(jnp.where(io_e == eidx[k:k + 1], pos, 0.0), axis=0, keepdims=True) for k in range(TOP_K)]
    rank_ref[...] = jnp.concatenate(ranks, axis=0).astype(I32) - 1
    total = carry[...] + jnp.sum(mask, axis=1, keepdims=True)
    carry[...] = total
    cnt_ref[...] = total.astype(I32)


def _route(logits_t, b_router, n_valid):
    mp = logits_t.shape[1]
    tn = ROUTE_TILE
    tok_spec = pl.BlockSpec((TOP_K, tn), lambda i: (0, i))
    return pl.pallas_call(
        functools.partial(_route_kernel, n_valid),
        grid=(mp // tn,),
        in_specs=[
            pl.BlockSpec((N_EXPERTS, tn), lambda i: (0, i)),
            pl.BlockSpec((N_EXPERTS, 1), lambda i: (0, 0)),
        ],
        out_specs=[tok_spec, tok_spec, tok_spec, pl.BlockSpec((N_EXPERTS, LANES), lambda i: (0, 0))],
        out_shape=[
            jax.ShapeDtypeStruct((TOP_K, mp), I32),
            jax.ShapeDtypeStruct((TOP_K, mp), F32),
            jax.ShapeDtypeStruct((TOP_K, mp), I32),
            jax.ShapeDtypeStruct((N_EXPERTS, LANES), I32),
        ],
        scratch_shapes=[pltpu.VMEM((N_EXPERTS, LANES), F32)],
        compiler_params=_cparams(("arbitrary",)),
        name="route",
    )(logits_t, b_router.reshape(N_EXPERTS, 1))


TILE_USED = 1
TILE_FIRST = 2


def _experts_kernel(te_ref, ts_ref, tf_ref, tok_ref, h_hbm, wg_ref, wu_ref, wd_ref, y_ref,
                    xbuf, sem, wg_bf, wu_bf, wd_bf):
    j = pl.program_id(0)
    n_steps = pl.num_programs(0)
    tm = EXPERT_TILE
    slot = j % 2
    unroll = 8

    def gather_tile(tile, dst_slot):
        start = ts_ref[tile]

        def body(g, carry):
            for r in range(unroll):
                i = g * unroll + r
                tok = tok_ref[start + i]
                pltpu.make_async_copy(
                    h_hbm.at[pl.ds(pl.multiple_of(tok * ROW_CHUNKS, ROW_CHUNKS), ROW_CHUNKS), :],
                    xbuf.at[dst_slot, pl.ds(pl.multiple_of(i * ROW_CHUNKS, ROW_CHUNKS), ROW_CHUNKS), :],
                    sem.at[dst_slot]).start()
            return carry

        lax.fori_loop(0, tm // unroll, body, 0)

    used = (tf_ref[j] & TILE_USED) != 0

    @pl.when(jnp.logical_and(j == 0, used))
    def _():
        gather_tile(0, 0)

    nxt = jnp.minimum(j + 1, n_steps - 1)

    @pl.when(jnp.logical_and(j + 1 < n_steps, (tf_ref[nxt] & TILE_USED) != 0))
    def _():
        gather_tile(j + 1, 1 - slot)

    @pl.when((tf_ref[j] & TILE_FIRST) != 0)
    def _():
        wg_bf[...] = wg_ref[0].astype(BF16)
        wu_bf[...] = wu_ref[0].astype(BF16)
        wd_bf[...] = wd_ref[0].astype(BF16)

    @pl.when(used)
    def _():
        pltpu.make_async_copy(h_hbm.at[pl.ds(0, tm * ROW_CHUNKS), :], xbuf.at[slot], sem.at[slot]).wait()
        x = jnp.concatenate(
            [xbuf[slot, pl.ds(c, tm, stride=ROW_CHUNKS), :] for c in range(ROW_CHUNKS)], axis=1).astype(BF16)
        hg = jnp.dot(x, wg_bf[...], preferred_element_type=F32)
        hu = jnp.dot(x, wu_bf[...], preferred_element_type=F32)
        h = (hg * jax.nn.sigmoid(hg) * hu).astype(BF16)
        y = jnp.dot(h, wd_bf[...], preferred_element_type=F32)
        for c in range(ROW_CHUNKS):
            y_ref[pl.ds(c, tm, stride=ROW_CHUNKS), :] = y[:, c * LANES:(c + 1) * LANES]

    @pl.when(jnp.logical_not(used))
    def _():
        y_ref[...] = jnp.zeros(y_ref.shape, F32)


def _experts(tile_expert, tile_start, tile_flags, sorted_tok, h2_all, w_gate, w_up, w_down):
    n_tiles = tile_expert.shape[0]
    tm = EXPERT_TILE
    grid_spec = pltpu.PrefetchScalarGridSpec(
        num_scalar_prefetch=4,
        grid=(n_tiles,),
        in_specs=[
            pl.BlockSpec(memory_space=pl.ANY),
            pl.BlockSpec((1, D_MODEL, D_EXPERT), lambda j, te, ts, tf, tok: (te[j], 0, 0)),
            pl.BlockSpec((1, D_MODEL, D_EXPERT), lambda j, te, ts, tf, tok: (te[j], 0, 0)),
            pl.BlockSpec((1, D_EXPERT, D_MODEL), lambda j, te, ts, tf, tok: (te[j], 0, 0)),
        ],
        out_specs=pl.BlockSpec((tm * ROW_CHUNKS, LANES), lambda j, te, ts, tf, tok: (j, 0)),
        scratch_shapes=[
            pltpu.VMEM((2, tm * ROW_CHUNKS, LANES), F32),
            pltpu.SemaphoreType.DMA((2,)),
            pltpu.VMEM((D_MODEL, D_EXPERT), BF16),
            pltpu.VMEM((D_MODEL, D_EXPERT), BF16),
            pltpu.VMEM((D_EXPERT, D_MODEL), BF16),
        ],
    )
    return pl.pallas_call(
        _experts_kernel,
        grid_spec=grid_spec,
        out_shape=jax.ShapeDtypeStruct((n_tiles * tm * ROW_CHUNKS, LANES), F32),
        compiler_params=_cparams(("arbitrary",)),
        name="experts",
    )(tile_expert, tile_start, tile_flags, sorted_tok, h2_all, w_gate, w_up, w_down)


def _combine_kernel(slot_ref, y_hbm, w_ref, base_ref, g2_ref, o_ref, buf, sem):
    i = pl.program_id(0)
    n_steps = pl.num_programs(0)
    tc = base_ref.shape[0]
    n = tc * TOP_K
    unroll = 8
    slot = i % 2

    def gather_tile(tile, dst):
        def body(g, carry):
            for r in range(unroll):
                a = g * unroll + r
                s = slot_ref[tile * n + a]
                pltpu.make_async_copy(
                    y_hbm.at[pl.ds(pl.multiple_of(s * ROW_CHUNKS, ROW_CHUNKS), ROW_CHUNKS), :],
                    buf.at[dst, pl.ds(pl.multiple_of(a * ROW_CHUNKS, ROW_CHUNKS), ROW_CHUNKS), :],
                    sem.at[dst]).start()
            return carry

        lax.fori_loop(0, n // unroll, body, 0)

    @pl.when(i == 0)
    def _():
        gather_tile(0, 0)

    @pl.when(i + 1 < n_steps)
    def _():
        gather_tile(i + 1, 1 - slot)

    pltpu.make_async_copy(y_hbm.at[pl.ds(0, n * ROW_CHUNKS), :], buf.at[slot], sem.at[slot]).wait()
    w = w_ref[...]
    g2 = g2_ref[0]
    for c in range(ROW_CHUNKS):
        acc = jnp.zeros((tc, LANES), F32)
        for k in range(TOP_K):
            acc = acc + w[:, k:k + 1] * buf[slot, pl.ds(k * tc * ROW_CHUNKS + c, tc, stride=ROW_CHUNKS), :]
        sl = slice(c * LANES, (c + 1) * LANES)
        o_ref[:, sl] = base_ref[:, sl] + g2[:, sl] * acc


def _combine(slot_flat, y_all, wts, base, gate2, tc, rows_per_mod):
    m = base.shape[0]
    r = gate2.shape[1]
    grid_spec = pltpu.PrefetchScalarGridSpec(
        num_scalar_prefetch=1,
        grid=(m // tc,),
        in_specs=[
            pl.BlockSpec(memory_space=pl.ANY),
            pl.BlockSpec((tc, TOP_K), lambda i, s: (i, 0)),
            pl.BlockSpec((tc, D_MODEL), lambda i, s: (i, 0)),
            pl.BlockSpec((1, r, D_MODEL), lambda i, s: (i // rows_per_mod, 0, 0)),
        ],
        out_specs=pl.BlockSpec((tc, D_MODEL), lambda i, s: (i, 0)),
        scratch_shapes=[
            pltpu.VMEM((2, tc * TOP_K * ROW_CHUNKS, LANES), F32),
            pltpu.SemaphoreType.DMA((2,)),
        ],
    )
    return pl.pallas_call(
        _combine_kernel,
        grid_spec=grid_spec,
        out_shape=jax.ShapeDtypeStruct((m, D_MODEL), F32),
        compiler_params=_cparams(("arbitrary",)),
        name="combine",
    )(slot_flat, y_all, wts, base, gate2)


def _tile_major_slots(slots, tc):
    k, m = slots.shape
    return slots.reshape(k, m // tc, tc).transpose(1, 0, 2).reshape(-1)


def _moe_routed(h2_all, logits_t, b_router, w_gate, w_up, w_down):
    n_tok = logits_t.shape[1]
    mp = -(-n_tok // ROUTE_TILE) * ROUTE_TILE
    logits_t = jnp.pad(logits_t, ((0, 0), (0, mp - n_tok)))
    eidx, wts, rank, counts = _route(logits_t, b_router, n_tok)
    eidx, wts, rank, counts = eidx[:, :n_tok], wts[:, :n_tok], rank[:, :n_tok], counts[:, 0]

    tm = EXPERT_TILE
    n_tiles = n_tok * TOP_K // tm + N_EXPERTS
    tiles_e = (counts + tm - 1) // tm
    tile_end = jnp.cumsum(tiles_e)
    tile_begin = tile_end - tiles_e
    dense_begin = jnp.cumsum(counts) - counts
    e_ids = jnp.arange(N_EXPERTS, dtype=I32)
    pad_begin = jnp.sum(jnp.where(eidx[..., None] == e_ids, tile_begin * tm, 0), axis=-1)
    slots = pad_begin + rank
    keys = (eidx * n_tok + jnp.arange(n_tok, dtype=I32)[None, :]).reshape(-1)
    sorted_tok = jnp.concatenate([jnp.sort(keys) % n_tok, jnp.zeros((tm,), I32)])
    tj = jnp.arange(n_tiles, dtype=I32)
    total_tiles = tile_end[-1]
    used = tj < total_tiles
    tj_used = jnp.minimum(tj, total_tiles - 1)
    te = jnp.sum((tile_end[None, :] <= tj_used[:, None]).astype(I32), axis=1)
    local = tj - tile_begin[te]
    ts = jnp.where(used, dense_begin[te] + local * tm, 0).astype(I32)
    tf = used.astype(I32) * TILE_USED + jnp.logical_and(used, local == 0).astype(I32) * TILE_FIRST
    y_all = _experts(te, ts, tf, sorted_tok, h2_all, w_gate, w_up, w_down)
    return y_all, slots, wts.T


def kernel(x_prompt, x_sample, cache_k_win, cache_v_win, state_pool, c_prompt, c_sample, w_ada, b_ada, g_mix, w_in, q_norm, k_norm, w_pool, pool_scale, w_out, g_ffn, w_router, b_router, w_gate, w_up, w_down, ws_gate, ws_up, ws_down):
    depth = w_ada.shape[0]
    assert depth == 1
    l = 0
    nb, seq, _ = x_prompt.shape
    bd, dec_seq, _ = x_sample.shape
    assert dec_seq == 1 and seq % (TOKEN_TILE * 16) == 0
    n_prompt = nb * seq
    n_tok = n_prompt + bd
    n_keep = min(MAX_WINDOW, seq)
    tiles_per_batch = seq // TOKEN_TILE

    n_mod_rows = -(-(nb + bd) // 8) * 8
    c_all = jnp.concatenate([c_prompt, c_sample, jnp.zeros((n_mod_rows - nb - bd, D_MODEL), F32)], axis=0)
    mods = _ada(c_all, w_ada[l], b_ada[l])
    mods_p = [mods[:nb, j * D_MODEL:(j + 1) * D_MODEL].reshape(nb, 1, D_MODEL) for j in range(6)]
    mods_s = [mods[nb:nb + bd, j * D_MODEL:(j + 1) * D_MODEL].reshape(1, bd, D_MODEL) for j in range(6)]

    g_mix_l = g_mix[l].reshape(1, D_MODEL)
    g_ffn_l = g_ffn[l].reshape(1, D_MODEL)
    qn = q_norm[l].reshape(1, HEAD_DIM)
    kn = k_norm[l].reshape(1, HEAD_DIM)
    w_in_bf = w_in[l].astype(BF16)
    w_out_bf = w_out[l].astype(BF16)
    w_pool_bf = w_pool[l].astype(BF16)
    ps = pool_scale[l].reshape(1, D_POOL)
    w_router_t = w_router[l].T
    wsg_bf, wsu_bf, wsd_bf = ws_gate[l].astype(BF16), ws_up[l].astype(BF16), ws_down[l].astype(BF16)

    rope_p = _rope_tables(jnp.arange(seq, dtype=I32))
    rope_s = _rope_tables(jnp.full((bd,), PAST_LEN, I32))
    xp2 = x_prompt.reshape(n_prompt, D_MODEL)
    xs2 = x_sample.reshape(bd, D_MODEL)
    assert n_keep % TOKEN_TILE == 0
    qp, kp, vp, up, kwin_p, vwin_p = _in_proj(xp2, mods_p[0], mods_p[1], g_mix_l, w_in_bf, qn, kn, rope_p,
                                              TOKEN_TILE, tiles_per_batch, tiles_per_batch, n_keep // TOKEN_TILE)
    qs, _, _, us, k_new, v_new = _in_proj(xs2, mods_s[0], mods_s[1], g_mix_l, w_in_bf, qn, kn, rope_s, bd, 1, 1, 1)

    to_seq = lambda t: t.reshape(nb, seq, -1)
    attn_p = _attn_prompt(to_seq(qp), to_seq(kp), to_seq(vp)).reshape(n_prompt, D_ATTN)
    pool_p = _pool_prompt(to_seq(up), w_pool_bf, ps).reshape(n_prompt, D_POOL)
    heads = lambda t: t.reshape(bd, N_HEADS, HEAD_DIM)
    ck, cv = cache_k_win[l], cache_v_win[l]
    k_new, v_new = heads(k_new), heads(v_new)
    attn_s = _attn_decode(heads(qs), k_new, v_new, ck, cv).reshape(bd, D_ATTN)
    pool_s = _pool_decode(us, jnp.swapaxes(state_pool[l], 0, 1), w_pool_bf, ps)
    k_win_s, v_win_s = _window_update(ck, cv, k_new, v_new)

    base_p, h2_all, lg_p = _out_proj(xp2, attn_p, pool_p, mods_p[2], mods_p[3], mods_p[4], mods_p[5],
                                     w_out_bf, g_ffn_l, w_router_t, wsg_bf, wsu_bf, wsd_bf,
                                     TOKEN_TILE, tiles_per_batch, n_tok, None, 0)
    base_s, h2_all, lg_s = _out_proj(xs2, attn_s, pool_s, mods_s[2], mods_s[3], mods_s[4], mods_s[5],
                                     w_out_bf, g_ffn_l, w_router_t, wsg_bf, wsu_bf, wsd_bf,
                                     bd, 1, n_tok, h2_all, n_prompt // bd)

    y_all, slots, wts_t = _moe_routed(h2_all, jnp.concatenate([lg_p, lg_s], axis=1), b_router[l],
                                      w_gate[l], w_up[l], w_down[l])

    yp = _combine(_tile_major_slots(slots[:, :n_prompt], COMBINE_TILE), y_all, wts_t[:n_prompt], base_p,
                  mods_p[5], COMBINE_TILE, seq // COMBINE_TILE)
    ys = _combine(_tile_major_slots(slots[:, n_prompt:n_tok], bd), y_all, wts_t[n_prompt:n_tok], base_s,
                  mods_s[5], bd, 1)

    y_prompt = yp.reshape(nb, seq, D_MODEL)
    y_sample = ys.reshape(bd, dec_seq, D_MODEL)
    win = lambda t: t.reshape(1, nb, n_keep, N_HEADS, HEAD_DIM)
    pool_p_state = to_seq(up)[:, seq - POOL_STATE:][None]
    pool_s_state = jnp.concatenate([state_pool[l][:, 1:], us[:, None, :]], axis=1)[None]
    return (y_prompt, y_sample, win(kwin_p), win(vwin_p), pool_p_state,
            k_win_s[None], v_win_s[None], pool_s_state)
```

```python
import functools

import jax
import jax.numpy as jnp
from jax import lax
from jax.experimental import pallas as pl
from jax.experimental.pallas import tpu as pltpu

F32 = jnp.float32
BF16 = jnp.bfloat16
I32 = jnp.int32

D_MODEL = 2048
N_HEADS = 8
HEAD_DIM = 128
D_ATTN = N_HEADS * HEAD_DIM
D_POOL = D_MODEL - D_ATTN
D_IN = 3 * D_ATTN + D_POOL
ROPE_DIM = HEAD_DIM // 4
ROPE_HALF = ROPE_DIM // 2
ROPE_THETA = 500000.0
DILATED_GROUPS = ((128, 1), (512, 4), (2048, 16))
QBLOCK = 128
ATTN_SCALE = HEAD_DIM ** -0.5
POOL_WINDOWS = (2, 4, 8, 16)
POOL_GROUP_DIM = D_POOL // len(POOL_WINDOWS)
POOL_STATE = max(POOL_WINDOWS) - 1
POOL_HALO = 16
N_EXPERTS = 64
TOP_K = 8
N_EXPERT_GROUPS = 8
GROUP_SIZE = N_EXPERTS // N_EXPERT_GROUPS
TOPK_GROUPS = 4
D_EXPERT = 512
D_SHARED = 512
ROUTED_SCALE = 2.5
EPS = 1e-6
PAST_LEN = 16384
MAX_WINDOW = 2048

LANES = 128
ROW_CHUNKS = D_MODEL // LANES
NEG_BIG = -1e30
LOG2_E = 1.4426950408889634

TOKEN_TILE = 256
EXPERT_TILE = 256
COMBINE_TILE = 128
ROUTE_TILE = 256
VMEM_LIMIT = 60 * 1024 * 1024


def _cparams(sem, vmem=VMEM_LIMIT):
    return pltpu.CompilerParams(dimension_semantics=sem, vmem_limit_bytes=vmem)


def _ada_kernel(c_ref, w_ref, b_ref, o_ref):
    c = c_ref[...]
    s = (c * jax.nn.sigmoid(c)).astype(BF16)
    o_ref[...] = jnp.dot(s, w_ref[...].astype(BF16), preferred_element_type=F32) + b_ref[...]


def _ada(c_all, w_ada, b_ada):
    rows = c_all.shape[0]
    n = w_ada.shape[1]
    tn = 1024
    return pl.pallas_call(
        _ada_kernel,
        grid=(n // tn,),
        in_specs=[
            pl.BlockSpec((rows, D_MODEL), lambda j: (0, 0)),
            pl.BlockSpec((D_MODEL, tn), lambda j: (0, j)),
            pl.BlockSpec((1, tn), lambda j: (0, j)),
        ],
        out_specs=pl.BlockSpec((rows, tn), lambda j: (0, j)),
        out_shape=jax.ShapeDtypeStruct((rows, n), F32),
        compiler_params=_cparams(("parallel",)),
        name="ada_modulation",
    )(c_all, w_ada, b_ada.reshape(1, n))


def _rms(x):
    return x * lax.rsqrt(jnp.mean(x * x, axis=-1, keepdims=True) + EPS)


def _in_kernel(x_ref, sh_ref, sc_ref, g_ref, w_ref, qn_ref, kn_ref, c_ref, s1_ref, s2_ref,
               q_ref, k_ref, v_ref, u_ref, kw_ref, vw_ref):
    tm = x_ref.shape[0]
    x = x_ref[...]
    h = _rms(x) * g_ref[...] * (1.0 + sc_ref[0]) + sh_ref[0]
    hb = h.astype(BF16)
    cos = c_ref[...]
    s1 = s1_ref[...]
    s2 = s2_ref[...]

    def head_rows(win_ref, hd, val):
        win_ref[pl.ds(hd, tm, stride=N_HEADS), :] = val

    def qk(sec, nrm, out_ref, win_ref):
        z = jnp.dot(hb, w_ref[:, sec * D_ATTN:(sec + 1) * D_ATTN], preferred_element_type=F32)
        for hd in range(N_HEADS):
            sl = slice(hd * HEAD_DIM, (hd + 1) * HEAD_DIM)
            r = _rms(z[:, sl]) * nrm
            r = r * cos + pltpu.roll(r, HEAD_DIM - ROPE_HALF, 1) * s1 + pltpu.roll(r, ROPE_HALF, 1) * s2
            out_ref[:, sl] = r
            if win_ref is not None:
                head_rows(win_ref, hd, r)

    qk(0, qn_ref[...], q_ref, None)
    qk(1, kn_ref[...], k_ref, kw_ref)
    v = jnp.dot(hb, w_ref[:, 2 * D_ATTN:3 * D_ATTN], preferred_element_type=F32)
    v_ref[...] = v
    for hd in range(N_HEADS):
        head_rows(vw_ref, hd, v[:, hd * HEAD_DIM:(hd + 1) * HEAD_DIM])
    u_ref[...] = jnp.dot(hb, w_ref[:, 3 * D_ATTN:], preferred_element_type=F32)


def _in_proj(x2d, shift, scale, g_mix, w_in_bf, q_norm, k_norm, rope, tm, rows_per_mod, rope_tiles, keep_tiles):
    m = x2d.shape[0]
    r = shift.shape[1]
    n_groups = m // tm // rows_per_mod
    skip = rows_per_mod - keep_tiles
    mod_spec = pl.BlockSpec((1, r, D_MODEL), lambda i: (i // rows_per_mod, 0, 0))
    rope_spec = pl.BlockSpec((tm, HEAD_DIM), lambda i: (i % rope_tiles, 0))
    out_spec = pl.BlockSpec((tm, D_ATTN), lambda i: (i, 0))
    out_sd = jax.ShapeDtypeStruct((m, D_ATTN), F32)
    win_spec = pl.BlockSpec(
        (tm * N_HEADS, HEAD_DIM),
        lambda i: ((i // rows_per_mod) * keep_tiles + jnp.maximum(i % rows_per_mod - skip, 0), 0))
    win_sd = jax.ShapeDtypeStruct((n_groups * keep_tiles * tm * N_HEADS, HEAD_DIM), F32)
    return pl.pallas_call(
        _in_kernel,
        grid=(m // tm,),
        in_specs=[
            pl.BlockSpec((tm, D_MODEL), lambda i: (i, 0)),
            mod_spec, mod_spec,
            pl.BlockSpec((1, D_MODEL), lambda i: (0, 0)),
            pl.BlockSpec((D_MODEL, D_IN), lambda i: (0, 0)),
            pl.BlockSpec((1, HEAD_DIM), lambda i: (0, 0)),
            pl.BlockSpec((1, HEAD_DIM), lambda i: (0, 0)),
            rope_spec, rope_spec, rope_spec,
        ],
        out_specs=[out_spec] * 4 + [win_spec] * 2,
        out_shape=[out_sd] * 4 + [win_sd] * 2,
        compiler_params=_cparams(("arbitrary",)),
        name="in_proj",
    )(x2d, shift, scale, g_mix, w_in_bf, q_norm, k_norm, *rope)


def _rope_tables(pos):
    inv_freq = jnp.float32(ROPE_THETA) ** (-jnp.arange(ROPE_HALF, dtype=F32) / ROPE_HALF)
    ang = pos.astype(F32)[:, None] * inv_freq[None, :]
    cos, sin = jnp.cos(ang), jnp.sin(ang)
    t = pos.shape[0]
    rest = HEAD_DIM - ROPE_DIM
    c = jnp.concatenate([cos, cos, jnp.ones((t, rest), F32)], axis=1)
    s1 = jnp.concatenate([-sin, jnp.zeros((t, ROPE_HALF + rest), F32)], axis=1)
    s2 = jnp.concatenate([jnp.zeros((t, ROPE_HALF), F32), sin, jnp.zeros((t, rest), F32)], axis=1)
    return c, s1, s2


ATTN_INTERLEAVE = 4


def _attn_prompt_kernel(q_ref, k_ref, v_ref, o_ref, m_scr, l_scr, acc_scr):
    seq = q_ref.shape[1]
    m_scr[...] = jnp.full(m_scr.shape, NEG_BIG, F32)
    l_scr[...] = jnp.zeros(l_scr.shape, F32)
    acc_scr[...] = jnp.zeros(acc_scr.shape, F32)
    qi = lax.broadcasted_iota(I32, (QBLOCK, 2 * QBLOCK), 0)
    kj = lax.broadcasted_iota(I32, (QBLOCK, 2 * QBLOCK), 1)
    dist = qi + QBLOCK - kj
    band_mask = jnp.logical_and(dist >= 0, dist <= QBLOCK)
    bias_band = jnp.where(band_mask, 0.0, NEG_BIG)
    bias_first = jnp.where(jnp.logical_and(band_mask, kj >= QBLOCK), 0.0, NEG_BIG)
    ones = jnp.ones((2 * QBLOCK, HEAD_DIM), BF16)
    dims = (((1,), (1,)), ((), ()))
    q_scale = ATTN_SCALE * LOG2_E

    for window, dil in DILATED_GROUPS:
        assert window // dil == QBLOCK
        n_units = seq // QBLOCK
        assert n_units % ATTN_INTERLEAVE == 0 and (dil % ATTN_INTERLEAVE == 0 or dil == 1)

        def rows(start, dil=dil):
            return pl.ds(start, QBLOCK) if dil == 1 else pl.ds(start, QBLOCK, stride=dil)

        def body(it, carry, dil=dil, rows=rows):
            loaded = []
            for u in range(ATTN_INTERLEAVE):
                f = it * ATTN_INTERLEAVE + u
                blk, res = f // dil, f % dil
                start = blk * (QBLOCK * dil) + res
                prev = jnp.maximum(start - QBLOCK * dil, res)
                q = (q_ref[0, rows(start), :] * q_scale).astype(BF16)
                k = jnp.concatenate([k_ref[0, rows(prev), :], k_ref[0, rows(start), :]], axis=0).astype(BF16)
                v = jnp.concatenate([v_ref[0, rows(prev), :], v_ref[0, rows(start), :]], axis=0).astype(BF16)
                v_ones = jnp.concatenate([v, ones], axis=1)
                state = (m_scr[rows(start), :], l_scr[rows(start), :], acc_scr[rows(start), :])
                loaded.append((start, blk, q, k, v_ones, state))
            scores = [lax.dot_general(q, k, dims, preferred_element_type=F32)
                      + jnp.where(blk > 0, bias_band, bias_first)
                      for _, blk, q, k, _, _ in loaded]
            probs = []
            for s, (_, _, _, _, _, (m_old, _, _)) in zip(scores, loaded):
                m_new = jnp.maximum(m_old, jnp.max(s, axis=1, keepdims=True))
                p = jnp.exp2(s - jnp.concatenate([m_new, m_new], axis=1)).astype(BF16)
                probs.append((m_new, jnp.exp2(m_old - m_new), p))
            updated = []
            for (m_new, alpha, p), (start, _, _, _, v_ones, (_, l_old, a_old)) in zip(probs, loaded):
                pv = jnp.dot(p, v_ones, preferred_element_type=F32)
                updated.append((start, m_new, alpha * l_old + pv[:, HEAD_DIM:], alpha * a_old + pv[:, :HEAD_DIM]))
            for start, m_new, l_new, a_new in updated:
                m_scr[rows(start), :] = m_new
                l_scr[rows(start), :] = l_new
                acc_scr[rows(start), :] = a_new
            return carry

        lax.fori_loop(0, n_units // ATTN_INTERLEAVE, body, 0)

    o_ref[0] = acc_scr[...] / l_scr[...]


def _attn_prompt(q, k, v):
    b, s, _ = q.shape
    spec = pl.BlockSpec((1, s, HEAD_DIM), lambda bi, hi: (bi, 0, hi))
    return pl.pallas_call(
        _attn_prompt_kernel,
        grid=(b, N_HEADS),
        in_specs=[spec, spec, spec],
        out_specs=spec,
        out_shape=jax.ShapeDtypeStruct((b, s, D_ATTN), F32),
        scratch_shapes=[pltpu.VMEM((s, HEAD_DIM), F32)] * 3,
        compiler_params=_cparams(("parallel", "parallel")),
        name="attn_prompt",
    )(q, k, v)


def _attn_decode_kernel(q_ref, kn_ref, vn_ref, k1_ref, k2_ref, k3_ref, v1_ref, v2_ref, v3_ref, o_ref):
    q = q_ref[0]
    k_new = kn_ref[0]
    v_new = vn_ref[0]
    n_grp = len(DILATED_GROUPS)

    def scores(kb):
        return jnp.sum(kb * q[None], axis=-1, keepdims=True) * ATTN_SCALE

    kcs = [k1_ref[0], k2_ref[0][:, 0], k3_ref[0][:, 0]]
    vcs = [v1_ref[0], v2_ref[0][:, 0], v3_ref[0][:, 0]]
    s_new = jnp.sum(k_new * q, axis=-1, keepdims=True) * ATTN_SCALE
    s_grp = [scores(kb) for kb in kcs]
    m = s_new
    for s in s_grp:
        m = jnp.maximum(m, jnp.max(s, axis=0))
    p_new = jnp.exp(s_new - m)
    den = n_grp * p_new
    num = n_grp * p_new * v_new
    for s, vb in zip(s_grp, vcs):
        p = jnp.exp(s - m[None])
        den = den + jnp.sum(p, axis=0)
        num = num + jnp.sum(p * vb, axis=0)
    o_ref[0] = num / den


def _attn_decode(q, k_new, v_new, cache_k, cache_v):
    bd, n_buf = cache_k.shape[0], cache_k.shape[1]
    assert n_buf == MAX_WINDOW

    def views(cache):
        out, specs = [], []
        for window, dil in DILATED_GROUPS:
            band = window // dil
            assert band == QBLOCK and n_buf % dil == 0 and n_buf // dil >= band
            if dil == 1:
                out.append(cache)
                specs.append(pl.BlockSpec((1, band, N_HEADS, HEAD_DIM),
                                          lambda b, nb=n_buf // band: (b, nb - 1, 0, 0)))
            else:
                out.append(cache.reshape(bd, n_buf // dil, dil, N_HEADS, HEAD_DIM))
                specs.append(pl.BlockSpec((1, band, 1, N_HEADS, HEAD_DIM),
                                          lambda b, nb=n_buf // dil // band: (b, nb - 1, 0, 0, 0)))
        return out, specs

    kv, kspecs = views(cache_k)
    vv, vspecs = views(cache_v)
    tok = pl.BlockSpec((1, N_HEADS, HEAD_DIM), lambda b: (b, 0, 0))
    return pl.pallas_call(
        _attn_decode_kernel,
        grid=(bd,),
        in_specs=[tok, tok, tok] + kspecs + vspecs,
        out_specs=tok,
        out_shape=jax.ShapeDtypeStruct((bd, N_HEADS, HEAD_DIM), F32),
        compiler_params=_cparams(("parallel",)),
        name="attn_decode",
    )(q, k_new, v_new, *kv, *vv)


WINDOW_CHUNK = 1024


def _window_update_kernel(ck_ref, cv_ref, nk_ref, nv_ref, kn_ref, vn_ref, ok_ref, ov_ref):
    rows = ck_ref.shape[1]
    last = pl.program_id(1) == pl.num_programs(1) - 1
    for cur, nxt, new, out in ((ck_ref, nk_ref, kn_ref, ok_ref), (cv_ref, nv_ref, vn_ref, ov_ref)):
        out[0, pl.ds(0, rows - 1)] = cur[0, pl.ds(1, rows - 1)]
        out[0, rows - 1] = jnp.where(last, new[0], nxt[0, 0])


def _window_update(cache_k, cache_v, k_new, v_new):
    bd, n_buf = cache_k.shape[0], cache_k.shape[1]
    rows = WINDOW_CHUNK
    assert n_buf % rows == 0
    tail = cache_k.shape[2:]
    chunk = pl.BlockSpec((1, rows) + tail, lambda b, c: (b, c, 0, 0))
    nxt = pl.BlockSpec((1, 1) + tail, lambda b, c: (b, jnp.minimum((c + 1) * rows, n_buf - 1), 0, 0))
    new = pl.BlockSpec((1,) + tail, lambda b, c: (b, 0, 0))
    sd = jax.ShapeDtypeStruct(cache_k.shape, cache_k.dtype)
    return pl.pallas_call(
        _window_update_kernel,
        grid=(bd, n_buf // rows),
        in_specs=[chunk, chunk, nxt, nxt, new, new],
        out_specs=[chunk, chunk],
        out_shape=[sd, sd],
        compiler_params=_cparams(("parallel", "parallel")),
        name="window_update",
    )(cache_k, cache_v, cache_k, cache_v, k_new, v_new)


def _pool_project(d_groups, wp_ref, ps_ref, out_ref):
    for g, d in enumerate(d_groups):
        sl = slice(g * POOL_GROUP_DIM, (g + 1) * POOL_GROUP_DIM)
        out_ref[:, sl] = jnp.dot(d.astype(BF16), wp_ref[g], preferred_element_type=F32) * ps_ref[:, sl]


def _pool_prompt_kernel(u_ref, prev_ref, wp_ref, ps_ref, o_ref):
    i = pl.program_id(1)
    tp = u_ref.shape[1]
    u = u_ref[0]
    prev = jnp.where(i > 0, prev_ref[0], 0.0)
    ext = jnp.concatenate([prev, u], axis=0)
    pos = i * tp + lax.broadcasted_iota(I32, (tp, 1), 0)
    d_groups = []
    for g, w in enumerate(POOL_WINDOWS):
        sl = slice(g * POOL_GROUP_DIM, (g + 1) * POOL_GROUP_DIM)
        a = ext[:, sl]
        span = 1
        while span < w:
            a = a + pltpu.roll(a, span, 0)
            span *= 2
        win = a[POOL_HALO:, :]
        cnt = jnp.minimum(w, pos + 1).astype(F32)
        d_groups.append(win / cnt - u[:, sl])
    _pool_project(d_groups, wp_ref, ps_ref, o_ref.at[0])


def _pool_prompt(u, w_pool_bf, pool_scale):
    b, s, _ = u.shape
    tp = 512
    halo_blocks = tp // POOL_HALO
    return pl.pallas_call(
        _pool_prompt_kernel,
        grid=(b, s // tp),
        in_specs=[
            pl.BlockSpec((1, tp, D_POOL), lambda bi, i: (bi, i, 0)),
            pl.BlockSpec((1, POOL_HALO, D_POOL), lambda bi, i: (bi, jnp.maximum(i * halo_blocks - 1, 0), 0)),
            pl.BlockSpec(w_pool_bf.shape, lambda bi, i: (0, 0, 0)),
            pl.BlockSpec((1, D_POOL), lambda bi, i: (0, 0)),
        ],
        out_specs=pl.BlockSpec((1, tp, D_POOL), lambda bi, i: (bi, i, 0)),
        out_shape=jax.ShapeDtypeStruct((b, s, D_POOL), F32),
        compiler_params=_cparams(("parallel", "parallel")),
        name="pool_prompt",
    )(u, u, w_pool_bf, pool_scale)


def _pool_decode_kernel(u_ref, st_ref, wp_ref, ps_ref, o_ref):
    u = u_ref[...]
    d_groups = []
    for g, w in enumerate(POOL_WINDOWS):
        sl = slice(g * POOL_GROUP_DIM, (g + 1) * POOL_GROUP_DIM)
        win = u[:, sl]
        for j in range(1, w):
            win = win + st_ref[POOL_STATE - j][:, sl]
        d_groups.append(win / float(w) - u[:, sl])
    _pool_project(d_groups, wp_ref, ps_ref, o_ref)


def _pool_decode(u, state_t, w_pool_bf, pool_scale):
    assert PAST_LEN + 1 >= max(POOL_WINDOWS)
    return pl.pallas_call(
        _pool_decode_kernel,
        out_shape=jax.ShapeDtypeStruct(u.shape, F32),
        compiler_params=_cparams(None),
        name="pool_decode",
    )(u, state_t, w_pool_bf, pool_scale)


def _split_bf16(x):
    hi = x.astype(BF16)
    lo = (x - hi.astype(F32)).astype(BF16)
    return hi, lo


def _out_kernel(x_ref, a_ref, p_ref, g1_ref, sh_ref, sc_ref, g2_ref, wo_ref, gf_ref, wr_ref,
                wsg_ref, wsu_ref, wsd_ref, base_ref, h2_ref, lg_ref):
    tm = x_ref.shape[0]
    mix = (jnp.dot(a_ref[...].astype(BF16), wo_ref[:D_ATTN, :], preferred_element_type=F32)
           + jnp.dot(p_ref[...].astype(BF16), wo_ref[D_ATTN:, :], preferred_element_type=F32))
    x1 = x_ref[...] + g1_ref[0] * mix
    h2 = _rms(x1) * gf_ref[...] * (1.0 + sc_ref[0]) + sh_ref[0]
    for c in range(ROW_CHUNKS):
        h2_ref[pl.ds(c, tm, stride=ROW_CHUNKS), :] = h2[:, c * LANES:(c + 1) * LANES]
    h_hi, h_lo = _split_bf16(h2)
    w_hi, w_lo = _split_bf16(wr_ref[...])
    dims = (((1,), (1,)), ((), ()))
    lg_ref[...] = (lax.dot_general(w_hi, h_hi, dims, preferred_element_type=F32)
                   + lax.dot_general(w_hi, h_lo, dims, preferred_element_type=F32)
                   + lax.dot_general(w_lo, h_hi, dims, preferred_element_type=F32))
    sg = jnp.dot(h_hi, wsg_ref[...], preferred_element_type=F32)
    su = jnp.dot(h_hi, wsu_ref[...], preferred_element_type=F32)
    hs = (sg * jax.nn.sigmoid(sg) * su).astype(BF16)
    base_ref[...] = x1 + g2_ref[0] * jnp.dot(hs, wsd_ref[...], preferred_element_type=F32)


def _out_proj(x2d, attn, pool, gate1, shift2, scale2, gate2, w_out_bf, g_ffn, w_router_t,
              ws_gate_bf, ws_up_bf, ws_down_bf, tm, rows_per_mod, h2_tokens, h2_all, h2_block0):
    m = x2d.shape[0]
    r = gate1.shape[1]
    n_main = m // tm
    h2_rows = h2_tokens * ROW_CHUNKS
    n_fill = 0 if h2_all is not None else pl.cdiv(h2_rows - m * ROW_CHUNKS, tm * ROW_CHUNKS)
    last = n_main - 1
    row = lambda i: jnp.minimum(i, last)
    mod_spec = pl.BlockSpec((1, r, D_MODEL), lambda i: (row(i) // rows_per_mod, 0, 0))
    const = lambda shape: pl.BlockSpec(shape, lambda i: (0,) * len(shape))
    in_specs = [
        pl.BlockSpec((tm, D_MODEL), lambda i: (row(i), 0)),
        pl.BlockSpec((tm, D_ATTN), lambda i: (row(i), 0)),
        pl.BlockSpec((tm, D_POOL), lambda i: (row(i), 0)),
        mod_spec, mod_spec, mod_spec, mod_spec,
        const((D_MODEL, D_MODEL)),
        const((1, D_MODEL)),
        const((N_EXPERTS, D_MODEL)),
        const((D_MODEL, D_SHARED)), const((D_MODEL, D_SHARED)), const((D_SHARED, D_MODEL)),
    ]
    args = [x2d, attn, pool, gate1, shift2, scale2, gate2, w_out_bf, g_ffn, w_router_t,
            ws_gate_bf, ws_up_bf, ws_down_bf]
    aliases = {}
    n_in = len(args)
    if h2_all is not None:
        in_specs.append(pl.BlockSpec(memory_space=pl.ANY))
        args.append(h2_all)
        aliases = {n_in: 1}

    def kernel(*refs):
        refs = refs[:n_in] + refs[len(args):]
        if n_fill == 0:
            _out_kernel(*refs)
            return
        step = pl.program_id(0)
        pl.when(step < n_main)(lambda: _out_kernel(*refs))

        @pl.when(step >= n_main)
        def _():
            h2_ref = refs[n_in + 1]
            h2_ref[...] = jnp.zeros(h2_ref.shape, F32)

    return pl.pallas_call(
        kernel,
        grid=(n_main + n_fill,),
        in_specs=in_specs,
        out_specs=[
            pl.BlockSpec((tm, D_MODEL), lambda i: (row(i), 0)),
            pl.BlockSpec((tm * ROW_CHUNKS, LANES), lambda i: (i + h2_block0, 0)),
            pl.BlockSpec((N_EXPERTS, tm), lambda i: (0, row(i))),
        ],
        out_shape=[
            jax.ShapeDtypeStruct((m, D_MODEL), F32),
            jax.ShapeDtypeStruct((h2_rows, LANES), F32),
            jax.ShapeDtypeStruct((N_EXPERTS, m), F32),
        ],
        input_output_aliases=aliases,
        compiler_params=_cparams(("arbitrary",)),
        name="out_proj",
    )(*args)


def _first_index_of_max(v, iota, size, axis):
    m = jnp.max(v, axis=axis, keepdims=True)
    idx = jnp.min(jnp.where(v == m, iota, size), axis=axis, keepdims=True)
    return m, idx


def _route_kernel(n_valid, lg_ref, b_ref, eidx_ref, wts_ref, cnt_ref, carry):
    i = pl.program_id(0)
    tn = lg_ref.shape[1]

    @pl.when(i == 0)
    def _():
        carry[...] = jnp.zeros(carry.shape, F32)

    scores = jax.nn.sigmoid(lg_ref[...])
    biased = scores + b_ref[...]
    grp = biased.reshape(N_EXPERT_GROUPS, GROUP_SIZE, tn)
    io_g = lax.broadcasted_iota(I32, grp.shape, 1)
    m1, i1 = _first_index_of_max(grp, io_g, GROUP_SIZE, 1)
    m2 = jnp.max(jnp.where(io_g == i1, -jnp.inf, grp), axis=1, keepdims=True)
    gscore = (m1 + m2)[:, 0, :]
    io_n = lax.broadcasted_iota(I32, gscore.shape, 0)
    gsel = jnp.zeros(gscore.shape, jnp.bool_)
    for _ in range(TOPK_GROUPS):
        _, gi = _first_index_of_max(gscore, io_n, N_EXPERT_GROUPS, 0)
        hit = io_n == gi
        gsel = jnp.logical_or(gsel, hit)
        gscore = jnp.where(hit, -jnp.inf, gscore)
    emask = jnp.broadcast_to(gsel[:, None, :], grp.shape).reshape(N_EXPERTS, tn)
    cand = jnp.where(emask, biased, -jnp.inf)
    io_e = lax.broadcasted_iota(I32, cand.shape, 0)
    picked = jnp.zeros(cand.shape, jnp.bool_)
    eidx, sel = [], []
    for _ in range(TOP_K):
        _, ei = _first_index_of_max(cand, io_e, N_EXPERTS, 0)
        hit = io_e == ei
        eidx.append(ei)
        sel.append(jnp.sum(jnp.where(hit, scores, 0.0), axis=0, keepdims=True))
        picked = jnp.logical_or(picked, hit)
        cand = jnp.where(hit, -jnp.inf, cand)
    sel = jnp.concatenate(sel, axis=0)
    eidx = jnp.concatenate(eidx, axis=0)
    wts_ref[...] = sel / jnp.sum(sel, axis=0, keepdims=True) * ROUTED_SCALE
    eidx_ref[...] = eidx

    tok = i * tn + lax.broadcasted_iota(I32, cand.shape, 1)
    mask = jnp.logical_and(picked, tok < n_valid).astype(F32)
    total = carry[...] + jnp.sum(mask, axis=1, keepdims=True)
    carry[...] = total
    cnt_ref[...] = total.astype(I32)


def _route(logits_t, b_router, n_valid):
    mp = logits_t.shape[1]
    tn = ROUTE_TILE
    tok_spec = pl.BlockSpec((TOP_K, tn), lambda i: (0, i))
    return pl.pallas_call(
        functools.partial(_route_kernel, n_valid),
        grid=(mp // tn,),
        in_specs=[
            pl.BlockSpec((N_EXPERTS, tn), lambda i: (0, i)),
            pl.BlockSpec((N_EXPERTS, 1), lambda i: (0, 0)),
        ],
        out_specs=[tok_spec, tok_spec, pl.BlockSpec((N_EXPERTS, LANES), lambda i: (0, 0))],
        out_shape=[
            jax.ShapeDtypeStruct((TOP_K, mp), I32),
            jax.ShapeDtypeStruct((TOP_K, mp), F32),
            jax.ShapeDtypeStruct((N_EXPERTS, LANES), I32),
        ],
        scratch_shapes=[pltpu.VMEM((N_EXPERTS, LANES), F32)],
        compiler_params=_cparams(("arbitrary",)),
        name="route",
    )(logits_t, b_router.reshape(N_EXPERTS, 1))


TILE_USED = 1
TILE_FIRST = 2
TILE_WSLOT = 4
TILE_PREV_USED = 8
PAIR_BITS = 3
assert 1 << PAIR_BITS == TOP_K
DUMMY_TOKENS = 2 * EXPERT_TILE // TOP_K


def _experts_kernel(n_tok, te_ref, ts_ref, tn_ref, tf_ref, ne_ref, sv_ref,
                    h_hbm, wg_hbm, wu_hbm, wd_hbm, y_hbm,
                    xbuf, ybuf, xs, wg_f, wu_f, wd_f, wg_bf, wu_bf, wd_bf, gsem, ssem, wsem, zsem):
    j = pl.program_id(0)
    last = pl.num_programs(0) - 1
    tm = EXPERT_TILE
    rc = ROW_CHUNKS
    plane_tokens = n_tok + DUMMY_TOKENS
    slot = j % 2
    other = 1 - slot
    flags = tf_ref[j]
    used = (flags & TILE_USED) != 0
    first = (flags & TILE_FIRST) != 0
    prev_used = (flags & TILE_PREV_USED) != 0
    wslot = (flags // TILE_WSLOT) & 1
    prv = jnp.maximum(j - 1, 0)
    nxt = jnp.minimum(j + 1, last)

    def buf_rows(i):
        return pl.ds(i * rc, rc) if isinstance(i, int) else pl.ds(pl.multiple_of(i * rc, rc), rc)

    def gather_row(i, start, dst):
        tok = sv_ref[start + i] >> PAIR_BITS
        pltpu.make_async_copy(h_hbm.at[pl.ds(pl.multiple_of(tok * rc, rc), rc), :],
                              xbuf.at[dst, buf_rows(i), :], gsem.at[dst]).start()

    def scatter_row(i, start, n_valid, src):
        code = jnp.where(i < n_valid, sv_ref[start + i], n_tok * TOP_K + src * tm + i)
        row = ((code & (TOP_K - 1)) * plane_tokens + (code >> PAIR_BITS)) * rc
        pltpu.make_async_copy(ybuf.at[src, buf_rows(i), :],
                              y_hbm.at[pl.ds(pl.multiple_of(row, rc), rc), :], ssem.at[src]).start()

    def scatter_tile_loop(tile, src):
        start, n_valid = ts_ref[tile], tn_ref[tile]
        lax.fori_loop(0, tm, lambda i, c: (scatter_row(i, start, n_valid, src), c)[1], 0)

    def wait_gather(dst):
        pltpu.make_async_copy(h_hbm.at[pl.ds(0, tm * rc), :], xbuf.at[dst], gsem.at[dst]).wait()

    def wait_scatter(src):
        pltpu.make_async_copy(ybuf.at[src], y_hbm.at[pl.ds(0, tm * rc), :], ssem.at[src]).wait()

    def weight_copies(e, ws):
        return [pltpu.make_async_copy(src.at[e], dst.at[ws], wsem.at[ws])
                for src, dst in ((wg_hbm, wg_f), (wu_hbm, wu_f), (wd_hbm, wd_f))]

    @pl.when(j == 0)
    def _():
        ybuf[1] = jnp.zeros(ybuf.shape[1:], F32)
        fills = [pltpu.make_async_copy(ybuf.at[1, pl.ds(0, DUMMY_TOKENS * rc), :],
                                       y_hbm.at[pl.ds((k * plane_tokens + n_tok) * rc, DUMMY_TOKENS * rc), :],
                                       zsem.at[0]) for k in range(TOP_K)]
        for c in fills:
            c.start()
        for c in fills:
            c.wait()
        for c in weight_copies(te_ref[0], 0):
            c.start()
        start0 = ts_ref[0]
        lax.fori_loop(0, tm, lambda i, c: (gather_row(i, start0, 0), c)[1], 0)

    @pl.when(first)
    def _():
        for c in weight_copies(te_ref[j], wslot):
            c.wait()
        wg_bf[...] = wg_f[wslot].astype(BF16)
        wu_bf[...] = wu_f[wslot].astype(BF16)
        wd_bf[...] = wd_f[wslot].astype(BF16)
        nxt_e = ne_ref[j]

        @pl.when(nxt_e >= 0)
        def _():
            for c in weight_copies(nxt_e, 1 - wslot):
                c.start()

    @pl.when(jnp.logical_and(used, j > 0))
    def _():
        wait_scatter(slot)

    @pl.when(used)
    def _():
        wait_gather(slot)
        xs[...] = jnp.concatenate(
            [xbuf[slot, pl.ds(c, tm, stride=rc), :] for c in range(rc)], axis=1).astype(BF16)
        start_n = ts_ref[nxt]
        for i in range(tm):
            gather_row(i, start_n, other)
        start_p = ts_ref[prv]
        n_valid_p = jnp.where(j > 0, tn_ref[prv], 0)
        for i in range(tm):
            scatter_row(i, start_p, n_valid_p, other)
        x = xs[...]
        hg = jnp.dot(x, wg_bf[...], preferred_element_type=F32)
        hu = jnp.dot(x, wu_bf[...], preferred_element_type=F32)
        h = (hg * jax.nn.sigmoid(hg) * hu).astype(BF16)
        y = jnp.dot(h, wd_bf[...], preferred_element_type=F32)
        for c in range(rc):
            ybuf[slot, pl.ds(c, tm, stride=rc), :] = y[:, c * LANES:(c + 1) * LANES]

    @pl.when(jnp.logical_and(jnp.logical_not(used), prev_used))
    def _():
        wait_gather(slot)
        wait_scatter(slot)
        scatter_tile_loop(prv, other)
        wait_scatter(other)

    @pl.when(jnp.logical_and(used, j == last))
    def _():
        wait_gather(other)
        wait_scatter(other)
        scatter_tile_loop(j, slot)
        wait_scatter(slot)


def _experts(tile_expert, tile_start, tile_rows, tile_flags, next_expert, sorted_pairs, h2_all,
             w_gate, w_up, w_down, n_tok):
    n_tiles = tile_expert.shape[0]
    tm = EXPERT_TILE
    any_spec = pl.BlockSpec(memory_space=pl.ANY)
    grid_spec = pltpu.PrefetchScalarGridSpec(
        num_scalar_prefetch=6,
        grid=(n_tiles,),
        in_specs=[any_spec] * 4,
        out_specs=any_spec,
        scratch_shapes=[
            pltpu.VMEM((2, tm * ROW_CHUNKS, LANES), F32),
            pltpu.VMEM((2, tm * ROW_CHUNKS, LANES), F32),
            pltpu.VMEM((tm, D_MODEL), BF16),
            pltpu.VMEM((2, D_MODEL, D_EXPERT), F32),
            pltpu.VMEM((2, D_MODEL, D_EXPERT), F32),
            pltpu.VMEM((2, D_EXPERT, D_MODEL), F32),
            pltpu.VMEM((D_MODEL, D_EXPERT), BF16),
            pltpu.VMEM((D_MODEL, D_EXPERT), BF16),
            pltpu.VMEM((D_EXPERT, D_MODEL), BF16),
            pltpu.SemaphoreType.DMA((2,)),
            pltpu.SemaphoreType.DMA((2,)),
            pltpu.SemaphoreType.DMA((2,)),
            pltpu.SemaphoreType.DMA((1,)),
        ],
    )
    return pl.pallas_call(
        functools.partial(_experts_kernel, n_tok),
        grid_spec=grid_spec,
        out_shape=jax.ShapeDtypeStruct((TOP_K * (n_tok + DUMMY_TOKENS) * ROW_CHUNKS, LANES), F32),
        compiler_params=_cparams(("arbitrary",)),
        name="experts",
    )(tile_expert, tile_start, tile_rows, tile_flags, next_expert, sorted_pairs, h2_all, w_gate, w_up, w_down)


def _combine_kernel(y_ref, w_ref, base_ref, g2_ref, o_ref):
    tc = base_ref.shape[0]
    w = w_ref[...]
    g2 = g2_ref[0]
    for c in range(ROW_CHUNKS):
        acc = jnp.zeros((tc, LANES), F32)
        for k in range(TOP_K):
            acc = acc + w[:, k:k + 1] * y_ref[k, pl.ds(c, tc, stride=ROW_CHUNKS), :]
        sl = slice(c * LANES, (c + 1) * LANES)
        o_ref[:, sl] = base_ref[:, sl] + g2[:, sl] * acc


def _combine(y_planes, wts, base, gate2, tc, rows_per_mod, block0):
    m = base.shape[0]
    r = gate2.shape[1]
    return pl.pallas_call(
        _combine_kernel,
        grid=(m // tc,),
        in_specs=[
            pl.BlockSpec((TOP_K, tc * ROW_CHUNKS, LANES), lambda i: (0, i + block0, 0)),
            pl.BlockSpec((tc, TOP_K), lambda i: (i, 0)),
            pl.BlockSpec((tc, D_MODEL), lambda i: (i, 0)),
            pl.BlockSpec((1, r, D_MODEL), lambda i: (i // rows_per_mod, 0, 0)),
        ],
        out_specs=pl.BlockSpec((tc, D_MODEL), lambda i: (i, 0)),
        out_shape=jax.ShapeDtypeStruct((m, D_MODEL), F32),
        compiler_params=_cparams(("parallel",)),
        name="combine",
    )(y_planes, wts, base, gate2)


def _moe_routed(h2_all, logits_t, b_router, w_gate, w_up, w_down):
    n_tok = logits_t.shape[1]
    mp = -(-n_tok // ROUTE_TILE) * ROUTE_TILE
    logits_t = jnp.pad(logits_t, ((0, 0), (0, mp - n_tok)))
    eidx, wts, counts = _route(logits_t, b_router, n_tok)
    eidx, wts, counts = eidx[:, :n_tok], wts[:, :n_tok], counts[:, 0]

    tm = EXPERT_TILE
    n_pairs = n_tok * TOP_K
    n_tiles = n_pairs // tm + N_EXPERTS
    pair_code = jnp.arange(n_tok, dtype=I32)[None, :] * TOP_K + jnp.arange(TOP_K, dtype=I32)[:, None]
    sorted_pairs = jnp.sort((eidx * n_pairs + pair_code).reshape(-1)) % n_pairs
    sorted_pairs = jnp.concatenate([sorted_pairs, jnp.zeros((tm,), I32)])
    tiles_e = (counts + tm - 1) // tm
    tile_end = jnp.cumsum(tiles_e)
    tile_begin = tile_end - tiles_e
    dense_begin = jnp.cumsum(counts) - counts
    tj = jnp.arange(n_tiles, dtype=I32)
    total_tiles = tile_end[-1]
    used = tj < total_tiles
    tj_used = jnp.minimum(tj, total_tiles - 1)
    te = jnp.sum((tile_end[None, :] <= tj_used[:, None]).astype(I32), axis=1)
    local = tj - tile_begin[te]
    ts = jnp.where(used, dense_begin[te] + local * tm, 0).astype(I32)
    tn = jnp.where(used, jnp.clip(counts[te] - local * tm, 0, tm), 0).astype(I32)
    first = jnp.logical_and(used, local == 0)
    e_ids = jnp.arange(N_EXPERTS, dtype=I32)
    busy = counts > 0
    wslot = ((jnp.cumsum(busy.astype(I32)) - 1) % 2)[te]
    later = jnp.logical_and(e_ids[None, :] > e_ids[:, None], busy[None, :])
    succ = jnp.min(jnp.where(later, e_ids[None, :], N_EXPERTS), axis=1)
    succ = jnp.where(succ == N_EXPERTS, -1, succ)
    ne = jnp.where(first, succ[te], -1).astype(I32)
    prev_used = jnp.concatenate([jnp.zeros((1,), jnp.bool_), used[:-1]])
    tf = (used.astype(I32) * TILE_USED + first.astype(I32) * TILE_FIRST + wslot * TILE_WSLOT
          + prev_used.astype(I32) * TILE_PREV_USED)
    y_all = _experts(te, ts, tn, tf, ne, sorted_pairs, h2_all, w_gate, w_up, w_down, n_tok)
    return y_all.reshape(TOP_K, (n_tok + DUMMY_TOKENS) * ROW_CHUNKS, LANES), wts.T


def kernel(x_prompt, x_sample, cache_k_win, cache_v_win, state_pool, c_prompt, c_sample, w_ada, b_ada, g_mix, w_in, q_norm, k_norm, w_pool, pool_scale, w_out, g_ffn, w_router, b_router, w_gate, w_up, w_down, ws_gate, ws_up, ws_down):
    depth = w_ada.shape[0]
    assert depth == 1
    l = 0
    nb, seq, _ = x_prompt.shape
    bd, dec_seq, _ = x_sample.shape
    assert dec_seq == 1 and seq % (TOKEN_TILE * 16) == 0
    n_prompt = nb * seq
    n_tok = n_prompt + bd
    n_keep = min(MAX_WINDOW, seq)
    tiles_per_batch = seq // TOKEN_TILE

    n_mod_rows = -(-(nb + bd) // 8) * 8
    c_all = jnp.concatenate([c_prompt, c_sample, jnp.zeros((n_mod_rows - nb - bd, D_MODEL), F32)], axis=0)
    mods = _ada(c_all, w_ada[l], b_ada[l])
    mods_p = [mods[:nb, j * D_MODEL:(j + 1) * D_MODEL].reshape(nb, 1, D_MODEL) for j in range(6)]
    mods_s = [mods[nb:nb + bd, j * D_MODEL:(j + 1) * D_MODEL].reshape(1, bd, D_MODEL) for j in range(6)]

    g_mix_l = g_mix[l].reshape(1, D_MODEL)
    g_ffn_l = g_ffn[l].reshape(1, D_MODEL)
    qn = q_norm[l].reshape(1, HEAD_DIM)
    kn = k_norm[l].reshape(1, HEAD_DIM)
    w_in_bf = w_in[l].astype(BF16)
    w_out_bf = w_out[l].astype(BF16)
    w_pool_bf = w_pool[l].astype(BF16)
    ps = pool_scale[l].reshape(1, D_POOL)
    w_router_t = w_router[l].T
    wsg_bf, wsu_bf, wsd_bf = ws_gate[l].astype(BF16), ws_up[l].astype(BF16), ws_down[l].astype(BF16)

    rope_p = _rope_tables(jnp.arange(seq, dtype=I32))
    rope_s = _rope_tables(jnp.full((bd,), PAST_LEN, I32))
    xp2 = x_prompt.reshape(n_prompt, D_MODEL)
    xs2 = x_sample.reshape(bd, D_MODEL)
    assert n_keep % TOKEN_TILE == 0
    qp, kp, vp, up, kwin_p, vwin_p = _in_proj(xp2, mods_p[0], mods_p[1], g_mix_l, w_in_bf, qn, kn, rope_p,
                                              TOKEN_TILE, tiles_per_batch, tiles_per_batch, n_keep // TOKEN_TILE)
    qs, _, _, us, k_new, v_new = _in_proj(xs2, mods_s[0], mods_s[1], g_mix_l, w_in_bf, qn, kn, rope_s, bd, 1, 1, 1)

    to_seq = lambda t: t.reshape(nb, seq, -1)
    attn_p = _attn_prompt(to_seq(qp), to_seq(kp), to_seq(vp)).reshape(n_prompt, D_ATTN)
    pool_p = _pool_prompt(to_seq(up), w_pool_bf, ps).reshape(n_prompt, D_POOL)
    heads = lambda t: t.reshape(bd, N_HEADS, HEAD_DIM)
    ck, cv = cache_k_win[l], cache_v_win[l]
    k_new, v_new = heads(k_new), heads(v_new)
    attn_s = _attn_decode(heads(qs), k_new, v_new, ck, cv).reshape(bd, D_ATTN)
    pool_s = _pool_decode(us, jnp.swapaxes(state_pool[l], 0, 1), w_pool_bf, ps)
    k_win_s, v_win_s = _window_update(ck, cv, k_new, v_new)

    base_p, h2_all, lg_p = _out_proj(xp2, attn_p, pool_p, mods_p[2], mods_p[3], mods_p[4], mods_p[5],
                                     w_out_bf, g_ffn_l, w_router_t, wsg_bf, wsu_bf, wsd_bf,
                                     TOKEN_TILE, tiles_per_batch, n_tok, None, 0)
    base_s, h2_all, lg_s = _out_proj(xs2, attn_s, pool_s, mods_s[2], mods_s[3], mods_s[4], mods_s[5],
                                     w_out_bf, g_ffn_l, w_router_t, wsg_bf, wsu_bf, wsd_bf,
                                     bd, 1, n_tok, h2_all, n_prompt // bd)

    y_planes, wts_t = _moe_routed(h2_all, jnp.concatenate([lg_p, lg_s], axis=1), b_router[l],
                                  w_gate[l], w_up[l], w_down[l])

    yp = _combine(y_planes, wts_t[:n_prompt], base_p, mods_p[5], COMBINE_TILE, seq // COMBINE_TILE, 0)
    ys = _combine(y_planes, wts_t[n_prompt:n_tok], base_s, mods_s[5], bd, 1, n_prompt // bd)

    y_prompt = yp.reshape(nb, seq, D_MODEL)
    y_sample = ys.reshape(bd, dec_seq, D_MODEL)
    win = lambda t: t.reshape(1, nb, n_keep, N_HEADS, HEAD_DIM)
    pool_p_state = to_seq(up)[:, seq - POOL_STATE:][None]
    pool_s_state = jnp.concatenate([state_pool[l][:, 1:], us[:, None, :]], axis=1)[None]
    return (y_prompt, y_sample, win(kwin_p), win(vwin_p), pool_p_state,
            k_win_s[None], v_win_s[None], pool_s_state)
```

```python
import functools

import jax
import jax.numpy as jnp
from jax import lax
from jax.experimental import pallas as pl
from jax.experimental.pallas import tpu as pltpu

F32 = jnp.float32
BF16 = jnp.bfloat16
I32 = jnp.int32

D_MODEL = 2048
N_HEADS = 8
HEAD_DIM = 128
D_ATTN = N_HEADS * HEAD_DIM
D_POOL = D_MODEL - D_ATTN
D_IN = 3 * D_ATTN + D_POOL
ROPE_DIM = HEAD_DIM // 4
ROPE_HALF = ROPE_DIM // 2
ROPE_THETA = 500000.0
DILATED_GROUPS = ((128, 1), (512, 4), (2048, 16))
QBLOCK = 128
ATTN_SCALE = HEAD_DIM ** -0.5
POOL_WINDOWS = (2, 4, 8, 16)
POOL_GROUP_DIM = D_POOL // len(POOL_WINDOWS)
POOL_STATE = max(POOL_WINDOWS) - 1
POOL_HALO = 16
N_EXPERTS = 64
TOP_K = 8
N_EXPERT_GROUPS = 8
GROUP_SIZE = N_EXPERTS // N_EXPERT_GROUPS
TOPK_GROUPS = 4
D_EXPERT = 512
D_SHARED = 512
ROUTED_SCALE = 2.5
EPS = 1e-6
PAST_LEN = 16384
MAX_WINDOW = 2048

LANES = 128
ROW_CHUNKS = D_MODEL // LANES
NEG_BIG = -1e30
LOG2_E = 1.4426950408889634

TOKEN_TILE = 256
EXPERT_TILE = 256
COMBINE_TILE = 128
ROUTE_TILE = 256
VMEM_LIMIT = 60 * 1024 * 1024


def _cparams(sem, vmem=VMEM_LIMIT):
    return pltpu.CompilerParams(dimension_semantics=sem, vmem_limit_bytes=vmem)


def _ada_kernel(c_ref, w_ref, b_ref, o_ref):
    c = c_ref[...]
    s = (c * jax.nn.sigmoid(c)).astype(BF16)
    o_ref[...] = jnp.dot(s, w_ref[...].astype(BF16), preferred_element_type=F32) + b_ref[...]


def _ada(c_all, w_ada, b_ada):
    rows = c_all.shape[0]
    n = w_ada.shape[1]
    tn = 1024
    return pl.pallas_call(
        _ada_kernel,
        grid=(n // tn,),
        in_specs=[
            pl.BlockSpec((rows, D_MODEL), lambda j: (0, 0)),
            pl.BlockSpec((D_MODEL, tn), lambda j: (0, j)),
            pl.BlockSpec((1, tn), lambda j: (0, j)),
        ],
        out_specs=pl.BlockSpec((rows, tn), lambda j: (0, j)),
        out_shape=jax.ShapeDtypeStruct((rows, n), F32),
        compiler_params=_cparams(("parallel",)),
        name="ada_modulation",
    )(c_all, w_ada, b_ada.reshape(1, n))


def _rms(x):
    return x * lax.rsqrt(jnp.mean(x * x, axis=-1, keepdims=True) + EPS)


def _in_kernel(x_ref, sh_ref, sc_ref, g_ref, w_ref, qn_ref, kn_ref, c_ref, s1_ref, s2_ref,
               q_ref, k_ref, v_ref, u_ref, kw_ref, vw_ref):
    tm = x_ref.shape[0]
    x = x_ref[...]
    h = _rms(x) * g_ref[...] * (1.0 + sc_ref[0]) + sh_ref[0]
    hb = h.astype(BF16)
    cos = c_ref[...]
    s1 = s1_ref[...]
    s2 = s2_ref[...]

    def head_rows(win_ref, hd, val):
        win_ref[pl.ds(hd, tm, stride=N_HEADS), :] = val

    def qk(sec, nrm, out_ref, win_ref):
        z = jnp.dot(hb, w_ref[:, sec * D_ATTN:(sec + 1) * D_ATTN], preferred_element_type=F32)
        for hd in range(N_HEADS):
            sl = slice(hd * HEAD_DIM, (hd + 1) * HEAD_DIM)
            r = _rms(z[:, sl]) * nrm
            r = r * cos + pltpu.roll(r, HEAD_DIM - ROPE_HALF, 1) * s1 + pltpu.roll(r, ROPE_HALF, 1) * s2
            out_ref[:, sl] = r
            if win_ref is not None:
                head_rows(win_ref, hd, r)

    qk(0, qn_ref[...], q_ref, None)
    qk(1, kn_ref[...], k_ref, kw_ref)
    v = jnp.dot(hb, w_ref[:, 2 * D_ATTN:3 * D_ATTN], preferred_element_type=F32)
    v_ref[...] = v
    for hd in range(N_HEADS):
        head_rows(vw_ref, hd, v[:, hd * HEAD_DIM:(hd + 1) * HEAD_DIM])
    u_ref[...] = jnp.dot(hb, w_ref[:, 3 * D_ATTN:], preferred_element_type=F32)


def _in_proj(x2d, shift, scale, g_mix, w_in_bf, q_norm, k_norm, rope, tm, rows_per_mod, rope_tiles, keep_tiles):
    m = x2d.shape[0]
    r = shift.shape[1]
    n_groups = m // tm // rows_per_mod
    skip = rows_per_mod - keep_tiles
    mod_spec = pl.BlockSpec((1, r, D_MODEL), lambda i: (i // rows_per_mod, 0, 0))
    rope_spec = pl.BlockSpec((tm, HEAD_DIM), lambda i: (i % rope_tiles, 0))
    out_spec = pl.BlockSpec((tm, D_ATTN), lambda i: (i, 0))
    out_sd = jax.ShapeDtypeStruct((m, D_ATTN), F32)
    win_spec = pl.BlockSpec(
        (tm * N_HEADS, HEAD_DIM),
        lambda i: ((i // rows_per_mod) * keep_tiles + jnp.maximum(i % rows_per_mod - skip, 0), 0))
    win_sd = jax.ShapeDtypeStruct((n_groups * keep_tiles * tm * N_HEADS, HEAD_DIM), F32)
    return pl.pallas_call(
        _in_kernel,
        grid=(m // tm,),
        in_specs=[
            pl.BlockSpec((tm, D_MODEL), lambda i: (i, 0)),
            mod_spec, mod_spec,
            pl.BlockSpec((1, D_MODEL), lambda i: (0, 0)),
            pl.BlockSpec((D_MODEL, D_IN), lambda i: (0, 0)),
            pl.BlockSpec((1, HEAD_DIM), lambda i: (0, 0)),
            pl.BlockSpec((1, HEAD_DIM), lambda i: (0, 0)),
            rope_spec, rope_spec, rope_spec,
        ],
        out_specs=[out_spec] * 4 + [win_spec] * 2,
        out_shape=[out_sd] * 4 + [win_sd] * 2,
        compiler_params=_cparams(("arbitrary",)),
        name="in_proj",
    )(x2d, shift, scale, g_mix, w_in_bf, q_norm, k_norm, *rope)


def _rope_tables(pos):
    inv_freq = jnp.float32(ROPE_THETA) ** (-jnp.arange(ROPE_HALF, dtype=F32) / ROPE_HALF)
    ang = pos.astype(F32)[:, None] * inv_freq[None, :]
    cos, sin = jnp.cos(ang), jnp.sin(ang)
    t = pos.shape[0]
    rest = HEAD_DIM - ROPE_DIM
    c = jnp.concatenate([cos, cos, jnp.ones((t, rest), F32)], axis=1)
    s1 = jnp.concatenate([-sin, jnp.zeros((t, ROPE_HALF + rest), F32)], axis=1)
    s2 = jnp.concatenate([jnp.zeros((t, ROPE_HALF), F32), sin, jnp.zeros((t, rest), F32)], axis=1)
    return c, s1, s2


ATTN_INTERLEAVE = 4


def _attn_prompt_kernel(q_ref, k_ref, v_ref, o_ref, m_scr, l_scr, acc_scr):
    seq = q_ref.shape[1]
    m_scr[...] = jnp.full(m_scr.shape, NEG_BIG, F32)
    l_scr[...] = jnp.zeros(l_scr.shape, F32)
    acc_scr[...] = jnp.zeros(acc_scr.shape, F32)
    qi = lax.broadcasted_iota(I32, (QBLOCK, 2 * QBLOCK), 0)
    kj = lax.broadcasted_iota(I32, (QBLOCK, 2 * QBLOCK), 1)
    dist = qi + QBLOCK - kj
    band_mask = jnp.logical_and(dist >= 0, dist <= QBLOCK)
    bias_band = jnp.where(band_mask, 0.0, NEG_BIG)
    bias_first = jnp.where(jnp.logical_and(band_mask, kj >= QBLOCK), 0.0, NEG_BIG)
    ones = jnp.ones((2 * QBLOCK, HEAD_DIM), BF16)
    dims = (((1,), (1,)), ((), ()))
    q_scale = ATTN_SCALE * LOG2_E

    for window, dil in DILATED_GROUPS:
        assert window // dil == QBLOCK
        n_units = seq // QBLOCK
        assert n_units % ATTN_INTERLEAVE == 0 and (dil % ATTN_INTERLEAVE == 0 or dil == 1)

        def rows(start, dil=dil):
            return pl.ds(start, QBLOCK) if dil == 1 else pl.ds(start, QBLOCK, stride=dil)

        def body(it, carry, dil=dil, rows=rows):
            loaded = []
            for u in range(ATTN_INTERLEAVE):
                f = it * ATTN_INTERLEAVE + u
                blk, res = f // dil, f % dil
                start = blk * (QBLOCK * dil) + res
                prev = jnp.maximum(start - QBLOCK * dil, res)
                q = (q_ref[0, rows(start), :] * q_scale).astype(BF16)
                k = jnp.concatenate([k_ref[0, rows(prev), :], k_ref[0, rows(start), :]], axis=0).astype(BF16)
                v = jnp.concatenate([v_ref[0, rows(prev), :], v_ref[0, rows(start), :]], axis=0).astype(BF16)
                v_ones = jnp.concatenate([v, ones], axis=1)
                state = (m_scr[rows(start), :], l_scr[rows(start), :], acc_scr[rows(start), :])
                loaded.append((start, blk, q, k, v_ones, state))
            scores = [lax.dot_general(q, k, dims, preferred_element_type=F32)
                      + jnp.where(blk > 0, bias_band, bias_first)
                      for _, blk, q, k, _, _ in loaded]
            probs = []
            for s, (_, _, _, _, _, (m_old, _, _)) in zip(scores, loaded):
                m_new = jnp.maximum(m_old, jnp.max(s, axis=1, keepdims=True))
                p = jnp.exp2(s - jnp.concatenate([m_new, m_new], axis=1)).astype(BF16)
                probs.append((m_new, jnp.exp2(m_old - m_new), p))
            updated = []
            for (m_new, alpha, p), (start, _, _, _, v_ones, (_, l_old, a_old)) in zip(probs, loaded):
                pv = jnp.dot(p, v_ones, preferred_element_type=F32)
                updated.append((start, m_new, alpha * l_old + pv[:, HEAD_DIM:], alpha * a_old + pv[:, :HEAD_DIM]))
            for start, m_new, l_new, a_new in updated:
                m_scr[rows(start), :] = m_new
                l_scr[rows(start), :] = l_new
                acc_scr[rows(start), :] = a_new
            return carry

        lax.fori_loop(0, n_units // ATTN_INTERLEAVE, body, 0)

    o_ref[0] = acc_scr[...] / l_scr[...]


def _attn_prompt(q, k, v):
    b, s, _ = q.shape
    spec = pl.BlockSpec((1, s, HEAD_DIM), lambda bi, hi: (bi, 0, hi))
    return pl.pallas_call(
        _attn_prompt_kernel,
        grid=(b, N_HEADS),
        in_specs=[spec, spec, spec],
        out_specs=spec,
        out_shape=jax.ShapeDtypeStruct((b, s, D_ATTN), F32),
        scratch_shapes=[pltpu.VMEM((s, HEAD_DIM), F32)] * 3,
        compiler_params=_cparams(("parallel", "parallel")),
        name="attn_prompt",
    )(q, k, v)


def _attn_decode_kernel(q_ref, kn_ref, vn_ref, k1_ref, k2_ref, k3_ref, v1_ref, v2_ref, v3_ref, o_ref):
    q = q_ref[0]
    k_new = kn_ref[0]
    v_new = vn_ref[0]
    n_grp = len(DILATED_GROUPS)

    def scores(kb):
        return jnp.sum(kb * q[None], axis=-1, keepdims=True) * ATTN_SCALE

    kcs = [k1_ref[0], k2_ref[0][:, 0], k3_ref[0][:, 0]]
    vcs = [v1_ref[0], v2_ref[0][:, 0], v3_ref[0][:, 0]]
    s_new = jnp.sum(k_new * q, axis=-1, keepdims=True) * ATTN_SCALE
    s_grp = [scores(kb) for kb in kcs]
    m = s_new
    for s in s_grp:
        m = jnp.maximum(m, jnp.max(s, axis=0))
    p_new = jnp.exp(s_new - m)
    den = n_grp * p_new
    num = n_grp * p_new * v_new
    for s, vb in zip(s_grp, vcs):
        p = jnp.exp(s - m[None])
        den = den + jnp.sum(p, axis=0)
        num = num + jnp.sum(p * vb, axis=0)
    o_ref[0] = num / den


def _attn_decode(q, k_new, v_new, cache_k, cache_v):
    bd, n_buf = cache_k.shape[0], cache_k.shape[1]
    assert n_buf == MAX_WINDOW

    def views(cache):
        out, specs = [], []
        for window, dil in DILATED_GROUPS:
            band = window // dil
            assert band == QBLOCK and n_buf % dil == 0 and n_buf // dil >= band
            if dil == 1:
                out.append(cache)
                specs.append(pl.BlockSpec((1, band, N_HEADS, HEAD_DIM),
                                          lambda b, nb=n_buf // band: (b, nb - 1, 0, 0)))
            else:
                out.append(cache.reshape(bd, n_buf // dil, dil, N_HEADS, HEAD_DIM))
                specs.append(pl.BlockSpec((1, band, 1, N_HEADS, HEAD_DIM),
                                          lambda b, nb=n_buf // dil // band: (b, nb - 1, 0, 0, 0)))
        return out, specs

    kv, kspecs = views(cache_k)
    vv, vspecs = views(cache_v)
    tok = pl.BlockSpec((1, N_HEADS, HEAD_DIM), lambda b: (b, 0, 0))
    return pl.pallas_call(
        _attn_decode_kernel,
        grid=(bd,),
        in_specs=[tok, tok, tok] + kspecs + vspecs,
        out_specs=tok,
        out_shape=jax.ShapeDtypeStruct((bd, N_HEADS, HEAD_DIM), F32),
        compiler_params=_cparams(("parallel",)),
        name="attn_decode",
    )(q, k_new, v_new, *kv, *vv)


WINDOW_CHUNK = 1024


def _window_update_kernel(ck_ref, cv_ref, nk_ref, nv_ref, kn_ref, vn_ref, ok_ref, ov_ref):
    rows = ck_ref.shape[1]
    last = pl.program_id(1) == pl.num_programs(1) - 1
    for cur, nxt, new, out in ((ck_ref, nk_ref, kn_ref, ok_ref), (cv_ref, nv_ref, vn_ref, ov_ref)):
        out[0, pl.ds(0, rows - 1)] = cur[0, pl.ds(1, rows - 1)]
        out[0, rows - 1] = jnp.where(last, new[0], nxt[0, 0])


def _window_update(cache_k, cache_v, k_new, v_new):
    bd, n_buf = cache_k.shape[0], cache_k.shape[1]
    rows = WINDOW_CHUNK
    assert n_buf % rows == 0
    tail = cache_k.shape[2:]
    chunk = pl.BlockSpec((1, rows) + tail, lambda b, c: (b, c, 0, 0))
    nxt = pl.BlockSpec((1, 1) + tail, lambda b, c: (b, jnp.minimum((c + 1) * rows, n_buf - 1), 0, 0))
    new = pl.BlockSpec((1,) + tail, lambda b, c: (b, 0, 0))
    sd = jax.ShapeDtypeStruct(cache_k.shape, cache_k.dtype)
    return pl.pallas_call(
        _window_update_kernel,
        grid=(bd, n_buf // rows),
        in_specs=[chunk, chunk, nxt, nxt, new, new],
        out_specs=[chunk, chunk],
        out_shape=[sd, sd],
        compiler_params=_cparams(("parallel", "parallel")),
        name="window_update",
    )(cache_k, cache_v, cache_k, cache_v, k_new, v_new)


def _pool_project(d_groups, wp_ref, ps_ref, out_ref):
    for g, d in enumerate(d_groups):
        sl = slice(g * POOL_GROUP_DIM, (g + 1) * POOL_GROUP_DIM)
        out_ref[:, sl] = jnp.dot(d.astype(BF16), wp_ref[g], preferred_element_type=F32) * ps_ref[:, sl]


def _pool_prompt_kernel(u_ref, prev_ref, wp_ref, ps_ref, o_ref):
    i = pl.program_id(1)
    tp = u_ref.shape[1]
    u = u_ref[0]
    prev = jnp.where(i > 0, prev_ref[0], 0.0)
    ext = jnp.concatenate([prev, u], axis=0)
    pos = i * tp + lax.broadcasted_iota(I32, (tp, 1), 0)
    d_groups = []
    for g, w in enumerate(POOL_WINDOWS):
        sl = slice(g * POOL_GROUP_DIM, (g + 1) * POOL_GROUP_DIM)
        a = ext[:, sl]
        span = 1
        while span < w:
            a = a + pltpu.roll(a, span, 0)
            span *= 2
        win = a[POOL_HALO:, :]
        cnt = jnp.minimum(w, pos + 1).astype(F32)
        d_groups.append(win / cnt - u[:, sl])
    _pool_project(d_groups, wp_ref, ps_ref, o_ref.at[0])


def _pool_prompt(u, w_pool_bf, pool_scale):
    b, s, _ = u.shape
    tp = 512
    halo_blocks = tp // POOL_HALO
    return pl.pallas_call(
        _pool_prompt_kernel,
        grid=(b, s // tp),
        in_specs=[
            pl.BlockSpec((1, tp, D_POOL), lambda bi, i: (bi, i, 0)),
            pl.BlockSpec((1, POOL_HALO, D_POOL), lambda bi, i: (bi, jnp.maximum(i * halo_blocks - 1, 0), 0)),
            pl.BlockSpec(w_pool_bf.shape, lambda bi, i: (0, 0, 0)),
            pl.BlockSpec((1, D_POOL), lambda bi, i: (0, 0)),
        ],
        out_specs=pl.BlockSpec((1, tp, D_POOL), lambda bi, i: (bi, i, 0)),
        out_shape=jax.ShapeDtypeStruct((b, s, D_POOL), F32),
        compiler_params=_cparams(("parallel", "parallel")),
        name="pool_prompt",
    )(u, u, w_pool_bf, pool_scale)


def _pool_decode_kernel(u_ref, st_ref, wp_ref, ps_ref, o_ref):
    u = u_ref[...]
    d_groups = []
    for g, w in enumerate(POOL_WINDOWS):
        sl = slice(g * POOL_GROUP_DIM, (g + 1) * POOL_GROUP_DIM)
        win = u[:, sl]
        for j in range(1, w):
            win = win + st_ref[POOL_STATE - j][:, sl]
        d_groups.append(win / float(w) - u[:, sl])
    _pool_project(d_groups, wp_ref, ps_ref, o_ref)


def _pool_decode(u, state_t, w_pool_bf, pool_scale):
    assert PAST_LEN + 1 >= max(POOL_WINDOWS)
    return pl.pallas_call(
        _pool_decode_kernel,
        out_shape=jax.ShapeDtypeStruct(u.shape, F32),
        compiler_params=_cparams(None),
        name="pool_decode",
    )(u, state_t, w_pool_bf, pool_scale)


def _split_bf16(x):
    hi = x.astype(BF16)
    lo = (x - hi.astype(F32)).astype(BF16)
    return hi, lo


def _out_kernel(x_ref, a_ref, p_ref, g1_ref, sh_ref, sc_ref, g2_ref, wo_ref, gf_ref, wr_ref,
                wsg_ref, wsu_ref, wsd_ref, base_ref, h2_ref, lg_ref):
    tm = x_ref.shape[0]
    mix = (jnp.dot(a_ref[...].astype(BF16), wo_ref[:D_ATTN, :], preferred_element_type=F32)
           + jnp.dot(p_ref[...].astype(BF16), wo_ref[D_ATTN:, :], preferred_element_type=F32))
    x1 = x_ref[...] + g1_ref[0] * mix
    h2 = _rms(x1) * gf_ref[...] * (1.0 + sc_ref[0]) + sh_ref[0]
    for c in range(ROW_CHUNKS):
        h2_ref[pl.ds(c, tm, stride=ROW_CHUNKS), :] = h2[:, c * LANES:(c + 1) * LANES]
    h_hi, h_lo = _split_bf16(h2)
    w_hi, w_lo = _split_bf16(wr_ref[...])
    dims = (((1,), (1,)), ((), ()))
    lg_ref[...] = (lax.dot_general(w_hi, h_hi, dims, preferred_element_type=F32)
                   + lax.dot_general(w_hi, h_lo, dims, preferred_element_type=F32)
                   + lax.dot_general(w_lo, h_hi, dims, preferred_element_type=F32))
    sg = jnp.dot(h_hi, wsg_ref[...], preferred_element_type=F32)
    su = jnp.dot(h_hi, wsu_ref[...], preferred_element_type=F32)
    hs = (sg * jax.nn.sigmoid(sg) * su).astype(BF16)
    base_ref[...] = x1 + g2_ref[0] * jnp.dot(hs, wsd_ref[...], preferred_element_type=F32)


def _out_proj(x2d, attn, pool, gate1, shift2, scale2, gate2, w_out_bf, g_ffn, w_router_t,
              ws_gate_bf, ws_up_bf, ws_down_bf, tm, rows_per_mod, h2_tokens, h2_all, h2_block0):
    m = x2d.shape[0]
    r = gate1.shape[1]
    n_main = m // tm
    h2_rows = h2_tokens * ROW_CHUNKS
    n_fill = 0 if h2_all is not None else pl.cdiv(h2_rows - m * ROW_CHUNKS, tm * ROW_CHUNKS)
    last = n_main - 1
    row = lambda i: jnp.minimum(i, last)
    mod_spec = pl.BlockSpec((1, r, D_MODEL), lambda i: (row(i) // rows_per_mod, 0, 0))
    const = lambda shape: pl.BlockSpec(shape, lambda i: (0,) * len(shape))
    in_specs = [
        pl.BlockSpec((tm, D_MODEL), lambda i: (row(i), 0)),
        pl.BlockSpec((tm, D_ATTN), lambda i: (row(i), 0)),
        pl.BlockSpec((tm, D_POOL), lambda i: (row(i), 0)),
        mod_spec, mod_spec, mod_spec, mod_spec,
        const((D_MODEL, D_MODEL)),
        const((1, D_MODEL)),
        const((N_EXPERTS, D_MODEL)),
        const((D_MODEL, D_SHARED)), const((D_MODEL, D_SHARED)), const((D_SHARED, D_MODEL)),
    ]
    args = [x2d, attn, pool, gate1, shift2, scale2, gate2, w_out_bf, g_ffn, w_router_t,
            ws_gate_bf, ws_up_bf, ws_down_bf]
    aliases = {}
    n_in = len(args)
    if h2_all is not None:
        in_specs.append(pl.BlockSpec(memory_space=pl.ANY))
        args.append(h2_all)
        aliases = {n_in: 1}

    def kernel(*refs):
        refs = refs[:n_in] + refs[len(args):]
        if n_fill == 0:
            _out_kernel(*refs)
            return
        step = pl.program_id(0)
        pl.when(step < n_main)(lambda: _out_kernel(*refs))

        @pl.when(step >= n_main)
        def _():
            h2_ref = refs[n_in + 1]
            h2_ref[...] = jnp.zeros(h2_ref.shape, F32)

    return pl.pallas_call(
        kernel,
        grid=(n_main + n_fill,),
        in_specs=in_specs,
        out_specs=[
            pl.BlockSpec((tm, D_MODEL), lambda i: (row(i), 0)),
            pl.BlockSpec((tm * ROW_CHUNKS, LANES), lambda i: (i + h2_block0, 0)),
            pl.BlockSpec((N_EXPERTS, tm), lambda i: (0, row(i))),
        ],
        out_shape=[
            jax.ShapeDtypeStruct((m, D_MODEL), F32),
            jax.ShapeDtypeStruct((h2_rows, LANES), F32),
            jax.ShapeDtypeStruct((N_EXPERTS, m), F32),
        ],
        input_output_aliases=aliases,
        compiler_params=_cparams(("arbitrary",)),
        name="out_proj",
    )(*args)


def _first_index_of_max(v, iota, size, axis):
    m = jnp.max(v, axis=axis, keepdims=True)
    idx = jnp.min(jnp.where(v == m, iota, size), axis=axis, keepdims=True)
    return m, idx


def _route_kernel(n_valid, lg_ref, b_ref, eidx_ref, wts_ref, cnt_ref, carry):
    i = pl.program_id(0)
    tn = lg_ref.shape[1]

    @pl.when(i == 0)
    def _():
        carry[...] = jnp.zeros(carry.shape, F32)

    scores = jax.nn.sigmoid(lg_ref[...])
    biased = scores + b_ref[...]
    grp = biased.reshape(N_EXPERT_GROUPS, GROUP_SIZE, tn)
    io_g = lax.broadcasted_iota(I32, grp.shape, 1)
    m1, i1 = _first_index_of_max(grp, io_g, GROUP_SIZE, 1)
    m2 = jnp.max(jnp.where(io_g == i1, -jnp.inf, grp), axis=1, keepdims=True)
    gscore = (m1 + m2)[:, 0, :]
    io_n = lax.broadcasted_iota(I32, gscore.shape, 0)
    gsel = jnp.zeros(gscore.shape, jnp.bool_)
    for _ in range(TOPK_GROUPS):
        _, gi = _first_index_of_max(gscore, io_n, N_EXPERT_GROUPS, 0)
        hit = io_n == gi
        gsel = jnp.logical_or(gsel, hit)
        gscore = jnp.where(hit, -jnp.inf, gscore)
    emask = jnp.broadcast_to(gsel[:, None, :], grp.shape).reshape(N_EXPERTS, tn)
    cand = jnp.where(emask, biased, -jnp.inf)
    io_e = lax.broadcasted_iota(I32, cand.shape, 0)
    picked = jnp.zeros(cand.shape, jnp.bool_)
    eidx, sel = [], []
    for _ in range(TOP_K):
        _, ei = _first_index_of_max(cand, io_e, N_EXPERTS, 0)
        hit = io_e == ei
        eidx.append(ei)
        sel.append(jnp.sum(jnp.where(hit, scores, 0.0), axis=0, keepdims=True))
        picked = jnp.logical_or(picked, hit)
        cand = jnp.where(hit, -jnp.inf, cand)
    sel = jnp.concatenate(sel, axis=0)
    eidx = jnp.concatenate(eidx, axis=0)
    wts_ref[...] = sel / jnp.sum(sel, axis=0, keepdims=True) * ROUTED_SCALE
    eidx_ref[...] = eidx

    tok = i * tn + lax.broadcasted_iota(I32, cand.shape, 1)
    mask = jnp.logical_and(picked, tok < n_valid).astype(F32)
    total = carry[...] + jnp.sum(mask, axis=1, keepdims=True)
    carry[...] = total
    cnt_ref[...] = total.astype(I32)


def _route(logits_t, b_router, n_valid):
    mp = logits_t.shape[1]
    tn = ROUTE_TILE
    tok_spec = pl.BlockSpec((TOP_K, tn), lambda i: (0, i))
    return pl.pallas_call(
        functools.partial(_route_kernel, n_valid),
        grid=(mp // tn,),
        in_specs=[
            pl.BlockSpec((N_EXPERTS, tn), lambda i: (0, i)),
            pl.BlockSpec((N_EXPERTS, 1), lambda i: (0, 0)),
        ],
        out_specs=[tok_spec, tok_spec, pl.BlockSpec((N_EXPERTS, LANES), lambda i: (0, 0))],
        out_shape=[
            jax.ShapeDtypeStruct((TOP_K, mp), I32),
            jax.ShapeDtypeStruct((TOP_K, mp), F32),
            jax.ShapeDtypeStruct((N_EXPERTS, LANES), I32),
        ],
        scratch_shapes=[pltpu.VMEM((N_EXPERTS, LANES), F32)],
        compiler_params=_cparams(("arbitrary",)),
        name="route",
    )(logits_t, b_router.reshape(N_EXPERTS, 1))


TILE_USED = 1
TILE_FIRST = 2
TILE_WSLOT = 4
TILE_PREV_USED = 8
PAIR_BITS = 3
assert 1 << PAIR_BITS == TOP_K
DUMMY_TOKENS = 2 * EXPERT_TILE // TOP_K
GATHER_SLOTS = 3


def _experts_kernel(n_tok, te_ref, ts_ref, tn_ref, tf_ref, ne_ref, sv_ref,
                    h_hbm, wg_hbm, wu_hbm, wd_hbm, y_hbm,
                    xbuf, ybuf, xs, wg_f, wu_f, wd_f, wg_bf, wu_bf, wd_bf, gsem, ssem, wsem, zsem):
    j = pl.program_id(0)
    last = pl.num_programs(0) - 1
    tm = EXPERT_TILE
    rc = ROW_CHUNKS
    plane_tokens = n_tok + DUMMY_TOKENS
    slot = j % 2
    other = 1 - slot
    gslot = j % GATHER_SLOTS
    flags = tf_ref[j]
    used = (flags & TILE_USED) != 0
    first = (flags & TILE_FIRST) != 0
    prev_used = (flags & TILE_PREV_USED) != 0
    wslot = (flags // TILE_WSLOT) & 1
    prv = jnp.maximum(j - 1, 0)
    ahead = jnp.minimum(j + GATHER_SLOTS - 1, last)

    def buf_rows(i):
        return pl.ds(i * rc, rc) if isinstance(i, int) else pl.ds(pl.multiple_of(i * rc, rc), rc)

    def gather_row(i, start, dst):
        tok = sv_ref[start + i] >> PAIR_BITS
        pltpu.make_async_copy(h_hbm.at[pl.ds(pl.multiple_of(tok * rc, rc), rc), :],
                              xbuf.at[dst, buf_rows(i), :], gsem.at[dst]).start(priority=0)

    def scatter_row(i, start, n_valid, src):
        code = jnp.where(i < n_valid, sv_ref[start + i], n_tok * TOP_K + src * tm + i)
        row = ((code & (TOP_K - 1)) * plane_tokens + (code >> PAIR_BITS)) * rc
        pltpu.make_async_copy(ybuf.at[src, buf_rows(i), :],
                              y_hbm.at[pl.ds(pl.multiple_of(row, rc), rc), :],
                              ssem.at[src]).start(priority=i % 2 if isinstance(i, int) else 0)

    def scatter_tile_loop(tile, src):
        start, n_valid = ts_ref[tile], tn_ref[tile]
        lax.fori_loop(0, tm, lambda i, c: (scatter_row(i, start, n_valid, src), c)[1], 0)

    def wait_gather(dst):
        pltpu.make_async_copy(h_hbm.at[pl.ds(0, tm * rc), :], xbuf.at[dst], gsem.at[dst]).wait()

    def wait_scatter(src):
        pltpu.make_async_copy(ybuf.at[src], y_hbm.at[pl.ds(0, tm * rc), :], ssem.at[src]).wait()

    def weight_copies(e, ws):
        return [pltpu.make_async_copy(src.at[e], dst.at[ws], wsem.at[ws])
                for src, dst in ((wg_hbm, wg_f), (wu_hbm, wu_f), (wd_hbm, wd_f))]

    @pl.when(j == 0)
    def _():
        ybuf[1] = jnp.zeros(ybuf.shape[1:], F32)
        fills = [pltpu.make_async_copy(ybuf.at[1, pl.ds(0, DUMMY_TOKENS * rc), :],
                                       y_hbm.at[pl.ds((k * plane_tokens + n_tok) * rc, DUMMY_TOKENS * rc), :],
                                       zsem.at[0]) for k in range(TOP_K)]
        for c in fills:
            c.start()
        for c in fills:
            c.wait()
        for c in weight_copies(te_ref[0], 0):
            c.start(priority=1)
        for t in range(GATHER_SLOTS - 1):
            start_t = ts_ref[jnp.minimum(t, last)]
            lax.fori_loop(0, tm, lambda i, c, start_t=start_t, t=t: (gather_row(i, start_t, t), c)[1], 0)

    @pl.when(first)
    def _():
        for c in weight_copies(te_ref[j], wslot):
            c.wait()
        wg_bf[...] = wg_f[wslot].astype(BF16)
        wu_bf[...] = wu_f[wslot].astype(BF16)
        wd_bf[...] = wd_f[wslot].astype(BF16)
        nxt_e = ne_ref[j]

        @pl.when(nxt_e >= 0)
        def _():
            for c in weight_copies(nxt_e, 1 - wslot):
                c.start(priority=1)

    @pl.when(jnp.logical_and(used, j > 0))
    def _():
        wait_scatter(slot)

    @pl.when(used)
    def _():
        wait_gather(gslot)
        xs[...] = jnp.concatenate(
            [xbuf[gslot, pl.ds(c, tm, stride=rc), :] for c in range(rc)], axis=1).astype(BF16)
        start_n = ts_ref[ahead]
        ahead_slot = (j + GATHER_SLOTS - 1) % GATHER_SLOTS
        for i in range(tm):
            gather_row(i, start_n, ahead_slot)
        start_p = ts_ref[prv]
        n_valid_p = jnp.where(j > 0, tn_ref[prv], 0)
        for i in range(tm):
            scatter_row(i, start_p, n_valid_p, other)
        x = xs[...]
        hg = jnp.dot(x, wg_bf[...], preferred_element_type=F32)
        hu = jnp.dot(x, wu_bf[...], preferred_element_type=F32)
        h = (hg * jax.nn.sigmoid(hg) * hu).astype(BF16)
        y = jnp.dot(h, wd_bf[...], preferred_element_type=F32)
        for c in range(rc):
            ybuf[slot, pl.ds(c, tm, stride=rc), :] = y[:, c * LANES:(c + 1) * LANES]

    @pl.when(jnp.logical_and(jnp.logical_not(used), prev_used))
    def _():
        for t in range(GATHER_SLOTS - 1):
            wait_gather((j + t) % GATHER_SLOTS)
        wait_scatter(slot)
        scatter_tile_loop(prv, other)
        wait_scatter(other)

    @pl.when(jnp.logical_and(used, j == last))
    def _():
        for t in range(1, GATHER_SLOTS):
            wait_gather((j + t) % GATHER_SLOTS)
        wait_scatter(other)
        scatter_tile_loop(j, slot)
        wait_scatter(slot)


def _experts(tile_expert, tile_start, tile_rows, tile_flags, next_expert, sorted_pairs, h2_all,
             w_gate, w_up, w_down, n_tok):
    n_tiles = tile_expert.shape[0]
    tm = EXPERT_TILE
    any_spec = pl.BlockSpec(memory_space=pl.ANY)
    grid_spec = pltpu.PrefetchScalarGridSpec(
        num_scalar_prefetch=6,
        grid=(n_tiles,),
        in_specs=[any_spec] * 4,
        out_specs=any_spec,
        scratch_shapes=[
            pltpu.VMEM((GATHER_SLOTS, tm * ROW_CHUNKS, LANES), F32),
            pltpu.VMEM((2, tm * ROW_CHUNKS, LANES), F32),
            pltpu.VMEM((tm, D_MODEL), BF16),
            pltpu.VMEM((2, D_MODEL, D_EXPERT), F32),
            pltpu.VMEM((2, D_MODEL, D_EXPERT), F32),
            pltpu.VMEM((2, D_EXPERT, D_MODEL), F32),
            pltpu.VMEM((D_MODEL, D_EXPERT), BF16),
            pltpu.VMEM((D_MODEL, D_EXPERT), BF16),
            pltpu.VMEM((D_EXPERT, D_MODEL), BF16),
            pltpu.SemaphoreType.DMA((GATHER_SLOTS,)),
            pltpu.SemaphoreType.DMA((2,)),
            pltpu.SemaphoreType.DMA((2,)),
            pltpu.SemaphoreType.DMA((1,)),
        ],
    )
    return pl.pallas_call(
        functools.partial(_experts_kernel, n_tok),
        grid_spec=grid_spec,
        out_shape=jax.ShapeDtypeStruct((TOP_K * (n_tok + DUMMY_TOKENS) * ROW_CHUNKS, LANES), F32),
        compiler_params=_cparams(("arbitrary",)),
        name="experts",
    )(tile_expert, tile_start, tile_rows, tile_flags, next_expert, sorted_pairs, h2_all, w_gate, w_up, w_down)


def _combine_kernel(y_ref, w_ref, base_ref, g2_ref, o_ref):
    tc = base_ref.shape[0]
    w = w_ref[...]
    g2 = g2_ref[0]
    for c in range(ROW_CHUNKS):
        acc = jnp.zeros((tc, LANES), F32)
        for k in range(TOP_K):
            acc = acc + w[:, k:k + 1] * y_ref[k, pl.ds(c, tc, stride=ROW_CHUNKS), :]
        sl = slice(c * LANES, (c + 1) * LANES)
        o_ref[:, sl] = base_ref[:, sl] + g2[:, sl] * acc


def _combine(y_planes, wts, base, gate2, tc, rows_per_mod, block0):
    m = base.shape[0]
    r = gate2.shape[1]
    return pl.pallas_call(
        _combine_kernel,
        grid=(m // tc,),
        in_specs=[
            pl.BlockSpec((TOP_K, tc * ROW_CHUNKS, LANES), lambda i: (0, i + block0, 0)),
            pl.BlockSpec((tc, TOP_K), lambda i: (i, 0)),
            pl.BlockSpec((tc, D_MODEL), lambda i: (i, 0)),
            pl.BlockSpec((1, r, D_MODEL), lambda i: (i // rows_per_mod, 0, 0)),
        ],
        out_specs=pl.BlockSpec((tc, D_MODEL), lambda i: (i, 0)),
        out_shape=jax.ShapeDtypeStruct((m, D_MODEL), F32),
        compiler_params=_cparams(("parallel",)),
        name="combine",
    )(y_planes, wts, base, gate2)


def _moe_routed(h2_all, logits_t, b_router, w_gate, w_up, w_down):
    n_tok = logits_t.shape[1]
    mp = -(-n_tok // ROUTE_TILE) * ROUTE_TILE
    logits_t = jnp.pad(logits_t, ((0, 0), (0, mp - n_tok)))
    eidx, wts, counts = _route(logits_t, b_router, n_tok)
    eidx, wts, counts = eidx[:, :n_tok], wts[:, :n_tok], counts[:, 0]

    tm = EXPERT_TILE
    n_pairs = n_tok * TOP_K
    n_tiles = n_pairs // tm + N_EXPERTS
    pair_code = jnp.arange(n_tok, dtype=I32)[None, :] * TOP_K + jnp.arange(TOP_K, dtype=I32)[:, None]
    (sorted_pairs,) = lax.sort(((eidx * n_pairs + pair_code).reshape(-1),), is_stable=False)
    sorted_pairs = jnp.concatenate([sorted_pairs % n_pairs, jnp.zeros((tm,), I32)])
    tiles_e = (counts + tm - 1) // tm
    tile_end = jnp.cumsum(tiles_e)
    tile_begin = tile_end - tiles_e
    dense_begin = jnp.cumsum(counts) - counts
    tj = jnp.arange(n_tiles, dtype=I32)
    total_tiles = tile_end[-1]
    used = tj < total_tiles
    tj_used = jnp.minimum(tj, total_tiles - 1)[:, None]
    member = jnp.logical_and(tile_begin[None, :] <= tj_used, tj_used < tile_end[None, :])
    of_tile = lambda per_expert: jnp.sum(jnp.where(member, per_expert[None, :], 0), axis=1)
    e_ids = jnp.arange(N_EXPERTS, dtype=I32)
    te = of_tile(e_ids)
    local = tj - of_tile(tile_begin)
    ts = jnp.where(used, of_tile(dense_begin) + local * tm, 0).astype(I32)
    tn = jnp.where(used, jnp.clip(of_tile(counts) - local * tm, 0, tm), 0).astype(I32)
    first = jnp.logical_and(used, local == 0)
    busy = counts > 0
    wslot = of_tile((jnp.cumsum(busy.astype(I32)) - 1) % 2)
    later = jnp.logical_and(e_ids[None, :] > e_ids[:, None], busy[None, :])
    succ = jnp.min(jnp.where(later, e_ids[None, :], N_EXPERTS), axis=1)
    succ = jnp.where(succ == N_EXPERTS, -1, succ)
    ne = jnp.where(first, of_tile(succ), -1).astype(I32)
    prev_used = jnp.concatenate([jnp.zeros((1,), jnp.bool_), used[:-1]])
    tf = (used.astype(I32) * TILE_USED + first.astype(I32) * TILE_FIRST + wslot * TILE_WSLOT
          + prev_used.astype(I32) * TILE_PREV_USED)
    y_all = _experts(te, ts, tn, tf, ne, sorted_pairs, h2_all, w_gate, w_up, w_down, n_tok)
    return y_all.reshape(TOP_K, (n_tok + DUMMY_TOKENS) * ROW_CHUNKS, LANES), wts.T


def kernel(x_prompt, x_sample, cache_k_win, cache_v_win, state_pool, c_prompt, c_sample, w_ada, b_ada, g_mix, w_in, q_norm, k_norm, w_pool, pool_scale, w_out, g_ffn, w_router, b_router, w_gate, w_up, w_down, ws_gate, ws_up, ws_down):
    depth = w_ada.shape[0]
    assert depth == 1
    l = 0
    nb, seq, _ = x_prompt.shape
    bd, dec_seq, _ = x_sample.shape
    assert dec_seq == 1 and seq % (TOKEN_TILE * 16) == 0
    n_prompt = nb * seq
    n_tok = n_prompt + bd
    n_keep = min(MAX_WINDOW, seq)
    tiles_per_batch = seq // TOKEN_TILE

    n_mod_rows = -(-(nb + bd) // 8) * 8
    c_all = jnp.concatenate([c_prompt, c_sample, jnp.zeros((n_mod_rows - nb - bd, D_MODEL), F32)], axis=0)
    mods = _ada(c_all, w_ada[l], b_ada[l])
    mods_p = [mods[:nb, j * D_MODEL:(j + 1) * D_MODEL].reshape(nb, 1, D_MODEL) for j in range(6)]
    mods_s = [mods[nb:nb + bd, j * D_MODEL:(j + 1) * D_MODEL].reshape(1, bd, D_MODEL) for j in range(6)]

    g_mix_l = g_mix[l].reshape(1, D_MODEL)
    g_ffn_l = g_ffn[l].reshape(1, D_MODEL)
    qn = q_norm[l].reshape(1, HEAD_DIM)
    kn = k_norm[l].reshape(1, HEAD_DIM)
    w_in_bf = w_in[l].astype(BF16)
    w_out_bf = w_out[l].astype(BF16)
    w_pool_bf = w_pool[l].astype(BF16)
    ps = pool_scale[l].reshape(1, D_POOL)
    w_router_t = w_router[l].T
    wsg_bf, wsu_bf, wsd_bf = ws_gate[l].astype(BF16), ws_up[l].astype(BF16), ws_down[l].astype(BF16)

    rope_p = _rope_tables(jnp.arange(seq, dtype=I32))
    rope_s = _rope_tables(jnp.full((bd,), PAST_LEN, I32))
    xp2 = x_prompt.reshape(n_prompt, D_MODEL)
    xs2 = x_sample.reshape(bd, D_MODEL)
    assert n_keep % TOKEN_TILE == 0
    qp, kp, vp, up, kwin_p, vwin_p = _in_proj(xp2, mods_p[0], mods_p[1], g_mix_l, w_in_bf, qn, kn, rope_p,
                                              TOKEN_TILE, tiles_per_batch, tiles_per_batch, n_keep // TOKEN_TILE)
    qs, _, _, us, k_new, v_new = _in_proj(xs2, mods_s[0], mods_s[1], g_mix_l, w_in_bf, qn, kn, rope_s, bd, 1, 1, 1)

    to_seq = lambda t: t.reshape(nb, seq, -1)
    attn_p = _attn_prompt(to_seq(qp), to_seq(kp), to_seq(vp)).reshape(n_prompt, D_ATTN)
    pool_p = _pool_prompt(to_seq(up), w_pool_bf, ps).reshape(n_prompt, D_POOL)
    heads = lambda t: t.reshape(bd, N_HEADS, HEAD_DIM)
    ck, cv = cache_k_win[l], cache_v_win[l]
    k_new, v_new = heads(k_new), heads(v_new)
    attn_s = _attn_decode(heads(qs), k_new, v_new, ck, cv).reshape(bd, D_ATTN)
    pool_s = _pool_decode(us, jnp.swapaxes(state_pool[l], 0, 1), w_pool_bf, ps)
    k_win_s, v_win_s = _window_update(ck, cv, k_new, v_new)

    base_p, h2_all, lg_p = _out_proj(xp2, attn_p, pool_p, mods_p[2], mods_p[3], mods_p[4], mods_p[5],
                                     w_out_bf, g_ffn_l, w_router_t, wsg_bf, wsu_bf, wsd_bf,
                                     TOKEN_TILE, tiles_per_batch, n_tok, None, 0)
    base_s, h2_all, lg_s = _out_proj(xs2, attn_s, pool_s, mods_s[2], mods_s[3], mods_s[4], mods_s[5],
                                     w_out_bf, g_ffn_l, w_router_t, wsg_bf, wsu_bf, wsd_bf,
                                     bd, 1, n_tok, h2_all, n_prompt // bd)

    y_planes, wts_t = _moe_routed(h2_all, jnp.concatenate([lg_p, lg_s], axis=1), b_router[l],
                                  w_gate[l], w_up[l], w_down[l])

    yp = _combine(y_planes, wts_t[:n_prompt], base_p, mods_p[5], COMBINE_TILE, seq // COMBINE_TILE, 0)
    ys = _combine(y_planes, wts_t[n_prompt:n_tok], base_s, mods_s[5], bd, 1, n_prompt // bd)

    y_prompt = yp.reshape(nb, seq, D_MODEL)
    y_sample = ys.reshape(bd, dec_seq, D_MODEL)
    win = lambda t: t.reshape(1, nb, n_keep, N_HEADS, HEAD_DIM)
    pool_p_state = to_seq(up)[:, seq - POOL_STATE:][None]
    pool_s_state = jnp.concatenate([state_pool[l][:, 1:], us[:, None, :]], axis=1)[None]
    return (y_prompt, y_sample, win(kwin_p), win(vwin_p), pool_p_state,
            k_win_s[None], v_win_s[None], pool_s_state)
```

```python
import functools

import jax
import jax.numpy as jnp
from jax import lax
from jax.experimental import pallas as pl
from jax.experimental.pallas import tpu as pltpu

F32 = jnp.float32
BF16 = jnp.bfloat16
I32 = jnp.int32

D_MODEL = 2048
N_HEADS = 8
HEAD_DIM = 128
D_ATTN = N_HEADS * HEAD_DIM
D_POOL = D_MODEL - D_ATTN
D_IN = 3 * D_ATTN + D_POOL
ROPE_DIM = HEAD_DIM // 4
ROPE_HALF = ROPE_DIM // 2
ROPE_THETA = 500000.0
DILATED_GROUPS = ((128, 1), (512, 4), (2048, 16))
QBLOCK = 128
ATTN_SCALE = HEAD_DIM ** -0.5
POOL_WINDOWS = (2, 4, 8, 16)
POOL_GROUP_DIM = D_POOL // len(POOL_WINDOWS)
POOL_STATE = max(POOL_WINDOWS) - 1
POOL_HALO = 16
N_EXPERTS = 64
TOP_K = 8
N_EXPERT_GROUPS = 8
GROUP_SIZE = N_EXPERTS // N_EXPERT_GROUPS
TOPK_GROUPS = 4
D_EXPERT = 512
D_SHARED = 512
ROUTED_SCALE = 2.5
EPS = 1e-6
PAST_LEN = 16384
MAX_WINDOW = 2048

LANES = 128
ROW_CHUNKS = D_MODEL // LANES
NEG_BIG = -1e30
LOG2_E = 1.4426950408889634

TOKEN_TILE = 256
EXPERT_TILE = 256
COMBINE_TILE = 128
ROUTE_TILE = 256
VMEM_LIMIT = 60 * 1024 * 1024


def _cparams(sem, vmem=VMEM_LIMIT):
    return pltpu.CompilerParams(dimension_semantics=sem, vmem_limit_bytes=vmem)


def _ada_kernel(c_ref, w_ref, b_ref, o_ref):
    c = c_ref[...]
    s = (c * jax.nn.sigmoid(c)).astype(BF16)
    o_ref[...] = jnp.dot(s, w_ref[...].astype(BF16), preferred_element_type=F32) + b_ref[...]


def _ada(c_all, w_ada, b_ada):
    rows = c_all.shape[0]
    n = w_ada.shape[1]
    tn = 1024
    return pl.pallas_call(
        _ada_kernel,
        grid=(n // tn,),
        in_specs=[
            pl.BlockSpec((rows, D_MODEL), lambda j: (0, 0)),
            pl.BlockSpec((D_MODEL, tn), lambda j: (0, j)),
            pl.BlockSpec((1, tn), lambda j: (0, j)),
        ],
        out_specs=pl.BlockSpec((rows, tn), lambda j: (0, j)),
        out_shape=jax.ShapeDtypeStruct((rows, n), F32),
        compiler_params=_cparams(("parallel",)),
        name="ada_modulation",
    )(c_all, w_ada, b_ada.reshape(1, n))


def _rms(x):
    return x * lax.rsqrt(jnp.mean(x * x, axis=-1, keepdims=True) + EPS)


def _in_kernel(x_ref, sh_ref, sc_ref, g_ref, w_ref, qn_ref, kn_ref, c_ref, s1_ref, s2_ref,
               q_ref, k_ref, v_ref, u_ref, kw_ref, vw_ref):
    tm = x_ref.shape[0]
    x = x_ref[...]
    h = _rms(x) * g_ref[...] * (1.0 + sc_ref[0]) + sh_ref[0]
    hb = h.astype(BF16)
    cos = c_ref[...]
    s1 = s1_ref[...]
    s2 = s2_ref[...]

    def head_rows(win_ref, hd, val):
        win_ref[pl.ds(hd, tm, stride=N_HEADS), :] = val

    def qk(sec, nrm, out_ref, win_ref):
        z = jnp.dot(hb, w_ref[:, sec * D_ATTN:(sec + 1) * D_ATTN], preferred_element_type=F32)
        for hd in range(N_HEADS):
            sl = slice(hd * HEAD_DIM, (hd + 1) * HEAD_DIM)
            r = _rms(z[:, sl]) * nrm
            r = r * cos + pltpu.roll(r, HEAD_DIM - ROPE_HALF, 1) * s1 + pltpu.roll(r, ROPE_HALF, 1) * s2
            out_ref[:, sl] = r
            if win_ref is not None:
                head_rows(win_ref, hd, r)

    qk(0, qn_ref[...], q_ref, None)
    qk(1, kn_ref[...], k_ref, kw_ref)
    v = jnp.dot(hb, w_ref[:, 2 * D_ATTN:3 * D_ATTN], preferred_element_type=F32)
    v_ref[...] = v
    for hd in range(N_HEADS):
        head_rows(vw_ref, hd, v[:, hd * HEAD_DIM:(hd + 1) * HEAD_DIM])
    u_ref[...] = jnp.dot(hb, w_ref[:, 3 * D_ATTN:], preferred_element_type=F32)


def _in_proj(x2d, shift, scale, g_mix, w_in_bf, q_norm, k_norm, rope, tm, rows_per_mod, rope_tiles, keep_tiles):
    m = x2d.shape[0]
    r = shift.shape[1]
    n_groups = m // tm // rows_per_mod
    skip = rows_per_mod - keep_tiles
    mod_spec = pl.BlockSpec((1, r, D_MODEL), lambda i: (i // rows_per_mod, 0, 0))
    rope_spec = pl.BlockSpec((tm, HEAD_DIM), lambda i: (i % rope_tiles, 0))
    out_spec = pl.BlockSpec((tm, D_ATTN), lambda i: (i, 0))
    out_sd = jax.ShapeDtypeStruct((m, D_ATTN), F32)
    win_spec = pl.BlockSpec(
        (tm * N_HEADS, HEAD_DIM),
        lambda i: ((i // rows_per_mod) * keep_tiles + jnp.maximum(i % rows_per_mod - skip, 0), 0))
    win_sd = jax.ShapeDtypeStruct((n_groups * keep_tiles * tm * N_HEADS, HEAD_DIM), F32)
    return pl.pallas_call(
        _in_kernel,
        grid=(m // tm,),
        in_specs=[
            pl.BlockSpec((tm, D_MODEL), lambda i: (i, 0)),
            mod_spec, mod_spec,
            pl.BlockSpec((1, D_MODEL), lambda i: (0, 0)),
            pl.BlockSpec((D_MODEL, D_IN), lambda i: (0, 0)),
            pl.BlockSpec((1, HEAD_DIM), lambda i: (0, 0)),
            pl.BlockSpec((1, HEAD_DIM), lambda i: (0, 0)),
            rope_spec, rope_spec, rope_spec,
        ],
        out_specs=[out_spec] * 4 + [win_spec] * 2,
        out_shape=[out_sd] * 4 + [win_sd] * 2,
        compiler_params=_cparams(("arbitrary",)),
        name="in_proj",
    )(x2d, shift, scale, g_mix, w_in_bf, q_norm, k_norm, *rope)


def _rope_tables(pos):
    inv_freq = jnp.float32(ROPE_THETA) ** (-jnp.arange(ROPE_HALF, dtype=F32) / ROPE_HALF)
    ang = pos.astype(F32)[:, None] * inv_freq[None, :]
    cos, sin = jnp.cos(ang), jnp.sin(ang)
    t = pos.shape[0]
    rest = HEAD_DIM - ROPE_DIM
    c = jnp.concatenate([cos, cos, jnp.ones((t, rest), F32)], axis=1)
    s1 = jnp.concatenate([-sin, jnp.zeros((t, ROPE_HALF + rest), F32)], axis=1)
    s2 = jnp.concatenate([jnp.zeros((t, ROPE_HALF), F32), sin, jnp.zeros((t, rest), F32)], axis=1)
    return c, s1, s2


ATTN_INTERLEAVE = 4
WINDOW_ROWS_PER_STEP = 2


def _attn_prompt_kernel(q_ref, k_ref, v_ref, ck_hbm, cv_hbm, kn_ref, vn_ref, o_ref, ok_hbm, ov_hbm,
                        m_scr, l_scr, acc_scr, wbuf, isem, osem):
    seq = q_ref.shape[1]
    n_buf = wbuf.shape[1]
    step = pl.program_id(0) * pl.num_programs(1) + pl.program_id(1)
    last_step = pl.num_programs(0) * pl.num_programs(1) - 1
    windows = ((ck_hbm, kn_ref, ok_hbm), (cv_hbm, vn_ref, ov_hbm))

    def window_in(row, a):
        return pltpu.make_async_copy(windows[a][0].at[row, pl.ds(1, n_buf - 1)],
                                     wbuf.at[a, pl.ds(0, n_buf - 1)], isem.at[a])

    def window_out(row, a):
        return pltpu.make_async_copy(wbuf.at[a], windows[a][2].at[row], osem.at[a])

    def start_in(row):
        for a in range(2):
            window_in(row, a).start()

    def finish_in_start_out(row):
        for a in range(2):
            window_in(row, a).wait()
            wbuf[a, n_buf - 1] = windows[a][1][row]
            window_out(row, a).start()

    def finish_out(row):
        for a in range(2):
            window_out(row, a).wait()

    row0 = step * WINDOW_ROWS_PER_STEP
    window_phases = [
        lambda: start_in(row0),
        lambda: finish_in_start_out(row0),
        lambda: (finish_out(row0), start_in(row0 + 1)),
        lambda: finish_in_start_out(row0 + 1),
    ]
    assert WINDOW_ROWS_PER_STEP == 2 and len(window_phases) == len(DILATED_GROUPS) + 1

    @pl.when(step > 0)
    def _():
        finish_out(row0 - 1)
    window_phases[0]()
    m_scr[...] = jnp.full(m_scr.shape, NEG_BIG, F32)
    l_scr[...] = jnp.zeros(l_scr.shape, F32)
    acc_scr[...] = jnp.zeros(acc_scr.shape, F32)
    qi = lax.broadcasted_iota(I32, (QBLOCK, 2 * QBLOCK), 0)
    kj = lax.broadcasted_iota(I32, (QBLOCK, 2 * QBLOCK), 1)
    dist = qi + QBLOCK - kj
    band_mask = jnp.logical_and(dist >= 0, dist <= QBLOCK)
    bias_band = jnp.where(band_mask, 0.0, NEG_BIG)
    bias_first = jnp.where(jnp.logical_and(band_mask, kj >= QBLOCK), 0.0, NEG_BIG)
    ones = jnp.ones((2 * QBLOCK, HEAD_DIM), BF16)
    dims = (((1,), (1,)), ((), ()))
    q_scale = ATTN_SCALE * LOG2_E

    for window, dil in DILATED_GROUPS:
        assert window // dil == QBLOCK
        n_units = seq // QBLOCK
        assert n_units % ATTN_INTERLEAVE == 0 and (dil % ATTN_INTERLEAVE == 0 or dil == 1)

        def rows(start, dil=dil):
            return pl.ds(start, QBLOCK) if dil == 1 else pl.ds(start, QBLOCK, stride=dil)

        def body(it, carry, dil=dil, rows=rows):
            loaded = []
            for u in range(ATTN_INTERLEAVE):
                f = it * ATTN_INTERLEAVE + u
                blk, res = f // dil, f % dil
                start = blk * (QBLOCK * dil) + res
                prev = jnp.maximum(start - QBLOCK * dil, res)
                q = (q_ref[0, rows(start), :] * q_scale).astype(BF16)
                k = jnp.concatenate([k_ref[0, rows(prev), :], k_ref[0, rows(start), :]], axis=0).astype(BF16)
                v = jnp.concatenate([v_ref[0, rows(prev), :], v_ref[0, rows(start), :]], axis=0).astype(BF16)
                v_ones = jnp.concatenate([v, ones], axis=1)
                state = (m_scr[rows(start), :], l_scr[rows(start), :], acc_scr[rows(start), :])
                loaded.append((start, blk, q, k, v_ones, state))
            scores = [lax.dot_general(q, k, dims, preferred_element_type=F32)
                      + jnp.where(blk > 0, bias_band, bias_first)
                      for _, blk, q, k, _, _ in loaded]
            probs = []
            for s, (_, _, _, _, _, (m_old, _, _)) in zip(scores, loaded):
                m_new = jnp.maximum(m_old, jnp.max(s, axis=1, keepdims=True))
                p = jnp.exp2(s - jnp.concatenate([m_new, m_new], axis=1)).astype(BF16)
                probs.append((m_new, jnp.exp2(m_old - m_new), p))
            updated = []
            for (m_new, alpha, p), (start, _, _, _, v_ones, (_, l_old, a_old)) in zip(probs, loaded):
                pv = jnp.dot(p, v_ones, preferred_element_type=F32)
                updated.append((start, m_new, alpha * l_old + pv[:, HEAD_DIM:], alpha * a_old + pv[:, :HEAD_DIM]))
            for start, m_new, l_new, a_new in updated:
                m_scr[rows(start), :] = m_new
                l_scr[rows(start), :] = l_new
                acc_scr[rows(start), :] = a_new
            return carry

        lax.fori_loop(0, n_units // ATTN_INTERLEAVE, body, 0)
        window_phases[DILATED_GROUPS.index((window, dil)) + 1]()

    @pl.when(step == last_step)
    def _():
        finish_out(row0 + 1)

    o_ref[0] = acc_scr[...] / l_scr[...]


def _attn_prompt(q, k, v, cache_k, cache_v, k_new, v_new):
    b, s, _ = q.shape
    bd, n_buf = cache_k.shape[0], cache_k.shape[1]
    assert bd == b * N_HEADS * WINDOW_ROWS_PER_STEP
    spec = pl.BlockSpec((1, s, HEAD_DIM), lambda bi, hi: (bi, 0, hi))
    any_spec = pl.BlockSpec(memory_space=pl.ANY)
    new_spec = pl.BlockSpec(k_new.shape, lambda bi, hi: (0, 0, 0))
    win_sd = jax.ShapeDtypeStruct(cache_k.shape, cache_k.dtype)
    return pl.pallas_call(
        _attn_prompt_kernel,
        grid=(b, N_HEADS),
        in_specs=[spec, spec, spec, any_spec, any_spec, new_spec, new_spec],
        out_specs=[spec, any_spec, any_spec],
        out_shape=[jax.ShapeDtypeStruct((b, s, D_ATTN), F32), win_sd, win_sd],
        scratch_shapes=[pltpu.VMEM((s, HEAD_DIM), F32)] * 3 + [
            pltpu.VMEM((2, n_buf) + cache_k.shape[2:], F32),
            pltpu.SemaphoreType.DMA((2,)),
            pltpu.SemaphoreType.DMA((2,)),
        ],
        compiler_params=_cparams(("arbitrary", "arbitrary")),
        name="attn_prompt",
    )(q, k, v, cache_k, cache_v, k_new, v_new)


def _attn_decode_kernel(q_ref, kn_ref, vn_ref, k1_ref, k2_ref, k3_ref, v1_ref, v2_ref, v3_ref, o_ref):
    q = q_ref[0]
    k_new = kn_ref[0]
    v_new = vn_ref[0]
    n_grp = len(DILATED_GROUPS)

    def scores(kb):
        return jnp.sum(kb * q[None], axis=-1, keepdims=True) * ATTN_SCALE

    kcs = [k1_ref[0], k2_ref[0][:, 0], k3_ref[0][:, 0]]
    vcs = [v1_ref[0], v2_ref[0][:, 0], v3_ref[0][:, 0]]
    s_new = jnp.sum(k_new * q, axis=-1, keepdims=True) * ATTN_SCALE
    s_grp = [scores(kb) for kb in kcs]
    m = s_new
    for s in s_grp:
        m = jnp.maximum(m, jnp.max(s, axis=0))
    p_new = jnp.exp(s_new - m)
    den = n_grp * p_new
    num = n_grp * p_new * v_new
    for s, vb in zip(s_grp, vcs):
        p = jnp.exp(s - m[None])
        den = den + jnp.sum(p, axis=0)
        num = num + jnp.sum(p * vb, axis=0)
    o_ref[0] = num / den


def _attn_decode(q, k_new, v_new, cache_k, cache_v):
    bd, n_buf = cache_k.shape[0], cache_k.shape[1]
    assert n_buf == MAX_WINDOW

    def views(cache):
        out, specs = [], []
        for window, dil in DILATED_GROUPS:
            band = window // dil
            assert band == QBLOCK and n_buf % dil == 0 and n_buf // dil >= band
            if dil == 1:
                out.append(cache)
                specs.append(pl.BlockSpec((1, band, N_HEADS, HEAD_DIM),
                                          lambda b, nb=n_buf // band: (b, nb - 1, 0, 0)))
            else:
                out.append(cache.reshape(bd, n_buf // dil, dil, N_HEADS, HEAD_DIM))
                specs.append(pl.BlockSpec((1, band, 1, N_HEADS, HEAD_DIM),
                                          lambda b, nb=n_buf // dil // band: (b, nb - 1, 0, 0, 0)))
        return out, specs

    kv, kspecs = views(cache_k)
    vv, vspecs = views(cache_v)
    tok = pl.BlockSpec((1, N_HEADS, HEAD_DIM), lambda b: (b, 0, 0))
    return pl.pallas_call(
        _attn_decode_kernel,
        grid=(bd,),
        in_specs=[tok, tok, tok] + kspecs + vspecs,
        out_specs=tok,
        out_shape=jax.ShapeDtypeStruct((bd, N_HEADS, HEAD_DIM), F32),
        compiler_params=_cparams(("parallel",)),
        name="attn_decode",
    )(q, k_new, v_new, *kv, *vv)


def _pool_project(d_groups, wp_ref, ps_ref, out_ref):
    for g, d in enumerate(d_groups):
        sl = slice(g * POOL_GROUP_DIM, (g + 1) * POOL_GROUP_DIM)
        out_ref[:, sl] = jnp.dot(d.astype(BF16), wp_ref[g], preferred_element_type=F32) * ps_ref[:, sl]


def _pool_prompt_kernel(u_ref, prev_ref, wp_ref, ps_ref, o_ref):
    i = pl.program_id(1)
    tp = u_ref.shape[1]
    u = u_ref[0]
    prev = jnp.where(i > 0, prev_ref[0], 0.0)
    ext = jnp.concatenate([prev, u], axis=0)
    pos = i * tp + lax.broadcasted_iota(I32, (tp, 1), 0)
    d_groups = []
    for g, w in enumerate(POOL_WINDOWS):
        sl = slice(g * POOL_GROUP_DIM, (g + 1) * POOL_GROUP_DIM)
        a = ext[:, sl]
        span = 1
        while span < w:
            a = a + pltpu.roll(a, span, 0)
            span *= 2
        win = a[POOL_HALO:, :]
        cnt = jnp.minimum(w, pos + 1).astype(F32)
        d_groups.append(win / cnt - u[:, sl])
    _pool_project(d_groups, wp_ref, ps_ref, o_ref.at[0])


def _pool_prompt(u, w_pool_bf, pool_scale):
    b, s, _ = u.shape
    tp = 512
    halo_blocks = tp // POOL_HALO
    return pl.pallas_call(
        _pool_prompt_kernel,
        grid=(b, s // tp),
        in_specs=[
            pl.BlockSpec((1, tp, D_POOL), lambda bi, i: (bi, i, 0)),
            pl.BlockSpec((1, POOL_HALO, D_POOL), lambda bi, i: (bi, jnp.maximum(i * halo_blocks - 1, 0), 0)),
            pl.BlockSpec(w_pool_bf.shape, lambda bi, i: (0, 0, 0)),
            pl.BlockSpec((1, D_POOL), lambda bi, i: (0, 0)),
        ],
        out_specs=pl.BlockSpec((1, tp, D_POOL), lambda bi, i: (bi, i, 0)),
        out_shape=jax.ShapeDtypeStruct((b, s, D_POOL), F32),
        compiler_params=_cparams(("parallel", "parallel")),
        name="pool_prompt",
    )(u, u, w_pool_bf, pool_scale)


def _pool_decode_kernel(u_ref, st_ref, wp_ref, ps_ref, o_ref):
    u = u_ref[...]
    d_groups = []
    for g, w in enumerate(POOL_WINDOWS):
        sl = slice(g * POOL_GROUP_DIM, (g + 1) * POOL_GROUP_DIM)
        win = u[:, sl]
        for j in range(1, w):
            win = win + st_ref[POOL_STATE - j][:, sl]
        d_groups.append(win / float(w) - u[:, sl])
    _pool_project(d_groups, wp_ref, ps_ref, o_ref)


def _pool_decode(u, state_t, w_pool_bf, pool_scale):
    assert PAST_LEN + 1 >= max(POOL_WINDOWS)
    return pl.pallas_call(
        _pool_decode_kernel,
        out_shape=jax.ShapeDtypeStruct(u.shape, F32),
        compiler_params=_cparams(None),
        name="pool_decode",
    )(u, state_t, w_pool_bf, pool_scale)


def _split_bf16(x):
    hi = x.astype(BF16)
    lo = (x - hi.astype(F32)).astype(BF16)
    return hi, lo


def _out_kernel(x_ref, a_ref, p_ref, g1_ref, sh_ref, sc_ref, g2_ref, wo_ref, gf_ref, wr_ref,
                wsg_ref, wsu_ref, wsd_ref, base_ref, h2_ref, lg_ref):
    tm = x_ref.shape[0]
    mix = (jnp.dot(a_ref[...].astype(BF16), wo_ref[:D_ATTN, :], preferred_element_type=F32)
           + jnp.dot(p_ref[...].astype(BF16), wo_ref[D_ATTN:, :], preferred_element_type=F32))
    x1 = x_ref[...] + g1_ref[0] * mix
    h2 = _rms(x1) * gf_ref[...] * (1.0 + sc_ref[0]) + sh_ref[0]
    for c in range(ROW_CHUNKS):
        h2_ref[pl.ds(c, tm, stride=ROW_CHUNKS), :] = h2[:, c * LANES:(c + 1) * LANES]
    h_hi, h_lo = _split_bf16(h2)
    w_hi, w_lo = _split_bf16(wr_ref[...])
    dims = (((1,), (1,)), ((), ()))
    lg_ref[...] = (lax.dot_general(w_hi, h_hi, dims, preferred_element_type=F32)
                   + lax.dot_general(w_hi, h_lo, dims, preferred_element_type=F32)
                   + lax.dot_general(w_lo, h_hi, dims, preferred_element_type=F32))
    sg = jnp.dot(h_hi, wsg_ref[...], preferred_element_type=F32)
    su = jnp.dot(h_hi, wsu_ref[...], preferred_element_type=F32)
    hs = (sg * jax.nn.sigmoid(sg) * su).astype(BF16)
    base_ref[...] = x1 + g2_ref[0] * jnp.dot(hs, wsd_ref[...], preferred_element_type=F32)


def _out_proj(x2d, attn, pool, gate1, shift2, scale2, gate2, w_out_bf, g_ffn, w_router_t,
              ws_gate_bf, ws_up_bf, ws_down_bf, tm, rows_per_mod, h2_tokens, h2_all, h2_block0):
    m = x2d.shape[0]
    r = gate1.shape[1]
    n_main = m // tm
    h2_rows = h2_tokens * ROW_CHUNKS
    n_fill = 0 if h2_all is not None else pl.cdiv(h2_rows - m * ROW_CHUNKS, tm * ROW_CHUNKS)
    last = n_main - 1
    row = lambda i: jnp.minimum(i, last)
    mod_spec = pl.BlockSpec((1, r, D_MODEL), lambda i: (row(i) // rows_per_mod, 0, 0))
    const = lambda shape: pl.BlockSpec(shape, lambda i: (0,) * len(shape))
    in_specs = [
        pl.BlockSpec((tm, D_MODEL), lambda i: (row(i), 0)),
        pl.BlockSpec((tm, D_ATTN), lambda i: (row(i), 0)),
        pl.BlockSpec((tm, D_POOL), lambda i: (row(i), 0)),
        mod_spec, mod_spec, mod_spec, mod_spec,
        const((D_MODEL, D_MODEL)),
        const((1, D_MODEL)),
        const((N_EXPERTS, D_MODEL)),
        const((D_MODEL, D_SHARED)), const((D_MODEL, D_SHARED)), const((D_SHARED, D_MODEL)),
    ]
    args = [x2d, attn, pool, gate1, shift2, scale2, gate2, w_out_bf, g_ffn, w_router_t,
            ws_gate_bf, ws_up_bf, ws_down_bf]
    aliases = {}
    n_in = len(args)
    if h2_all is not None:
        in_specs.append(pl.BlockSpec(memory_space=pl.ANY))
        args.append(h2_all)
        aliases = {n_in: 1}

    def kernel(*refs):
        refs = refs[:n_in] + refs[len(args):]
        if n_fill == 0:
            _out_kernel(*refs)
            return
        step = pl.program_id(0)
        pl.when(step < n_main)(lambda: _out_kernel(*refs))

        @pl.when(step >= n_main)
        def _():
            h2_ref = refs[n_in + 1]
            h2_ref[...] = jnp.zeros(h2_ref.shape, F32)

    return pl.pallas_call(
        kernel,
        grid=(n_main + n_fill,),
        in_specs=in_specs,
        out_specs=[
            pl.BlockSpec((tm, D_MODEL), lambda i: (row(i), 0)),
            pl.BlockSpec((tm * ROW_CHUNKS, LANES), lambda i: (i + h2_block0, 0)),
            pl.BlockSpec((N_EXPERTS, tm), lambda i: (0, row(i))),
        ],
        out_shape=[
            jax.ShapeDtypeStruct((m, D_MODEL), F32),
            jax.ShapeDtypeStruct((h2_rows, LANES), F32),
            jax.ShapeDtypeStruct((N_EXPERTS, m), F32),
        ],
        input_output_aliases=aliases,
        compiler_params=_cparams(("arbitrary",)),
        name="out_proj",
    )(*args)


def _first_index_of_max(v, iota, size, axis):
    m = jnp.max(v, axis=axis, keepdims=True)
    idx = jnp.min(jnp.where(v == m, iota, size), axis=axis, keepdims=True)
    return m, idx


def _route_kernel(n_valid, lg_ref, b_ref, eidx_ref, wts_ref, cnt_ref, carry):
    i = pl.program_id(0)
    tn = lg_ref.shape[1]

    @pl.when(i == 0)
    def _():
        carry[...] = jnp.zeros(carry.shape, F32)

    scores = jax.nn.sigmoid(lg_ref[...])
    biased = scores + b_ref[...]
    grp = biased.reshape(N_EXPERT_GROUPS, GROUP_SIZE, tn)
    io_g = lax.broadcasted_iota(I32, grp.shape, 1)
    m1, i1 = _first_index_of_max(grp, io_g, GROUP_SIZE, 1)
    m2 = jnp.max(jnp.where(io_g == i1, -jnp.inf, grp), axis=1, keepdims=True)
    gscore = (m1 + m2)[:, 0, :]
    io_n = lax.broadcasted_iota(I32, gscore.shape, 0)
    gsel = jnp.zeros(gscore.shape, jnp.bool_)
    for _ in range(TOPK_GROUPS):
        _, gi = _first_index_of_max(gscore, io_n, N_EXPERT_GROUPS, 0)
        hit = io_n == gi
        gsel = jnp.logical_or(gsel, hit)
        gscore = jnp.where(hit, -jnp.inf, gscore)
    emask = jnp.broadcast_to(gsel[:, None, :], grp.shape).reshape(N_EXPERTS, tn)
    cand = jnp.where(emask, biased, -jnp.inf)
    io_e = lax.broadcasted_iota(I32, cand.shape, 0)
    picked = jnp.zeros(cand.shape, jnp.bool_)
    eidx, sel = [], []
    for _ in range(TOP_K):
        _, ei = _first_index_of_max(cand, io_e, N_EXPERTS, 0)
        hit = io_e == ei
        eidx.append(ei)
        sel.append(jnp.sum(jnp.where(hit, scores, 0.0), axis=0, keepdims=True))
        picked = jnp.logical_or(picked, hit)
        cand = jnp.where(hit, -jnp.inf, cand)
    sel = jnp.concatenate(sel, axis=0)
    eidx = jnp.concatenate(eidx, axis=0)
    wts_ref[...] = sel / jnp.sum(sel, axis=0, keepdims=True) * ROUTED_SCALE
    eidx_ref[...] = eidx

    tok = i * tn + lax.broadcasted_iota(I32, cand.shape, 1)
    mask = jnp.logical_and(picked, tok < n_valid).astype(F32)
    total = carry[...] + jnp.sum(mask, axis=1, keepdims=True)
    carry[...] = total
    cnt_ref[...] = total.astype(I32)


def _route(logits_t, b_router, n_valid):
    mp = logits_t.shape[1]
    tn = ROUTE_TILE
    tok_spec = pl.BlockSpec((TOP_K, tn), lambda i: (0, i))
    return pl.pallas_call(
        functools.partial(_route_kernel, n_valid),
        grid=(mp // tn,),
        in_specs=[
            pl.BlockSpec((N_EXPERTS, tn), lambda i: (0, i)),
            pl.BlockSpec((N_EXPERTS, 1), lambda i: (0, 0)),
        ],
        out_specs=[tok_spec, tok_spec, pl.BlockSpec((N_EXPERTS, LANES), lambda i: (0, 0))],
        out_shape=[
            jax.ShapeDtypeStruct((TOP_K, mp), I32),
            jax.ShapeDtypeStruct((TOP_K, mp), F32),
            jax.ShapeDtypeStruct((N_EXPERTS, LANES), I32),
        ],
        scratch_shapes=[pltpu.VMEM((N_EXPERTS, LANES), F32)],
        compiler_params=_cparams(("arbitrary",)),
        name="route",
    )(logits_t, b_router.reshape(N_EXPERTS, 1))


TILE_USED = 1
TILE_FIRST = 2
TILE_WSLOT = 4
TILE_PREV_USED = 8
PAIR_BITS = 3
assert 1 << PAIR_BITS == TOP_K
DUMMY_TOKENS = 2 * EXPERT_TILE // TOP_K
GATHER_SLOTS = 3


def _experts_kernel(n_tok, te_ref, ts_ref, tn_ref, tf_ref, ne_ref, sv_ref,
                    h_hbm, wg_hbm, wu_hbm, wd_hbm, y_hbm,
                    xbuf, ybuf, xs, wg_f, wu_f, wd_f, wg_bf, wu_bf, wd_bf, gsem, ssem, wsem, zsem):
    j = pl.program_id(0)
    last = pl.num_programs(0) - 1
    tm = EXPERT_TILE
    rc = ROW_CHUNKS
    plane_tokens = n_tok + DUMMY_TOKENS
    slot = j % 2
    other = 1 - slot
    gslot = j % GATHER_SLOTS
    flags = tf_ref[j]
    used = (flags & TILE_USED) != 0
    first = (flags & TILE_FIRST) != 0
    prev_used = (flags & TILE_PREV_USED) != 0
    wslot = (flags // TILE_WSLOT) & 1
    prv = jnp.maximum(j - 1, 0)
    ahead = jnp.minimum(j + GATHER_SLOTS - 1, last)

    def buf_rows(i):
        return pl.ds(i * rc, rc) if isinstance(i, int) else pl.ds(pl.multiple_of(i * rc, rc), rc)

    def gather_row(i, start, dst):
        tok = sv_ref[start + i] >> PAIR_BITS
        pltpu.make_async_copy(h_hbm.at[pl.ds(pl.multiple_of(tok * rc, rc), rc), :],
                              xbuf.at[dst, buf_rows(i), :], gsem.at[dst]).start(priority=0)

    def scatter_row(i, start, n_valid, src):
        code = jnp.where(i < n_valid, sv_ref[start + i], n_tok * TOP_K + src * tm + i)
        row = ((code & (TOP_K - 1)) * plane_tokens + (code >> PAIR_BITS)) * rc
        pltpu.make_async_copy(ybuf.at[src, buf_rows(i), :],
                              y_hbm.at[pl.ds(pl.multiple_of(row, rc), rc), :],
                              ssem.at[src]).start(priority=i % 2 if isinstance(i, int) else 0)

    def scatter_tile_loop(tile, src):
        start, n_valid = ts_ref[tile], tn_ref[tile]
        lax.fori_loop(0, tm, lambda i, c: (scatter_row(i, start, n_valid, src), c)[1], 0)

    def wait_gather(dst):
        pltpu.make_async_copy(h_hbm.at[pl.ds(0, tm * rc), :], xbuf.at[dst], gsem.at[dst]).wait()

    def wait_scatter(src):
        pltpu.make_async_copy(ybuf.at[src], y_hbm.at[pl.ds(0, tm * rc), :], ssem.at[src]).wait()

    def weight_copies(e, ws):
        return [pltpu.make_async_copy(src.at[e], dst.at[ws], wsem.at[ws])
                for src, dst in ((wg_hbm, wg_f), (wu_hbm, wu_f), (wd_hbm, wd_f))]

    @pl.when(j == 0)
    def _():
        ybuf[1] = jnp.zeros(ybuf.shape[1:], F32)
        fills = [pltpu.make_async_copy(ybuf.at[1, pl.ds(0, DUMMY_TOKENS * rc), :],
                                       y_hbm.at[pl.ds((k * plane_tokens + n_tok) * rc, DUMMY_TOKENS * rc), :],
                                       zsem.at[0]) for k in range(TOP_K)]
        for c in fills:
            c.start()
        for c in fills:
            c.wait()
        for c in weight_copies(te_ref[0], 0):
            c.start(priority=1)
        for t in range(GATHER_SLOTS - 1):
            start_t = ts_ref[jnp.minimum(t, last)]
            lax.fori_loop(0, tm, lambda i, c, start_t=start_t, t=t: (gather_row(i, start_t, t), c)[1], 0)

    @pl.when(first)
    def _():
        for c in weight_copies(te_ref[j], wslot):
            c.wait()
        wg_bf[...] = wg_f[wslot].astype(BF16)
        wu_bf[...] = wu_f[wslot].astype(BF16)
        wd_bf[...] = wd_f[wslot].astype(BF16)
        nxt_e = ne_ref[j]

        @pl.when(nxt_e >= 0)
        def _():
            for c in weight_copies(nxt_e, 1 - wslot):
                c.start(priority=1)

    @pl.when(jnp.logical_and(used, j > 0))
    def _():
        wait_scatter(slot)

    @pl.when(used)
    def _():
        wait_gather(gslot)
        xs[...] = jnp.concatenate(
            [xbuf[gslot, pl.ds(c, tm, stride=rc), :] for c in range(rc)], axis=1).astype(BF16)
        start_n = ts_ref[ahead]
        ahead_slot = (j + GATHER_SLOTS - 1) % GATHER_SLOTS
        for i in range(tm):
            gather_row(i, start_n, ahead_slot)
        start_p = ts_ref[prv]
        n_valid_p = jnp.where(j > 0, tn_ref[prv], 0)
        for i in range(tm):
            scatter_row(i, start_p, n_valid_p, other)
        x = xs[...]
        hg = jnp.dot(x, wg_bf[...], preferred_element_type=F32)
        hu = jnp.dot(x, wu_bf[...], preferred_element_type=F32)
        h = (hg * jax.nn.sigmoid(hg) * hu).astype(BF16)
        y = jnp.dot(h, wd_bf[...], preferred_element_type=F32)
        for c in range(rc):
            ybuf[slot, pl.ds(c, tm, stride=rc), :] = y[:, c * LANES:(c + 1) * LANES]

    @pl.when(jnp.logical_and(jnp.logical_not(used), prev_used))
    def _():
        for t in range(GATHER_SLOTS - 1):
            wait_gather((j + t) % GATHER_SLOTS)
        wait_scatter(slot)
        scatter_tile_loop(prv, other)
        wait_scatter(other)

    @pl.when(jnp.logical_and(used, j == last))
    def _():
        for t in range(1, GATHER_SLOTS):
            wait_gather((j + t) % GATHER_SLOTS)
        wait_scatter(other)
        scatter_tile_loop(j, slot)
        wait_scatter(slot)


def _experts(tile_expert, tile_start, tile_rows, tile_flags, next_expert, sorted_pairs, h2_all,
             w_gate, w_up, w_down, n_tok):
    n_tiles = tile_expert.shape[0]
    tm = EXPERT_TILE
    any_spec = pl.BlockSpec(memory_space=pl.ANY)
    grid_spec = pltpu.PrefetchScalarGridSpec(
        num_scalar_prefetch=6,
        grid=(n_tiles,),
        in_specs=[any_spec] * 4,
        out_specs=any_spec,
        scratch_shapes=[
            pltpu.VMEM((GATHER_SLOTS, tm * ROW_CHUNKS, LANES), F32),
            pltpu.VMEM((2, tm * ROW_CHUNKS, LANES), F32),
            pltpu.VMEM((tm, D_MODEL), BF16),
            pltpu.VMEM((2, D_MODEL, D_EXPERT), F32),
            pltpu.VMEM((2, D_MODEL, D_EXPERT), F32),
            pltpu.VMEM((2, D_EXPERT, D_MODEL), F32),
            pltpu.VMEM((D_MODEL, D_EXPERT), BF16),
            pltpu.VMEM((D_MODEL, D_EXPERT), BF16),
            pltpu.VMEM((D_EXPERT, D_MODEL), BF16),
            pltpu.SemaphoreType.DMA((GATHER_SLOTS,)),
            pltpu.SemaphoreType.DMA((2,)),
            pltpu.SemaphoreType.DMA((2,)),
            pltpu.SemaphoreType.DMA((1,)),
        ],
    )
    return pl.pallas_call(
        functools.partial(_experts_kernel, n_tok),
        grid_spec=grid_spec,
        out_shape=jax.ShapeDtypeStruct((TOP_K * (n_tok + DUMMY_TOKENS) * ROW_CHUNKS, LANES), F32),
        compiler_params=_cparams(("arbitrary",)),
        name="experts",
    )(tile_expert, tile_start, tile_rows, tile_flags, next_expert, sorted_pairs, h2_all, w_gate, w_up, w_down)


def _combine_kernel(y_ref, w_ref, base_ref, g2_ref, o_ref):
    tc = base_ref.shape[0]
    w = w_ref[...]
    g2 = g2_ref[0]
    for c in range(ROW_CHUNKS):
        acc = jnp.zeros((tc, LANES), F32)
        for k in range(TOP_K):
            acc = acc + w[:, k:k + 1] * y_ref[k, pl.ds(c, tc, stride=ROW_CHUNKS), :]
        sl = slice(c * LANES, (c + 1) * LANES)
        o_ref[:, sl] = base_ref[:, sl] + g2[:, sl] * acc


def _combine(y_planes, wts, base, gate2, tc, rows_per_mod, block0):
    m = base.shape[0]
    r = gate2.shape[1]
    return pl.pallas_call(
        _combine_kernel,
        grid=(m // tc,),
        in_specs=[
            pl.BlockSpec((TOP_K, tc * ROW_CHUNKS, LANES), lambda i: (0, i + block0, 0)),
            pl.BlockSpec((tc, TOP_K), lambda i: (i, 0)),
            pl.BlockSpec((tc, D_MODEL), lambda i: (i, 0)),
            pl.BlockSpec((1, r, D_MODEL), lambda i: (i // rows_per_mod, 0, 0)),
        ],
        out_specs=pl.BlockSpec((tc, D_MODEL), lambda i: (i, 0)),
        out_shape=jax.ShapeDtypeStruct((m, D_MODEL), F32),
        compiler_params=_cparams(("parallel",)),
        name="combine",
    )(y_planes, wts, base, gate2)


def _moe_routed(h2_all, logits_t, b_router, w_gate, w_up, w_down):
    n_tok = logits_t.shape[1]
    mp = -(-n_tok // ROUTE_TILE) * ROUTE_TILE
    logits_t = jnp.pad(logits_t, ((0, 0), (0, mp - n_tok)))
    eidx, wts, counts = _route(logits_t, b_router, n_tok)
    eidx, wts, counts = eidx[:, :n_tok], wts[:, :n_tok], counts[:, 0]

    tm = EXPERT_TILE
    n_pairs = n_tok * TOP_K
    n_tiles = n_pairs // tm + N_EXPERTS
    pair_code = jnp.arange(n_tok, dtype=I32)[None, :] * TOP_K + jnp.arange(TOP_K, dtype=I32)[:, None]
    (sorted_pairs,) = lax.sort(((eidx * n_pairs + pair_code).reshape(-1),), is_stable=False)
    sorted_pairs = jnp.concatenate([sorted_pairs % n_pairs, jnp.zeros((tm,), I32)])
    tiles_e = (counts + tm - 1) // tm
    tile_end = jnp.cumsum(tiles_e)
    tile_begin = tile_end - tiles_e
    dense_begin = jnp.cumsum(counts) - counts
    tj = jnp.arange(n_tiles, dtype=I32)
    total_tiles = tile_end[-1]
    used = tj < total_tiles
    tj_used = jnp.minimum(tj, total_tiles - 1)[:, None]
    member = jnp.logical_and(tile_begin[None, :] <= tj_used, tj_used < tile_end[None, :])
    of_tile = lambda per_expert: jnp.sum(jnp.where(member, per_expert[None, :], 0), axis=1)
    e_ids = jnp.arange(N_EXPERTS, dtype=I32)
    te = of_tile(e_ids)
    local = tj - of_tile(tile_begin)
    ts = jnp.where(used, of_tile(dense_begin) + local * tm, 0).astype(I32)
    tn = jnp.where(used, jnp.clip(of_tile(counts) - local * tm, 0, tm), 0).astype(I32)
    first = jnp.logical_and(used, local == 0)
    busy = counts > 0
    wslot = of_tile((jnp.cumsum(busy.astype(I32)) - 1) % 2)
    later = jnp.logical_and(e_ids[None, :] > e_ids[:, None], busy[None, :])
    succ = jnp.min(jnp.where(later, e_ids[None, :], N_EXPERTS), axis=1)
    succ = jnp.where(succ == N_EXPERTS, -1, succ)
    ne = jnp.where(first, of_tile(succ), -1).astype(I32)
    prev_used = jnp.concatenate([jnp.zeros((1,), jnp.bool_), used[:-1]])
    tf = (used.astype(I32) * TILE_USED + first.astype(I32) * TILE_FIRST + wslot * TILE_WSLOT
          + prev_used.astype(I32) * TILE_PREV_USED)
    y_all = _experts(te, ts, tn, tf, ne, sorted_pairs, h2_all, w_gate, w_up, w_down, n_tok)
    return y_all.reshape(TOP_K, (n_tok + DUMMY_TOKENS) * ROW_CHUNKS, LANES), wts.T


def kernel(x_prompt, x_sample, cache_k_win, cache_v_win, state_pool, c_prompt, c_sample, w_ada, b_ada, g_mix, w_in, q_norm, k_norm, w_pool, pool_scale, w_out, g_ffn, w_router, b_router, w_gate, w_up, w_down, ws_gate, ws_up, ws_down):
    depth = w_ada.shape[0]
    assert depth == 1
    l = 0
    nb, seq, _ = x_prompt.shape
    bd, dec_seq, _ = x_sample.shape
    assert dec_seq == 1 and seq % (TOKEN_TILE * 16) == 0
    n_prompt = nb * seq
    n_tok = n_prompt + bd
    n_keep = min(MAX_WINDOW, seq)
    tiles_per_batch = seq // TOKEN_TILE

    n_mod_rows = -(-(nb + bd) // 8) * 8
    c_all = jnp.concatenate([c_prompt, c_sample, jnp.zeros((n_mod_rows - nb - bd, D_MODEL), F32)], axis=0)
    mods = _ada(c_all, w_ada[l], b_ada[l])
    mods_p = [mods[:nb, j * D_MODEL:(j + 1) * D_MODEL].reshape(nb, 1, D_MODEL) for j in range(6)]
    mods_s = [mods[nb:nb + bd, j * D_MODEL:(j + 1) * D_MODEL].reshape(1, bd, D_MODEL) for j in range(6)]

    g_mix_l = g_mix[l].reshape(1, D_MODEL)
    g_ffn_l = g_ffn[l].reshape(1, D_MODEL)
    qn = q_norm[l].reshape(1, HEAD_DIM)
    kn = k_norm[l].reshape(1, HEAD_DIM)
    w_in_bf = w_in[l].astype(BF16)
    w_out_bf = w_out[l].astype(BF16)
    w_pool_bf = w_pool[l].astype(BF16)
    ps = pool_scale[l].reshape(1, D_POOL)
    w_router_t = w_router[l].T
    wsg_bf, wsu_bf, wsd_bf = ws_gate[l].astype(BF16), ws_up[l].astype(BF16), ws_down[l].astype(BF16)

    rope_p = _rope_tables(jnp.arange(seq, dtype=I32))
    rope_s = _rope_tables(jnp.full((bd,), PAST_LEN, I32))
    xp2 = x_prompt.reshape(n_prompt, D_MODEL)
    xs2 = x_sample.reshape(bd, D_MODEL)
    assert n_keep % TOKEN_TILE == 0
    qp, kp, vp, up, kwin_p, vwin_p = _in_proj(xp2, mods_p[0], mods_p[1], g_mix_l, w_in_bf, qn, kn, rope_p,
                                              TOKEN_TILE, tiles_per_batch, tiles_per_batch, n_keep // TOKEN_TILE)
    qs, _, _, us, k_new, v_new = _in_proj(xs2, mods_s[0], mods_s[1], g_mix_l, w_in_bf, qn, kn, rope_s, bd, 1, 1, 1)

    to_seq = lambda t: t.reshape(nb, seq, -1)
    heads = lambda t: t.reshape(bd, N_HEADS, HEAD_DIM)
    ck, cv = cache_k_win[l], cache_v_win[l]
    k_new, v_new = heads(k_new), heads(v_new)
    attn_p, k_win_s, v_win_s = _attn_prompt(to_seq(qp), to_seq(kp), to_seq(vp), ck, cv, k_new, v_new)
    attn_p = attn_p.reshape(n_prompt, D_ATTN)
    pool_p = _pool_prompt(to_seq(up), w_pool_bf, ps).reshape(n_prompt, D_POOL)
    attn_s = _attn_decode(heads(qs), k_new, v_new, ck, cv).reshape(bd, D_ATTN)
    pool_s = _pool_decode(us, jnp.swapaxes(state_pool[l], 0, 1), w_pool_bf, ps)

    base_p, h2_all, lg_p = _out_proj(xp2, attn_p, pool_p, mods_p[2], mods_p[3], mods_p[4], mods_p[5],
                                     w_out_bf, g_ffn_l, w_router_t, wsg_bf, wsu_bf, wsd_bf,
                                     TOKEN_TILE, tiles_per_batch, n_tok, None, 0)
    base_s, h2_all, lg_s = _out_proj(xs2, attn_s, pool_s, mods_s[2], mods_s[3], mods_s[4], mods_s[5],
                                     w_out_bf, g_ffn_l, w_router_t, wsg_bf, wsu_bf, wsd_bf,
                                     bd, 1, n_tok, h2_all, n_prompt // bd)

    y_planes, wts_t = _moe_routed(h2_all, jnp.concatenate([lg_p, lg_s], axis=1), b_router[l],
                                  w_gate[l], w_up[l], w_down[l])

    yp = _combine(y_planes, wts_t[:n_prompt], base_p, mods_p[5], COMBINE_TILE, seq // COMBINE_TILE, 0)
    ys = _combine(y_planes, wts_t[n_prompt:n_tok], base_s, mods_s[5], bd, 1, n_prompt // bd)

    y_prompt = yp.reshape(nb, seq, D_MODEL)
    y_sample = ys.reshape(bd, dec_seq, D_MODEL)
    win = lambda t: t.reshape(1, nb, n_keep, N_HEADS, HEAD_DIM)
    pool_p_state = to_seq(up)[:, seq - POOL_STATE:][None]
    pool_s_state = jnp.concatenate([state_pool[l][:, 1:], us[:, None, :]], axis=1)[None]
    return (y_prompt, y_sample, win(kwin_p), win(vwin_p), pool_p_state,
            k_win_s[None], v_win_s[None], pool_s_state)
```

```python
import functools

import jax
import jax.numpy as jnp
from jax import lax
from jax.experimental import pallas as pl
from jax.experimental.pallas import tpu as pltpu

F32 = jnp.float32
BF16 = jnp.bfloat16
I32 = jnp.int32

D_MODEL = 2048
N_HEADS = 8
HEAD_DIM = 128
D_ATTN = N_HEADS * HEAD_DIM
D_POOL = D_MODEL - D_ATTN
D_IN = 3 * D_ATTN + D_POOL
ROPE_DIM = HEAD_DIM // 4
ROPE_HALF = ROPE_DIM // 2
ROPE_THETA = 500000.0
DILATED_GROUPS = ((128, 1), (512, 4), (2048, 16))
QBLOCK = 128
ATTN_SCALE = HEAD_DIM ** -0.5
POOL_WINDOWS = (2, 4, 8, 16)
POOL_GROUP_DIM = D_POOL // len(POOL_WINDOWS)
POOL_STATE = max(POOL_WINDOWS) - 1
POOL_HALO = 16
N_EXPERTS = 64
TOP_K = 8
N_EXPERT_GROUPS = 8
GROUP_SIZE = N_EXPERTS // N_EXPERT_GROUPS
TOPK_GROUPS = 4
D_EXPERT = 512
D_SHARED = 512
ROUTED_SCALE = 2.5
EPS = 1e-6
PAST_LEN = 16384
MAX_WINDOW = 2048

LANES = 128
ROW_CHUNKS = D_MODEL // LANES
NEG_BIG = -1e30
LOG2_E = 1.4426950408889634

TOKEN_TILE = 256
EXPERT_TILE = 256
COMBINE_TILE = 128
ROUTE_TILE = 256
VMEM_LIMIT = 60 * 1024 * 1024


def _cparams(sem, vmem=VMEM_LIMIT):
    return pltpu.CompilerParams(dimension_semantics=sem, vmem_limit_bytes=vmem)


def _ada_kernel(c_ref, w_ref, b_ref, o_ref):
    c = c_ref[...]
    s = (c * jax.nn.sigmoid(c)).astype(BF16)
    o_ref[...] = jnp.dot(s, w_ref[...].astype(BF16), preferred_element_type=F32) + b_ref[...]


def _ada(c_all, w_ada, b_ada):
    rows = c_all.shape[0]
    n = w_ada.shape[1]
    tn = 1024
    return pl.pallas_call(
        _ada_kernel,
        grid=(n // tn,),
        in_specs=[
            pl.BlockSpec((rows, D_MODEL), lambda j: (0, 0)),
            pl.BlockSpec((D_MODEL, tn), lambda j: (0, j)),
            pl.BlockSpec((1, tn), lambda j: (0, j)),
        ],
        out_specs=pl.BlockSpec((rows, tn), lambda j: (0, j)),
        out_shape=jax.ShapeDtypeStruct((rows, n), F32),
        compiler_params=_cparams(("parallel",)),
        name="ada_modulation",
    )(c_all, w_ada, b_ada.reshape(1, n))


def _rms(x):
    return x * lax.rsqrt(jnp.mean(x * x, axis=-1, keepdims=True) + EPS)


def _in_kernel(x_ref, sh_ref, sc_ref, g_ref, w_ref, qn_ref, kn_ref, c_ref, s1_ref, s2_ref,
               q_ref, k_ref, v_ref, u_ref, kw_ref, vw_ref):
    tm = x_ref.shape[0]
    x = x_ref[...]
    h = _rms(x) * g_ref[...] * (1.0 + sc_ref[0]) + sh_ref[0]
    hb = h.astype(BF16)
    cos = c_ref[...]
    s1 = s1_ref[...]
    s2 = s2_ref[...]

    def head_rows(win_ref, hd, val):
        win_ref[pl.ds(hd, tm, stride=N_HEADS), :] = val

    def qk(sec, nrm, out_ref, win_ref):
        z = jnp.dot(hb, w_ref[:, sec * D_ATTN:(sec + 1) * D_ATTN], preferred_element_type=F32)
        for hd in range(N_HEADS):
            sl = slice(hd * HEAD_DIM, (hd + 1) * HEAD_DIM)
            r = _rms(z[:, sl]) * nrm
            r = r * cos + pltpu.roll(r, HEAD_DIM - ROPE_HALF, 1) * s1 + pltpu.roll(r, ROPE_HALF, 1) * s2
            out_ref[:, sl] = r
            if win_ref is not None:
                head_rows(win_ref, hd, r)

    qk(0, qn_ref[...], q_ref, None)
    qk(1, kn_ref[...], k_ref, kw_ref)
    v = jnp.dot(hb, w_ref[:, 2 * D_ATTN:3 * D_ATTN], preferred_element_type=F32)
    v_ref[...] = v
    for hd in range(N_HEADS):
        head_rows(vw_ref, hd, v[:, hd * HEAD_DIM:(hd + 1) * HEAD_DIM])
    u_ref[...] = jnp.dot(hb, w_ref[:, 3 * D_ATTN:], preferred_element_type=F32)


def _in_proj(x2d, shift, scale, g_mix, w_in_bf, q_norm, k_norm, rope, tm, rows_per_mod, rope_tiles, keep_tiles):
    m = x2d.shape[0]
    r = shift.shape[1]
    n_groups = m // tm // rows_per_mod
    skip = rows_per_mod - keep_tiles
    mod_spec = pl.BlockSpec((1, r, D_MODEL), lambda i: (i // rows_per_mod, 0, 0))
    rope_spec = pl.BlockSpec((tm, HEAD_DIM), lambda i: (i % rope_tiles, 0))
    out_spec = pl.BlockSpec((tm, D_ATTN), lambda i: (i, 0))
    out_sd = jax.ShapeDtypeStruct((m, D_ATTN), F32)
    win_spec = pl.BlockSpec(
        (tm * N_HEADS, HEAD_DIM),
        lambda i: ((i // rows_per_mod) * keep_tiles + jnp.maximum(i % rows_per_mod - skip, 0), 0))
    win_sd = jax.ShapeDtypeStruct((n_groups * keep_tiles * tm * N_HEADS, HEAD_DIM), F32)
    return pl.pallas_call(
        _in_kernel,
        grid=(m // tm,),
        in_specs=[
            pl.BlockSpec((tm, D_MODEL), lambda i: (i, 0)),
            mod_spec, mod_spec,
            pl.BlockSpec((1, D_MODEL), lambda i: (0, 0)),
            pl.BlockSpec((D_MODEL, D_IN), lambda i: (0, 0)),
            pl.BlockSpec((1, HEAD_DIM), lambda i: (0, 0)),
            pl.BlockSpec((1, HEAD_DIM), lambda i: (0, 0)),
            rope_spec, rope_spec, rope_spec,
        ],
        out_specs=[out_spec] * 4 + [win_spec] * 2,
        out_shape=[out_sd] * 4 + [win_sd] * 2,
        compiler_params=_cparams(("arbitrary",)),
        name="in_proj",
    )(x2d, shift, scale, g_mix, w_in_bf, q_norm, k_norm, *rope)


def _rope_tables(pos):
    inv_freq = jnp.float32(ROPE_THETA) ** (-jnp.arange(ROPE_HALF, dtype=F32) / ROPE_HALF)
    ang = pos.astype(F32)[:, None] * inv_freq[None, :]
    cos, sin = jnp.cos(ang), jnp.sin(ang)
    t = pos.shape[0]
    rest = HEAD_DIM - ROPE_DIM
    c = jnp.concatenate([cos, cos, jnp.ones((t, rest), F32)], axis=1)
    s1 = jnp.concatenate([-sin, jnp.zeros((t, ROPE_HALF + rest), F32)], axis=1)
    s2 = jnp.concatenate([jnp.zeros((t, ROPE_HALF), F32), sin, jnp.zeros((t, rest), F32)], axis=1)
    return c, s1, s2


ATTN_INTERLEAVE = 4
WINDOW_ROWS_PER_STEP = 2


def _attn_prompt_kernel(q_ref, k_ref, v_ref, ck_hbm, cv_hbm, kn_ref, vn_ref, o_ref, ok_hbm, ov_hbm,
                        m_scr, l_scr, acc_scr, wbuf, isem, osem):
    seq = q_ref.shape[1]
    n_buf = wbuf.shape[1]
    step = pl.program_id(0) * pl.num_programs(1) + pl.program_id(1)
    last_step = pl.num_programs(0) * pl.num_programs(1) - 1
    windows = ((ck_hbm, kn_ref, ok_hbm), (cv_hbm, vn_ref, ov_hbm))

    def window_in(row, a):
        return pltpu.make_async_copy(windows[a][0].at[row, pl.ds(1, n_buf - 1)],
                                     wbuf.at[a, pl.ds(0, n_buf - 1)], isem.at[a])

    def window_out(row, a):
        return pltpu.make_async_copy(wbuf.at[a], windows[a][2].at[row], osem.at[a])

    def start_in(row):
        for a in range(2):
            window_in(row, a).start()

    def finish_in_start_out(row):
        for a in range(2):
            window_in(row, a).wait()
            wbuf[a, n_buf - 1] = windows[a][1][row]
            window_out(row, a).start()

    def finish_out(row):
        for a in range(2):
            window_out(row, a).wait()

    row0 = step * WINDOW_ROWS_PER_STEP
    window_phases = [
        lambda: start_in(row0),
        lambda: finish_in_start_out(row0),
        lambda: (finish_out(row0), start_in(row0 + 1)),
        lambda: finish_in_start_out(row0 + 1),
    ]
    assert WINDOW_ROWS_PER_STEP == 2 and len(window_phases) == len(DILATED_GROUPS) + 1

    @pl.when(step > 0)
    def _():
        finish_out(row0 - 1)
    window_phases[0]()
    m_scr[...] = jnp.full(m_scr.shape, NEG_BIG, F32)
    l_scr[...] = jnp.zeros(l_scr.shape, F32)
    acc_scr[...] = jnp.zeros(acc_scr.shape, F32)
    qi = lax.broadcasted_iota(I32, (QBLOCK, 2 * QBLOCK), 0)
    kj = lax.broadcasted_iota(I32, (QBLOCK, 2 * QBLOCK), 1)
    dist = qi + QBLOCK - kj
    band_mask = jnp.logical_and(dist >= 0, dist <= QBLOCK)
    bias_band = jnp.where(band_mask, 0.0, NEG_BIG)
    bias_first = jnp.where(jnp.logical_and(band_mask, kj >= QBLOCK), 0.0, NEG_BIG)
    ones = jnp.ones((2 * QBLOCK, HEAD_DIM), BF16)
    dims = (((1,), (1,)), ((), ()))
    q_scale = ATTN_SCALE * LOG2_E

    for window, dil in DILATED_GROUPS:
        assert window // dil == QBLOCK
        n_units = seq // QBLOCK
        assert n_units % ATTN_INTERLEAVE == 0 and (dil % ATTN_INTERLEAVE == 0 or dil == 1)

        def rows(start, dil=dil):
            return pl.ds(start, QBLOCK) if dil == 1 else pl.ds(start, QBLOCK, stride=dil)

        def body(it, carry, dil=dil, rows=rows):
            loaded = []
            for u in range(ATTN_INTERLEAVE):
                f = it * ATTN_INTERLEAVE + u
                blk, res = f // dil, f % dil
                start = blk * (QBLOCK * dil) + res
                prev = jnp.maximum(start - QBLOCK * dil, res)
                q = (q_ref[0, rows(start), :] * q_scale).astype(BF16)
                k = jnp.concatenate([k_ref[0, rows(prev), :], k_ref[0, rows(start), :]], axis=0).astype(BF16)
                v = jnp.concatenate([v_ref[0, rows(prev), :], v_ref[0, rows(start), :]], axis=0).astype(BF16)
                v_ones = jnp.concatenate([v, ones], axis=1)
                state = (m_scr[rows(start), :], l_scr[rows(start), :], acc_scr[rows(start), :])
                loaded.append((start, blk, q, k, v_ones, state))
            scores = [lax.dot_general(q, k, dims, preferred_element_type=F32)
                      + jnp.where(blk > 0, bias_band, bias_first)
                      for _, blk, q, k, _, _ in loaded]
            probs = []
            for s, (_, _, _, _, _, (m_old, _, _)) in zip(scores, loaded):
                m_new = jnp.maximum(m_old, jnp.max(s, axis=1, keepdims=True))
                p = jnp.exp2(s - jnp.concatenate([m_new, m_new], axis=1)).astype(BF16)
                probs.append((m_new, jnp.exp2(m_old - m_new), p))
            updated = []
            for (m_new, alpha, p), (start, _, _, _, v_ones, (_, l_old, a_old)) in zip(probs, loaded):
                pv = jnp.dot(p, v_ones, preferred_element_type=F32)
                updated.append((start, m_new, alpha * l_old + pv[:, HEAD_DIM:], alpha * a_old + pv[:, :HEAD_DIM]))
            for start, m_new, l_new, a_new in updated:
                m_scr[rows(start), :] = m_new
                l_scr[rows(start), :] = l_new
                acc_scr[rows(start), :] = a_new
            return carry

        lax.fori_loop(0, n_units // ATTN_INTERLEAVE, body, 0)
        window_phases[DILATED_GROUPS.index((window, dil)) + 1]()

    @pl.when(step == last_step)
    def _():
        finish_out(row0 + 1)

    o_ref[0] = acc_scr[...] / l_scr[...]


def _attn_prompt(q, k, v, cache_k, cache_v, k_new, v_new):
    b, s, _ = q.shape
    bd, n_buf = cache_k.shape[0], cache_k.shape[1]
    assert bd == b * N_HEADS * WINDOW_ROWS_PER_STEP
    spec = pl.BlockSpec((1, s, HEAD_DIM), lambda bi, hi: (bi, 0, hi))
    any_spec = pl.BlockSpec(memory_space=pl.ANY)
    new_spec = pl.BlockSpec(k_new.shape, lambda bi, hi: (0, 0, 0))
    win_sd = jax.ShapeDtypeStruct(cache_k.shape, cache_k.dtype)
    return pl.pallas_call(
        _attn_prompt_kernel,
        grid=(b, N_HEADS),
        in_specs=[spec, spec, spec, any_spec, any_spec, new_spec, new_spec],
        out_specs=[spec, any_spec, any_spec],
        out_shape=[jax.ShapeDtypeStruct((b, s, D_ATTN), F32), win_sd, win_sd],
        scratch_shapes=[pltpu.VMEM((s, HEAD_DIM), F32)] * 3 + [
            pltpu.VMEM((2, n_buf) + cache_k.shape[2:], F32),
            pltpu.SemaphoreType.DMA((2,)),
            pltpu.SemaphoreType.DMA((2,)),
        ],
        compiler_params=_cparams(("arbitrary", "arbitrary")),
        name="attn_prompt",
    )(q, k, v, cache_k, cache_v, k_new, v_new)


def _attn_decode_kernel(q_ref, kn_ref, vn_ref, k1_ref, k2_ref, k3_ref, v1_ref, v2_ref, v3_ref, o_ref):
    q = q_ref[0]
    k_new = kn_ref[0]
    v_new = vn_ref[0]
    n_grp = len(DILATED_GROUPS)

    def scores(kb):
        return jnp.sum(kb * q[None], axis=-1, keepdims=True) * ATTN_SCALE

    kcs = [k1_ref[0], k2_ref[0][:, 0], k3_ref[0][:, 0]]
    vcs = [v1_ref[0], v2_ref[0][:, 0], v3_ref[0][:, 0]]
    s_new = jnp.sum(k_new * q, axis=-1, keepdims=True) * ATTN_SCALE
    s_grp = [scores(kb) for kb in kcs]
    m = s_new
    for s in s_grp:
        m = jnp.maximum(m, jnp.max(s, axis=0))
    p_new = jnp.exp(s_new - m)
    den = n_grp * p_new
    num = n_grp * p_new * v_new
    for s, vb in zip(s_grp, vcs):
        p = jnp.exp(s - m[None])
        den = den + jnp.sum(p, axis=0)
        num = num + jnp.sum(p * vb, axis=0)
    o_ref[0] = num / den


def _attn_decode(q, k_new, v_new, cache_k, cache_v):
    bd, n_buf = cache_k.shape[0], cache_k.shape[1]
    assert n_buf == MAX_WINDOW

    def views(cache):
        out, specs = [], []
        for window, dil in DILATED_GROUPS:
            band = window // dil
            assert band == QBLOCK and n_buf % dil == 0 and n_buf // dil >= band
            if dil == 1:
                out.append(cache)
                specs.append(pl.BlockSpec((1, band, N_HEADS, HEAD_DIM),
                                          lambda b, nb=n_buf // band: (b, nb - 1, 0, 0)))
            else:
                out.append(cache.reshape(bd, n_buf // dil, dil, N_HEADS, HEAD_DIM))
                specs.append(pl.BlockSpec((1, band, 1, N_HEADS, HEAD_DIM),
                                          lambda b, nb=n_buf // dil // band: (b, nb - 1, 0, 0, 0)))
        return out, specs

    kv, kspecs = views(cache_k)
    vv, vspecs = views(cache_v)
    tok = pl.BlockSpec((1, N_HEADS, HEAD_DIM), lambda b: (b, 0, 0))
    return pl.pallas_call(
        _attn_decode_kernel,
        grid=(bd,),
        in_specs=[tok, tok, tok] + kspecs + vspecs,
        out_specs=tok,
        out_shape=jax.ShapeDtypeStruct((bd, N_HEADS, HEAD_DIM), F32),
        compiler_params=_cparams(("parallel",)),
        name="attn_decode",
    )(q, k_new, v_new, *kv, *vv)


def _pool_project(d_groups, wp_ref, ps_ref, out_ref):
    for g, d in enumerate(d_groups):
        sl = slice(g * POOL_GROUP_DIM, (g + 1) * POOL_GROUP_DIM)
        out_ref[:, sl] = jnp.dot(d.astype(BF16), wp_ref[g], preferred_element_type=F32) * ps_ref[:, sl]


def _pool_prompt_kernel(u_ref, prev_ref, wp_ref, ps_ref, o_ref):
    i = pl.program_id(1)
    tp = u_ref.shape[1]
    u = u_ref[0]
    prev = jnp.where(i > 0, prev_ref[0], 0.0)
    ext = jnp.concatenate([prev, u], axis=0)
    pos = i * tp + lax.broadcasted_iota(I32, (tp, 1), 0)
    d_groups = []
    for g, w in enumerate(POOL_WINDOWS):
        sl = slice(g * POOL_GROUP_DIM, (g + 1) * POOL_GROUP_DIM)
        a = ext[:, sl]
        span = 1
        while span < w:
            a = a + pltpu.roll(a, span, 0)
            span *= 2
        win = a[POOL_HALO:, :]
        cnt = jnp.minimum(w, pos + 1).astype(F32)
        d_groups.append(win / cnt - u[:, sl])
    _pool_project(d_groups, wp_ref, ps_ref, o_ref.at[0])


def _pool_prompt(u, w_pool_bf, pool_scale):
    b, s, _ = u.shape
    tp = 512
    halo_blocks = tp // POOL_HALO
    return pl.pallas_call(
        _pool_prompt_kernel,
        grid=(b, s // tp),
        in_specs=[
            pl.BlockSpec((1, tp, D_POOL), lambda bi, i: (bi, i, 0)),
            pl.BlockSpec((1, POOL_HALO, D_POOL), lambda bi, i: (bi, jnp.maximum(i * halo_blocks - 1, 0), 0)),
            pl.BlockSpec(w_pool_bf.shape, lambda bi, i: (0, 0, 0)),
            pl.BlockSpec((1, D_POOL), lambda bi, i: (0, 0)),
        ],
        out_specs=pl.BlockSpec((1, tp, D_POOL), lambda bi, i: (bi, i, 0)),
        out_shape=jax.ShapeDtypeStruct((b, s, D_POOL), F32),
        compiler_params=_cparams(("parallel", "parallel")),
        name="pool_prompt",
    )(u, u, w_pool_bf, pool_scale)


def _pool_decode_kernel(u_ref, st_ref, wp_ref, ps_ref, o_ref):
    u = u_ref[...]
    d_groups = []
    for g, w in enumerate(POOL_WINDOWS):
        sl = slice(g * POOL_GROUP_DIM, (g + 1) * POOL_GROUP_DIM)
        win = u[:, sl]
        for j in range(1, w):
            win = win + st_ref[POOL_STATE - j][:, sl]
        d_groups.append(win / float(w) - u[:, sl])
    _pool_project(d_groups, wp_ref, ps_ref, o_ref)


def _pool_decode(u, state_t, w_pool_bf, pool_scale):
    assert PAST_LEN + 1 >= max(POOL_WINDOWS)
    return pl.pallas_call(
        _pool_decode_kernel,
        out_shape=jax.ShapeDtypeStruct(u.shape, F32),
        compiler_params=_cparams(None),
        name="pool_decode",
    )(u, state_t, w_pool_bf, pool_scale)


def _split_bf16(x):
    hi = x.astype(BF16)
    lo = (x - hi.astype(F32)).astype(BF16)
    return hi, lo


def _out_kernel(x_ref, a_ref, p_ref, g1_ref, sh_ref, sc_ref, g2_ref, wo_ref, gf_ref, wr_ref,
                wsg_ref, wsu_ref, wsd_ref, base_ref, h2_ref, lg_ref):
    tm = x_ref.shape[0]
    mix = (jnp.dot(a_ref[...].astype(BF16), wo_ref[:D_ATTN, :], preferred_element_type=F32)
           + jnp.dot(p_ref[...].astype(BF16), wo_ref[D_ATTN:, :], preferred_element_type=F32))
    x1 = x_ref[...] + g1_ref[0] * mix
    h2 = _rms(x1) * gf_ref[...] * (1.0 + sc_ref[0]) + sh_ref[0]
    for c in range(ROW_CHUNKS):
        h2_ref[pl.ds(c, tm, stride=ROW_CHUNKS), :] = h2[:, c * LANES:(c + 1) * LANES]
    h_hi, h_lo = _split_bf16(h2)
    w_hi, w_lo = _split_bf16(wr_ref[...])
    dims = (((1,), (1,)), ((), ()))
    lg_ref[...] = (lax.dot_general(w_hi, h_hi, dims, preferred_element_type=F32)
                   + lax.dot_general(w_hi, h_lo, dims, preferred_element_type=F32)
                   + lax.dot_general(w_lo, h_hi, dims, preferred_element_type=F32))
    sg = jnp.dot(h_hi, wsg_ref[...], preferred_element_type=F32)
    su = jnp.dot(h_hi, wsu_ref[...], preferred_element_type=F32)
    hs = (sg * jax.nn.sigmoid(sg) * su).astype(BF16)
    base_ref[...] = x1 + g2_ref[0] * jnp.dot(hs, wsd_ref[...], preferred_element_type=F32)


def _out_proj(x2d, attn, pool, gate1, shift2, scale2, gate2, w_out_bf, g_ffn, w_router_t,
              ws_gate_bf, ws_up_bf, ws_down_bf, tm, rows_per_mod, h2_tokens, h2_all, h2_block0):
    m = x2d.shape[0]
    r = gate1.shape[1]
    n_main = m // tm
    h2_rows = h2_tokens * ROW_CHUNKS
    n_fill = 0 if h2_all is not None else pl.cdiv(h2_rows - m * ROW_CHUNKS, tm * ROW_CHUNKS)
    last = n_main - 1
    row = lambda i: jnp.minimum(i, last)
    mod_spec = pl.BlockSpec((1, r, D_MODEL), lambda i: (row(i) // rows_per_mod, 0, 0))
    const = lambda shape: pl.BlockSpec(shape, lambda i: (0,) * len(shape))
    in_specs = [
        pl.BlockSpec((tm, D_MODEL), lambda i: (row(i), 0)),
        pl.BlockSpec((tm, D_ATTN), lambda i: (row(i), 0)),
        pl.BlockSpec((tm, D_POOL), lambda i: (row(i), 0)),
        mod_spec, mod_spec, mod_spec, mod_spec,
        const((D_MODEL, D_MODEL)),
        const((1, D_MODEL)),
        const((N_EXPERTS, D_MODEL)),
        const((D_MODEL, D_SHARED)), const((D_MODEL, D_SHARED)), const((D_SHARED, D_MODEL)),
    ]
    args = [x2d, attn, pool, gate1, shift2, scale2, gate2, w_out_bf, g_ffn, w_router_t,
            ws_gate_bf, ws_up_bf, ws_down_bf]
    aliases = {}
    n_in = len(args)
    if h2_all is not None:
        in_specs.append(pl.BlockSpec(memory_space=pl.ANY))
        args.append(h2_all)
        aliases = {n_in: 1}

    def kernel(*refs):
        refs = refs[:n_in] + refs[len(args):]
        if n_fill == 0:
            _out_kernel(*refs)
            return
        step = pl.program_id(0)
        pl.when(step < n_main)(lambda: _out_kernel(*refs))

        @pl.when(step >= n_main)
        def _():
            h2_ref = refs[n_in + 1]
            h2_ref[...] = jnp.zeros(h2_ref.shape, F32)

    return pl.pallas_call(
        kernel,
        grid=(n_main + n_fill,),
        in_specs=in_specs,
        out_specs=[
            pl.BlockSpec((tm, D_MODEL), lambda i: (row(i), 0)),
            pl.BlockSpec((tm * ROW_CHUNKS, LANES), lambda i: (i + h2_block0, 0)),
            pl.BlockSpec((N_EXPERTS, tm), lambda i: (0, row(i))),
        ],
        out_shape=[
            jax.ShapeDtypeStruct((m, D_MODEL), F32),
            jax.ShapeDtypeStruct((h2_rows, LANES), F32),
            jax.ShapeDtypeStruct((N_EXPERTS, m), F32),
        ],
        input_output_aliases=aliases,
        compiler_params=_cparams(("arbitrary",)),
        name="out_proj",
    )(*args)


def _first_index_of_max(v, iota, size, axis):
    m = jnp.max(v, axis=axis, keepdims=True)
    idx = jnp.min(jnp.where(v == m, iota, size), axis=axis, keepdims=True)
    return m, idx


def _route_kernel(n_valid, lg_ref, b_ref, eidx_ref, wts_ref, cnt_ref, carry):
    i = pl.program_id(0)
    tn = lg_ref.shape[1]

    @pl.when(i == 0)
    def _():
        carry[...] = jnp.zeros(carry.shape, F32)

    scores = jax.nn.sigmoid(lg_ref[...])
    biased = scores + b_ref[...]
    grp = biased.reshape(N_EXPERT_GROUPS, GROUP_SIZE, tn)
    io_g = lax.broadcasted_iota(I32, grp.shape, 1)
    m1, i1 = _first_index_of_max(grp, io_g, GROUP_SIZE, 1)
    m2 = jnp.max(jnp.where(io_g == i1, -jnp.inf, grp), axis=1, keepdims=True)
    gscore = (m1 + m2)[:, 0, :]
    io_n = lax.broadcasted_iota(I32, gscore.shape, 0)
    gsel = jnp.zeros(gscore.shape, jnp.bool_)
    for _ in range(TOPK_GROUPS):
        _, gi = _first_index_of_max(gscore, io_n, N_EXPERT_GROUPS, 0)
        hit = io_n == gi
        gsel = jnp.logical_or(gsel, hit)
        gscore = jnp.where(hit, -jnp.inf, gscore)
    emask = jnp.broadcast_to(gsel[:, None, :], grp.shape).reshape(N_EXPERTS, tn)
    cand = jnp.where(emask, biased, -jnp.inf)
    io_e = lax.broadcasted_iota(I32, cand.shape, 0)
    picked = jnp.zeros(cand.shape, jnp.bool_)
    eidx, sel = [], []
    for _ in range(TOP_K):
        _, ei = _first_index_of_max(cand, io_e, N_EXPERTS, 0)
        hit = io_e == ei
        eidx.append(ei)
        sel.append(jnp.sum(jnp.where(hit, scores, 0.0), axis=0, keepdims=True))
        picked = jnp.logical_or(picked, hit)
        cand = jnp.where(hit, -jnp.inf, cand)
    sel = jnp.concatenate(sel, axis=0)
    eidx = jnp.concatenate(eidx, axis=0)
    wts_ref[...] = sel / jnp.sum(sel, axis=0, keepdims=True) * ROUTED_SCALE
    eidx_ref[...] = eidx

    tok = i * tn + lax.broadcasted_iota(I32, cand.shape, 1)
    mask = jnp.logical_and(picked, tok < n_valid).astype(F32)
    total = carry[...] + jnp.sum(mask, axis=1, keepdims=True)
    carry[...] = total
    cnt_ref[...] = total.astype(I32)


def _route(logits_t, b_router, n_valid):
    mp = logits_t.shape[1]
    tn = ROUTE_TILE
    tok_spec = pl.BlockSpec((TOP_K, tn), lambda i: (0, i))
    return pl.pallas_call(
        functools.partial(_route_kernel, n_valid),
        grid=(mp // tn,),
        in_specs=[
            pl.BlockSpec((N_EXPERTS, tn), lambda i: (0, i)),
            pl.BlockSpec((N_EXPERTS, 1), lambda i: (0, 0)),
        ],
        out_specs=[tok_spec, tok_spec, pl.BlockSpec((N_EXPERTS, LANES), lambda i: (0, 0))],
        out_shape=[
            jax.ShapeDtypeStruct((TOP_K, mp), I32),
            jax.ShapeDtypeStruct((TOP_K, mp), F32),
            jax.ShapeDtypeStruct((N_EXPERTS, LANES), I32),
        ],
        scratch_shapes=[pltpu.VMEM((N_EXPERTS, LANES), F32)],
        compiler_params=_cparams(("arbitrary",)),
        name="route",
    )(logits_t, b_router.reshape(N_EXPERTS, 1))


TILE_USED = 1
TILE_FIRST = 2
TILE_WSLOT = 4
TILE_PREV_USED = 8
PAIR_BITS = 3
assert 1 << PAIR_BITS == TOP_K
DUMMY_TOKENS = 2 * EXPERT_TILE // TOP_K
GATHER_SLOTS = 3


def _experts_kernel(n_tok, te_ref, ts_ref, tn_ref, tf_ref, ne_ref, g_ref, s_ref,
                    h_hbm, wg_hbm, wu_hbm, wd_hbm, y_hbm,
                    xbuf, ybuf, xs, wg_f, wu_f, wd_f, wg_bf, wu_bf, wd_bf, gsem, ssem, wsem, zsem):
    j = pl.program_id(0)
    last = pl.num_programs(0) - 1
    tm = EXPERT_TILE
    rc = ROW_CHUNKS
    plane_tokens = n_tok + DUMMY_TOKENS
    slot = j % 2
    other = 1 - slot
    gslot = j % GATHER_SLOTS
    flags = tf_ref[j]
    used = (flags & TILE_USED) != 0
    first = (flags & TILE_FIRST) != 0
    prev_used = (flags & TILE_PREV_USED) != 0
    wslot = (flags // TILE_WSLOT) & 1
    prv = jnp.maximum(j - 1, 0)
    ahead = jnp.minimum(j + GATHER_SLOTS - 1, last)

    def buf_rows(i):
        return pl.ds(i * rc, rc) if isinstance(i, int) else pl.ds(pl.multiple_of(i * rc, rc), rc)

    def gather_row(i, start, dst):
        row = g_ref[start + i]
        pltpu.make_async_copy(h_hbm.at[pl.ds(pl.multiple_of(row, rc), rc), :],
                              xbuf.at[dst, buf_rows(i), :], gsem.at[dst]).start(priority=0)

    def scatter_row(i, start, n_valid, src):
        spare = ((i & (TOP_K - 1)) * plane_tokens + n_tok + (i >> PAIR_BITS)) * rc + src * ((tm >> PAIR_BITS) * rc)
        row = jnp.where(i < n_valid, s_ref[start + i], spare)
        pltpu.make_async_copy(ybuf.at[src, buf_rows(i), :],
                              y_hbm.at[pl.ds(pl.multiple_of(row, rc), rc), :],
                              ssem.at[src]).start(priority=i % 2 if isinstance(i, int) else 0)

    def scatter_tile_loop(tile, src):
        start, n_valid = ts_ref[tile], tn_ref[tile]
        lax.fori_loop(0, tm, lambda i, c: (scatter_row(i, start, n_valid, src), c)[1], 0)

    def wait_gather(dst):
        pltpu.make_async_copy(h_hbm.at[pl.ds(0, tm * rc), :], xbuf.at[dst], gsem.at[dst]).wait()

    def wait_scatter(src):
        pltpu.make_async_copy(ybuf.at[src], y_hbm.at[pl.ds(0, tm * rc), :], ssem.at[src]).wait()

    def weight_copies(e, ws):
        return [pltpu.make_async_copy(src.at[e], dst.at[ws], wsem.at[ws])
                for src, dst in ((wg_hbm, wg_f), (wu_hbm, wu_f), (wd_hbm, wd_f))]

    @pl.when(j == 0)
    def _():
        ybuf[1] = jnp.zeros(ybuf.shape[1:], F32)
        fills = [pltpu.make_async_copy(ybuf.at[1, pl.ds(0, DUMMY_TOKENS * rc), :],
                                       y_hbm.at[pl.ds((k * plane_tokens + n_tok) * rc, DUMMY_TOKENS * rc), :],
                                       zsem.at[0]) for k in range(TOP_K)]
        for c in fills:
            c.start()
        for c in fills:
            c.wait()
        for c in weight_copies(te_ref[0], 0):
            c.start(priority=1)
        for t in range(GATHER_SLOTS - 1):
            start_t = ts_ref[jnp.minimum(t, last)]
            lax.fori_loop(0, tm, lambda i, c, start_t=start_t, t=t: (gather_row(i, start_t, t), c)[1], 0)

    @pl.when(first)
    def _():
        for c in weight_copies(te_ref[j], wslot):
            c.wait()
        wg_bf[...] = wg_f[wslot].astype(BF16)
        wu_bf[...] = wu_f[wslot].astype(BF16)
        wd_bf[...] = wd_f[wslot].astype(BF16)
        nxt_e = ne_ref[j]

        @pl.when(nxt_e >= 0)
        def _():
            for c in weight_copies(nxt_e, 1 - wslot):
                c.start(priority=1)

    @pl.when(jnp.logical_and(used, j > 0))
    def _():
        wait_scatter(slot)

    @pl.when(used)
    def _():
        wait_gather(gslot)
        xs[...] = jnp.concatenate(
            [xbuf[gslot, pl.ds(c, tm, stride=rc), :] for c in range(rc)], axis=1).astype(BF16)
        start_n = ts_ref[ahead]
        ahead_slot = (j + GATHER_SLOTS - 1) % GATHER_SLOTS
        for i in range(tm):
            gather_row(i, start_n, ahead_slot)
        start_p = ts_ref[prv]
        n_valid_p = jnp.where(j > 0, tn_ref[prv], 0)
        for i in range(tm):
            scatter_row(i, start_p, n_valid_p, other)
        x = xs[...]
        hg = jnp.dot(x, wg_bf[...], preferred_element_type=F32)
        hu = jnp.dot(x, wu_bf[...], preferred_element_type=F32)
        h = (hg * jax.nn.sigmoid(hg) * hu).astype(BF16)
        y = jnp.dot(h, wd_bf[...], preferred_element_type=F32)
        for c in range(rc):
            ybuf[slot, pl.ds(c, tm, stride=rc), :] = y[:, c * LANES:(c + 1) * LANES]

    @pl.when(jnp.logical_and(jnp.logical_not(used), prev_used))
    def _():
        for t in range(GATHER_SLOTS - 1):
            wait_gather((j + t) % GATHER_SLOTS)
        wait_scatter(slot)
        scatter_tile_loop(prv, other)
        wait_scatter(other)

    @pl.when(jnp.logical_and(used, j == last))
    def _():
        for t in range(1, GATHER_SLOTS):
            wait_gather((j + t) % GATHER_SLOTS)
        wait_scatter(other)
        scatter_tile_loop(j, slot)
        wait_scatter(slot)


def _experts(tile_expert, tile_start, tile_rows, tile_flags, next_expert, gather_rows, scatter_rows, h2_all,
             w_gate, w_up, w_down, n_tok):
    n_tiles = tile_expert.shape[0]
    tm = EXPERT_TILE
    any_spec = pl.BlockSpec(memory_space=pl.ANY)
    grid_spec = pltpu.PrefetchScalarGridSpec(
        num_scalar_prefetch=7,
        grid=(n_tiles,),
        in_specs=[any_spec] * 4,
        out_specs=any_spec,
        scratch_shapes=[
            pltpu.VMEM((GATHER_SLOTS, tm * ROW_CHUNKS, LANES), F32),
            pltpu.VMEM((2, tm * ROW_CHUNKS, LANES), F32),
            pltpu.VMEM((tm, D_MODEL), BF16),
            pltpu.VMEM((2, D_MODEL, D_EXPERT), F32),
            pltpu.VMEM((2, D_MODEL, D_EXPERT), F32),
            pltpu.VMEM((2, D_EXPERT, D_MODEL), F32),
            pltpu.VMEM((D_MODEL, D_EXPERT), BF16),
            pltpu.VMEM((D_MODEL, D_EXPERT), BF16),
            pltpu.VMEM((D_EXPERT, D_MODEL), BF16),
            pltpu.SemaphoreType.DMA((GATHER_SLOTS,)),
            pltpu.SemaphoreType.DMA((2,)),
            pltpu.SemaphoreType.DMA((2,)),
            pltpu.SemaphoreType.DMA((1,)),
        ],
    )
    return pl.pallas_call(
        functools.partial(_experts_kernel, n_tok),
        grid_spec=grid_spec,
        out_shape=jax.ShapeDtypeStruct((TOP_K * (n_tok + DUMMY_TOKENS) * ROW_CHUNKS, LANES), F32),
        compiler_params=_cparams(("arbitrary",)),
        name="experts",
    )(tile_expert, tile_start, tile_rows, tile_flags, next_expert, gather_rows, scatter_rows, h2_all,
      w_gate, w_up, w_down)


def _combine_kernel(y_ref, w_ref, base_ref, g2_ref, o_ref):
    tc = base_ref.shape[0]
    w = w_ref[...]
    g2 = g2_ref[0]
    for c in range(ROW_CHUNKS):
        acc = jnp.zeros((tc, LANES), F32)
        for k in range(TOP_K):
            acc = acc + w[:, k:k + 1] * y_ref[k, pl.ds(c, tc, stride=ROW_CHUNKS), :]
        sl = slice(c * LANES, (c + 1) * LANES)
        o_ref[:, sl] = base_ref[:, sl] + g2[:, sl] * acc


def _combine(y_planes, wts, base, gate2, tc, rows_per_mod, block0):
    m = base.shape[0]
    r = gate2.shape[1]
    return pl.pallas_call(
        _combine_kernel,
        grid=(m // tc,),
        in_specs=[
            pl.BlockSpec((TOP_K, tc * ROW_CHUNKS, LANES), lambda i: (0, i + block0, 0)),
            pl.BlockSpec((tc, TOP_K), lambda i: (i, 0)),
            pl.BlockSpec((tc, D_MODEL), lambda i: (i, 0)),
            pl.BlockSpec((1, r, D_MODEL), lambda i: (i // rows_per_mod, 0, 0)),
        ],
        out_specs=pl.BlockSpec((tc, D_MODEL), lambda i: (i, 0)),
        out_shape=jax.ShapeDtypeStruct((m, D_MODEL), F32),
        compiler_params=_cparams(("parallel",)),
        name="combine",
    )(y_planes, wts, base, gate2)


def _moe_routed(h2_all, logits_t, b_router, w_gate, w_up, w_down):
    n_tok = logits_t.shape[1]
    mp = -(-n_tok // ROUTE_TILE) * ROUTE_TILE
    logits_t = jnp.pad(logits_t, ((0, 0), (0, mp - n_tok)))
    eidx, wts, counts = _route(logits_t, b_router, n_tok)
    eidx, wts, counts = eidx[:, :n_tok], wts[:, :n_tok], counts[:, 0]

    tm = EXPERT_TILE
    n_pairs = n_tok * TOP_K
    n_tiles = n_pairs // tm + N_EXPERTS
    pair_code = jnp.arange(n_tok, dtype=I32)[None, :] * TOP_K + jnp.arange(TOP_K, dtype=I32)[:, None]
    (sorted_pairs,) = lax.sort(((eidx * n_pairs + pair_code).reshape(-1),), is_stable=False)
    sorted_pairs = jnp.concatenate([sorted_pairs % n_pairs, jnp.zeros((tm,), I32)])
    pair_tok, pair_choice = sorted_pairs >> PAIR_BITS, sorted_pairs & (TOP_K - 1)
    gather_rows = pair_tok * ROW_CHUNKS
    scatter_rows = (pair_choice * (n_tok + DUMMY_TOKENS) + pair_tok) * ROW_CHUNKS
    tiles_e = (counts + tm - 1) // tm
    tile_end = jnp.cumsum(tiles_e)
    tile_begin = tile_end - tiles_e
    dense_begin = jnp.cumsum(counts) - counts
    tj = jnp.arange(n_tiles, dtype=I32)
    total_tiles = tile_end[-1]
    used = tj < total_tiles
    tj_used = jnp.minimum(tj, total_tiles - 1)[:, None]
    member = jnp.logical_and(tile_begin[None, :] <= tj_used, tj_used < tile_end[None, :])
    of_tile = lambda per_expert: jnp.sum(jnp.where(member, per_expert[None, :], 0), axis=1)
    e_ids = jnp.arange(N_EXPERTS, dtype=I32)
    te = of_tile(e_ids)
    local = tj - of_tile(tile_begin)
    ts = jnp.where(used, of_tile(dense_begin) + local * tm, 0).astype(I32)
    tn = jnp.where(used, jnp.clip(of_tile(counts) - local * tm, 0, tm), 0).astype(I32)
    first = jnp.logical_and(used, local == 0)
    busy = counts > 0
    wslot = of_tile((jnp.cumsum(busy.astype(I32)) - 1) % 2)
    later = jnp.logical_and(e_ids[None, :] > e_ids[:, None], busy[None, :])
    succ = jnp.min(jnp.where(later, e_ids[None, :], N_EXPERTS), axis=1)
    succ = jnp.where(succ == N_EXPERTS, -1, succ)
    ne = jnp.where(first, of_tile(succ), -1).astype(I32)
    prev_used = jnp.concatenate([jnp.zeros((1,), jnp.bool_), used[:-1]])
    tf = (used.astype(I32) * TILE_USED + first.astype(I32) * TILE_FIRST + wslot * TILE_WSLOT
          + prev_used.astype(I32) * TILE_PREV_USED)
    y_all = _experts(te, ts, tn, tf, ne, gather_rows, scatter_rows, h2_all, w_gate, w_up, w_down, n_tok)
    return y_all.reshape(TOP_K, (n_tok + DUMMY_TOKENS) * ROW_CHUNKS, LANES), wts.T


def kernel(x_prompt, x_sample, cache_k_win, cache_v_win, state_pool, c_prompt, c_sample, w_ada, b_ada, g_mix, w_in, q_norm, k_norm, w_pool, pool_scale, w_out, g_ffn, w_router, b_router, w_gate, w_up, w_down, ws_gate, ws_up, ws_down):
    depth = w_ada.shape[0]
    assert depth == 1
    l = 0
    nb, seq, _ = x_prompt.shape
    bd, dec_seq, _ = x_sample.shape
    assert dec_seq == 1 and seq % (TOKEN_TILE * 16) == 0
    n_prompt = nb * seq
    n_tok = n_prompt + bd
    n_keep = min(MAX_WINDOW, seq)
    tiles_per_batch = seq // TOKEN_TILE

    n_mod_rows = -(-(nb + bd) // 8) * 8
    c_all = jnp.concatenate([c_prompt, c_sample, jnp.zeros((n_mod_rows - nb - bd, D_MODEL), F32)], axis=0)
    mods = _ada(c_all, w_ada[l], b_ada[l])
    mods_p = [mods[:nb, j * D_MODEL:(j + 1) * D_MODEL].reshape(nb, 1, D_MODEL) for j in range(6)]
    mods_s = [mods[nb:nb + bd, j * D_MODEL:(j + 1) * D_MODEL].reshape(1, bd, D_MODEL) for j in range(6)]

    g_mix_l = g_mix[l].reshape(1, D_MODEL)
    g_ffn_l = g_ffn[l].reshape(1, D_MODEL)
    qn = q_norm[l].reshape(1, HEAD_DIM)
    kn = k_norm[l].reshape(1, HEAD_DIM)
    w_in_bf = w_in[l].astype(BF16)
    w_out_bf = w_out[l].astype(BF16)
    w_pool_bf = w_pool[l].astype(BF16)
    ps = pool_scale[l].reshape(1, D_POOL)
    w_router_t = w_router[l].T
    wsg_bf, wsu_bf, wsd_bf = ws_gate[l].astype(BF16), ws_up[l].astype(BF16), ws_down[l].astype(BF16)

    rope_p = _rope_tables(jnp.arange(seq, dtype=I32))
    rope_s = _rope_tables(jnp.full((bd,), PAST_LEN, I32))
    xp2 = x_prompt.reshape(n_prompt, D_MODEL)
    xs2 = x_sample.reshape(bd, D_MODEL)
    assert n_keep % TOKEN_TILE == 0
    qp, kp, vp, up, kwin_p, vwin_p = _in_proj(xp2, mods_p[0], mods_p[1], g_mix_l, w_in_bf, qn, kn, rope_p,
                                              TOKEN_TILE, tiles_per_batch, tiles_per_batch, n_keep // TOKEN_TILE)
    qs, _, _, us, k_new, v_new = _in_proj(xs2, mods_s[0], mods_s[1], g_mix_l, w_in_bf, qn, kn, rope_s, bd, 1, 1, 1)

    to_seq = lambda t: t.reshape(nb, seq, -1)
    heads = lambda t: t.reshape(bd, N_HEADS, HEAD_DIM)
    ck, cv = cache_k_win[l], cache_v_win[l]
    k_new, v_new = heads(k_new), heads(v_new)
    attn_p, k_win_s, v_win_s = _attn_prompt(to_seq(qp), to_seq(kp), to_seq(vp), ck, cv, k_new, v_new)
    attn_p = attn_p.reshape(n_prompt, D_ATTN)
    pool_p = _pool_prompt(to_seq(up), w_pool_bf, ps).reshape(n_prompt, D_POOL)
    attn_s = _attn_decode(heads(qs), k_new, v_new, ck, cv).reshape(bd, D_ATTN)
    pool_s = _pool_decode(us, jnp.swapaxes(state_pool[l], 0, 1), w_pool_bf, ps)

    base_p, h2_all, lg_p = _out_proj(xp2, attn_p, pool_p, mods_p[2], mods_p[3], mods_p[4], mods_p[5],
                                     w_out_bf, g_ffn_l, w_router_t, wsg_bf, wsu_bf, wsd_bf,
                                     TOKEN_TILE, tiles_per_batch, n_tok, None, 0)
    base_s, h2_all, lg_s = _out_proj(xs2, attn_s, pool_s, mods_s[2], mods_s[3], mods_s[4], mods_s[5],
                                     w_out_bf, g_ffn_l, w_router_t, wsg_bf, wsu_bf, wsd_bf,
                                     bd, 1, n_tok, h2_all, n_prompt // bd)

    y_planes, wts_t = _moe_routed(h2_all, jnp.concatenate([lg_p, lg_s], axis=1), b_router[l],
                                  w_gate[l], w_up[l], w_down[l])

    yp = _combine(y_planes, wts_t[:n_prompt], base_p, mods_p[5], COMBINE_TILE, seq // COMBINE_TILE, 0)
    ys = _combine(y_planes, wts_t[n_prompt:n_tok], base_s, mods_s[5], bd, 1, n_prompt // bd)

    y_prompt = yp.reshape(nb, seq, D_MODEL)
    y_sample = ys.reshape(bd, dec_seq, D_MODEL)
    win = lambda t: t.reshape(1, nb, n_keep, N_HEADS, HEAD_DIM)
    pool_p_state = to_seq(up)[:, seq - POOL_STATE:][None]
    pool_s_state = jnp.concatenate([state_pool[l][:, 1:], us[:, None, :]], axis=1)[None]
    return (y_prompt, y_sample, win(kwin_p), win(vwin_p), pool_p_state,
            k_win_s[None], v_win_s[None], pool_s_state)
```

```python
import functools

import jax
import jax.numpy as jnp
from jax import lax
from jax.experimental import pallas as pl
from jax.experimental.pallas import tpu as pltpu

F32 = jnp.float32
BF16 = jnp.bfloat16
I32 = jnp.int32

D_MODEL = 2048
N_HEADS = 8
HEAD_DIM = 128
D_ATTN = N_HEADS * HEAD_DIM
D_POOL = D_MODEL - D_ATTN
D_IN = 3 * D_ATTN + D_POOL
ROPE_DIM = HEAD_DIM // 4
ROPE_HALF = ROPE_DIM // 2
ROPE_THETA = 500000.0
DILATED_GROUPS = ((128, 1), (512, 4), (2048, 16))
QBLOCK = 128
ATTN_SCALE = HEAD_DIM ** -0.5
POOL_WINDOWS = (2, 4, 8, 16)
POOL_GROUP_DIM = D_POOL // len(POOL_WINDOWS)
POOL_STATE = max(POOL_WINDOWS) - 1
POOL_HALO = 16
N_EXPERTS = 64
TOP_K = 8
N_EXPERT_GROUPS = 8
GROUP_SIZE = N_EXPERTS // N_EXPERT_GROUPS
TOPK_GROUPS = 4
D_EXPERT = 512
D_SHARED = 512
ROUTED_SCALE = 2.5
EPS = 1e-6
PAST_LEN = 16384
MAX_WINDOW = 2048

LANES = 128
ROW_CHUNKS = D_MODEL // LANES
NEG_BIG = -1e30
LOG2_E = 1.4426950408889634

TOKEN_TILE = 256
EXPERT_TILE = 256
COMBINE_TILE = 256
ROUTE_TILE = 256
VMEM_LIMIT = 60 * 1024 * 1024


def _cparams(sem, vmem=VMEM_LIMIT):
    return pltpu.CompilerParams(dimension_semantics=sem, vmem_limit_bytes=vmem)


def _ada_kernel(c_ref, w_ref, b_ref, o_ref):
    c = c_ref[...]
    s = (c * jax.nn.sigmoid(c)).astype(BF16)
    o_ref[...] = jnp.dot(s, w_ref[...].astype(BF16), preferred_element_type=F32) + b_ref[...]


def _ada(c_all, w_ada, b_ada):
    rows = c_all.shape[0]
    n = w_ada.shape[1]
    tn = 1024
    return pl.pallas_call(
        _ada_kernel,
        grid=(n // tn,),
        in_specs=[
            pl.BlockSpec((rows, D_MODEL), lambda j: (0, 0)),
            pl.BlockSpec((D_MODEL, tn), lambda j: (0, j)),
            pl.BlockSpec((1, tn), lambda j: (0, j)),
        ],
        out_specs=pl.BlockSpec((rows, tn), lambda j: (0, j)),
        out_shape=jax.ShapeDtypeStruct((rows, n), F32),
        compiler_params=_cparams(("parallel",)),
        name="ada_modulation",
    )(c_all, w_ada, b_ada.reshape(1, n))


def _rms(x):
    return x * lax.rsqrt(jnp.mean(x * x, axis=-1, keepdims=True) + EPS)


def _in_kernel(x_ref, sh_ref, sc_ref, g_ref, w_ref, qn_ref, kn_ref, c_ref, s1_ref, s2_ref,
               q_ref, k_ref, v_ref, u_ref, kw_ref, vw_ref):
    tm = x_ref.shape[0]
    x = x_ref[...]
    h = _rms(x) * g_ref[...] * (1.0 + sc_ref[0]) + sh_ref[0]
    hb = h.astype(BF16)
    cos = c_ref[...]
    s1 = s1_ref[...]
    s2 = s2_ref[...]

    def head_rows(win_ref, hd, val):
        win_ref[pl.ds(hd, tm, stride=N_HEADS), :] = val

    def qk(sec, nrm, out_ref, win_ref):
        z = jnp.dot(hb, w_ref[:, sec * D_ATTN:(sec + 1) * D_ATTN], preferred_element_type=F32)
        for hd in range(N_HEADS):
            sl = slice(hd * HEAD_DIM, (hd + 1) * HEAD_DIM)
            r = _rms(z[:, sl]) * nrm
            r = r * cos + pltpu.roll(r, HEAD_DIM - ROPE_HALF, 1) * s1 + pltpu.roll(r, ROPE_HALF, 1) * s2
            out_ref[:, sl] = r
            if win_ref is not None:
                head_rows(win_ref, hd, r)

    qk(0, qn_ref[...], q_ref, None)
    qk(1, kn_ref[...], k_ref, kw_ref)
    v = jnp.dot(hb, w_ref[:, 2 * D_ATTN:3 * D_ATTN], preferred_element_type=F32)
    v_ref[...] = v
    for hd in range(N_HEADS):
        head_rows(vw_ref, hd, v[:, hd * HEAD_DIM:(hd + 1) * HEAD_DIM])
    u_ref[...] = jnp.dot(hb, w_ref[:, 3 * D_ATTN:], preferred_element_type=F32)


def _in_proj(x2d, shift, scale, g_mix, w_in_bf, q_norm, k_norm, rope, tm, rows_per_mod, rope_tiles, keep_tiles):
    m = x2d.shape[0]
    r = shift.shape[1]
    n_groups = m // tm // rows_per_mod
    skip = rows_per_mod - keep_tiles
    mod_spec = pl.BlockSpec((1, r, D_MODEL), lambda i: (i // rows_per_mod, 0, 0))
    rope_spec = pl.BlockSpec((tm, HEAD_DIM), lambda i: (i % rope_tiles, 0))
    out_spec = pl.BlockSpec((tm, D_ATTN), lambda i: (i, 0))
    out_sd = jax.ShapeDtypeStruct((m, D_ATTN), F32)
    win_spec = pl.BlockSpec(
        (tm * N_HEADS, HEAD_DIM),
        lambda i: ((i // rows_per_mod) * keep_tiles + jnp.maximum(i % rows_per_mod - skip, 0), 0))
    win_sd = jax.ShapeDtypeStruct((n_groups * keep_tiles * tm * N_HEADS, HEAD_DIM), F32)
    return pl.pallas_call(
        _in_kernel,
        grid=(m // tm,),
        in_specs=[
            pl.BlockSpec((tm, D_MODEL), lambda i: (i, 0)),
            mod_spec, mod_spec,
            pl.BlockSpec((1, D_MODEL), lambda i: (0, 0)),
            pl.BlockSpec((D_MODEL, D_IN), lambda i: (0, 0)),
            pl.BlockSpec((1, HEAD_DIM), lambda i: (0, 0)),
            pl.BlockSpec((1, HEAD_DIM), lambda i: (0, 0)),
            rope_spec, rope_spec, rope_spec,
        ],
        out_specs=[out_spec] * 4 + [win_spec] * 2,
        out_shape=[out_sd] * 4 + [win_sd] * 2,
        compiler_params=_cparams(("arbitrary",)),
        name="in_proj",
    )(x2d, shift, scale, g_mix, w_in_bf, q_norm, k_norm, *rope)


def _rope_tables(pos):
    inv_freq = jnp.float32(ROPE_THETA) ** (-jnp.arange(ROPE_HALF, dtype=F32) / ROPE_HALF)
    ang = pos.astype(F32)[:, None] * inv_freq[None, :]
    cos, sin = jnp.cos(ang), jnp.sin(ang)
    t = pos.shape[0]
    rest = HEAD_DIM - ROPE_DIM
    c = jnp.concatenate([cos, cos, jnp.ones((t, rest), F32)], axis=1)
    s1 = jnp.concatenate([-sin, jnp.zeros((t, ROPE_HALF + rest), F32)], axis=1)
    s2 = jnp.concatenate([jnp.zeros((t, ROPE_HALF), F32), sin, jnp.zeros((t, rest), F32)], axis=1)
    return c, s1, s2


ATTN_INTERLEAVE = 4
WINDOW_ROWS_PER_STEP = 2
REGROUP_MIN_DILATION = 16


def _attn_prompt_kernel(q_ref, k_ref, v_ref, ck_hbm, cv_hbm, kn_ref, vn_ref, o_ref, ok_hbm, ov_hbm,
                        m_scr, l_scr, acc_scr, q_rm, k_rm, v_rm, wbuf, isem, osem):
    seq = q_ref.shape[1]
    n_buf = wbuf.shape[1]
    step = pl.program_id(0) * pl.num_programs(1) + pl.program_id(1)
    last_step = pl.num_programs(0) * pl.num_programs(1) - 1
    windows = ((ck_hbm, kn_ref, ok_hbm), (cv_hbm, vn_ref, ov_hbm))

    def window_in(row, a):
        return pltpu.make_async_copy(windows[a][0].at[row, pl.ds(1, n_buf - 1)],
                                     wbuf.at[a, pl.ds(0, n_buf - 1)], isem.at[a])

    def window_out(row, a):
        return pltpu.make_async_copy(wbuf.at[a], windows[a][2].at[row], osem.at[a])

    def start_in(row):
        for a in range(2):
            window_in(row, a).start()

    def finish_in_start_out(row):
        for a in range(2):
            window_in(row, a).wait()
            wbuf[a, n_buf - 1] = windows[a][1][row]
            window_out(row, a).start()

    def finish_out(row):
        for a in range(2):
            window_out(row, a).wait()

    row0 = step * WINDOW_ROWS_PER_STEP
    window_phases = [
        lambda: start_in(row0),
        lambda: finish_in_start_out(row0),
        lambda: (finish_out(row0), start_in(row0 + 1)),
        lambda: finish_in_start_out(row0 + 1),
    ]
    assert WINDOW_ROWS_PER_STEP == 2 and len(window_phases) == len(DILATED_GROUPS) + 1

    @pl.when(step > 0)
    def _():
        finish_out(row0 - 1)
    window_phases[0]()
    m_scr[...] = jnp.full(m_scr.shape, NEG_BIG, F32)
    l_scr[...] = jnp.zeros(l_scr.shape, F32)
    acc_scr[...] = jnp.zeros(acc_scr.shape, F32)
    qi = lax.broadcasted_iota(I32, (QBLOCK, 2 * QBLOCK), 0)
    kj = lax.broadcasted_iota(I32, (QBLOCK, 2 * QBLOCK), 1)
    dist = qi + QBLOCK - kj
    band_mask = jnp.logical_and(dist >= 0, dist <= QBLOCK)
    bias_band = jnp.where(band_mask, 0.0, NEG_BIG)
    bias_first = jnp.where(jnp.logical_and(band_mask, kj >= QBLOCK), 0.0, NEG_BIG)
    ones = jnp.ones((2 * QBLOCK, HEAD_DIM), BF16)
    dims = (((1,), (1,)), ((), ()))
    q_scale = ATTN_SCALE * LOG2_E

    for window, dil in DILATED_GROUPS:
        assert window // dil == QBLOCK
        n_units = seq // QBLOCK
        assert n_units % ATTN_INTERLEAVE == 0 and (dil % ATTN_INTERLEAVE == 0 or dil == 1)

        def rows(start, dil=dil):
            return pl.ds(start, QBLOCK) if dil == 1 else pl.ds(start, QBLOCK, stride=dil)

        sub_len = seq // dil
        regroup = dil >= REGROUP_MIN_DILATION
        if regroup:
            for src, dst in ((q_ref, q_rm), (k_ref, k_rm), (v_ref, v_rm)):
                for r in range(dil):
                    dst[pl.ds(r * sub_len, sub_len), :] = src[0, pl.ds(r, sub_len, stride=dil), :]

        def qkv_blocks(blk, res, start, prev, rows=rows, regroup=regroup, sub_len=sub_len):
            if regroup:
                cur = pl.ds(res * sub_len + blk * QBLOCK, QBLOCK)
                prv = pl.ds(res * sub_len + jnp.maximum(blk - 1, 0) * QBLOCK, QBLOCK)
                return (q_rm[cur, :], k_rm[prv, :], k_rm[cur, :], v_rm[prv, :], v_rm[cur, :])
            return (q_ref[0, rows(start), :], k_ref[0, rows(prev), :], k_ref[0, rows(start), :],
                    v_ref[0, rows(prev), :], v_ref[0, rows(start), :])

        def body(it, carry, dil=dil, rows=rows, qkv_blocks=qkv_blocks):
            loaded = []
            for u in range(ATTN_INTERLEAVE):
                f = it * ATTN_INTERLEAVE + u
                blk, res = f // dil, f % dil
                start = blk * (QBLOCK * dil) + res
                prev = jnp.maximum(start - QBLOCK * dil, res)
                q_blk, k_prev, k_cur, v_prev, v_cur = qkv_blocks(blk, res, start, prev)
                q = (q_blk * q_scale).astype(BF16)
                k = jnp.concatenate([k_prev, k_cur], axis=0).astype(BF16)
                v = jnp.concatenate([v_prev, v_cur], axis=0).astype(BF16)
                v_ones = jnp.concatenate([v, ones], axis=1)
                state = (m_scr[rows(start), :], l_scr[rows(start), :], acc_scr[rows(start), :])
                loaded.append((start, blk, q, k, v_ones, state))
            scores = [lax.dot_general(q, k, dims, preferred_element_type=F32)
                      + jnp.where(blk > 0, bias_band, bias_first)
                      for _, blk, q, k, _, _ in loaded]
            probs = []
            for s, (_, _, _, _, _, (m_old, _, _)) in zip(scores, loaded):
                m_new = jnp.maximum(m_old, jnp.max(s, axis=1, keepdims=True))
                p = jnp.exp2(s - jnp.concatenate([m_new, m_new], axis=1)).astype(BF16)
                probs.append((m_new, jnp.exp2(m_old - m_new), p))
            updated = []
            for (m_new, alpha, p), (start, _, _, _, v_ones, (_, l_old, a_old)) in zip(probs, loaded):
                pv = jnp.dot(p, v_ones, preferred_element_type=F32)
                updated.append((start, m_new, alpha * l_old + pv[:, HEAD_DIM:], alpha * a_old + pv[:, :HEAD_DIM]))
            for start, m_new, l_new, a_new in updated:
                m_scr[rows(start), :] = m_new
                l_scr[rows(start), :] = l_new
                acc_scr[rows(start), :] = a_new
            return carry

        trips = n_units // ATTN_INTERLEAVE
        phase = window_phases[DILATED_GROUPS.index((window, dil)) + 1]
        if (window, dil) == DILATED_GROUPS[-1]:
            lax.fori_loop(0, trips // 2, body, 0)
            phase()
            lax.fori_loop(trips // 2, trips, body, 0)
        else:
            lax.fori_loop(0, trips, body, 0)
            phase()

    @pl.when(step == last_step)
    def _():
        finish_out(row0 + 1)

    o_ref[0] = acc_scr[...] / l_scr[...]


def _attn_prompt(q, k, v, cache_k, cache_v, k_new, v_new):
    b, s, _ = q.shape
    bd, n_buf = cache_k.shape[0], cache_k.shape[1]
    assert bd == b * N_HEADS * WINDOW_ROWS_PER_STEP
    spec = pl.BlockSpec((1, s, HEAD_DIM), lambda bi, hi: (bi, 0, hi))
    any_spec = pl.BlockSpec(memory_space=pl.ANY)
    new_spec = pl.BlockSpec(k_new.shape, lambda bi, hi: (0, 0, 0))
    win_sd = jax.ShapeDtypeStruct(cache_k.shape, cache_k.dtype)
    return pl.pallas_call(
        _attn_prompt_kernel,
        grid=(b, N_HEADS),
        in_specs=[spec, spec, spec, any_spec, any_spec, new_spec, new_spec],
        out_specs=[spec, any_spec, any_spec],
        out_shape=[jax.ShapeDtypeStruct((b, s, D_ATTN), F32), win_sd, win_sd],
        scratch_shapes=[pltpu.VMEM((s, HEAD_DIM), F32)] * 6 + [
            pltpu.VMEM((2, n_buf) + cache_k.shape[2:], F32),
            pltpu.SemaphoreType.DMA((2,)),
            pltpu.SemaphoreType.DMA((2,)),
        ],
        compiler_params=_cparams(("arbitrary", "arbitrary")),
        name="attn_prompt",
    )(q, k, v, cache_k, cache_v, k_new, v_new)


def _attn_decode_kernel(q_ref, kn_ref, vn_ref, k1_ref, k2_ref, k3_ref, v1_ref, v2_ref, v3_ref, o_ref):
    q = q_ref[0]
    k_new = kn_ref[0]
    v_new = vn_ref[0]
    n_grp = len(DILATED_GROUPS)

    def scores(kb):
        return jnp.sum(kb * q[None], axis=-1, keepdims=True) * ATTN_SCALE

    kcs = [k1_ref[0], k2_ref[0][:, 0], k3_ref[0][:, 0]]
    vcs = [v1_ref[0], v2_ref[0][:, 0], v3_ref[0][:, 0]]
    s_new = jnp.sum(k_new * q, axis=-1, keepdims=True) * ATTN_SCALE
    s_grp = [scores(kb) for kb in kcs]
    m = s_new
    for s in s_grp:
        m = jnp.maximum(m, jnp.max(s, axis=0))
    p_new = jnp.exp(s_new - m)
    den = n_grp * p_new
    num = n_grp * p_new * v_new
    for s, vb in zip(s_grp, vcs):
        p = jnp.exp(s - m[None])
        den = den + jnp.sum(p, axis=0)
        num = num + jnp.sum(p * vb, axis=0)
    o_ref[0] = num / den


def _attn_decode(q, k_new, v_new, cache_k, cache_v):
    bd, n_buf = cache_k.shape[0], cache_k.shape[1]
    assert n_buf == MAX_WINDOW

    def views(cache):
        out, specs = [], []
        for window, dil in DILATED_GROUPS:
            band = window // dil
            assert band == QBLOCK and n_buf % dil == 0 and n_buf // dil >= band
            if dil == 1:
                out.append(cache)
                specs.append(pl.BlockSpec((1, band, N_HEADS, HEAD_DIM),
                                          lambda b, nb=n_buf // band: (b, nb - 1, 0, 0)))
            else:
                out.append(cache.reshape(bd, n_buf // dil, dil, N_HEADS, HEAD_DIM))
                specs.append(pl.BlockSpec((1, band, 1, N_HEADS, HEAD_DIM),
                                          lambda b, nb=n_buf // dil // band: (b, nb - 1, 0, 0, 0)))
        return out, specs

    kv, kspecs = views(cache_k)
    vv, vspecs = views(cache_v)
    tok = pl.BlockSpec((1, N_HEADS, HEAD_DIM), lambda b: (b, 0, 0))
    return pl.pallas_call(
        _attn_decode_kernel,
        grid=(bd,),
        in_specs=[tok, tok, tok] + kspecs + vspecs,
        out_specs=tok,
        out_shape=jax.ShapeDtypeStruct((bd, N_HEADS, HEAD_DIM), F32),
        compiler_params=_cparams(("parallel",)),
        name="attn_decode",
    )(q, k_new, v_new, *kv, *vv)


def _pool_project(d_groups, wp_ref, ps_ref, out_ref):
    for g, d in enumerate(d_groups):
        sl = slice(g * POOL_GROUP_DIM, (g + 1) * POOL_GROUP_DIM)
        out_ref[:, sl] = jnp.dot(d.astype(BF16), wp_ref[g], preferred_element_type=F32) * ps_ref[:, sl]


def _pool_prompt_kernel(u_ref, prev_ref, wp_ref, ps_ref, o_ref):
    i = pl.program_id(1)
    tp = u_ref.shape[1]
    u = u_ref[0]
    prev = jnp.where(i > 0, prev_ref[0], 0.0)
    ext = jnp.concatenate([prev, u], axis=0)
    pos = i * tp + lax.broadcasted_iota(I32, (tp, 1), 0)
    d_groups = []
    for g, w in enumerate(POOL_WINDOWS):
        sl = slice(g * POOL_GROUP_DIM, (g + 1) * POOL_GROUP_DIM)
        a = ext[:, sl]
        span = 1
        while span < w:
            a = a + pltpu.roll(a, span, 0)
            span *= 2
        win = a[POOL_HALO:, :]
        cnt = jnp.minimum(w, pos + 1).astype(F32)
        d_groups.append(win / cnt - u[:, sl])
    _pool_project(d_groups, wp_ref, ps_ref, o_ref.at[0])


def _pool_prompt(u, w_pool_bf, pool_scale):
    b, s, _ = u.shape
    tp = 512
    halo_blocks = tp // POOL_HALO
    return pl.pallas_call(
        _pool_prompt_kernel,
        grid=(b, s // tp),
        in_specs=[
            pl.BlockSpec((1, tp, D_POOL), lambda bi, i: (bi, i, 0)),
            pl.BlockSpec((1, POOL_HALO, D_POOL), lambda bi, i: (bi, jnp.maximum(i * halo_blocks - 1, 0), 0)),
            pl.BlockSpec(w_pool_bf.shape, lambda bi, i: (0, 0, 0)),
            pl.BlockSpec((1, D_POOL), lambda bi, i: (0, 0)),
        ],
        out_specs=pl.BlockSpec((1, tp, D_POOL), lambda bi, i: (bi, i, 0)),
        out_shape=jax.ShapeDtypeStruct((b, s, D_POOL), F32),
        compiler_params=_cparams(("parallel", "parallel")),
        name="pool_prompt",
    )(u, u, w_pool_bf, pool_scale)


def _pool_decode_kernel(u_ref, st_ref, wp_ref, ps_ref, o_ref):
    u = u_ref[...]
    d_groups = []
    for g, w in enumerate(POOL_WINDOWS):
        sl = slice(g * POOL_GROUP_DIM, (g + 1) * POOL_GROUP_DIM)
        win = u[:, sl]
        for j in range(1, w):
            win = win + st_ref[POOL_STATE - j][:, sl]
        d_groups.append(win / float(w) - u[:, sl])
    _pool_project(d_groups, wp_ref, ps_ref, o_ref)


def _pool_decode(u, state_t, w_pool_bf, pool_scale):
    assert PAST_LEN + 1 >= max(POOL_WINDOWS)
    return pl.pallas_call(
        _pool_decode_kernel,
        out_shape=jax.ShapeDtypeStruct(u.shape, F32),
        compiler_params=_cparams(None),
        name="pool_decode",
    )(u, state_t, w_pool_bf, pool_scale)


def _split_bf16(x):
    hi = x.astype(BF16)
    lo = (x - hi.astype(F32)).astype(BF16)
    return hi, lo


def _out_kernel(x_ref, a_ref, p_ref, g1_ref, sh_ref, sc_ref, g2_ref, wo_ref, gf_ref, wr_ref,
                wsg_ref, wsu_ref, wsd_ref, base_ref, h2_ref, lg_ref):
    tm = x_ref.shape[0]
    mix = (jnp.dot(a_ref[...].astype(BF16), wo_ref[:D_ATTN, :], preferred_element_type=F32)
           + jnp.dot(p_ref[...].astype(BF16), wo_ref[D_ATTN:, :], preferred_element_type=F32))
    x1 = x_ref[...] + g1_ref[0] * mix
    h2 = _rms(x1) * gf_ref[...] * (1.0 + sc_ref[0]) + sh_ref[0]
    for c in range(ROW_CHUNKS):
        h2_ref[pl.ds(c, tm, stride=ROW_CHUNKS), :] = h2[:, c * LANES:(c + 1) * LANES]
    h_hi, h_lo = _split_bf16(h2)
    w_hi, w_lo = _split_bf16(wr_ref[...])
    dims = (((1,), (1,)), ((), ()))
    lg_ref[...] = (lax.dot_general(w_hi, h_hi, dims, preferred_element_type=F32)
                   + lax.dot_general(w_hi, h_lo, dims, preferred_element_type=F32)
                   + lax.dot_general(w_lo, h_hi, dims, preferred_element_type=F32))
    sg = jnp.dot(h_hi, wsg_ref[...], preferred_element_type=F32)
    su = jnp.dot(h_hi, wsu_ref[...], preferred_element_type=F32)
    hs = (sg * jax.nn.sigmoid(sg) * su).astype(BF16)
    base_ref[...] = x1 + g2_ref[0] * jnp.dot(hs, wsd_ref[...], preferred_element_type=F32)


def _out_proj(x2d, attn, pool, gate1, shift2, scale2, gate2, w_out_bf, g_ffn, w_router_t,
              ws_gate_bf, ws_up_bf, ws_down_bf, tm, rows_per_mod, h2_tokens, h2_all, h2_block0):
    m = x2d.shape[0]
    r = gate1.shape[1]
    n_main = m // tm
    h2_rows = h2_tokens * ROW_CHUNKS
    n_fill = 0 if h2_all is not None else pl.cdiv(h2_rows - m * ROW_CHUNKS, tm * ROW_CHUNKS)
    last = n_main - 1
    row = lambda i: jnp.minimum(i, last)
    mod_spec = pl.BlockSpec((1, r, D_MODEL), lambda i: (row(i) // rows_per_mod, 0, 0))
    const = lambda shape: pl.BlockSpec(shape, lambda i: (0,) * len(shape))
    in_specs = [
        pl.BlockSpec((tm, D_MODEL), lambda i: (row(i), 0)),
        pl.BlockSpec((tm, D_ATTN), lambda i: (row(i), 0)),
        pl.BlockSpec((tm, D_POOL), lambda i: (row(i), 0)),
        mod_spec, mod_spec, mod_spec, mod_spec,
        const((D_MODEL, D_MODEL)),
        const((1, D_MODEL)),
        const((N_EXPERTS, D_MODEL)),
        const((D_MODEL, D_SHARED)), const((D_MODEL, D_SHARED)), const((D_SHARED, D_MODEL)),
    ]
    args = [x2d, attn, pool, gate1, shift2, scale2, gate2, w_out_bf, g_ffn, w_router_t,
            ws_gate_bf, ws_up_bf, ws_down_bf]
    aliases = {}
    n_in = len(args)
    if h2_all is not None:
        in_specs.append(pl.BlockSpec(memory_space=pl.ANY))
        args.append(h2_all)
        aliases = {n_in: 1}

    def kernel(*refs):
        refs = refs[:n_in] + refs[len(args):]
        if n_fill == 0:
            _out_kernel(*refs)
            return
        step = pl.program_id(0)
        pl.when(step < n_main)(lambda: _out_kernel(*refs))

        @pl.when(step >= n_main)
        def _():
            h2_ref = refs[n_in + 1]
            h2_ref[...] = jnp.zeros(h2_ref.shape, F32)

    return pl.pallas_call(
        kernel,
        grid=(n_main + n_fill,),
        in_specs=in_specs,
        out_specs=[
            pl.BlockSpec((tm, D_MODEL), lambda i: (row(i), 0)),
            pl.BlockSpec((tm * ROW_CHUNKS, LANES), lambda i: (i + h2_block0, 0)),
            pl.BlockSpec((N_EXPERTS, tm), lambda i: (0, row(i))),
        ],
        out_shape=[
            jax.ShapeDtypeStruct((m, D_MODEL), F32),
            jax.ShapeDtypeStruct((h2_rows, LANES), F32),
            jax.ShapeDtypeStruct((N_EXPERTS, m), F32),
        ],
        input_output_aliases=aliases,
        compiler_params=_cparams(("arbitrary",)),
        name="out_proj",
    )(*args)


def _first_index_of_max(v, iota, size, axis):
    m = jnp.max(v, axis=axis, keepdims=True)
    idx = jnp.min(jnp.where(v == m, iota, size), axis=axis, keepdims=True)
    return m, idx


def _route_kernel(n_valid, lg_ref, b_ref, eidx_ref, wts_ref, cnt_ref, carry):
    i = pl.program_id(0)
    tn = lg_ref.shape[1]

    @pl.when(i == 0)
    def _():
        carry[...] = jnp.zeros(carry.shape, F32)

    scores = jax.nn.sigmoid(lg_ref[...])
    biased = scores + b_ref[...]
    grp = biased.reshape(N_EXPERT_GROUPS, GROUP_SIZE, tn)
    io_g = lax.broadcasted_iota(I32, grp.shape, 1)
    m1, i1 = _first_index_of_max(grp, io_g, GROUP_SIZE, 1)
    m2 = jnp.max(jnp.where(io_g == i1, -jnp.inf, grp), axis=1, keepdims=True)
    gscore = (m1 + m2)[:, 0, :]
    io_n = lax.broadcasted_iota(I32, gscore.shape, 0)
    gsel = jnp.zeros(gscore.shape, jnp.bool_)
    for _ in range(TOPK_GROUPS):
        _, gi = _first_index_of_max(gscore, io_n, N_EXPERT_GROUPS, 0)
        hit = io_n == gi
        gsel = jnp.logical_or(gsel, hit)
        gscore = jnp.where(hit, -jnp.inf, gscore)
    emask = jnp.broadcast_to(gsel[:, None, :], grp.shape).reshape(N_EXPERTS, tn)
    cand = jnp.where(emask, biased, -jnp.inf)
    io_e = lax.broadcasted_iota(I32, cand.shape, 0)
    picked = jnp.zeros(cand.shape, jnp.bool_)
    eidx, sel = [], []
    for _ in range(TOP_K):
        _, ei = _first_index_of_max(cand, io_e, N_EXPERTS, 0)
        hit = io_e == ei
        eidx.append(ei)
        sel.append(jnp.sum(jnp.where(hit, scores, 0.0), axis=0, keepdims=True))
        picked = jnp.logical_or(picked, hit)
        cand = jnp.where(hit, -jnp.inf, cand)
    sel = jnp.concatenate(sel, axis=0)
    eidx = jnp.concatenate(eidx, axis=0)
    wts_ref[...] = sel / jnp.sum(sel, axis=0, keepdims=True) * ROUTED_SCALE
    eidx_ref[...] = eidx

    tok = i * tn + lax.broadcasted_iota(I32, cand.shape, 1)
    mask = jnp.logical_and(picked, tok < n_valid).astype(F32)
    total = carry[...] + jnp.sum(mask, axis=1, keepdims=True)
    carry[...] = total
    cnt_ref[...] = total.astype(I32)


def _route(logits_t, b_router, n_valid):
    mp = logits_t.shape[1]
    tn = ROUTE_TILE
    tok_spec = pl.BlockSpec((TOP_K, tn), lambda i: (0, i))
    return pl.pallas_call(
        functools.partial(_route_kernel, n_valid),
        grid=(mp // tn,),
        in_specs=[
            pl.BlockSpec((N_EXPERTS, tn), lambda i: (0, i)),
            pl.BlockSpec((N_EXPERTS, 1), lambda i: (0, 0)),
        ],
        out_specs=[tok_spec, tok_spec, pl.BlockSpec((N_EXPERTS, LANES), lambda i: (0, 0))],
        out_shape=[
            jax.ShapeDtypeStruct((TOP_K, mp), I32),
            jax.ShapeDtypeStruct((TOP_K, mp), F32),
            jax.ShapeDtypeStruct((N_EXPERTS, LANES), I32),
        ],
        scratch_shapes=[pltpu.VMEM((N_EXPERTS, LANES), F32)],
        compiler_params=_cparams(("arbitrary",)),
        name="route",
    )(logits_t, b_router.reshape(N_EXPERTS, 1))


TILE_USED = 1
TILE_FIRST = 2
TILE_WSLOT = 4
TILE_PREV_USED = 8
PAIR_BITS = 3
assert 1 << PAIR_BITS == TOP_K
DUMMY_TOKENS = 2 * EXPERT_TILE // TOP_K
GATHER_SLOTS = 3


def _experts_kernel(n_tok, te_ref, ts_ref, tn_ref, tf_ref, ne_ref, g_ref, s_ref,
                    h_hbm, wg_hbm, wu_hbm, wd_hbm, y_hbm,
                    xbuf, ybuf, xs, wg_f, wu_f, wd_f, wg_bf, wu_bf, wd_bf, gsem, ssem, wsem, zsem):
    j = pl.program_id(0)
    last = pl.num_programs(0) - 1
    tm = EXPERT_TILE
    rc = ROW_CHUNKS
    plane_tokens = n_tok + DUMMY_TOKENS
    slot = j % 2
    other = 1 - slot
    gslot = j % GATHER_SLOTS
    flags = tf_ref[j]
    used = (flags & TILE_USED) != 0
    first = (flags & TILE_FIRST) != 0
    prev_used = (flags & TILE_PREV_USED) != 0
    wslot = (flags // TILE_WSLOT) & 1
    prv = jnp.maximum(j - 1, 0)
    ahead = jnp.minimum(j + GATHER_SLOTS - 1, last)

    def buf_rows(i):
        return pl.ds(i * rc, rc) if isinstance(i, int) else pl.ds(pl.multiple_of(i * rc, rc), rc)

    def gather_row(i, start, dst):
        row = g_ref[start + i]
        pltpu.make_async_copy(h_hbm.at[pl.ds(pl.multiple_of(row, rc), rc), :],
                              xbuf.at[dst, buf_rows(i), :], gsem.at[dst]).start(priority=0)

    def scatter_row(i, start, n_valid, src):
        spare = ((i & (TOP_K - 1)) * plane_tokens + n_tok + (i >> PAIR_BITS)) * rc + src * ((tm >> PAIR_BITS) * rc)
        row = jnp.where(i < n_valid, s_ref[start + i], spare)
        pltpu.make_async_copy(ybuf.at[src, buf_rows(i), :],
                              y_hbm.at[pl.ds(pl.multiple_of(row, rc), rc), :],
                              ssem.at[src]).start(priority=i % 2 if isinstance(i, int) else 0)

    def scatter_tile_loop(tile, src):
        start, n_valid = ts_ref[tile], tn_ref[tile]
        lax.fori_loop(0, tm, lambda i, c: (scatter_row(i, start, n_valid, src), c)[1], 0)

    def wait_gather(dst):
        pltpu.make_async_copy(h_hbm.at[pl.ds(0, tm * rc), :], xbuf.at[dst], gsem.at[dst]).wait()

    def wait_scatter(src):
        pltpu.make_async_copy(ybuf.at[src], y_hbm.at[pl.ds(0, tm * rc), :], ssem.at[src]).wait()

    def weight_copies(e, ws):
        return [pltpu.make_async_copy(src.at[e], dst.at[ws], wsem.at[ws])
                for src, dst in ((wg_hbm, wg_f), (wu_hbm, wu_f), (wd_hbm, wd_f))]

    @pl.when(j == 0)
    def _():
        ybuf[1] = jnp.zeros(ybuf.shape[1:], F32)
        fills = [pltpu.make_async_copy(ybuf.at[1, pl.ds(0, DUMMY_TOKENS * rc), :],
                                       y_hbm.at[pl.ds((k * plane_tokens + n_tok) * rc, DUMMY_TOKENS * rc), :],
                                       zsem.at[0]) for k in range(TOP_K)]
        for c in fills:
            c.start()
        for c in fills:
            c.wait()
        for c in weight_copies(te_ref[0], 0):
            c.start(priority=1)
        for t in range(GATHER_SLOTS - 1):
            start_t = ts_ref[jnp.minimum(t, last)]
            lax.fori_loop(0, tm, lambda i, c, start_t=start_t, t=t: (gather_row(i, start_t, t), c)[1], 0)

    @pl.when(first)
    def _():
        for c in weight_copies(te_ref[j], wslot):
            c.wait()
        wg_bf[...] = wg_f[wslot].astype(BF16)
        wu_bf[...] = wu_f[wslot].astype(BF16)
        wd_bf[...] = wd_f[wslot].astype(BF16)
        nxt_e = ne_ref[j]

        @pl.when(nxt_e >= 0)
        def _():
            for c in weight_copies(nxt_e, 1 - wslot):
                c.start(priority=1)

    @pl.when(jnp.logical_and(used, j > 0))
    def _():
        wait_scatter(slot)

    @pl.when(used)
    def _():
        wait_gather(gslot)
        xs[...] = jnp.concatenate(
            [xbuf[gslot, pl.ds(c, tm, stride=rc), :] for c in range(rc)], axis=1).astype(BF16)
        start_n = ts_ref[ahead]
        ahead_slot = (j + GATHER_SLOTS - 1) % GATHER_SLOTS
        for i in range(tm):
            gather_row(i, start_n, ahead_slot)
        start_p = ts_ref[prv]
        n_valid_p = jnp.where(j > 0, tn_ref[prv], 0)
        for i in range(tm):
            scatter_row(i, start_p, n_valid_p, other)
        x = xs[...]
        hg = jnp.dot(x, wg_bf[...], preferred_element_type=F32)
        hu = jnp.dot(x, wu_bf[...], preferred_element_type=F32)
        h = (hg * jax.nn.sigmoid(hg) * hu).astype(BF16)
        y = jnp.dot(h, wd_bf[...], preferred_element_type=F32)
        for c in range(rc):
            ybuf[slot, pl.ds(c, tm, stride=rc), :] = y[:, c * LANES:(c + 1) * LANES]

    @pl.when(jnp.logical_and(jnp.logical_not(used), prev_used))
    def _():
        for t in range(GATHER_SLOTS - 1):
            wait_gather((j + t) % GATHER_SLOTS)
        wait_scatter(slot)
        scatter_tile_loop(prv, other)
        wait_scatter(other)

    @pl.when(jnp.logical_and(used, j == last))
    def _():
        for t in range(1, GATHER_SLOTS):
            wait_gather((j + t) % GATHER_SLOTS)
        wait_scatter(other)
        scatter_tile_loop(j, slot)
        wait_scatter(slot)


def _experts(tile_expert, tile_start, tile_rows, tile_flags, next_expert, gather_rows, scatter_rows, h2_all,
             w_gate, w_up, w_down, n_tok):
    n_tiles = tile_expert.shape[0]
    tm = EXPERT_TILE
    any_spec = pl.BlockSpec(memory_space=pl.ANY)
    grid_spec = pltpu.PrefetchScalarGridSpec(
        num_scalar_prefetch=7,
        grid=(n_tiles,),
        in_specs=[any_spec] * 4,
        out_specs=any_spec,
        scratch_shapes=[
            pltpu.VMEM((GATHER_SLOTS, tm * ROW_CHUNKS, LANES), F32),
            pltpu.VMEM((2, tm * ROW_CHUNKS, LANES), F32),
            pltpu.VMEM((tm, D_MODEL), BF16),
            pltpu.VMEM((2, D_MODEL, D_EXPERT), F32),
            pltpu.VMEM((2, D_MODEL, D_EXPERT), F32),
            pltpu.VMEM((2, D_EXPERT, D_MODEL), F32),
            pltpu.VMEM((D_MODEL, D_EXPERT), BF16),
            pltpu.VMEM((D_MODEL, D_EXPERT), BF16),
            pltpu.VMEM((D_EXPERT, D_MODEL), BF16),
            pltpu.SemaphoreType.DMA((GATHER_SLOTS,)),
            pltpu.SemaphoreType.DMA((2,)),
            pltpu.SemaphoreType.DMA((2,)),
            pltpu.SemaphoreType.DMA((1,)),
        ],
    )
    return pl.pallas_call(
        functools.partial(_experts_kernel, n_tok),
        grid_spec=grid_spec,
        out_shape=jax.ShapeDtypeStruct((TOP_K * (n_tok + DUMMY_TOKENS) * ROW_CHUNKS, LANES), F32),
        compiler_params=_cparams(("arbitrary",)),
        name="experts",
    )(tile_expert, tile_start, tile_rows, tile_flags, next_expert, gather_rows, scatter_rows, h2_all,
      w_gate, w_up, w_down)


def _combine_kernel(y_ref, w_ref, base_ref, g2_ref, o_ref):
    tc = base_ref.shape[0]
    w = w_ref[...]
    g2 = g2_ref[0]
    for c in range(ROW_CHUNKS):
        acc = jnp.zeros((tc, LANES), F32)
        for k in range(TOP_K):
            acc = acc + w[:, k:k + 1] * y_ref[k, pl.ds(c, tc, stride=ROW_CHUNKS), :]
        sl = slice(c * LANES, (c + 1) * LANES)
        o_ref[:, sl] = base_ref[:, sl] + g2[:, sl] * acc


def _combine(y_planes, wts, base, gate2, tc, rows_per_mod, block0):
    m = base.shape[0]
    r = gate2.shape[1]
    return pl.pallas_call(
        _combine_kernel,
        grid=(m // tc,),
        in_specs=[
            pl.BlockSpec((TOP_K, tc * ROW_CHUNKS, LANES), lambda i: (0, i + block0, 0)),
            pl.BlockSpec((tc, TOP_K), lambda i: (i, 0)),
            pl.BlockSpec((tc, D_MODEL), lambda i: (i, 0)),
            pl.BlockSpec((1, r, D_MODEL), lambda i: (i // rows_per_mod, 0, 0)),
        ],
        out_specs=pl.BlockSpec((tc, D_MODEL), lambda i: (i, 0)),
        out_shape=jax.ShapeDtypeStruct((m, D_MODEL), F32),
        compiler_params=_cparams(("parallel",)),
        name="combine",
    )(y_planes, wts, base, gate2)


def _moe_routed(h2_all, logits_t, b_router, w_gate, w_up, w_down):
    n_tok = logits_t.shape[1]
    mp = -(-n_tok // ROUTE_TILE) * ROUTE_TILE
    logits_t = jnp.pad(logits_t, ((0, 0), (0, mp - n_tok)))
    eidx, wts, counts = _route(logits_t, b_router, n_tok)
    eidx, wts, counts = eidx[:, :n_tok], wts[:, :n_tok], counts[:, 0]

    tm = EXPERT_TILE
    n_pairs = n_tok * TOP_K
    n_tiles = n_pairs // tm + N_EXPERTS
    pair_code = jnp.arange(n_tok, dtype=I32)[None, :] * TOP_K + jnp.arange(TOP_K, dtype=I32)[:, None]
    (sorted_pairs,) = lax.sort(((eidx * n_pairs + pair_code).reshape(-1),), is_stable=False)
    sorted_pairs = jnp.concatenate([sorted_pairs % n_pairs, jnp.zeros((tm,), I32)])
    pair_tok, pair_choice = sorted_pairs >> PAIR_BITS, sorted_pairs & (TOP_K - 1)
    gather_rows = pair_tok * ROW_CHUNKS
    scatter_rows = (pair_choice * (n_tok + DUMMY_TOKENS) + pair_tok) * ROW_CHUNKS
    tiles_e = (counts + tm - 1) // tm
    tile_end = jnp.cumsum(tiles_e)
    tile_begin = tile_end - tiles_e
    dense_begin = jnp.cumsum(counts) - counts
    tj = jnp.arange(n_tiles, dtype=I32)
    total_tiles = tile_end[-1]
    used = tj < total_tiles
    tj_used = jnp.minimum(tj, total_tiles - 1)[:, None]
    member = jnp.logical_and(tile_begin[None, :] <= tj_used, tj_used < tile_end[None, :])
    of_tile = lambda per_expert: jnp.sum(jnp.where(member, per_expert[None, :], 0), axis=1)
    e_ids = jnp.arange(N_EXPERTS, dtype=I32)
    te = of_tile(e_ids)
    local = tj - of_tile(tile_begin)
    ts = jnp.where(used, of_tile(dense_begin) + local * tm, 0).astype(I32)
    tn = jnp.where(used, jnp.clip(of_tile(counts) - local * tm, 0, tm), 0).astype(I32)
    first = jnp.logical_and(used, local == 0)
    busy = counts > 0
    wslot = of_tile((jnp.cumsum(busy.astype(I32)) - 1) % 2)
    later = jnp.logical_and(e_ids[None, :] > e_ids[:, None], busy[None, :])
    succ = jnp.min(jnp.where(later, e_ids[None, :], N_EXPERTS), axis=1)
    succ = jnp.where(succ == N_EXPERTS, -1, succ)
    ne = jnp.where(first, of_tile(succ), -1).astype(I32)
    prev_used = jnp.concatenate([jnp.zeros((1,), jnp.bool_), used[:-1]])
    tf = (used.astype(I32) * TILE_USED + first.astype(I32) * TILE_FIRST + wslot * TILE_WSLOT
          + prev_used.astype(I32) * TILE_PREV_USED)
    y_all = _experts(te, ts, tn, tf, ne, gather_rows, scatter_rows, h2_all, w_gate, w_up, w_down, n_tok)
    return y_all.reshape(TOP_K, (n_tok + DUMMY_TOKENS) * ROW_CHUNKS, LANES), wts.T


def kernel(x_prompt, x_sample, cache_k_win, cache_v_win, state_pool, c_prompt, c_sample, w_ada, b_ada, g_mix, w_in, q_norm, k_norm, w_pool, pool_scale, w_out, g_ffn, w_router, b_router, w_gate, w_up, w_down, ws_gate, ws_up, ws_down):
    depth = w_ada.shape[0]
    assert depth == 1
    l = 0
    nb, seq, _ = x_prompt.shape
    bd, dec_seq, _ = x_sample.shape
    assert dec_seq == 1 and seq % (TOKEN_TILE * 16) == 0
    n_prompt = nb * seq
    n_tok = n_prompt + bd
    n_keep = min(MAX_WINDOW, seq)
    tiles_per_batch = seq // TOKEN_TILE

    n_mod_rows = -(-(nb + bd) // 8) * 8
    c_all = jnp.concatenate([c_prompt, c_sample, jnp.zeros((n_mod_rows - nb - bd, D_MODEL), F32)], axis=0)
    mods = _ada(c_all, w_ada[l], b_ada[l])
    mods_p = [mods[:nb, j * D_MODEL:(j + 1) * D_MODEL].reshape(nb, 1, D_MODEL) for j in range(6)]
    mods_s = [mods[nb:nb + bd, j * D_MODEL:(j + 1) * D_MODEL].reshape(1, bd, D_MODEL) for j in range(6)]

    g_mix_l = g_mix[l].reshape(1, D_MODEL)
    g_ffn_l = g_ffn[l].reshape(1, D_MODEL)
    qn = q_norm[l].reshape(1, HEAD_DIM)
    kn = k_norm[l].reshape(1, HEAD_DIM)
    w_in_bf = w_in[l].astype(BF16)
    w_out_bf = w_out[l].astype(BF16)
    w_pool_bf = w_pool[l].astype(BF16)
    ps = pool_scale[l].reshape(1, D_POOL)
    w_router_t = w_router[l].T
    wsg_bf, wsu_bf, wsd_bf = ws_gate[l].astype(BF16), ws_up[l].astype(BF16), ws_down[l].astype(BF16)

    rope_p = _rope_tables(jnp.arange(seq, dtype=I32))
    rope_s = _rope_tables(jnp.full((bd,), PAST_LEN, I32))
    xp2 = x_prompt.reshape(n_prompt, D_MODEL)
    xs2 = x_sample.reshape(bd, D_MODEL)
    assert n_keep % TOKEN_TILE == 0
    qp, kp, vp, up, kwin_p, vwin_p = _in_proj(xp2, mods_p[0], mods_p[1], g_mix_l, w_in_bf, qn, kn, rope_p,
                                              TOKEN_TILE, tiles_per_batch, tiles_per_batch, n_keep // TOKEN_TILE)
    qs, _, _, us, k_new, v_new = _in_proj(xs2, mods_s[0], mods_s[1], g_mix_l, w_in_bf, qn, kn, rope_s, bd, 1, 1, 1)

    to_seq = lambda t: t.reshape(nb, seq, -1)
    heads = lambda t: t.reshape(bd, N_HEADS, HEAD_DIM)
    ck, cv = cache_k_win[l], cache_v_win[l]
    k_new, v_new = heads(k_new), heads(v_new)
    attn_p, k_win_s, v_win_s = _attn_prompt(to_seq(qp), to_seq(kp), to_seq(vp), ck, cv, k_new, v_new)
    attn_p = attn_p.reshape(n_prompt, D_ATTN)
    pool_p = _pool_prompt(to_seq(up), w_pool_bf, ps).reshape(n_prompt, D_POOL)
    attn_s = _attn_decode(heads(qs), k_new, v_new, ck, cv).reshape(bd, D_ATTN)
    pool_s = _pool_decode(us, jnp.swapaxes(state_pool[l], 0, 1), w_pool_bf, ps)

    base_p, h2_all, lg_p = _out_proj(xp2, attn_p, pool_p, mods_p[2], mods_p[3], mods_p[4], mods_p[5],
                                     w_out_bf, g_ffn_l, w_router_t, wsg_bf, wsu_bf, wsd_bf,
                                     TOKEN_TILE, tiles_per_batch, n_tok, None, 0)
    base_s, h2_all, lg_s = _out_proj(xs2, attn_s, pool_s, mods_s[2], mods_s[3], mods_s[4], mods_s[5],
                                     w_out_bf, g_ffn_l, w_router_t, wsg_bf, wsu_bf, wsd_bf,
                                     bd, 1, n_tok, h2_all, n_prompt // bd)

    y_planes, wts_t = _moe_routed(h2_all, jnp.concatenate([lg_p, lg_s], axis=1), b_router[l],
                                  w_gate[l], w_up[l], w_down[l])

    yp = _combine(y_planes, wts_t[:n_prompt], base_p, mods_p[5], COMBINE_TILE, seq // COMBINE_TILE, 0)
    ys = _combine(y_planes, wts_t[n_prompt:n_tok], base_s, mods_s[5], bd, 1, n_prompt // bd)

    y_prompt = yp.reshape(nb, seq, D_MODEL)
    y_sample = ys.reshape(bd, dec_seq, D_MODEL)
    win = lambda t: t.reshape(1, nb, n_keep, N_HEADS, HEAD_DIM)
    pool_p_state = to_seq(up)[:, seq - POOL_STATE:][None]
    pool_s_state = jnp.concatenate([state_pool[l][:, 1:], us[:, None, :]], axis=1)[None]
    return (y_prompt, y_sample, win(kwin_p), win(vwin_p), pool_p_state,
            k_win_s[None], v_win_s[None], pool_s_state)
```

```python
import functools

import jax
import jax.numpy as jnp
from jax import lax
from jax.experimental import pallas as pl
from jax.experimental.pallas import tpu as pltpu

F32 = jnp.float32
BF16 = jnp.bfloat16
I32 = jnp.int32

D_MODEL = 2048
N_HEADS = 8
HEAD_DIM = 128
D_ATTN = N_HEADS * HEAD_DIM
D_POOL = D_MODEL - D_ATTN
D_IN = 3 * D_ATTN + D_POOL
ROPE_DIM = HEAD_DIM // 4
ROPE_HALF = ROPE_DIM // 2
ROPE_THETA = 500000.0
DILATED_GROUPS = ((128, 1), (512, 4), (2048, 16))
QBLOCK = 128
ATTN_SCALE = HEAD_DIM ** -0.5
POOL_WINDOWS = (2, 4, 8, 16)
POOL_GROUP_DIM = D_POOL // len(POOL_WINDOWS)
POOL_STATE = max(POOL_WINDOWS) - 1
POOL_HALO = 16
N_EXPERTS = 64
TOP_K = 8
N_EXPERT_GROUPS = 8
GROUP_SIZE = N_EXPERTS // N_EXPERT_GROUPS
TOPK_GROUPS = 4
D_EXPERT = 512
D_SHARED = 512
ROUTED_SCALE = 2.5
EPS = 1e-6
PAST_LEN = 16384
MAX_WINDOW = 2048

LANES = 128
ROW_CHUNKS = D_MODEL // LANES
NEG_BIG = -1e30
LOG2_E = 1.4426950408889634

TOKEN_TILE = 256
EXPERT_TILE = 256
COMBINE_TILE = 128
ROUTE_TILE = 256
VMEM_LIMIT = 60 * 1024 * 1024


def _cparams(sem, vmem=VMEM_LIMIT):
    return pltpu.CompilerParams(dimension_semantics=sem, vmem_limit_bytes=vmem)


def _ada_kernel(c_ref, w_ref, b_ref, o_ref):
    c = c_ref[...]
    s = (c * jax.nn.sigmoid(c)).astype(BF16)
    o_ref[...] = jnp.dot(s, w_ref[...].astype(BF16), preferred_element_type=F32) + b_ref[...]


def _ada(c_all, w_ada, b_ada):
    rows = c_all.shape[0]
    n = w_ada.shape[1]
    tn = 1024
    return pl.pallas_call(
        _ada_kernel,
        grid=(n // tn,),
        in_specs=[
            pl.BlockSpec((rows, D_MODEL), lambda j: (0, 0)),
            pl.BlockSpec((D_MODEL, tn), lambda j: (0, j)),
            pl.BlockSpec((1, tn), lambda j: (0, j)),
        ],
        out_specs=pl.BlockSpec((rows, tn), lambda j: (0, j)),
        out_shape=jax.ShapeDtypeStruct((rows, n), F32),
        compiler_params=_cparams(("parallel",)),
        name="ada_modulation",
    )(c_all, w_ada, b_ada.reshape(1, n))


def _rms(x):
    return x * lax.rsqrt(jnp.mean(x * x, axis=-1, keepdims=True) + EPS)


def _in_kernel(x_ref, sh_ref, sc_ref, g_ref, w_ref, qn_ref, kn_ref, c_ref, s1_ref, s2_ref,
               q_ref, k_ref, v_ref, u_ref, kw_ref, vw_ref):
    tm = x_ref.shape[0]
    x = x_ref[...]
    h = _rms(x) * g_ref[...] * (1.0 + sc_ref[0]) + sh_ref[0]
    hb = h.astype(BF16)
    cos = c_ref[...]
    s1 = s1_ref[...]
    s2 = s2_ref[...]

    def head_rows(win_ref, hd, val):
        win_ref[pl.ds(hd, tm, stride=N_HEADS), :] = val

    def qk(sec, nrm, out_ref, win_ref):
        z = jnp.dot(hb, w_ref[:, sec * D_ATTN:(sec + 1) * D_ATTN], preferred_element_type=F32)
        for hd in range(N_HEADS):
            sl = slice(hd * HEAD_DIM, (hd + 1) * HEAD_DIM)
            r = _rms(z[:, sl]) * nrm
            r = r * cos + pltpu.roll(r, HEAD_DIM - ROPE_HALF, 1) * s1 + pltpu.roll(r, ROPE_HALF, 1) * s2
            out_ref[:, sl] = r
            if win_ref is not None:
                head_rows(win_ref, hd, r)

    qk(0, qn_ref[...], q_ref, None)
    qk(1, kn_ref[...], k_ref, kw_ref)
    v = jnp.dot(hb, w_ref[:, 2 * D_ATTN:3 * D_ATTN], preferred_element_type=F32)
    v_ref[...] = v
    for hd in range(N_HEADS):
        head_rows(vw_ref, hd, v[:, hd * HEAD_DIM:(hd + 1) * HEAD_DIM])
    u_ref[...] = jnp.dot(hb, w_ref[:, 3 * D_ATTN:], preferred_element_type=F32)


def _in_proj(x2d, shift, scale, g_mix, w_in_bf, q_norm, k_norm, rope, tm, rows_per_mod, rope_tiles, keep_tiles):
    m = x2d.shape[0]
    r = shift.shape[1]
    n_groups = m // tm // rows_per_mod
    skip = rows_per_mod - keep_tiles
    mod_spec = pl.BlockSpec((1, r, D_MODEL), lambda i: (i // rows_per_mod, 0, 0))
    rope_spec = pl.BlockSpec((tm, HEAD_DIM), lambda i: (i % rope_tiles, 0))
    out_spec = pl.BlockSpec((tm, D_ATTN), lambda i: (i, 0))
    out_sd = jax.ShapeDtypeStruct((m, D_ATTN), F32)
    win_spec = pl.BlockSpec(
        (tm * N_HEADS, HEAD_DIM),
        lambda i: ((i // rows_per_mod) * keep_tiles + jnp.maximum(i % rows_per_mod - skip, 0), 0))
    win_sd = jax.ShapeDtypeStruct((n_groups * keep_tiles * tm * N_HEADS, HEAD_DIM), F32)
    return pl.pallas_call(
        _in_kernel,
        grid=(m // tm,),
        in_specs=[
            pl.BlockSpec((tm, D_MODEL), lambda i: (i, 0)),
            mod_spec, mod_spec,
            pl.BlockSpec((1, D_MODEL), lambda i: (0, 0)),
            pl.BlockSpec((D_MODEL, D_IN), lambda i: (0, 0)),
            pl.BlockSpec((1, HEAD_DIM), lambda i: (0, 0)),
            pl.BlockSpec((1, HEAD_DIM), lambda i: (0, 0)),
            rope_spec, rope_spec, rope_spec,
        ],
        out_specs=[out_spec] * 4 + [win_spec] * 2,
        out_shape=[out_sd] * 4 + [win_sd] * 2,
        compiler_params=_cparams(("arbitrary",)),
        name="in_proj",
    )(x2d, shift, scale, g_mix, w_in_bf, q_norm, k_norm, *rope)


def _rope_tables(pos):
    inv_freq = jnp.float32(ROPE_THETA) ** (-jnp.arange(ROPE_HALF, dtype=F32) / ROPE_HALF)
    ang = pos.astype(F32)[:, None] * inv_freq[None, :]
    cos, sin = jnp.cos(ang), jnp.sin(ang)
    t = pos.shape[0]
    rest = HEAD_DIM - ROPE_DIM
    c = jnp.concatenate([cos, cos, jnp.ones((t, rest), F32)], axis=1)
    s1 = jnp.concatenate([-sin, jnp.zeros((t, ROPE_HALF + rest), F32)], axis=1)
    s2 = jnp.concatenate([jnp.zeros((t, ROPE_HALF), F32), sin, jnp.zeros((t, rest), F32)], axis=1)
    return c, s1, s2


ATTN_INTERLEAVE = 4
WINDOW_ROWS_PER_STEP = 2
REGROUP_MIN_DILATION = 16


def _attn_prompt_kernel(q_ref, k_ref, v_ref, ck_hbm, cv_hbm, kn_ref, vn_ref, o_ref, ok_hbm, ov_hbm,
                        m_scr, l_scr, acc_scr, q_rm, k_rm, v_rm, wbuf, isem, osem):
    seq = q_ref.shape[1]
    n_buf = wbuf.shape[1]
    step = pl.program_id(0) * pl.num_programs(1) + pl.program_id(1)
    last_step = pl.num_programs(0) * pl.num_programs(1) - 1
    windows = ((ck_hbm, kn_ref, ok_hbm), (cv_hbm, vn_ref, ov_hbm))

    def window_in(row, a):
        return pltpu.make_async_copy(windows[a][0].at[row, pl.ds(1, n_buf - 1)],
                                     wbuf.at[a, pl.ds(0, n_buf - 1)], isem.at[a])

    def window_out(row, a):
        return pltpu.make_async_copy(wbuf.at[a], windows[a][2].at[row], osem.at[a])

    def start_in(row):
        for a in range(2):
            window_in(row, a).start()

    def finish_in_start_out(row):
        for a in range(2):
            window_in(row, a).wait()
            wbuf[a, n_buf - 1] = windows[a][1][row]
            window_out(row, a).start()

    def finish_out(row):
        for a in range(2):
            window_out(row, a).wait()

    row0 = step * WINDOW_ROWS_PER_STEP
    window_phases = [
        lambda: start_in(row0),
        lambda: finish_in_start_out(row0),
        lambda: (finish_out(row0), start_in(row0 + 1)),
        lambda: finish_in_start_out(row0 + 1),
    ]
    assert WINDOW_ROWS_PER_STEP == 2 and len(window_phases) == len(DILATED_GROUPS) + 1

    @pl.when(step > 0)
    def _():
        finish_out(row0 - 1)
    window_phases[0]()
    m_scr[...] = jnp.full(m_scr.shape, NEG_BIG, F32)
    l_scr[...] = jnp.zeros(l_scr.shape, F32)
    acc_scr[...] = jnp.zeros(acc_scr.shape, F32)
    qi = lax.broadcasted_iota(I32, (QBLOCK, 2 * QBLOCK), 0)
    kj = lax.broadcasted_iota(I32, (QBLOCK, 2 * QBLOCK), 1)
    dist = qi + QBLOCK - kj
    band_mask = jnp.logical_and(dist >= 0, dist <= QBLOCK)
    bias_band = jnp.where(band_mask, 0.0, NEG_BIG)
    bias_first = jnp.where(jnp.logical_and(band_mask, kj >= QBLOCK), 0.0, NEG_BIG)
    ones = jnp.ones((2 * QBLOCK, HEAD_DIM), BF16)
    dims = (((1,), (1,)), ((), ()))
    q_scale = ATTN_SCALE * LOG2_E

    for window, dil in DILATED_GROUPS:
        assert window // dil == QBLOCK
        n_units = seq // QBLOCK
        assert n_units % ATTN_INTERLEAVE == 0 and (dil % ATTN_INTERLEAVE == 0 or dil == 1)

        def rows(start, dil=dil):
            return pl.ds(start, QBLOCK) if dil == 1 else pl.ds(start, QBLOCK, stride=dil)

        sub_len = seq // dil
        regroup = dil >= REGROUP_MIN_DILATION
        if regroup:
            for src, dst in ((q_ref, q_rm), (k_ref, k_rm), (v_ref, v_rm)):
                for r in range(dil):
                    dst[pl.ds(r * sub_len, sub_len), :] = src[0, pl.ds(r, sub_len, stride=dil), :]

        def qkv_blocks(blk, res, start, prev, rows=rows, regroup=regroup, sub_len=sub_len):
            if regroup:
                cur = pl.ds(res * sub_len + blk * QBLOCK, QBLOCK)
                prv = pl.ds(res * sub_len + jnp.maximum(blk - 1, 0) * QBLOCK, QBLOCK)
                return (q_rm[cur, :], k_rm[prv, :], k_rm[cur, :], v_rm[prv, :], v_rm[cur, :])
            return (q_ref[0, rows(start), :], k_ref[0, rows(prev), :], k_ref[0, rows(start), :],
                    v_ref[0, rows(prev), :], v_ref[0, rows(start), :])

        def body(it, carry, dil=dil, rows=rows, qkv_blocks=qkv_blocks):
            loaded = []
            for u in range(ATTN_INTERLEAVE):
                f = it * ATTN_INTERLEAVE + u
                blk, res = f // dil, f % dil
                start = blk * (QBLOCK * dil) + res
                prev = jnp.maximum(start - QBLOCK * dil, res)
                q_blk, k_prev, k_cur, v_prev, v_cur = qkv_blocks(blk, res, start, prev)
                q = (q_blk * q_scale).astype(BF16)
                k = jnp.concatenate([k_prev, k_cur], axis=0).astype(BF16)
                v = jnp.concatenate([v_prev, v_cur], axis=0).astype(BF16)
                v_ones = jnp.concatenate([v, ones], axis=1)
                state = (m_scr[rows(start), :], l_scr[rows(start), :], acc_scr[rows(start), :])
                loaded.append((start, blk, q, k, v_ones, state))
            scores = [lax.dot_general(q, k, dims, preferred_element_type=F32)
                      + jnp.where(blk > 0, bias_band, bias_first)
                      for _, blk, q, k, _, _ in loaded]
            probs = []
            for s, (_, _, _, _, _, (m_old, _, _)) in zip(scores, loaded):
                m_new = jnp.maximum(m_old, jnp.max(s, axis=1, keepdims=True))
                p = jnp.exp2(s - jnp.concatenate([m_new, m_new], axis=1)).astype(BF16)
                probs.append((m_new, jnp.exp2(m_old - m_new), p))
            updated = []
            for (m_new, alpha, p), (start, _, _, _, v_ones, (_, l_old, a_old)) in zip(probs, loaded):
                pv = jnp.dot(p, v_ones, preferred_element_type=F32)
                updated.append((start, m_new, alpha * l_old + pv[:, HEAD_DIM:], alpha * a_old + pv[:, :HEAD_DIM]))
            for start, m_new, l_new, a_new in updated:
                m_scr[rows(start), :] = m_new
                l_scr[rows(start), :] = l_new
                acc_scr[rows(start), :] = a_new
            return carry

        trips = n_units // ATTN_INTERLEAVE
        phase = window_phases[DILATED_GROUPS.index((window, dil)) + 1]
        if (window, dil) == DILATED_GROUPS[-1]:
            lax.fori_loop(0, trips // 2, body, 0)
            phase()
            lax.fori_loop(trips // 2, trips, body, 0)
        else:
            lax.fori_loop(0, trips, body, 0)
            phase()

    @pl.when(step == last_step)
    def _():
        finish_out(row0 + 1)

    o_ref[0] = acc_scr[...] / l_scr[...]


def _attn_prompt(q, k, v, cache_k, cache_v, k_new, v_new):
    b, s, _ = q.shape
    bd, n_buf = cache_k.shape[0], cache_k.shape[1]
    assert bd == b * N_HEADS * WINDOW_ROWS_PER_STEP
    spec = pl.BlockSpec((1, s, HEAD_DIM), lambda bi, hi: (bi, 0, hi))
    any_spec = pl.BlockSpec(memory_space=pl.ANY)
    new_spec = pl.BlockSpec(k_new.shape, lambda bi, hi: (0, 0, 0))
    win_sd = jax.ShapeDtypeStruct(cache_k.shape, cache_k.dtype)
    return pl.pallas_call(
        _attn_prompt_kernel,
        grid=(b, N_HEADS),
        in_specs=[spec, spec, spec, any_spec, any_spec, new_spec, new_spec],
        out_specs=[spec, any_spec, any_spec],
        out_shape=[jax.ShapeDtypeStruct((b, s, D_ATTN), F32), win_sd, win_sd],
        scratch_shapes=[pltpu.VMEM((s, HEAD_DIM), F32)] * 6 + [
            pltpu.VMEM((2, n_buf) + cache_k.shape[2:], F32),
            pltpu.SemaphoreType.DMA((2,)),
            pltpu.SemaphoreType.DMA((2,)),
        ],
        compiler_params=_cparams(("arbitrary", "arbitrary")),
        name="attn_prompt",
    )(q, k, v, cache_k, cache_v, k_new, v_new)


def _attn_decode_kernel(q_ref, kn_ref, vn_ref, k1_ref, k2_ref, k3_ref, v1_ref, v2_ref, v3_ref, o_ref):
    q = q_ref[0]
    k_new = kn_ref[0]
    v_new = vn_ref[0]
    n_grp = len(DILATED_GROUPS)

    def scores(kb):
        return jnp.sum(kb * q[None], axis=-1, keepdims=True) * ATTN_SCALE

    kcs = [k1_ref[0], k2_ref[0][:, 0], k3_ref[0][:, 0]]
    vcs = [v1_ref[0], v2_ref[0][:, 0], v3_ref[0][:, 0]]
    s_new = jnp.sum(k_new * q, axis=-1, keepdims=True) * ATTN_SCALE
    s_grp = [scores(kb) for kb in kcs]
    m = s_new
    for s in s_grp:
        m = jnp.maximum(m, jnp.max(s, axis=0))
    p_new = jnp.exp(s_new - m)
    den = n_grp * p_new
    num = n_grp * p_new * v_new
    for s, vb in zip(s_grp, vcs):
        p = jnp.exp(s - m[None])
        den = den + jnp.sum(p, axis=0)
        num = num + jnp.sum(p * vb, axis=0)
    o_ref[0] = num / den


def _attn_decode(q, k_new, v_new, cache_k, cache_v):
    bd, n_buf = cache_k.shape[0], cache_k.shape[1]
    assert n_buf == MAX_WINDOW

    def views(cache):
        out, specs = [], []
        for window, dil in DILATED_GROUPS:
            band = window // dil
            assert band == QBLOCK and n_buf % dil == 0 and n_buf // dil >= band
            if dil == 1:
                out.append(cache)
                specs.append(pl.BlockSpec((1, band, N_HEADS, HEAD_DIM),
                                          lambda b, nb=n_buf // band: (b, nb - 1, 0, 0)))
            else:
                out.append(cache.reshape(bd, n_buf // dil, dil, N_HEADS, HEAD_DIM))
                specs.append(pl.BlockSpec((1, band, 1, N_HEADS, HEAD_DIM),
                                          lambda b, nb=n_buf // dil // band: (b, nb - 1, 0, 0, 0)))
        return out, specs

    kv, kspecs = views(cache_k)
    vv, vspecs = views(cache_v)
    tok = pl.BlockSpec((1, N_HEADS, HEAD_DIM), lambda b: (b, 0, 0))
    return pl.pallas_call(
        _attn_decode_kernel,
        grid=(bd,),
        in_specs=[tok, tok, tok] + kspecs + vspecs,
        out_specs=tok,
        out_shape=jax.ShapeDtypeStruct((bd, N_HEADS, HEAD_DIM), F32),
        compiler_params=_cparams(("parallel",)),
        name="attn_decode",
    )(q, k_new, v_new, *kv, *vv)


def _pool_project(d_groups, wp_ref, ps_ref, out_ref):
    for g, d in enumerate(d_groups):
        sl = slice(g * POOL_GROUP_DIM, (g + 1) * POOL_GROUP_DIM)
        out_ref[:, sl] = jnp.dot(d.astype(BF16), wp_ref[g], preferred_element_type=F32) * ps_ref[:, sl]


def _pool_prompt_kernel(u_ref, prev_ref, wp_ref, ps_ref, o_ref):
    i = pl.program_id(1)
    tp = u_ref.shape[1]
    u = u_ref[0]
    prev = jnp.where(i > 0, prev_ref[0], 0.0)
    ext = jnp.concatenate([prev, u], axis=0)
    pos = i * tp + lax.broadcasted_iota(I32, (tp, 1), 0)
    d_groups = []
    for g, w in enumerate(POOL_WINDOWS):
        sl = slice(g * POOL_GROUP_DIM, (g + 1) * POOL_GROUP_DIM)
        a = ext[:, sl]
        span = 1
        while span < w:
            a = a + pltpu.roll(a, span, 0)
            span *= 2
        win = a[POOL_HALO:, :]
        cnt = jnp.minimum(w, pos + 1).astype(F32)
        d_groups.append(win / cnt - u[:, sl])
    _pool_project(d_groups, wp_ref, ps_ref, o_ref.at[0])


def _pool_prompt(u, w_pool_bf, pool_scale):
    b, s, _ = u.shape
    tp = 512
    halo_blocks = tp // POOL_HALO
    return pl.pallas_call(
        _pool_prompt_kernel,
        grid=(b, s // tp),
        in_specs=[
            pl.BlockSpec((1, tp, D_POOL), lambda bi, i: (bi, i, 0)),
            pl.BlockSpec((1, POOL_HALO, D_POOL), lambda bi, i: (bi, jnp.maximum(i * halo_blocks - 1, 0), 0)),
            pl.BlockSpec(w_pool_bf.shape, lambda bi, i: (0, 0, 0)),
            pl.BlockSpec((1, D_POOL), lambda bi, i: (0, 0)),
        ],
        out_specs=pl.BlockSpec((1, tp, D_POOL), lambda bi, i: (bi, i, 0)),
        out_shape=jax.ShapeDtypeStruct((b, s, D_POOL), F32),
        compiler_params=_cparams(("parallel", "parallel")),
        name="pool_prompt",
    )(u, u, w_pool_bf, pool_scale)


def _pool_decode_kernel(u_ref, st_ref, wp_ref, ps_ref, o_ref):
    u = u_ref[...]
    d_groups = []
    for g, w in enumerate(POOL_WINDOWS):
        sl = slice(g * POOL_GROUP_DIM, (g + 1) * POOL_GROUP_DIM)
        win = u[:, sl]
        for j in range(1, w):
            win = win + st_ref[POOL_STATE - j][:, sl]
        d_groups.append(win / float(w) - u[:, sl])
    _pool_project(d_groups, wp_ref, ps_ref, o_ref)


def _pool_decode(u, state_t, w_pool_bf, pool_scale):
    assert PAST_LEN + 1 >= max(POOL_WINDOWS)
    return pl.pallas_call(
        _pool_decode_kernel,
        out_shape=jax.ShapeDtypeStruct(u.shape, F32),
        compiler_params=_cparams(None),
        name="pool_decode",
    )(u, state_t, w_pool_bf, pool_scale)


def _split_bf16(x):
    hi = x.astype(BF16)
    lo = (x - hi.astype(F32)).astype(BF16)
    return hi, lo


def _out_kernel(x_ref, a_ref, p_ref, g1_ref, sh_ref, sc_ref, g2_ref, wo_ref, gf_ref, wr_ref,
                wsg_ref, wsu_ref, wsd_ref, base_ref, h2_ref, lg_ref):
    tm = x_ref.shape[0]
    mix = (jnp.dot(a_ref[...].astype(BF16), wo_ref[:D_ATTN, :], preferred_element_type=F32)
           + jnp.dot(p_ref[...].astype(BF16), wo_ref[D_ATTN:, :], preferred_element_type=F32))
    x1 = x_ref[...] + g1_ref[0] * mix
    h2 = _rms(x1) * gf_ref[...] * (1.0 + sc_ref[0]) + sh_ref[0]
    for c in range(ROW_CHUNKS):
        h2_ref[pl.ds(c, tm, stride=ROW_CHUNKS), :] = h2[:, c * LANES:(c + 1) * LANES]
    h_hi, h_lo = _split_bf16(h2)
    w_hi, w_lo = _split_bf16(wr_ref[...])
    dims = (((1,), (1,)), ((), ()))
    lg_ref[...] = (lax.dot_general(w_hi, h_hi, dims, preferred_element_type=F32)
                   + lax.dot_general(w_hi, h_lo, dims, preferred_element_type=F32)
                   + lax.dot_general(w_lo, h_hi, dims, preferred_element_type=F32))
    sg = jnp.dot(h_hi, wsg_ref[...], preferred_element_type=F32)
    su = jnp.dot(h_hi, wsu_ref[...], preferred_element_type=F32)
    hs = (sg * jax.nn.sigmoid(sg) * su).astype(BF16)
    base_ref[...] = x1 + g2_ref[0] * jnp.dot(hs, wsd_ref[...], preferred_element_type=F32)


def _out_proj(x2d, attn, pool, gate1, shift2, scale2, gate2, w_out_bf, g_ffn, w_router_t,
              ws_gate_bf, ws_up_bf, ws_down_bf, tm, rows_per_mod, h2_tokens, h2_all, h2_block0):
    m = x2d.shape[0]
    r = gate1.shape[1]
    n_main = m // tm
    h2_rows = h2_tokens * ROW_CHUNKS
    n_fill = 0 if h2_all is not None else pl.cdiv(h2_rows - m * ROW_CHUNKS, tm * ROW_CHUNKS)
    last = n_main - 1
    row = lambda i: jnp.minimum(i, last)
    mod_spec = pl.BlockSpec((1, r, D_MODEL), lambda i: (row(i) // rows_per_mod, 0, 0))
    const = lambda shape: pl.BlockSpec(shape, lambda i: (0,) * len(shape))
    in_specs = [
        pl.BlockSpec((tm, D_MODEL), lambda i: (row(i), 0)),
        pl.BlockSpec((tm, D_ATTN), lambda i: (row(i), 0)),
        pl.BlockSpec((tm, D_POOL), lambda i: (row(i), 0)),
        mod_spec, mod_spec, mod_spec, mod_spec,
        const((D_MODEL, D_MODEL)),
        const((1, D_MODEL)),
        const((N_EXPERTS, D_MODEL)),
        const((D_MODEL, D_SHARED)), const((D_MODEL, D_SHARED)), const((D_SHARED, D_MODEL)),
    ]
    args = [x2d, attn, pool, gate1, shift2, scale2, gate2, w_out_bf, g_ffn, w_router_t,
            ws_gate_bf, ws_up_bf, ws_down_bf]
    aliases = {}
    n_in = len(args)
    if h2_all is not None:
        in_specs.append(pl.BlockSpec(memory_space=pl.ANY))
        args.append(h2_all)
        aliases = {n_in: 1}

    def kernel(*refs):
        refs = refs[:n_in] + refs[len(args):]
        if n_fill == 0:
            _out_kernel(*refs)
            return
        step = pl.program_id(0)
        pl.when(step < n_main)(lambda: _out_kernel(*refs))

        @pl.when(step >= n_main)
        def _():
            h2_ref = refs[n_in + 1]
            h2_ref[...] = jnp.zeros(h2_ref.shape, F32)

    return pl.pallas_call(
        kernel,
        grid=(n_main + n_fill,),
        in_specs=in_specs,
        out_specs=[
            pl.BlockSpec((tm, D_MODEL), lambda i: (row(i), 0)),
            pl.BlockSpec((tm * ROW_CHUNKS, LANES), lambda i: (i + h2_block0, 0)),
            pl.BlockSpec((N_EXPERTS, tm), lambda i: (0, row(i))),
        ],
        out_shape=[
            jax.ShapeDtypeStruct((m, D_MODEL), F32),
            jax.ShapeDtypeStruct((h2_rows, LANES), F32),
            jax.ShapeDtypeStruct((N_EXPERTS, m), F32),
        ],
        input_output_aliases=aliases,
        compiler_params=_cparams(("arbitrary",)),
        name="out_proj",
    )(*args)


def _first_index_of_max(v, iota, size, axis):
    m = jnp.max(v, axis=axis, keepdims=True)
    idx = jnp.min(jnp.where(v == m, iota, size), axis=axis, keepdims=True)
    return m, idx


def _route_kernel(n_valid, lg_ref, b_ref, eidx_ref, wts_ref, cnt_ref, carry):
    i = pl.program_id(0)
    tn = lg_ref.shape[1]

    @pl.when(i == 0)
    def _():
        carry[...] = jnp.zeros(carry.shape, F32)

    scores = jax.nn.sigmoid(lg_ref[...])
    biased = scores + b_ref[...]
    grp = biased.reshape(N_EXPERT_GROUPS, GROUP_SIZE, tn)
    io_g = lax.broadcasted_iota(I32, grp.shape, 1)
    m1, i1 = _first_index_of_max(grp, io_g, GROUP_SIZE, 1)
    m2 = jnp.max(jnp.where(io_g == i1, -jnp.inf, grp), axis=1, keepdims=True)
    gscore = (m1 + m2)[:, 0, :]
    io_n = lax.broadcasted_iota(I32, gscore.shape, 0)
    gsel = jnp.zeros(gscore.shape, jnp.bool_)
    for _ in range(TOPK_GROUPS):
        _, gi = _first_index_of_max(gscore, io_n, N_EXPERT_GROUPS, 0)
        hit = io_n == gi
        gsel = jnp.logical_or(gsel, hit)
        gscore = jnp.where(hit, -jnp.inf, gscore)
    emask = jnp.broadcast_to(gsel[:, None, :], grp.shape).reshape(N_EXPERTS, tn)
    cand = jnp.where(emask, biased, -jnp.inf)
    io_e = lax.broadcasted_iota(I32, cand.shape, 0)
    picked = jnp.zeros(cand.shape, jnp.bool_)
    eidx, sel = [], []
    for _ in range(TOP_K):
        _, ei = _first_index_of_max(cand, io_e, N_EXPERTS, 0)
        hit = io_e == ei
        eidx.append(ei)
        sel.append(jnp.sum(jnp.where(hit, scores, 0.0), axis=0, keepdims=True))
        picked = jnp.logical_or(picked, hit)
        cand = jnp.where(hit, -jnp.inf, cand)
    sel = jnp.concatenate(sel, axis=0)
    eidx = jnp.concatenate(eidx, axis=0)
    wts_ref[...] = sel / jnp.sum(sel, axis=0, keepdims=True) * ROUTED_SCALE
    eidx_ref[...] = eidx

    tok = i * tn + lax.broadcasted_iota(I32, cand.shape, 1)
    mask = jnp.logical_and(picked, tok < n_valid).astype(F32)
    total = carry[...] + jnp.sum(mask, axis=1, keepdims=True)
    carry[...] = total
    cnt_ref[...] = total.astype(I32)


def _route(logits_t, b_router, n_valid):
    mp = logits_t.shape[1]
    tn = ROUTE_TILE
    tok_spec = pl.BlockSpec((TOP_K, tn), lambda i: (0, i))
    return pl.pallas_call(
        functools.partial(_route_kernel, n_valid),
        grid=(mp // tn,),
        in_specs=[
            pl.BlockSpec((N_EXPERTS, tn), lambda i: (0, i)),
            pl.BlockSpec((N_EXPERTS, 1), lambda i: (0, 0)),
        ],
        out_specs=[tok_spec, tok_spec, pl.BlockSpec((N_EXPERTS, LANES), lambda i: (0, 0))],
        out_shape=[
            jax.ShapeDtypeStruct((TOP_K, mp), I32),
            jax.ShapeDtypeStruct((TOP_K, mp), F32),
            jax.ShapeDtypeStruct((N_EXPERTS, LANES), I32),
        ],
        scratch_shapes=[pltpu.VMEM((N_EXPERTS, LANES), F32)],
        compiler_params=_cparams(("arbitrary",)),
        name="route",
    )(logits_t, b_router.reshape(N_EXPERTS, 1))


TILE_USED = 1
TILE_FIRST = 2
TILE_WSLOT = 4
TILE_PREV_USED = 8
PAIR_BITS = 3
assert 1 << PAIR_BITS == TOP_K
DUMMY_TOKENS = 2 * EXPERT_TILE // TOP_K
GATHER_SLOTS = 3


def _experts_kernel(n_tok, te_ref, ts_ref, tn_ref, tf_ref, ne_ref, g_ref, s_ref,
                    h_hbm, wg_hbm, wu_hbm, wd_hbm, y_hbm,
                    xbuf, yrows, ybuf, xs, wg_f, wu_f, wd_f, wg_bf, wu_bf, wd_bf, gsem, ssem, wsem, zsem):
    j = pl.program_id(0)
    last = pl.num_programs(0) - 1
    tm = EXPERT_TILE
    rc = ROW_CHUNKS
    plane_tokens = n_tok + DUMMY_TOKENS
    slot = j % 2
    other = 1 - slot
    gslot = j % GATHER_SLOTS
    flags = tf_ref[j]
    used = (flags & TILE_USED) != 0
    first = (flags & TILE_FIRST) != 0
    prev_used = (flags & TILE_PREV_USED) != 0
    wslot = (flags // TILE_WSLOT) & 1
    prv = jnp.maximum(j - 1, 0)
    ahead = jnp.minimum(j + GATHER_SLOTS - 1, last)

    def buf_rows(i):
        return pl.ds(i * rc, rc) if isinstance(i, int) else pl.ds(pl.multiple_of(i * rc, rc), rc)

    def gather_row(i, start, dst):
        row = g_ref[start + i]
        pltpu.make_async_copy(h_hbm.at[pl.ds(pl.multiple_of(row, rc), rc), :],
                              xbuf.at[dst, buf_rows(i), :], gsem.at[dst]).start(priority=0)

    def scatter_row(i, start, n_valid, src):
        spare = ((i & (TOP_K - 1)) * plane_tokens + n_tok + (i >> PAIR_BITS)) * rc + src * ((tm >> PAIR_BITS) * rc)
        row = jnp.where(i < n_valid, s_ref[start + i], spare)
        pltpu.make_async_copy(ybuf.at[src, buf_rows(i), :],
                              y_hbm.at[pl.ds(pl.multiple_of(row, rc), rc), :],
                              ssem.at[src]).start(priority=i % 2 if isinstance(i, int) else 0)

    def scatter_tile_loop(tile, src):
        start, n_valid = ts_ref[tile], tn_ref[tile]
        lax.fori_loop(0, tm, lambda i, c: (scatter_row(i, start, n_valid, src), c)[1], 0)

    def wait_gather(dst):
        pltpu.make_async_copy(h_hbm.at[pl.ds(0, tm * rc), :], xbuf.at[dst], gsem.at[dst]).wait()

    def wait_scatter(src):
        pltpu.make_async_copy(ybuf.at[src], y_hbm.at[pl.ds(0, tm * rc), :], ssem.at[src]).wait()

    def weight_copies(e, ws):
        return [pltpu.make_async_copy(src.at[e], dst.at[ws], wsem.at[ws])
                for src, dst in ((wg_hbm, wg_f), (wu_hbm, wu_f), (wd_hbm, wd_f))]

    @pl.when(j == 0)
    def _():
        ybuf[1] = jnp.zeros(ybuf.shape[1:], ybuf.dtype)
        fills = [pltpu.make_async_copy(ybuf.at[1, pl.ds(0, DUMMY_TOKENS * rc), :],
                                       y_hbm.at[pl.ds((k * plane_tokens + n_tok) * rc, DUMMY_TOKENS * rc), :],
                                       zsem.at[0]) for k in range(TOP_K)]
        for c in fills:
            c.start()
        for c in fills:
            c.wait()
        for c in weight_copies(te_ref[0], 0):
            c.start(priority=1)
        for t in range(GATHER_SLOTS - 1):
            start_t = ts_ref[jnp.minimum(t, last)]
            lax.fori_loop(0, tm, lambda i, c, start_t=start_t, t=t: (gather_row(i, start_t, t), c)[1], 0)

    @pl.when(first)
    def _():
        for c in weight_copies(te_ref[j], wslot):
            c.wait()
        wg_bf[...] = wg_f[wslot].astype(BF16)
        wu_bf[...] = wu_f[wslot].astype(BF16)
        wd_bf[...] = wd_f[wslot].astype(BF16)
        nxt_e = ne_ref[j]

        @pl.when(nxt_e >= 0)
        def _():
            for c in weight_copies(nxt_e, 1 - wslot):
                c.start(priority=1)

    @pl.when(jnp.logical_and(used, j > 0))
    def _():
        wait_scatter(slot)

    @pl.when(used)
    def _():
        wait_gather(gslot)
        xs[...] = jnp.concatenate(
            [xbuf[gslot, pl.ds(c, tm, stride=rc), :] for c in range(rc)], axis=1).astype(BF16)
        start_n = ts_ref[ahead]
        ahead_slot = (j + GATHER_SLOTS - 1) % GATHER_SLOTS
        for i in range(tm):
            gather_row(i, start_n, ahead_slot)
        start_p = ts_ref[prv]
        n_valid_p = jnp.where(j > 0, tn_ref[prv], 0)
        for i in range(tm):
            scatter_row(i, start_p, n_valid_p, other)
        x = xs[...]
        hg = jnp.dot(x, wg_bf[...], preferred_element_type=F32)
        hu = jnp.dot(x, wu_bf[...], preferred_element_type=F32)
        h = (hg * jax.nn.sigmoid(hg) * hu).astype(BF16)
        y = jnp.dot(h, wd_bf[...], preferred_element_type=F32)
        for c in range(rc):
            yrows[pl.ds(c, tm, stride=rc), :] = y[:, c * LANES:(c + 1) * LANES]
        ybuf[slot] = yrows[...].astype(ybuf.dtype)

    @pl.when(jnp.logical_and(jnp.logical_not(used), prev_used))
    def _():
        for t in range(GATHER_SLOTS - 1):
            wait_gather((j + t) % GATHER_SLOTS)
        wait_scatter(slot)
        scatter_tile_loop(prv, other)
        wait_scatter(other)

    @pl.when(jnp.logical_and(used, j == last))
    def _():
        for t in range(1, GATHER_SLOTS):
            wait_gather((j + t) % GATHER_SLOTS)
        wait_scatter(other)
        scatter_tile_loop(j, slot)
        wait_scatter(slot)


def _experts(tile_expert, tile_start, tile_rows, tile_flags, next_expert, gather_rows, scatter_rows, h2_all,
             w_gate, w_up, w_down, n_tok):
    n_tiles = tile_expert.shape[0]
    tm = EXPERT_TILE
    any_spec = pl.BlockSpec(memory_space=pl.ANY)
    grid_spec = pltpu.PrefetchScalarGridSpec(
        num_scalar_prefetch=7,
        grid=(n_tiles,),
        in_specs=[any_spec] * 4,
        out_specs=any_spec,
        scratch_shapes=[
            pltpu.VMEM((GATHER_SLOTS, tm * ROW_CHUNKS, LANES), F32),
            pltpu.VMEM((tm * ROW_CHUNKS, LANES), F32),
            pltpu.VMEM((2, tm * ROW_CHUNKS, LANES), BF16),
            pltpu.VMEM((tm, D_MODEL), BF16),
            pltpu.VMEM((2, D_MODEL, D_EXPERT), F32),
            pltpu.VMEM((2, D_MODEL, D_EXPERT), F32),
            pltpu.VMEM((2, D_EXPERT, D_MODEL), F32),
            pltpu.VMEM((D_MODEL, D_EXPERT), BF16),
            pltpu.VMEM((D_MODEL, D_EXPERT), BF16),
            pltpu.VMEM((D_EXPERT, D_MODEL), BF16),
            pltpu.SemaphoreType.DMA((GATHER_SLOTS,)),
            pltpu.SemaphoreType.DMA((2,)),
            pltpu.SemaphoreType.DMA((2,)),
            pltpu.SemaphoreType.DMA((1,)),
        ],
    )
    return pl.pallas_call(
        functools.partial(_experts_kernel, n_tok),
        grid_spec=grid_spec,
        out_shape=jax.ShapeDtypeStruct((TOP_K * (n_tok + DUMMY_TOKENS) * ROW_CHUNKS, LANES), BF16),
        compiler_params=_cparams(("arbitrary",)),
        name="experts",
    )(tile_expert, tile_start, tile_rows, tile_flags, next_expert, gather_rows, scatter_rows, h2_all,
      w_gate, w_up, w_down)


def _combine_kernel(y_ref, w_ref, base_ref, g2_ref, o_ref, yf):
    tc = base_ref.shape[0]
    w = w_ref[...]
    g2 = g2_ref[0]
    yf[...] = y_ref[...].astype(F32)
    for c in range(ROW_CHUNKS):
        acc = jnp.zeros((tc, LANES), F32)
        for k in range(TOP_K):
            acc = acc + w[:, k:k + 1] * yf[k, pl.ds(c, tc, stride=ROW_CHUNKS), :]
        sl = slice(c * LANES, (c + 1) * LANES)
        o_ref[:, sl] = base_ref[:, sl] + g2[:, sl] * acc


def _combine(y_planes, wts, base, gate2, tc, rows_per_mod, block0):
    m = base.shape[0]
    r = gate2.shape[1]
    return pl.pallas_call(
        _combine_kernel,
        grid=(m // tc,),
        in_specs=[
            pl.BlockSpec((TOP_K, tc * ROW_CHUNKS, LANES), lambda i: (0, i + block0, 0)),
            pl.BlockSpec((tc, TOP_K), lambda i: (i, 0)),
            pl.BlockSpec((tc, D_MODEL), lambda i: (i, 0)),
            pl.BlockSpec((1, r, D_MODEL), lambda i: (i // rows_per_mod, 0, 0)),
        ],
        out_specs=pl.BlockSpec((tc, D_MODEL), lambda i: (i, 0)),
        out_shape=jax.ShapeDtypeStruct((m, D_MODEL), F32),
        scratch_shapes=[pltpu.VMEM((TOP_K, tc * ROW_CHUNKS, LANES), F32)],
        compiler_params=_cparams(("parallel",)),
        name="combine",
    )(y_planes, wts, base, gate2)


def _moe_routed(h2_all, logits_t, b_router, w_gate, w_up, w_down):
    n_tok = logits_t.shape[1]
    mp = -(-n_tok // ROUTE_TILE) * ROUTE_TILE
    logits_t = jnp.pad(logits_t, ((0, 0), (0, mp - n_tok)))
    eidx, wts, counts = _route(logits_t, b_router, n_tok)
    eidx, wts, counts = eidx[:, :n_tok], wts[:, :n_tok], counts[:, 0]

    tm = EXPERT_TILE
    n_pairs = n_tok * TOP_K
    n_tiles = n_pairs // tm + N_EXPERTS
    pair_code = jnp.arange(n_tok, dtype=I32)[None, :] * TOP_K + jnp.arange(TOP_K, dtype=I32)[:, None]
    (sorted_pairs,) = lax.sort(((eidx * n_pairs + pair_code).reshape(-1),), is_stable=False)
    sorted_pairs = jnp.concatenate([sorted_pairs % n_pairs, jnp.zeros((tm,), I32)])
    pair_tok, pair_choice = sorted_pairs >> PAIR_BITS, sorted_pairs & (TOP_K - 1)
    gather_rows = pair_tok * ROW_CHUNKS
    scatter_rows = (pair_choice * (n_tok + DUMMY_TOKENS) + pair_tok) * ROW_CHUNKS
    tiles_e = (counts + tm - 1) // tm
    tile_end = jnp.cumsum(tiles_e)
    tile_begin = tile_end - tiles_e
    dense_begin = jnp.cumsum(counts) - counts
    tj = jnp.arange(n_tiles, dtype=I32)
    total_tiles = tile_end[-1]
    used = tj < total_tiles
    tj_used = jnp.minimum(tj, total_tiles - 1)[:, None]
    member = jnp.logical_and(tile_begin[None, :] <= tj_used, tj_used < tile_end[None, :])
    of_tile = lambda per_expert: jnp.sum(jnp.where(member, per_expert[None, :], 0), axis=1)
    e_ids = jnp.arange(N_EXPERTS, dtype=I32)
    te = of_tile(e_ids)
    local = tj - of_tile(tile_begin)
    ts = jnp.where(used, of_tile(dense_begin) + local * tm, 0).astype(I32)
    tn = jnp.where(used, jnp.clip(of_tile(counts) - local * tm, 0, tm), 0).astype(I32)
    first = jnp.logical_and(used, local == 0)
    busy = counts > 0
    wslot = of_tile((jnp.cumsum(busy.astype(I32)) - 1) % 2)
    later = jnp.logical_and(e_ids[None, :] > e_ids[:, None], busy[None, :])
    succ = jnp.min(jnp.where(later, e_ids[None, :], N_EXPERTS), axis=1)
    succ = jnp.where(succ == N_EXPERTS, -1, succ)
    ne = jnp.where(first, of_tile(succ), -1).astype(I32)
    prev_used = jnp.concatenate([jnp.zeros((1,), jnp.bool_), used[:-1]])
    tf = (used.astype(I32) * TILE_USED + first.astype(I32) * TILE_FIRST + wslot * TILE_WSLOT
          + prev_used.astype(I32) * TILE_PREV_USED)
    y_all = _experts(te, ts, tn, tf, ne, gather_rows, scatter_rows, h2_all, w_gate, w_up, w_down, n_tok)
    return y_all.reshape(TOP_K, (n_tok + DUMMY_TOKENS) * ROW_CHUNKS, LANES), wts.T


def kernel(x_prompt, x_sample, cache_k_win, cache_v_win, state_pool, c_prompt, c_sample, w_ada, b_ada, g_mix, w_in, q_norm, k_norm, w_pool, pool_scale, w_out, g_ffn, w_router, b_router, w_gate, w_up, w_down, ws_gate, ws_up, ws_down):
    depth = w_ada.shape[0]
    assert depth == 1
    l = 0
    nb, seq, _ = x_prompt.shape
    bd, dec_seq, _ = x_sample.shape
    assert dec_seq == 1 and seq % (TOKEN_TILE * 16) == 0
    n_prompt = nb * seq
    n_tok = n_prompt + bd
    n_keep = min(MAX_WINDOW, seq)
    tiles_per_batch = seq // TOKEN_TILE

    n_mod_rows = -(-(nb + bd) // 8) * 8
    c_all = jnp.concatenate([c_prompt, c_sample, jnp.zeros((n_mod_rows - nb - bd, D_MODEL), F32)], axis=0)
    mods = _ada(c_all, w_ada[l], b_ada[l])
    mods_p = [mods[:nb, j * D_MODEL:(j + 1) * D_MODEL].reshape(nb, 1, D_MODEL) for j in range(6)]
    mods_s = [mods[nb:nb + bd, j * D_MODEL:(j + 1) * D_MODEL].reshape(1, bd, D_MODEL) for j in range(6)]

    g_mix_l = g_mix[l].reshape(1, D_MODEL)
    g_ffn_l = g_ffn[l].reshape(1, D_MODEL)
    qn = q_norm[l].reshape(1, HEAD_DIM)
    kn = k_norm[l].reshape(1, HEAD_DIM)
    w_in_bf = w_in[l].astype(BF16)
    w_out_bf = w_out[l].astype(BF16)
    w_pool_bf = w_pool[l].astype(BF16)
    ps = pool_scale[l].reshape(1, D_POOL)
    w_router_t = w_router[l].T
    wsg_bf, wsu_bf, wsd_bf = ws_gate[l].astype(BF16), ws_up[l].astype(BF16), ws_down[l].astype(BF16)

    rope_p = _rope_tables(jnp.arange(seq, dtype=I32))
    rope_s = _rope_tables(jnp.full((bd,), PAST_LEN, I32))
    xp2 = x_prompt.reshape(n_prompt, D_MODEL)
    xs2 = x_sample.reshape(bd, D_MODEL)
    assert n_keep % TOKEN_TILE == 0
    qp, kp, vp, up, kwin_p, vwin_p = _in_proj(xp2, mods_p[0], mods_p[1], g_mix_l, w_in_bf, qn, kn, rope_p,
                                              TOKEN_TILE, tiles_per_batch, tiles_per_batch, n_keep // TOKEN_TILE)
    qs, _, _, us, k_new, v_new = _in_proj(xs2, mods_s[0], mods_s[1], g_mix_l, w_in_bf, qn, kn, rope_s, bd, 1, 1, 1)

    to_seq = lambda t: t.reshape(nb, seq, -1)
    heads = lambda t: t.reshape(bd, N_HEADS, HEAD_DIM)
    ck, cv = cache_k_win[l], cache_v_win[l]
    k_new, v_new = heads(k_new), heads(v_new)
    attn_p, k_win_s, v_win_s = _attn_prompt(to_seq(qp), to_seq(kp), to_seq(vp), ck, cv, k_new, v_new)
    attn_p = attn_p.reshape(n_prompt, D_ATTN)
    pool_p = _pool_prompt(to_seq(up), w_pool_bf, ps).reshape(n_prompt, D_POOL)
    attn_s = _attn_decode(heads(qs), k_new, v_new, ck, cv).reshape(bd, D_ATTN)
    pool_s = _pool_decode(us, jnp.swapaxes(state_pool[l], 0, 1), w_pool_bf, ps)

    base_p, h2_all, lg_p = _out_proj(xp2, attn_p, pool_p, mods_p[2], mods_p[3], mods_p[4], mods_p[5],
                                     w_out_bf, g_ffn_l, w_router_t, wsg_bf, wsu_bf, wsd_bf,
                                     TOKEN_TILE, tiles_per_batch, n_tok, None, 0)
    base_s, h2_all, lg_s = _out_proj(xs2, attn_s, pool_s, mods_s[2], mods_s[3], mods_s[4], mods_s[5],
                                     w_out_bf, g_ffn_l, w_router_t, wsg_bf, wsu_bf, wsd_bf,
                                     bd, 1, n_tok, h2_all, n_prompt // bd)

    y_planes, wts_t = _moe_routed(h2_all, jnp.concatenate([lg_p, lg_s], axis=1), b_router[l],
                                  w_gate[l], w_up[l], w_down[l])

    yp = _combine(y_planes, wts_t[:n_prompt], base_p, mods_p[5], COMBINE_TILE, seq // COMBINE_TILE, 0)
    ys = _combine(y_planes, wts_t[n_prompt:n_tok], base_s, mods_s[5], bd, 1, n_prompt // bd)

    y_prompt = yp.reshape(nb, seq, D_MODEL)
    y_sample = ys.reshape(bd, dec_seq, D_MODEL)
    win = lambda t: t.reshape(1, nb, n_keep, N_HEADS, HEAD_DIM)
    pool_p_state = to_seq(up)[:, seq - POOL_STATE:][None]
    pool_s_state = jnp.concatenate([state_pool[l][:, 1:], us[:, None, :]], axis=1)[None]
    return (y_prompt, y_sample, win(kwin_p), win(vwin_p), pool_p_state,
            k_win_s[None], v_win_s[None], pool_s_state)
```

```python
import functools

import jax
import jax.numpy as jnp
from jax import lax
from jax.experimental import pallas as pl
from jax.experimental.pallas import tpu as pltpu

F32 = jnp.float32
BF16 = jnp.bfloat16
I32 = jnp.int32

D_MODEL = 2048
N_HEADS = 8
HEAD_DIM = 128
D_ATTN = N_HEADS * HEAD_DIM
D_POOL = D_MODEL - D_ATTN
D_IN = 3 * D_ATTN + D_POOL
ROPE_DIM = HEAD_DIM // 4
ROPE_HALF = ROPE_DIM // 2
ROPE_THETA = 500000.0
DILATED_GROUPS = ((128, 1), (512, 4), (2048, 16))
QBLOCK = 128
ATTN_SCALE = HEAD_DIM ** -0.5
POOL_WINDOWS = (2, 4, 8, 16)
POOL_GROUP_DIM = D_POOL // len(POOL_WINDOWS)
POOL_STATE = max(POOL_WINDOWS) - 1
POOL_HALO = 16
N_EXPERTS = 64
TOP_K = 8
N_EXPERT_GROUPS = 8
GROUP_SIZE = N_EXPERTS // N_EXPERT_GROUPS
TOPK_GROUPS = 4
D_EXPERT = 512
D_SHARED = 512
ROUTED_SCALE = 2.5
EPS = 1e-6
PAST_LEN = 16384
MAX_WINDOW = 2048

LANES = 128
ROW_CHUNKS = D_MODEL // LANES
NEG_BIG = -1e30
LOG2_E = 1.4426950408889634

TOKEN_TILE = 256
EXPERT_TILE = 256
COMBINE_TILE = 128
ROUTE_TILE = 256
VMEM_LIMIT = 60 * 1024 * 1024


def _cparams(sem, vmem=VMEM_LIMIT):
    return pltpu.CompilerParams(dimension_semantics=sem, vmem_limit_bytes=vmem)


def _ada_kernel(c_ref, w_ref, b_ref, o_ref):
    c = c_ref[...]
    s = (c * jax.nn.sigmoid(c)).astype(BF16)
    o_ref[...] = jnp.dot(s, w_ref[...].astype(BF16), preferred_element_type=F32) + b_ref[...]


def _ada(c_all, w_ada, b_ada):
    rows = c_all.shape[0]
    n = w_ada.shape[1]
    tn = 1024
    return pl.pallas_call(
        _ada_kernel,
        grid=(n // tn,),
        in_specs=[
            pl.BlockSpec((rows, D_MODEL), lambda j: (0, 0)),
            pl.BlockSpec((D_MODEL, tn), lambda j: (0, j)),
            pl.BlockSpec((1, tn), lambda j: (0, j)),
        ],
        out_specs=pl.BlockSpec((rows, tn), lambda j: (0, j)),
        out_shape=jax.ShapeDtypeStruct((rows, n), F32),
        compiler_params=_cparams(("parallel",)),
        name="ada_modulation",
    )(c_all, w_ada, b_ada.reshape(1, n))


def _rms(x):
    return x * lax.rsqrt(jnp.mean(x * x, axis=-1, keepdims=True) + EPS)


def _in_kernel(x_ref, sh_ref, sc_ref, g_ref, w_ref, qn_ref, kn_ref, c_ref, s1_ref, s2_ref,
               q_ref, k_ref, v_ref, u_ref, kw_ref, vw_ref):
    tm = x_ref.shape[0]
    x = x_ref[...]
    h = _rms(x) * g_ref[...] * (1.0 + sc_ref[0]) + sh_ref[0]
    hb = h.astype(BF16)
    cos = c_ref[...]
    s1 = s1_ref[...]
    s2 = s2_ref[...]

    def head_rows(win_ref, hd, val):
        win_ref[pl.ds(hd, tm, stride=N_HEADS), :] = val

    def qk(sec, nrm, out_ref, win_ref):
        z = jnp.dot(hb, w_ref[:, sec * D_ATTN:(sec + 1) * D_ATTN], preferred_element_type=F32)
        for hd in range(N_HEADS):
            sl = slice(hd * HEAD_DIM, (hd + 1) * HEAD_DIM)
            r = _rms(z[:, sl]) * nrm
            r = r * cos + pltpu.roll(r, HEAD_DIM - ROPE_HALF, 1) * s1 + pltpu.roll(r, ROPE_HALF, 1) * s2
            out_ref[:, sl] = r
            if win_ref is not None:
                head_rows(win_ref, hd, r)

    qk(0, qn_ref[...], q_ref, None)
    qk(1, kn_ref[...], k_ref, kw_ref)
    v = jnp.dot(hb, w_ref[:, 2 * D_ATTN:3 * D_ATTN], preferred_element_type=F32)
    v_ref[...] = v
    for hd in range(N_HEADS):
        head_rows(vw_ref, hd, v[:, hd * HEAD_DIM:(hd + 1) * HEAD_DIM])
    u_ref[...] = jnp.dot(hb, w_ref[:, 3 * D_ATTN:], preferred_element_type=F32)


def _in_proj(x2d, shift, scale, g_mix, w_in_bf, q_norm, k_norm, rope, tm, rows_per_mod, rope_tiles, keep_tiles):
    m = x2d.shape[0]
    r = shift.shape[1]
    n_groups = m // tm // rows_per_mod
    skip = rows_per_mod - keep_tiles
    mod_spec = pl.BlockSpec((1, r, D_MODEL), lambda i: (i // rows_per_mod, 0, 0))
    rope_spec = pl.BlockSpec((tm, HEAD_DIM), lambda i: (i % rope_tiles, 0))
    out_spec = pl.BlockSpec((tm, D_ATTN), lambda i: (i, 0))
    out_sd = jax.ShapeDtypeStruct((m, D_ATTN), F32)
    win_spec = pl.BlockSpec(
        (tm * N_HEADS, HEAD_DIM),
        lambda i: ((i // rows_per_mod) * keep_tiles + jnp.maximum(i % rows_per_mod - skip, 0), 0))
    win_sd = jax.ShapeDtypeStruct((n_groups * keep_tiles * tm * N_HEADS, HEAD_DIM), F32)
    return pl.pallas_call(
        _in_kernel,
        grid=(m // tm,),
        in_specs=[
            pl.BlockSpec((tm, D_MODEL), lambda i: (i, 0)),
            mod_spec, mod_spec,
            pl.BlockSpec((1, D_MODEL), lambda i: (0, 0)),
            pl.BlockSpec((D_MODEL, D_IN), lambda i: (0, 0)),
            pl.BlockSpec((1, HEAD_DIM), lambda i: (0, 0)),
            pl.BlockSpec((1, HEAD_DIM), lambda i: (0, 0)),
            rope_spec, rope_spec, rope_spec,
        ],
        out_specs=[out_spec] * 4 + [win_spec] * 2,
        out_shape=[out_sd] * 4 + [win_sd] * 2,
        compiler_params=_cparams(("arbitrary",)),
        name="in_proj",
    )(x2d, shift, scale, g_mix, w_in_bf, q_norm, k_norm, *rope)


def _rope_tables(pos):
    inv_freq = jnp.float32(ROPE_THETA) ** (-jnp.arange(ROPE_HALF, dtype=F32) / ROPE_HALF)
    ang = pos.astype(F32)[:, None] * inv_freq[None, :]
    cos, sin = jnp.cos(ang), jnp.sin(ang)
    t = pos.shape[0]
    rest = HEAD_DIM - ROPE_DIM
    c = jnp.concatenate([cos, cos, jnp.ones((t, rest), F32)], axis=1)
    s1 = jnp.concatenate([-sin, jnp.zeros((t, ROPE_HALF + rest), F32)], axis=1)
    s2 = jnp.concatenate([jnp.zeros((t, ROPE_HALF), F32), sin, jnp.zeros((t, rest), F32)], axis=1)
    return c, s1, s2


ATTN_INTERLEAVE = 4
WINDOW_ROWS_PER_STEP = 2
REGROUP_MIN_DILATION = 16


def _attn_prompt_kernel(q_ref, k_ref, v_ref, ck_hbm, cv_hbm, kn_ref, vn_ref, o_ref, ok_hbm, ov_hbm,
                        m_scr, l_scr, acc_scr, q_rm, k_rm, v_rm, wbuf, isem, osem):
    seq = q_ref.shape[1]
    n_buf = wbuf.shape[1]
    step = pl.program_id(0) * pl.num_programs(1) + pl.program_id(1)
    last_step = pl.num_programs(0) * pl.num_programs(1) - 1
    windows = ((ck_hbm, kn_ref, ok_hbm), (cv_hbm, vn_ref, ov_hbm))

    def window_in(row, a):
        return pltpu.make_async_copy(windows[a][0].at[row, pl.ds(1, n_buf - 1)],
                                     wbuf.at[a, pl.ds(0, n_buf - 1)], isem.at[a])

    def window_out(row, a):
        return pltpu.make_async_copy(wbuf.at[a], windows[a][2].at[row], osem.at[a])

    def start_in(row):
        for a in range(2):
            window_in(row, a).start()

    def finish_in_start_out(row):
        for a in range(2):
            window_in(row, a).wait()
            wbuf[a, n_buf - 1] = windows[a][1][row]
            window_out(row, a).start()

    def finish_out(row):
        for a in range(2):
            window_out(row, a).wait()

    row0 = step * WINDOW_ROWS_PER_STEP
    window_phases = [
        lambda: start_in(row0),
        lambda: finish_in_start_out(row0),
        lambda: (finish_out(row0), start_in(row0 + 1)),
        lambda: finish_in_start_out(row0 + 1),
    ]
    assert WINDOW_ROWS_PER_STEP == 2 and len(window_phases) == len(DILATED_GROUPS) + 1

    @pl.when(step > 0)
    def _():
        finish_out(row0 - 1)
    window_phases[0]()
    m_scr[...] = jnp.full(m_scr.shape, NEG_BIG, F32)
    l_scr[...] = jnp.zeros(l_scr.shape, F32)
    acc_scr[...] = jnp.zeros(acc_scr.shape, F32)
    qi = lax.broadcasted_iota(I32, (QBLOCK, 2 * QBLOCK), 0)
    kj = lax.broadcasted_iota(I32, (QBLOCK, 2 * QBLOCK), 1)
    dist = qi + QBLOCK - kj
    band_mask = jnp.logical_and(dist >= 0, dist <= QBLOCK)
    bias_band = jnp.where(band_mask, 0.0, NEG_BIG)
    bias_first = jnp.where(jnp.logical_and(band_mask, kj >= QBLOCK), 0.0, NEG_BIG)
    ones = jnp.ones((2 * QBLOCK, HEAD_DIM), BF16)
    dims = (((1,), (1,)), ((), ()))
    q_scale = ATTN_SCALE * LOG2_E

    for window, dil in DILATED_GROUPS:
        assert window // dil == QBLOCK
        n_units = seq // QBLOCK
        assert n_units % ATTN_INTERLEAVE == 0 and (dil % ATTN_INTERLEAVE == 0 or dil == 1)

        def rows(start, dil=dil):
            return pl.ds(start, QBLOCK) if dil == 1 else pl.ds(start, QBLOCK, stride=dil)

        sub_len = seq // dil
        regroup = dil >= REGROUP_MIN_DILATION
        if regroup:
            for src, dst in ((q_ref, q_rm), (k_ref, k_rm), (v_ref, v_rm)):
                for r in range(dil):
                    dst[pl.ds(r * sub_len, sub_len), :] = src[0, pl.ds(r, sub_len, stride=dil), :]

        def qkv_blocks(blk, res, start, prev, rows=rows, regroup=regroup, sub_len=sub_len):
            if regroup:
                cur = pl.ds(res * sub_len + blk * QBLOCK, QBLOCK)
                prv = pl.ds(res * sub_len + jnp.maximum(blk - 1, 0) * QBLOCK, QBLOCK)
                return (q_rm[cur, :], k_rm[prv, :], k_rm[cur, :], v_rm[prv, :], v_rm[cur, :])
            return (q_ref[0, rows(start), :], k_ref[0, rows(prev), :], k_ref[0, rows(start), :],
                    v_ref[0, rows(prev), :], v_ref[0, rows(start), :])

        def body(it, carry, dil=dil, rows=rows, qkv_blocks=qkv_blocks):
            loaded = []
            for u in range(ATTN_INTERLEAVE):
                f = it * ATTN_INTERLEAVE + u
                blk, res = f // dil, f % dil
                start = blk * (QBLOCK * dil) + res
                prev = jnp.maximum(start - QBLOCK * dil, res)
                q_blk, k_prev, k_cur, v_prev, v_cur = qkv_blocks(blk, res, start, prev)
                q = (q_blk * q_scale).astype(BF16)
                k = jnp.concatenate([k_prev, k_cur], axis=0).astype(BF16)
                v = jnp.concatenate([v_prev, v_cur], axis=0).astype(BF16)
                v_ones = jnp.concatenate([v, ones], axis=1)
                state = (m_scr[rows(start), :], l_scr[rows(start), :], acc_scr[rows(start), :])
                loaded.append((start, blk, q, k, v_ones, state))
            scores = [lax.dot_general(q, k, dims, preferred_element_type=F32)
                      + jnp.where(blk > 0, bias_band, bias_first)
                      for _, blk, q, k, _, _ in loaded]
            probs = []
            for s, (_, _, _, _, _, (m_old, _, _)) in zip(scores, loaded):
                m_new = jnp.maximum(m_old, jnp.max(s, axis=1, keepdims=True))
                p = jnp.exp2(s - jnp.concatenate([m_new, m_new], axis=1)).astype(BF16)
                probs.append((m_new, jnp.exp2(m_old - m_new), p))
            updated = []
            for (m_new, alpha, p), (start, _, _, _, v_ones, (_, l_old, a_old)) in zip(probs, loaded):
                pv = jnp.dot(p, v_ones, preferred_element_type=F32)
                updated.append((start, m_new, alpha * l_old + pv[:, HEAD_DIM:], alpha * a_old + pv[:, :HEAD_DIM]))
            for start, m_new, l_new, a_new in updated:
                m_scr[rows(start), :] = m_new
                l_scr[rows(start), :] = l_new
                acc_scr[rows(start), :] = a_new
            return carry

        trips = n_units // ATTN_INTERLEAVE
        phase = window_phases[DILATED_GROUPS.index((window, dil)) + 1]
        if (window, dil) == DILATED_GROUPS[-1]:
            lax.fori_loop(0, trips // 2, body, 0)
            phase()
            lax.fori_loop(trips // 2, trips, body, 0)
        else:
            lax.fori_loop(0, trips, body, 0)
            phase()

    @pl.when(step == last_step)
    def _():
        finish_out(row0 + 1)

    o_ref[0] = acc_scr[...] / l_scr[...]


def _attn_prompt(q, k, v, cache_k, cache_v, k_new, v_new):
    b, s, _ = q.shape
    bd, n_buf = cache_k.shape[0], cache_k.shape[1]
    assert bd == b * N_HEADS * WINDOW_ROWS_PER_STEP
    spec = pl.BlockSpec((1, s, HEAD_DIM), lambda bi, hi: (bi, 0, hi))
    any_spec = pl.BlockSpec(memory_space=pl.ANY)
    new_spec = pl.BlockSpec(k_new.shape, lambda bi, hi: (0, 0, 0))
    win_sd = jax.ShapeDtypeStruct(cache_k.shape, cache_k.dtype)
    return pl.pallas_call(
        _attn_prompt_kernel,
        grid=(b, N_HEADS),
        in_specs=[spec, spec, spec, any_spec, any_spec, new_spec, new_spec],
        out_specs=[spec, any_spec, any_spec],
        out_shape=[jax.ShapeDtypeStruct((b, s, D_ATTN), F32), win_sd, win_sd],
        scratch_shapes=[pltpu.VMEM((s, HEAD_DIM), F32)] * 6 + [
            pltpu.VMEM((2, n_buf) + cache_k.shape[2:], F32),
            pltpu.SemaphoreType.DMA((2,)),
            pltpu.SemaphoreType.DMA((2,)),
        ],
        compiler_params=_cparams(("arbitrary", "arbitrary")),
        name="attn_prompt",
    )(q, k, v, cache_k, cache_v, k_new, v_new)


def _attn_decode_kernel(q_ref, kn_ref, vn_ref, k1_ref, k2_ref, k3_ref, v1_ref, v2_ref, v3_ref, o_ref):
    q = q_ref[0]
    k_new = kn_ref[0]
    v_new = vn_ref[0]
    n_grp = len(DILATED_GROUPS)

    def scores(kb):
        return jnp.sum(kb * q[None], axis=-1, keepdims=True) * ATTN_SCALE

    kcs = [k1_ref[0], k2_ref[0][:, 0], k3_ref[0][:, 0]]
    vcs = [v1_ref[0], v2_ref[0][:, 0], v3_ref[0][:, 0]]
    s_new = jnp.sum(k_new * q, axis=-1, keepdims=True) * ATTN_SCALE
    s_grp = [scores(kb) for kb in kcs]
    m = s_new
    for s in s_grp:
        m = jnp.maximum(m, jnp.max(s, axis=0))
    p_new = jnp.exp(s_new - m)
    den = n_grp * p_new
    num = n_grp * p_new * v_new
    for s, vb in zip(s_grp, vcs):
        p = jnp.exp(s - m[None])
        den = den + jnp.sum(p, axis=0)
        num = num + jnp.sum(p * vb, axis=0)
    o_ref[0] = num / den


def _attn_decode(q, k_new, v_new, cache_k, cache_v):
    bd, n_buf = cache_k.shape[0], cache_k.shape[1]
    assert n_buf == MAX_WINDOW

    def views(cache):
        out, specs = [], []
        for window, dil in DILATED_GROUPS:
            band = window // dil
            assert band == QBLOCK and n_buf % dil == 0 and n_buf // dil >= band
            if dil == 1:
                out.append(cache)
                specs.append(pl.BlockSpec((1, band, N_HEADS, HEAD_DIM),
                                          lambda b, nb=n_buf // band: (b, nb - 1, 0, 0)))
            else:
                out.append(cache.reshape(bd, n_buf // dil, dil, N_HEADS, HEAD_DIM))
                specs.append(pl.BlockSpec((1, band, 1, N_HEADS, HEAD_DIM),
                                          lambda b, nb=n_buf // dil // band: (b, nb - 1, 0, 0, 0)))
        return out, specs

    kv, kspecs = views(cache_k)
    vv, vspecs = views(cache_v)
    tok = pl.BlockSpec((1, N_HEADS, HEAD_DIM), lambda b: (b, 0, 0))
    return pl.pallas_call(
        _attn_decode_kernel,
        grid=(bd,),
        in_specs=[tok, tok, tok] + kspecs + vspecs,
        out_specs=tok,
        out_shape=jax.ShapeDtypeStruct((bd, N_HEADS, HEAD_DIM), F32),
        compiler_params=_cparams(("parallel",)),
        name="attn_decode",
    )(q, k_new, v_new, *kv, *vv)


def _pool_project(d_groups, wp_ref, ps_ref, out_ref):
    for g, d in enumerate(d_groups):
        sl = slice(g * POOL_GROUP_DIM, (g + 1) * POOL_GROUP_DIM)
        out_ref[:, sl] = jnp.dot(d.astype(BF16), wp_ref[g], preferred_element_type=F32) * ps_ref[:, sl]


def _pool_prompt_kernel(u_ref, prev_ref, wp_ref, ps_ref, o_ref):
    i = pl.program_id(1)
    tp = u_ref.shape[1]
    u = u_ref[0]
    prev = jnp.where(i > 0, prev_ref[0], 0.0)
    ext = jnp.concatenate([prev, u], axis=0)
    pos = i * tp + lax.broadcasted_iota(I32, (tp, 1), 0)
    d_groups = []
    for g, w in enumerate(POOL_WINDOWS):
        sl = slice(g * POOL_GROUP_DIM, (g + 1) * POOL_GROUP_DIM)
        a = ext[:, sl]
        span = 1
        while span < w:
            a = a + pltpu.roll(a, span, 0)
            span *= 2
        win = a[POOL_HALO:, :]
        cnt = jnp.minimum(w, pos + 1).astype(F32)
        d_groups.append(win / cnt - u[:, sl])
    _pool_project(d_groups, wp_ref, ps_ref, o_ref.at[0])


def _pool_prompt(u, w_pool_bf, pool_scale):
    b, s, _ = u.shape
    tp = 512
    halo_blocks = tp // POOL_HALO
    return pl.pallas_call(
        _pool_prompt_kernel,
        grid=(b, s // tp),
        in_specs=[
            pl.BlockSpec((1, tp, D_POOL), lambda bi, i: (bi, i, 0)),
            pl.BlockSpec((1, POOL_HALO, D_POOL), lambda bi, i: (bi, jnp.maximum(i * halo_blocks - 1, 0), 0)),
            pl.BlockSpec(w_pool_bf.shape, lambda bi, i: (0, 0, 0)),
            pl.BlockSpec((1, D_POOL), lambda bi, i: (0, 0)),
        ],
        out_specs=pl.BlockSpec((1, tp, D_POOL), lambda bi, i: (bi, i, 0)),
        out_shape=jax.ShapeDtypeStruct((b, s, D_POOL), F32),
        compiler_params=_cparams(("parallel", "parallel")),
        name="pool_prompt",
    )(u, u, w_pool_bf, pool_scale)


def _pool_decode_kernel(u_ref, st_ref, wp_ref, ps_ref, o_ref):
    u = u_ref[...]
    d_groups = []
    for g, w in enumerate(POOL_WINDOWS):
        sl = slice(g * POOL_GROUP_DIM, (g + 1) * POOL_GROUP_DIM)
        win = u[:, sl]
        for j in range(1, w):
            win = win + st_ref[POOL_STATE - j][:, sl]
        d_groups.append(win / float(w) - u[:, sl])
    _pool_project(d_groups, wp_ref, ps_ref, o_ref)


def _pool_decode(u, state_t, w_pool_bf, pool_scale):
    assert PAST_LEN + 1 >= max(POOL_WINDOWS)
    return pl.pallas_call(
        _pool_decode_kernel,
        out_shape=jax.ShapeDtypeStruct(u.shape, F32),
        compiler_params=_cparams(None),
        name="pool_decode",
    )(u, state_t, w_pool_bf, pool_scale)


def _split_bf16(x):
    hi = x.astype(BF16)
    lo = (x - hi.astype(F32)).astype(BF16)
    return hi, lo


def _out_kernel(x_ref, a_ref, p_ref, g1_ref, sh_ref, sc_ref, g2_ref, wo_ref, gf_ref, wr_ref,
                wsg_ref, wsu_ref, wsd_ref, base_ref, h2_ref, lg_ref):
    tm = x_ref.shape[0]
    mix = (jnp.dot(a_ref[...].astype(BF16), wo_ref[:D_ATTN, :], preferred_element_type=F32)
           + jnp.dot(p_ref[...].astype(BF16), wo_ref[D_ATTN:, :], preferred_element_type=F32))
    x1 = x_ref[...] + g1_ref[0] * mix
    h2 = _rms(x1) * gf_ref[...] * (1.0 + sc_ref[0]) + sh_ref[0]
    for c in range(ROW_CHUNKS):
        h2_ref[pl.ds(c, tm, stride=ROW_CHUNKS), :] = h2[:, c * LANES:(c + 1) * LANES]
    h_hi, h_lo = _split_bf16(h2)
    w_hi, w_lo = _split_bf16(wr_ref[...])
    dims = (((1,), (1,)), ((), ()))
    lg_ref[...] = (lax.dot_general(w_hi, h_hi, dims, preferred_element_type=F32)
                   + lax.dot_general(w_hi, h_lo, dims, preferred_element_type=F32)
                   + lax.dot_general(w_lo, h_hi, dims, preferred_element_type=F32))
    sg = jnp.dot(h_hi, wsg_ref[...], preferred_element_type=F32)
    su = jnp.dot(h_hi, wsu_ref[...], preferred_element_type=F32)
    hs = (sg * jax.nn.sigmoid(sg) * su).astype(BF16)
    base_ref[...] = x1 + g2_ref[0] * jnp.dot(hs, wsd_ref[...], preferred_element_type=F32)


def _out_proj(x2d, attn, pool, gate1, shift2, scale2, gate2, w_out_bf, g_ffn, w_router_t,
              ws_gate_bf, ws_up_bf, ws_down_bf, tm, rows_per_mod, h2_tokens, h2_all, h2_block0):
    m = x2d.shape[0]
    r = gate1.shape[1]
    n_main = m // tm
    h2_rows = h2_tokens * ROW_CHUNKS
    n_fill = 0 if h2_all is not None else pl.cdiv(h2_rows - m * ROW_CHUNKS, tm * ROW_CHUNKS)
    last = n_main - 1
    row = lambda i: jnp.minimum(i, last)
    mod_spec = pl.BlockSpec((1, r, D_MODEL), lambda i: (row(i) // rows_per_mod, 0, 0))
    const = lambda shape: pl.BlockSpec(shape, lambda i: (0,) * len(shape))
    in_specs = [
        pl.BlockSpec((tm, D_MODEL), lambda i: (row(i), 0)),
        pl.BlockSpec((tm, D_ATTN), lambda i: (row(i), 0)),
        pl.BlockSpec((tm, D_POOL), lambda i: (row(i), 0)),
        mod_spec, mod_spec, mod_spec, mod_spec,
        const((D_MODEL, D_MODEL)),
        const((1, D_MODEL)),
        const((N_EXPERTS, D_MODEL)),
        const((D_MODEL, D_SHARED)), const((D_MODEL, D_SHARED)), const((D_SHARED, D_MODEL)),
    ]
    args = [x2d, attn, pool, gate1, shift2, scale2, gate2, w_out_bf, g_ffn, w_router_t,
            ws_gate_bf, ws_up_bf, ws_down_bf]
    aliases = {}
    n_in = len(args)
    if h2_all is not None:
        in_specs.append(pl.BlockSpec(memory_space=pl.ANY))
        args.append(h2_all)
        aliases = {n_in: 1}

    def kernel(*refs):
        refs = refs[:n_in] + refs[len(args):]
        if n_fill == 0:
            _out_kernel(*refs)
            return
        step = pl.program_id(0)
        pl.when(step < n_main)(lambda: _out_kernel(*refs))

        @pl.when(step >= n_main)
        def _():
            h2_ref = refs[n_in + 1]
            h2_ref[...] = jnp.zeros(h2_ref.shape, F32)

    return pl.pallas_call(
        kernel,
        grid=(n_main + n_fill,),
        in_specs=in_specs,
        out_specs=[
            pl.BlockSpec((tm, D_MODEL), lambda i: (row(i), 0)),
            pl.BlockSpec((tm * ROW_CHUNKS, LANES), lambda i: (i + h2_block0, 0)),
            pl.BlockSpec((N_EXPERTS, tm), lambda i: (0, row(i))),
        ],
        out_shape=[
            jax.ShapeDtypeStruct((m, D_MODEL), F32),
            jax.ShapeDtypeStruct((h2_rows, LANES), F32),
            jax.ShapeDtypeStruct((N_EXPERTS, m), F32),
        ],
        input_output_aliases=aliases,
        compiler_params=_cparams(("arbitrary",)),
        name="out_proj",
    )(*args)


def _first_index_of_max(v, iota, size, axis):
    m = jnp.max(v, axis=axis, keepdims=True)
    idx = jnp.min(jnp.where(v == m, iota, size), axis=axis, keepdims=True)
    return m, idx


def _route_kernel(n_valid, lg_ref, b_ref, eidx_ref, wts_ref, cnt_ref, carry):
    i = pl.program_id(0)
    tn = lg_ref.shape[1]

    @pl.when(i == 0)
    def _():
        carry[...] = jnp.zeros(carry.shape, F32)

    scores = jax.nn.sigmoid(lg_ref[...])
    biased = scores + b_ref[...]
    grp = biased.reshape(N_EXPERT_GROUPS, GROUP_SIZE, tn)
    io_g = lax.broadcasted_iota(I32, grp.shape, 1)
    m1, i1 = _first_index_of_max(grp, io_g, GROUP_SIZE, 1)
    m2 = jnp.max(jnp.where(io_g == i1, -jnp.inf, grp), axis=1, keepdims=True)
    gscore = (m1 + m2)[:, 0, :]
    io_n = lax.broadcasted_iota(I32, gscore.shape, 0)
    gsel = jnp.zeros(gscore.shape, jnp.bool_)
    for _ in range(TOPK_GROUPS):
        _, gi = _first_index_of_max(gscore, io_n, N_EXPERT_GROUPS, 0)
        hit = io_n == gi
        gsel = jnp.logical_or(gsel, hit)
        gscore = jnp.where(hit, -jnp.inf, gscore)
    emask = jnp.broadcast_to(gsel[:, None, :], grp.shape).reshape(N_EXPERTS, tn)
    cand = jnp.where(emask, biased, -jnp.inf)
    io_e = lax.broadcasted_iota(I32, cand.shape, 0)
    picked = jnp.zeros(cand.shape, jnp.bool_)
    eidx, sel = [], []
    for _ in range(TOP_K):
        _, ei = _first_index_of_max(cand, io_e, N_EXPERTS, 0)
        hit = io_e == ei
        eidx.append(ei)
        sel.append(jnp.sum(jnp.where(hit, scores, 0.0), axis=0, keepdims=True))
        picked = jnp.logical_or(picked, hit)
        cand = jnp.where(hit, -jnp.inf, cand)
    sel = jnp.concatenate(sel, axis=0)
    eidx = jnp.concatenate(eidx, axis=0)
    wts_ref[...] = sel / jnp.sum(sel, axis=0, keepdims=True) * ROUTED_SCALE
    eidx_ref[...] = eidx

    tok = i * tn + lax.broadcasted_iota(I32, cand.shape, 1)
    mask = jnp.logical_and(picked, tok < n_valid).astype(F32)
    total = carry[...] + jnp.sum(mask, axis=1, keepdims=True)
    carry[...] = total
    cnt_ref[...] = total.astype(I32)


def _route(logits_t, b_router, n_valid):
    mp = logits_t.shape[1]
    tn = ROUTE_TILE
    tok_spec = pl.BlockSpec((TOP_K, tn), lambda i: (0, i))
    return pl.pallas_call(
        functools.partial(_route_kernel, n_valid),
        grid=(mp // tn,),
        in_specs=[
            pl.BlockSpec((N_EXPERTS, tn), lambda i: (0, i)),
            pl.BlockSpec((N_EXPERTS, 1), lambda i: (0, 0)),
        ],
        out_specs=[tok_spec, tok_spec, pl.BlockSpec((N_EXPERTS, LANES), lambda i: (0, 0))],
        out_shape=[
            jax.ShapeDtypeStruct((TOP_K, mp), I32),
            jax.ShapeDtypeStruct((TOP_K, mp), F32),
            jax.ShapeDtypeStruct((N_EXPERTS, LANES), I32),
        ],
        scratch_shapes=[pltpu.VMEM((N_EXPERTS, LANES), F32)],
        compiler_params=_cparams(("arbitrary",)),
        name="route",
    )(logits_t, b_router.reshape(N_EXPERTS, 1))


TILE_USED = 1
TILE_FIRST = 2
TILE_WSLOT = 4
TILE_PREV_USED = 8
PAIR_BITS = 3
assert 1 << PAIR_BITS == TOP_K
DUMMY_TOKENS = 2 * EXPERT_TILE // TOP_K
GATHER_SLOTS = 3


def _experts_kernel(n_tok, te_ref, ts_ref, tn_ref, tf_ref, ne_ref, g_ref, s_ref,
                    h_hbm, wg_hbm, wu_hbm, wd_hbm, y_hbm,
                    xbuf, ybuf, xs, wg_f, wu_f, wd_f, wg_bf, wu_bf, wd_bf, gsem, ssem, wsem, zsem):
    j = pl.program_id(0)
    last = pl.num_programs(0) - 1
    tm = EXPERT_TILE
    rc = ROW_CHUNKS
    plane_tokens = n_tok + DUMMY_TOKENS
    slot = j % 2
    other = 1 - slot
    gslot = j % GATHER_SLOTS
    flags = tf_ref[j]
    used = (flags & TILE_USED) != 0
    first = (flags & TILE_FIRST) != 0
    prev_used = (flags & TILE_PREV_USED) != 0
    wslot = (flags // TILE_WSLOT) & 1
    prv = jnp.maximum(j - 1, 0)
    ahead = jnp.minimum(j + GATHER_SLOTS - 1, last)

    def buf_rows(i):
        return pl.ds(i * rc, rc) if isinstance(i, int) else pl.ds(pl.multiple_of(i * rc, rc), rc)

    def gather_row(i, start, dst):
        row = g_ref[start + i]
        pltpu.make_async_copy(h_hbm.at[pl.ds(pl.multiple_of(row, rc), rc), :],
                              xbuf.at[dst, buf_rows(i), :], gsem.at[dst]).start(priority=0)

    def scatter_row(i, start, n_valid, src):
        spare = ((i & (TOP_K - 1)) * plane_tokens + n_tok + (i >> PAIR_BITS)) * rc + src * ((tm >> PAIR_BITS) * rc)
        row = jnp.where(i < n_valid, s_ref[start + i], spare)
        pltpu.make_async_copy(ybuf.at[src, buf_rows(i), :],
                              y_hbm.at[pl.ds(pl.multiple_of(row, rc), rc), :],
                              ssem.at[src]).start(priority=i % 2 if isinstance(i, int) else 0)

    def scatter_tile_loop(tile, src):
        start, n_valid = ts_ref[tile], tn_ref[tile]
        lax.fori_loop(0, tm, lambda i, c: (scatter_row(i, start, n_valid, src), c)[1], 0)

    def wait_gather(dst):
        pltpu.make_async_copy(h_hbm.at[pl.ds(0, tm * rc), :], xbuf.at[dst], gsem.at[dst]).wait()

    def wait_scatter(src):
        pltpu.make_async_copy(ybuf.at[src], y_hbm.at[pl.ds(0, tm * rc), :], ssem.at[src]).wait()

    def weight_copies(e, ws):
        return [pltpu.make_async_copy(src.at[e], dst.at[ws], wsem.at[ws])
                for src, dst in ((wg_hbm, wg_f), (wu_hbm, wu_f), (wd_hbm, wd_f))]

    @pl.when(j == 0)
    def _():
        ybuf[1] = jnp.zeros(ybuf.shape[1:], ybuf.dtype)
        fills = [pltpu.make_async_copy(ybuf.at[1, pl.ds(0, DUMMY_TOKENS * rc), :],
                                       y_hbm.at[pl.ds((k * plane_tokens + n_tok) * rc, DUMMY_TOKENS * rc), :],
                                       zsem.at[0]) for k in range(TOP_K)]
        for c in fills:
            c.start()
        for c in fills:
            c.wait()
        for c in weight_copies(te_ref[0], 0):
            c.start(priority=1)
        for t in range(GATHER_SLOTS - 1):
            start_t = ts_ref[jnp.minimum(t, last)]
            lax.fori_loop(0, tm, lambda i, c, start_t=start_t, t=t: (gather_row(i, start_t, t), c)[1], 0)

    @pl.when(first)
    def _():
        for c in weight_copies(te_ref[j], wslot):
            c.wait()
        wg_bf[...] = wg_f[wslot].astype(BF16)
        wu_bf[...] = wu_f[wslot].astype(BF16)
        wd_bf[...] = wd_f[wslot].astype(BF16)
        nxt_e = ne_ref[j]

        @pl.when(nxt_e >= 0)
        def _():
            for c in weight_copies(nxt_e, 1 - wslot):
                c.start(priority=1)

    @pl.when(jnp.logical_and(used, j > 0))
    def _():
        wait_scatter(slot)

    @pl.when(used)
    def _():
        wait_gather(gslot)
        xs[...] = jnp.concatenate(
            [xbuf[gslot, pl.ds(c, tm, stride=rc), :] for c in range(rc)], axis=1).astype(BF16)
        start_n = ts_ref[ahead]
        ahead_slot = (j + GATHER_SLOTS - 1) % GATHER_SLOTS
        for i in range(tm):
            gather_row(i, start_n, ahead_slot)
        start_p = ts_ref[prv]
        n_valid_p = jnp.where(j > 0, tn_ref[prv], 0)
        for i in range(tm):
            scatter_row(i, start_p, n_valid_p, other)
        x = xs[...]
        hg = jnp.dot(x, wg_bf[...], preferred_element_type=F32)
        hu = jnp.dot(x, wu_bf[...], preferred_element_type=F32)
        h = (hg * jax.nn.sigmoid(hg) * hu).astype(BF16)
        y = jnp.dot(h, wd_bf[...], preferred_element_type=F32)
        for c in range(rc):
            ybuf[slot, pl.ds(c, tm, stride=rc), :] = y[:, c * LANES:(c + 1) * LANES]

    @pl.when(jnp.logical_and(jnp.logical_not(used), prev_used))
    def _():
        for t in range(GATHER_SLOTS - 1):
            wait_gather((j + t) % GATHER_SLOTS)
        wait_scatter(slot)
        scatter_tile_loop(prv, other)
        wait_scatter(other)

    @pl.when(jnp.logical_and(used, j == last))
    def _():
        for t in range(1, GATHER_SLOTS):
            wait_gather((j + t) % GATHER_SLOTS)
        wait_scatter(other)
        scatter_tile_loop(j, slot)
        wait_scatter(slot)


def _experts(tile_expert, tile_start, tile_rows, tile_flags, next_expert, gather_rows, scatter_rows, h2_all,
             w_gate, w_up, w_down, n_tok):
    n_tiles = tile_expert.shape[0]
    tm = EXPERT_TILE
    any_spec = pl.BlockSpec(memory_space=pl.ANY)
    grid_spec = pltpu.PrefetchScalarGridSpec(
        num_scalar_prefetch=7,
        grid=(n_tiles,),
        in_specs=[any_spec] * 4,
        out_specs=any_spec,
        scratch_shapes=[
            pltpu.VMEM((GATHER_SLOTS, tm * ROW_CHUNKS, LANES), F32),
            pltpu.VMEM((2, tm * ROW_CHUNKS, LANES), F32),
            pltpu.VMEM((tm, D_MODEL), BF16),
            pltpu.VMEM((2, D_MODEL, D_EXPERT), F32),
            pltpu.VMEM((2, D_MODEL, D_EXPERT), F32),
            pltpu.VMEM((2, D_EXPERT, D_MODEL), F32),
            pltpu.VMEM((D_MODEL, D_EXPERT), BF16),
            pltpu.VMEM((D_MODEL, D_EXPERT), BF16),
            pltpu.VMEM((D_EXPERT, D_MODEL), BF16),
            pltpu.SemaphoreType.DMA((GATHER_SLOTS,)),
            pltpu.SemaphoreType.DMA((2,)),
            pltpu.SemaphoreType.DMA((2,)),
            pltpu.SemaphoreType.DMA((1,)),
        ],
    )
    return pl.pallas_call(
        functools.partial(_experts_kernel, n_tok),
        grid_spec=grid_spec,
        out_shape=jax.ShapeDtypeStruct((TOP_K * (n_tok + DUMMY_TOKENS) * ROW_CHUNKS, LANES), F32),
        compiler_params=_cparams(("arbitrary",)),
        name="experts",
    )(tile_expert, tile_start, tile_rows, tile_flags, next_expert, gather_rows, scatter_rows, h2_all,
      w_gate, w_up, w_down)


COMBINE_UNROLL = 4


def _combine_kernel(w_ref, y_ref, base_ref, g2_ref, o_ref, acc_rows):
    i = pl.program_id(0)
    tc = base_ref.shape[0]
    rc = ROW_CHUNKS

    def body(g, carry):
        for u in range(COMBINE_UNROLL):
            t = g * COMBINE_UNROLL + u
            rows = pl.ds(pl.multiple_of(t * rc, rc), rc)
            w0 = (i * tc + t) * TOP_K
            acc = w_ref[w0] * y_ref[0, rows, :]
            for k in range(1, TOP_K):
                acc = acc + w_ref[w0 + k] * y_ref[k, rows, :]
            acc_rows[rows, :] = acc
        return carry

    lax.fori_loop(0, tc // COMBINE_UNROLL, body, 0)
    g2 = g2_ref[0]
    for c in range(rc):
        sl = slice(c * LANES, (c + 1) * LANES)
        o_ref[:, sl] = base_ref[:, sl] + g2[:, sl] * acc_rows[pl.ds(c, tc, stride=rc), :]


def _combine(y_planes, wts, base, gate2, tc, rows_per_mod, block0):
    m = base.shape[0]
    r = gate2.shape[1]
    assert tc % COMBINE_UNROLL == 0
    grid_spec = pltpu.PrefetchScalarGridSpec(
        num_scalar_prefetch=1,
        grid=(m // tc,),
        in_specs=[
            pl.BlockSpec((TOP_K, tc * ROW_CHUNKS, LANES), lambda i, w: (0, i + block0, 0)),
            pl.BlockSpec((tc, D_MODEL), lambda i, w: (i, 0)),
            pl.BlockSpec((1, r, D_MODEL), lambda i, w: (i // rows_per_mod, 0, 0)),
        ],
        out_specs=pl.BlockSpec((tc, D_MODEL), lambda i, w: (i, 0)),
        scratch_shapes=[pltpu.VMEM((tc * ROW_CHUNKS, LANES), F32)],
    )
    return pl.pallas_call(
        _combine_kernel,
        grid_spec=grid_spec,
        out_shape=jax.ShapeDtypeStruct((m, D_MODEL), F32),
        compiler_params=_cparams(("parallel",)),
        name="combine",
    )(wts.reshape(-1), y_planes, base, gate2)


def _moe_routed(h2_all, logits_t, b_router, w_gate, w_up, w_down):
    n_tok = logits_t.shape[1]
    mp = -(-n_tok // ROUTE_TILE) * ROUTE_TILE
    logits_t = jnp.pad(logits_t, ((0, 0), (0, mp - n_tok)))
    eidx, wts, counts = _route(logits_t, b_router, n_tok)
    eidx, wts, counts = eidx[:, :n_tok], wts[:, :n_tok], counts[:, 0]

    tm = EXPERT_TILE
    n_pairs = n_tok * TOP_K
    n_tiles = n_pairs // tm + N_EXPERTS
    pair_code = jnp.arange(n_tok, dtype=I32)[None, :] * TOP_K + jnp.arange(TOP_K, dtype=I32)[:, None]
    (sorted_pairs,) = lax.sort(((eidx * n_pairs + pair_code).reshape(-1),), is_stable=False)
    sorted_pairs = jnp.concatenate([sorted_pairs % n_pairs, jnp.zeros((tm,), I32)])
    pair_tok, pair_choice = sorted_pairs >> PAIR_BITS, sorted_pairs & (TOP_K - 1)
    gather_rows = pair_tok * ROW_CHUNKS
    scatter_rows = (pair_choice * (n_tok + DUMMY_TOKENS) + pair_tok) * ROW_CHUNKS
    tiles_e = (counts + tm - 1) // tm
    tile_end = jnp.cumsum(tiles_e)
    tile_begin = tile_end - tiles_e
    dense_begin = jnp.cumsum(counts) - counts
    tj = jnp.arange(n_tiles, dtype=I32)
    total_tiles = tile_end[-1]
    used = tj < total_tiles
    tj_used = jnp.minimum(tj, total_tiles - 1)[:, None]
    member = jnp.logical_and(tile_begin[None, :] <= tj_used, tj_used < tile_end[None, :])
    of_tile = lambda per_expert: jnp.sum(jnp.where(member, per_expert[None, :], 0), axis=1)
    e_ids = jnp.arange(N_EXPERTS, dtype=I32)
    te = of_tile(e_ids)
    local = tj - of_tile(tile_begin)
    ts = jnp.where(used, of_tile(dense_begin) + local * tm, 0).astype(I32)
    tn = jnp.where(used, jnp.clip(of_tile(counts) - local * tm, 0, tm), 0).astype(I32)
    first = jnp.logical_and(used, local == 0)
    busy = counts > 0
    wslot = of_tile((jnp.cumsum(busy.astype(I32)) - 1) % 2)
    later = jnp.logical_and(e_ids[None, :] > e_ids[:, None], busy[None, :])
    succ = jnp.min(jnp.where(later, e_ids[None, :], N_EXPERTS), axis=1)
    succ = jnp.where(succ == N_EXPERTS, -1, succ)
    ne = jnp.where(first, of_tile(succ), -1).astype(I32)
    prev_used = jnp.concatenate([jnp.zeros((1,), jnp.bool_), used[:-1]])
    tf = (used.astype(I32) * TILE_USED + first.astype(I32) * TILE_FIRST + wslot * TILE_WSLOT
          + prev_used.astype(I32) * TILE_PREV_USED)
    y_all = _experts(te, ts, tn, tf, ne, gather_rows, scatter_rows, h2_all, w_gate, w_up, w_down, n_tok)
    return y_all.reshape(TOP_K, (n_tok + DUMMY_TOKENS) * ROW_CHUNKS, LANES), wts.T


def kernel(x_prompt, x_sample, cache_k_win, cache_v_win, state_pool, c_prompt, c_sample, w_ada, b_ada, g_mix, w_in, q_norm, k_norm, w_pool, pool_scale, w_out, g_ffn, w_router, b_router, w_gate, w_up, w_down, ws_gate, ws_up, ws_down):
    depth = w_ada.shape[0]
    assert depth == 1
    l = 0
    nb, seq, _ = x_prompt.shape
    bd, dec_seq, _ = x_sample.shape
    assert dec_seq == 1 and seq % (TOKEN_TILE * 16) == 0
    n_prompt = nb * seq
    n_tok = n_prompt + bd
    n_keep = min(MAX_WINDOW, seq)
    tiles_per_batch = seq // TOKEN_TILE

    n_mod_rows = -(-(nb + bd) // 8) * 8
    c_all = jnp.concatenate([c_prompt, c_sample, jnp.zeros((n_mod_rows - nb - bd, D_MODEL), F32)], axis=0)
    mods = _ada(c_all, w_ada[l], b_ada[l])
    mods_p = [mods[:nb, j * D_MODEL:(j + 1) * D_MODEL].reshape(nb, 1, D_MODEL) for j in range(6)]
    mods_s = [mods[nb:nb + bd, j * D_MODEL:(j + 1) * D_MODEL].reshape(1, bd, D_MODEL) for j in range(6)]

    g_mix_l = g_mix[l].reshape(1, D_MODEL)
    g_ffn_l = g_ffn[l].reshape(1, D_MODEL)
    qn = q_norm[l].reshape(1, HEAD_DIM)
    kn = k_norm[l].reshape(1, HEAD_DIM)
    w_in_bf = w_in[l].astype(BF16)
    w_out_bf = w_out[l].astype(BF16)
    w_pool_bf = w_pool[l].astype(BF16)
    ps = pool_scale[l].reshape(1, D_POOL)
    w_router_t = w_router[l].T
    wsg_bf, wsu_bf, wsd_bf = ws_gate[l].astype(BF16), ws_up[l].astype(BF16), ws_down[l].astype(BF16)

    rope_p = _rope_tables(jnp.arange(seq, dtype=I32))
    rope_s = _rope_tables(jnp.full((bd,), PAST_LEN, I32))
    xp2 = x_prompt.reshape(n_prompt, D_MODEL)
    xs2 = x_sample.reshape(bd, D_MODEL)
    assert n_keep % TOKEN_TILE == 0
    qp, kp, vp, up, kwin_p, vwin_p = _in_proj(xp2, mods_p[0], mods_p[1], g_mix_l, w_in_bf, qn, kn, rope_p,
                                              TOKEN_TILE, tiles_per_batch, tiles_per_batch, n_keep // TOKEN_TILE)
    qs, _, _, us, k_new, v_new = _in_proj(xs2, mods_s[0], mods_s[1], g_mix_l, w_in_bf, qn, kn, rope_s, bd, 1, 1, 1)

    to_seq = lambda t: t.reshape(nb, seq, -1)
    heads = lambda t: t.reshape(bd, N_HEADS, HEAD_DIM)
    ck, cv = cache_k_win[l], cache_v_win[l]
    k_new, v_new = heads(k_new), heads(v_new)
    attn_p, k_win_s, v_win_s = _attn_prompt(to_seq(qp), to_seq(kp), to_seq(vp), ck, cv, k_new, v_new)
    attn_p = attn_p.reshape(n_prompt, D_ATTN)
    pool_p = _pool_prompt(to_seq(up), w_pool_bf, ps).reshape(n_prompt, D_POOL)
    attn_s = _attn_decode(heads(qs), k_new, v_new, ck, cv).reshape(bd, D_ATTN)
    pool_s = _pool_decode(us, jnp.swapaxes(state_pool[l], 0, 1), w_pool_bf, ps)

    base_p, h2_all, lg_p = _out_proj(xp2, attn_p, pool_p, mods_p[2], mods_p[3], mods_p[4], mods_p[5],
                                     w_out_bf, g_ffn_l, w_router_t, wsg_bf, wsu_bf, wsd_bf,
                                     TOKEN_TILE, tiles_per_batch, n_tok, None, 0)
    base_s, h2_all, lg_s = _out_proj(xs2, attn_s, pool_s, mods_s[2], mods_s[3], mods_s[4], mods_s[5],
                                     w_out_bf, g_ffn_l, w_router_t, wsg_bf, wsu_bf, wsd_bf,
                                     bd, 1, n_tok, h2_all, n_prompt // bd)

    y_planes, wts_t = _moe_routed(h2_all, jnp.concatenate([lg_p, lg_s], axis=1), b_router[l],
                                  w_gate[l], w_up[l], w_down[l])

    yp = _combine(y_planes, wts_t[:n_prompt], base_p, mods_p[5], COMBINE_TILE, seq // COMBINE_TILE, 0)
    ys = _combine(y_planes, wts_t[n_prompt:n_tok], base_s, mods_s[5], bd, 1, n_prompt // bd)

    y_prompt = yp.reshape(nb, seq, D_MODEL)
    y_sample = ys.reshape(bd, dec_seq, D_MODEL)
    win = lambda t: t.reshape(1, nb, n_keep, N_HEADS, HEAD_DIM)
    pool_p_state = to_seq(up)[:, seq - POOL_STATE:][None]
    pool_s_state = jnp.concatenate([state_pool[l][:, 1:], us[:, None, :]], axis=1)[None]
    return (y_prompt, y_sample, win(kwin_p), win(vwin_p), pool_p_state,
            k_win_s[None], v_win_s[None], pool_s_state)
```

```python
import functools

import jax
import jax.numpy as jnp
from jax import lax
from jax.experimental import pallas as pl
from jax.experimental.pallas import tpu as pltpu

F32 = jnp.float32
BF16 = jnp.bfloat16
I32 = jnp.int32

D_MODEL = 2048
N_HEADS = 8
HEAD_DIM = 128
D_ATTN = N_HEADS * HEAD_DIM
D_POOL = D_MODEL - D_ATTN
D_IN = 3 * D_ATTN + D_POOL
ROPE_DIM = HEAD_DIM // 4
ROPE_HALF = ROPE_DIM // 2
ROPE_THETA = 500000.0
DILATED_GROUPS = ((128, 1), (512, 4), (2048, 16))
QBLOCK = 128
ATTN_SCALE = HEAD_DIM ** -0.5
POOL_WINDOWS = (2, 4, 8, 16)
POOL_GROUP_DIM = D_POOL // len(POOL_WINDOWS)
POOL_STATE = max(POOL_WINDOWS) - 1
POOL_HALO = 16
N_EXPERTS = 64
TOP_K = 8
N_EXPERT_GROUPS = 8
GROUP_SIZE = N_EXPERTS // N_EXPERT_GROUPS
TOPK_GROUPS = 4
D_EXPERT = 512
D_SHARED = 512
ROUTED_SCALE = 2.5
EPS = 1e-6
PAST_LEN = 16384
MAX_WINDOW = 2048

LANES = 128
ROW_CHUNKS = D_MODEL // LANES
NEG_BIG = -1e30
LOG2_E = 1.4426950408889634

TOKEN_TILE = 256
EXPERT_TILE = 256
COMBINE_TILE = 128
ROUTE_TILE = 256
VMEM_LIMIT = 60 * 1024 * 1024


def _cparams(sem, vmem=VMEM_LIMIT):
    return pltpu.CompilerParams(dimension_semantics=sem, vmem_limit_bytes=vmem)


def _ada_kernel(c_ref, w_ref, b_ref, o_ref):
    c = c_ref[...]
    s = (c * jax.nn.sigmoid(c)).astype(BF16)
    o_ref[...] = jnp.dot(s, w_ref[...].astype(BF16), preferred_element_type=F32) + b_ref[...]


def _ada(c_all, w_ada, b_ada):
    rows = c_all.shape[0]
    n = w_ada.shape[1]
    tn = 1024
    return pl.pallas_call(
        _ada_kernel,
        grid=(n // tn,),
        in_specs=[
            pl.BlockSpec((rows, D_MODEL), lambda j: (0, 0)),
            pl.BlockSpec((D_MODEL, tn), lambda j: (0, j)),
            pl.BlockSpec((1, tn), lambda j: (0, j)),
        ],
        out_specs=pl.BlockSpec((rows, tn), lambda j: (0, j)),
        out_shape=jax.ShapeDtypeStruct((rows, n), F32),
        compiler_params=_cparams(("parallel",)),
        name="ada_modulation",
    )(c_all, w_ada, b_ada.reshape(1, n))


def _rms(x):
    return x * lax.rsqrt(jnp.mean(x * x, axis=-1, keepdims=True) + EPS)


def _in_kernel(x_ref, sh_ref, sc_ref, g_ref, w_ref, qn_ref, kn_ref, c_ref, s1_ref, s2_ref,
               q_ref, k_ref, v_ref, u_ref, kw_ref, vw_ref):
    tm = x_ref.shape[0]
    x = x_ref[...]
    h = _rms(x) * g_ref[...] * (1.0 + sc_ref[0]) + sh_ref[0]
    hb = h.astype(BF16)
    cos = c_ref[...]
    s1 = s1_ref[...]
    s2 = s2_ref[...]

    def head_rows(win_ref, hd, val):
        win_ref[pl.ds(hd, tm, stride=N_HEADS), :] = val

    def qk(sec, nrm, out_ref, win_ref):
        z = jnp.dot(hb, w_ref[:, sec * D_ATTN:(sec + 1) * D_ATTN], preferred_element_type=F32)
        for hd in range(N_HEADS):
            sl = slice(hd * HEAD_DIM, (hd + 1) * HEAD_DIM)
            r = _rms(z[:, sl]) * nrm
            r = r * cos + pltpu.roll(r, HEAD_DIM - ROPE_HALF, 1) * s1 + pltpu.roll(r, ROPE_HALF, 1) * s2
            out_ref[:, sl] = r
            if win_ref is not None:
                head_rows(win_ref, hd, r)

    qk(0, qn_ref[...], q_ref, None)
    qk(1, kn_ref[...], k_ref, kw_ref)
    v = jnp.dot(hb, w_ref[:, 2 * D_ATTN:3 * D_ATTN], preferred_element_type=F32)
    v_ref[...] = v
    for hd in range(N_HEADS):
        head_rows(vw_ref, hd, v[:, hd * HEAD_DIM:(hd + 1) * HEAD_DIM])
    u_ref[...] = jnp.dot(hb, w_ref[:, 3 * D_ATTN:], preferred_element_type=F32)


def _in_proj(x2d, shift, scale, g_mix, w_in_bf, q_norm, k_norm, rope, tm, rows_per_mod, rope_tiles, keep_tiles):
    m = x2d.shape[0]
    r = shift.shape[1]
    n_groups = m // tm // rows_per_mod
    skip = rows_per_mod - keep_tiles
    mod_spec = pl.BlockSpec((1, r, D_MODEL), lambda i: (i // rows_per_mod, 0, 0))
    rope_spec = pl.BlockSpec((tm, HEAD_DIM), lambda i: (i % rope_tiles, 0))
    out_spec = pl.BlockSpec((tm, D_ATTN), lambda i: (i, 0))
    out_sd = jax.ShapeDtypeStruct((m, D_ATTN), F32)
    win_spec = pl.BlockSpec(
        (tm * N_HEADS, HEAD_DIM),
        lambda i: ((i // rows_per_mod) * keep_tiles + jnp.maximum(i % rows_per_mod - skip, 0), 0))
    win_sd = jax.ShapeDtypeStruct((n_groups * keep_tiles * tm * N_HEADS, HEAD_DIM), F32)
    return pl.pallas_call(
        _in_kernel,
        grid=(m // tm,),
        in_specs=[
            pl.BlockSpec((tm, D_MODEL), lambda i: (i, 0)),
            mod_spec, mod_spec,
            pl.BlockSpec((1, D_MODEL), lambda i: (0, 0)),
            pl.BlockSpec((D_MODEL, D_IN), lambda i: (0, 0)),
            pl.BlockSpec((1, HEAD_DIM), lambda i: (0, 0)),
            pl.BlockSpec((1, HEAD_DIM), lambda i: (0, 0)),
            rope_spec, rope_spec, rope_spec,
        ],
        out_specs=[out_spec] * 4 + [win_spec] * 2,
        out_shape=[out_sd] * 4 + [win_sd] * 2,
        compiler_params=_cparams(("arbitrary",)),
        name="in_proj",
    )(x2d, shift, scale, g_mix, w_in_bf, q_norm, k_norm, *rope)


def _rope_tables(pos):
    inv_freq = jnp.float32(ROPE_THETA) ** (-jnp.arange(ROPE_HALF, dtype=F32) / ROPE_HALF)
    ang = pos.astype(F32)[:, None] * inv_freq[None, :]
    cos, sin = jnp.cos(ang), jnp.sin(ang)
    t = pos.shape[0]
    rest = HEAD_DIM - ROPE_DIM
    c = jnp.concatenate([cos, cos, jnp.ones((t, rest), F32)], axis=1)
    s1 = jnp.concatenate([-sin, jnp.zeros((t, ROPE_HALF + rest), F32)], axis=1)
    s2 = jnp.concatenate([jnp.zeros((t, ROPE_HALF), F32), sin, jnp.zeros((t, rest), F32)], axis=1)
    return c, s1, s2


ATTN_INTERLEAVE = 4
WINDOW_ROWS_PER_STEP = 2
REGROUP_MIN_DILATION = 16


REGROUP_STRIDE = 4


def _to_residue_major(read_rows, dst, tmp, seq, sub_len):
    n, part = REGROUP_STRIDE, seq // REGROUP_STRIDE
    for a in range(n):
        tmp[pl.ds(a * part, part), :] = read_rows(pl.ds(a, part, stride=n))
    for a in range(n):
        for s in range(n):
            dst[pl.ds((s * n + a) * sub_len, sub_len), :] = tmp[pl.ds(a * part + s, sub_len, stride=n), :]


def _from_residue_major(src, dst, tmp, seq, sub_len):
    n, part = REGROUP_STRIDE, seq // REGROUP_STRIDE
    for a in range(n):
        for s in range(n):
            tmp[pl.ds(a * part + s, sub_len, stride=n), :] = src[pl.ds((s * n + a) * sub_len, sub_len), :]
    for a in range(n):
        dst[pl.ds(a, part, stride=n), :] = tmp[pl.ds(a * part, part), :]


def _attn_prompt_kernel(q_ref, k_ref, v_ref, ck_hbm, cv_hbm, kn_ref, vn_ref, o_ref, ok_hbm, ov_hbm,
                        m_scr, l_scr, acc_scr, q_rm, k_rm, v_rm, m_rm, l_rm, a_rm, wbuf, isem, osem):
    seq = q_ref.shape[1]
    n_buf = wbuf.shape[1]
    step = pl.program_id(0) * pl.num_programs(1) + pl.program_id(1)
    last_step = pl.num_programs(0) * pl.num_programs(1) - 1
    windows = ((ck_hbm, kn_ref, ok_hbm), (cv_hbm, vn_ref, ov_hbm))

    def window_in(row, a):
        return pltpu.make_async_copy(windows[a][0].at[row, pl.ds(1, n_buf - 1)],
                                     wbuf.at[a, pl.ds(0, n_buf - 1)], isem.at[a])

    def window_out(row, a):
        return pltpu.make_async_copy(wbuf.at[a], windows[a][2].at[row], osem.at[a])

    def start_in(row):
        for a in range(2):
            window_in(row, a).start()

    def finish_in_start_out(row):
        for a in range(2):
            window_in(row, a).wait()
            wbuf[a, n_buf - 1] = windows[a][1][row]
            window_out(row, a).start()

    def finish_out(row):
        for a in range(2):
            window_out(row, a).wait()

    row0 = step * WINDOW_ROWS_PER_STEP
    window_phases = [
        lambda: start_in(row0),
        lambda: finish_in_start_out(row0),
        lambda: (finish_out(row0), start_in(row0 + 1)),
        lambda: finish_in_start_out(row0 + 1),
    ]
    assert WINDOW_ROWS_PER_STEP == 2 and len(window_phases) == len(DILATED_GROUPS) + 1

    @pl.when(step > 0)
    def _():
        finish_out(row0 - 1)
    window_phases[0]()
    m_scr[...] = jnp.full(m_scr.shape, NEG_BIG, F32)
    l_scr[...] = jnp.zeros(l_scr.shape, F32)
    acc_scr[...] = jnp.zeros(acc_scr.shape, F32)
    qi = lax.broadcasted_iota(I32, (QBLOCK, 2 * QBLOCK), 0)
    kj = lax.broadcasted_iota(I32, (QBLOCK, 2 * QBLOCK), 1)
    dist = qi + QBLOCK - kj
    band_mask = jnp.logical_and(dist >= 0, dist <= QBLOCK)
    bias_band = jnp.where(band_mask, 0.0, NEG_BIG)
    bias_first = jnp.where(jnp.logical_and(band_mask, kj >= QBLOCK), 0.0, NEG_BIG)
    ones = jnp.ones((2 * QBLOCK, HEAD_DIM), BF16)
    dims = (((1,), (1,)), ((), ()))
    q_scale = ATTN_SCALE * LOG2_E

    regrouped = None
    for window, dil in DILATED_GROUPS:
        assert window // dil == QBLOCK
        n_units = seq // QBLOCK
        assert n_units % ATTN_INTERLEAVE == 0 and (dil % ATTN_INTERLEAVE == 0 or dil == 1)

        def rows(start, dil=dil):
            return pl.ds(start, QBLOCK) if dil == 1 else pl.ds(start, QBLOCK, stride=dil)

        sub_len = seq // dil
        regroup = dil >= REGROUP_MIN_DILATION
        if regroup:
            assert dil == REGROUP_STRIDE ** 2 and regrouped is None
            regrouped = (dil, sub_len)
            for src, dst in ((q_ref, q_rm), (k_ref, k_rm), (v_ref, v_rm)):
                _to_residue_major(lambda rows_, src=src: src[0, rows_, :], dst, m_rm, seq, sub_len)
            m_rm[...] = jnp.full(m_rm.shape, NEG_BIG, F32)
            l_rm[...] = jnp.zeros(l_rm.shape, F32)
            a_rm[...] = jnp.zeros(a_rm.shape, F32)

        def qkv_blocks(blk, res, start, prev, rows=rows, regroup=regroup, sub_len=sub_len):
            if regroup:
                cur = pl.ds(res * sub_len + blk * QBLOCK, QBLOCK)
                prv = pl.ds(res * sub_len + jnp.maximum(blk - 1, 0) * QBLOCK, QBLOCK)
                return (q_rm[cur, :], k_rm[prv, :], k_rm[cur, :], v_rm[prv, :], v_rm[cur, :])
            return (q_ref[0, rows(start), :], k_ref[0, rows(prev), :], k_ref[0, rows(start), :],
                    v_ref[0, rows(prev), :], v_ref[0, rows(start), :])

        state_refs = (m_rm, l_rm, a_rm) if regroup else (m_scr, l_scr, acc_scr)

        def state_rows(blk, res, start, rows=rows, regroup=regroup, sub_len=sub_len):
            return pl.ds(res * sub_len + blk * QBLOCK, QBLOCK) if regroup else rows(start)

        def body(it, carry, dil=dil, rows=rows, qkv_blocks=qkv_blocks, state_refs=state_refs, state_rows=state_rows):
            m_st, l_st, a_st = state_refs
            loaded = []
            for u in range(ATTN_INTERLEAVE):
                f = it * ATTN_INTERLEAVE + u
                blk, res = f // dil, f % dil
                start = blk * (QBLOCK * dil) + res
                prev = jnp.maximum(start - QBLOCK * dil, res)
                q_blk, k_prev, k_cur, v_prev, v_cur = qkv_blocks(blk, res, start, prev)
                q = (q_blk * q_scale).astype(BF16)
                k = jnp.concatenate([k_prev, k_cur], axis=0).astype(BF16)
                v = jnp.concatenate([v_prev, v_cur], axis=0).astype(BF16)
                v_ones = jnp.concatenate([v, ones], axis=1)
                st = state_rows(blk, res, start)
                state = (m_st[st, :], l_st[st, :], a_st[st, :])
                loaded.append((st, blk, q, k, v_ones, state))
            scores = [lax.dot_general(q, k, dims, preferred_element_type=F32)
                      + jnp.where(blk > 0, bias_band, bias_first)
                      for _, blk, q, k, _, _ in loaded]
            probs = []
            for s, (_, _, _, _, _, (m_old, _, _)) in zip(scores, loaded):
                m_new = jnp.maximum(m_old, jnp.max(s, axis=1, keepdims=True))
                p = jnp.exp2(s - jnp.concatenate([m_new, m_new], axis=1)).astype(BF16)
                probs.append((m_new, jnp.exp2(m_old - m_new), p))
            updated = []
            for (m_new, alpha, p), (start, _, _, _, v_ones, (_, l_old, a_old)) in zip(probs, loaded):
                pv = jnp.dot(p, v_ones, preferred_element_type=F32)
                updated.append((start, m_new, alpha * l_old + pv[:, HEAD_DIM:], alpha * a_old + pv[:, :HEAD_DIM]))
            for st, m_new, l_new, a_new in updated:
                m_st[st, :] = m_new
                l_st[st, :] = l_new
                a_st[st, :] = a_new
            return carry

        trips = n_units // ATTN_INTERLEAVE
        phase = window_phases[DILATED_GROUPS.index((window, dil)) + 1]
        if (window, dil) == DILATED_GROUPS[-1]:
            lax.fori_loop(0, trips // 2, body, 0)
            phase()
            lax.fori_loop(trips // 2, trips, body, 0)
        else:
            lax.fori_loop(0, trips, body, 0)
            phase()

    @pl.when(step == last_step)
    def _():
        finish_out(row0 + 1)

    if regrouped is None:
        o_ref[0] = acc_scr[...] / l_scr[...]
    else:
        _, sub_len = regrouped
        _from_residue_major(a_rm, q_rm, k_rm, seq, sub_len)
        _from_residue_major(l_rm, k_rm, a_rm, seq, sub_len)
        _from_residue_major(m_rm, v_rm, a_rm, seq, sub_len)
        m_a, m_b = m_scr[...], v_rm[...]
        m_all = jnp.maximum(m_a, m_b)
        w_a, w_b = jnp.exp2(m_a - m_all), jnp.exp2(m_b - m_all)
        o_ref[0] = (acc_scr[...] * w_a + q_rm[...] * w_b) / (l_scr[...] * w_a + k_rm[...] * w_b)


def _attn_prompt(q, k, v, cache_k, cache_v, k_new, v_new):
    b, s, _ = q.shape
    bd, n_buf = cache_k.shape[0], cache_k.shape[1]
    assert bd == b * N_HEADS * WINDOW_ROWS_PER_STEP
    spec = pl.BlockSpec((1, s, HEAD_DIM), lambda bi, hi: (bi, 0, hi))
    any_spec = pl.BlockSpec(memory_space=pl.ANY)
    new_spec = pl.BlockSpec(k_new.shape, lambda bi, hi: (0, 0, 0))
    win_sd = jax.ShapeDtypeStruct(cache_k.shape, cache_k.dtype)
    return pl.pallas_call(
        _attn_prompt_kernel,
        grid=(b, N_HEADS),
        in_specs=[spec, spec, spec, any_spec, any_spec, new_spec, new_spec],
        out_specs=[spec, any_spec, any_spec],
        out_shape=[jax.ShapeDtypeStruct((b, s, D_ATTN), F32), win_sd, win_sd],
        scratch_shapes=[pltpu.VMEM((s, HEAD_DIM), F32)] * 9 + [
            pltpu.VMEM((2, n_buf) + cache_k.shape[2:], F32),
            pltpu.SemaphoreType.DMA((2,)),
            pltpu.SemaphoreType.DMA((2,)),
        ],
        compiler_params=_cparams(("arbitrary", "arbitrary")),
        name="attn_prompt",
    )(q, k, v, cache_k, cache_v, k_new, v_new)


def _attn_decode_kernel(q_ref, kn_ref, vn_ref, k1_ref, k2_ref, k3_ref, v1_ref, v2_ref, v3_ref, o_ref):
    q = q_ref[0]
    k_new = kn_ref[0]
    v_new = vn_ref[0]
    n_grp = len(DILATED_GROUPS)

    def scores(kb):
        return jnp.sum(kb * q[None], axis=-1, keepdims=True) * ATTN_SCALE

    kcs = [k1_ref[0], k2_ref[0][:, 0], k3_ref[0][:, 0]]
    vcs = [v1_ref[0], v2_ref[0][:, 0], v3_ref[0][:, 0]]
    s_new = jnp.sum(k_new * q, axis=-1, keepdims=True) * ATTN_SCALE
    s_grp = [scores(kb) for kb in kcs]
    m = s_new
    for s in s_grp:
        m = jnp.maximum(m, jnp.max(s, axis=0))
    p_new = jnp.exp(s_new - m)
    den = n_grp * p_new
    num = n_grp * p_new * v_new
    for s, vb in zip(s_grp, vcs):
        p = jnp.exp(s - m[None])
        den = den + jnp.sum(p, axis=0)
        num = num + jnp.sum(p * vb, axis=0)
    o_ref[0] = num / den


def _attn_decode(q, k_new, v_new, cache_k, cache_v):
    bd, n_buf = cache_k.shape[0], cache_k.shape[1]
    assert n_buf == MAX_WINDOW

    def views(cache):
        out, specs = [], []
        for window, dil in DILATED_GROUPS:
            band = window // dil
            assert band == QBLOCK and n_buf % dil == 0 and n_buf // dil >= band
            if dil == 1:
                out.append(cache)
                specs.append(pl.BlockSpec((1, band, N_HEADS, HEAD_DIM),
                                          lambda b, nb=n_buf // band: (b, nb - 1, 0, 0)))
            else:
                out.append(cache.reshape(bd, n_buf // dil, dil, N_HEADS, HEAD_DIM))
                specs.append(pl.BlockSpec((1, band, 1, N_HEADS, HEAD_DIM),
                                          lambda b, nb=n_buf // dil // band: (b, nb - 1, 0, 0, 0)))
        return out, specs

    kv, kspecs = views(cache_k)
    vv, vspecs = views(cache_v)
    tok = pl.BlockSpec((1, N_HEADS, HEAD_DIM), lambda b: (b, 0, 0))
    return pl.pallas_call(
        _attn_decode_kernel,
        grid=(bd,),
        in_specs=[tok, tok, tok] + kspecs + vspecs,
        out_specs=tok,
        out_shape=jax.ShapeDtypeStruct((bd, N_HEADS, HEAD_DIM), F32),
        compiler_params=_cparams(("parallel",)),
        name="attn_decode",
    )(q, k_new, v_new, *kv, *vv)


def _pool_project(d_groups, wp_ref, ps_ref, out_ref):
    for g, d in enumerate(d_groups):
        sl = slice(g * POOL_GROUP_DIM, (g + 1) * POOL_GROUP_DIM)
        out_ref[:, sl] = jnp.dot(d.astype(BF16), wp_ref[g], preferred_element_type=F32) * ps_ref[:, sl]


def _pool_prompt_kernel(u_ref, prev_ref, wp_ref, ps_ref, o_ref):
    i = pl.program_id(1)
    tp = u_ref.shape[1]
    u = u_ref[0]
    prev = jnp.where(i > 0, prev_ref[0], 0.0)
    ext = jnp.concatenate([prev, u], axis=0)
    pos = i * tp + lax.broadcasted_iota(I32, (tp, 1), 0)
    d_groups = []
    for g, w in enumerate(POOL_WINDOWS):
        sl = slice(g * POOL_GROUP_DIM, (g + 1) * POOL_GROUP_DIM)
        a = ext[:, sl]
        span = 1
        while span < w:
            a = a + pltpu.roll(a, span, 0)
            span *= 2
        win = a[POOL_HALO:, :]
        cnt = jnp.minimum(w, pos + 1).astype(F32)
        d_groups.append(win / cnt - u[:, sl])
    _pool_project(d_groups, wp_ref, ps_ref, o_ref.at[0])


def _pool_prompt(u, w_pool_bf, pool_scale):
    b, s, _ = u.shape
    tp = 512
    halo_blocks = tp // POOL_HALO
    return pl.pallas_call(
        _pool_prompt_kernel,
        grid=(b, s // tp),
        in_specs=[
            pl.BlockSpec((1, tp, D_POOL), lambda bi, i: (bi, i, 0)),
            pl.BlockSpec((1, POOL_HALO, D_POOL), lambda bi, i: (bi, jnp.maximum(i * halo_blocks - 1, 0), 0)),
            pl.BlockSpec(w_pool_bf.shape, lambda bi, i: (0, 0, 0)),
            pl.BlockSpec((1, D_POOL), lambda bi, i: (0, 0)),
        ],
        out_specs=pl.BlockSpec((1, tp, D_POOL), lambda bi, i: (bi, i, 0)),
        out_shape=jax.ShapeDtypeStruct((b, s, D_POOL), F32),
        compiler_params=_cparams(("parallel", "parallel")),
        name="pool_prompt",
    )(u, u, w_pool_bf, pool_scale)


def _pool_decode_kernel(u_ref, st_ref, wp_ref, ps_ref, o_ref):
    u = u_ref[...]
    d_groups = []
    for g, w in enumerate(POOL_WINDOWS):
        sl = slice(g * POOL_GROUP_DIM, (g + 1) * POOL_GROUP_DIM)
        win = u[:, sl]
        for j in range(1, w):
            win = win + st_ref[POOL_STATE - j][:, sl]
        d_groups.append(win / float(w) - u[:, sl])
    _pool_project(d_groups, wp_ref, ps_ref, o_ref)


def _pool_decode(u, state_t, w_pool_bf, pool_scale):
    assert PAST_LEN + 1 >= max(POOL_WINDOWS)
    return pl.pallas_call(
        _pool_decode_kernel,
        out_shape=jax.ShapeDtypeStruct(u.shape, F32),
        compiler_params=_cparams(None),
        name="pool_decode",
    )(u, state_t, w_pool_bf, pool_scale)


def _split_bf16(x):
    hi = x.astype(BF16)
    lo = (x - hi.astype(F32)).astype(BF16)
    return hi, lo


def _out_kernel(x_ref, a_ref, p_ref, g1_ref, sh_ref, sc_ref, g2_ref, wo_ref, gf_ref, wrh_ref, wrl_ref,
                wsg_ref, wsu_ref, wsd_ref, base_ref, h2_ref, lg_ref):
    tm = x_ref.shape[0]
    mix = (jnp.dot(a_ref[...].astype(BF16), wo_ref[:D_ATTN, :], preferred_element_type=F32)
           + jnp.dot(p_ref[...].astype(BF16), wo_ref[D_ATTN:, :], preferred_element_type=F32))
    x1 = x_ref[...] + g1_ref[0] * mix
    h2 = _rms(x1) * gf_ref[...] * (1.0 + sc_ref[0]) + sh_ref[0]
    for c in range(ROW_CHUNKS):
        h2_ref[pl.ds(c, tm, stride=ROW_CHUNKS), :] = h2[:, c * LANES:(c + 1) * LANES]
    h_hi, h_lo = _split_bf16(h2)
    w_hi, w_lo = wrh_ref[...], wrl_ref[...]
    lg = (jnp.dot(h_hi, w_hi, preferred_element_type=F32) + jnp.dot(h_lo, w_hi, preferred_element_type=F32)
          + jnp.dot(h_hi, w_lo, preferred_element_type=F32))
    if tm < LANES:
        lg = jnp.concatenate([lg, jnp.zeros((LANES - tm, LANES), F32)], axis=0)
    lg_ref[...] = lg.T[:N_EXPERTS, :tm]
    sg = jnp.dot(h_hi, wsg_ref[...], preferred_element_type=F32)
    su = jnp.dot(h_hi, wsu_ref[...], preferred_element_type=F32)
    hs = (sg * jax.nn.sigmoid(sg) * su).astype(BF16)
    base_ref[...] = x1 + g2_ref[0] * jnp.dot(hs, wsd_ref[...], preferred_element_type=F32)


def _out_proj(x2d, attn, pool, gate1, shift2, scale2, gate2, w_out_bf, g_ffn, w_router_hi, w_router_lo,
              ws_gate_bf, ws_up_bf, ws_down_bf, tm, rows_per_mod, h2_tokens, h2_all, h2_block0):
    m = x2d.shape[0]
    r = gate1.shape[1]
    n_main = m // tm
    h2_rows = h2_tokens * ROW_CHUNKS
    n_fill = 0 if h2_all is not None else pl.cdiv(h2_rows - m * ROW_CHUNKS, tm * ROW_CHUNKS)
    last = n_main - 1
    row = lambda i: jnp.minimum(i, last)
    mod_spec = pl.BlockSpec((1, r, D_MODEL), lambda i: (row(i) // rows_per_mod, 0, 0))
    const = lambda shape: pl.BlockSpec(shape, lambda i: (0,) * len(shape))
    in_specs = [
        pl.BlockSpec((tm, D_MODEL), lambda i: (row(i), 0)),
        pl.BlockSpec((tm, D_ATTN), lambda i: (row(i), 0)),
        pl.BlockSpec((tm, D_POOL), lambda i: (row(i), 0)),
        mod_spec, mod_spec, mod_spec, mod_spec,
        const((D_MODEL, D_MODEL)),
        const((1, D_MODEL)),
        const((D_MODEL, LANES)), const((D_MODEL, LANES)),
        const((D_MODEL, D_SHARED)), const((D_MODEL, D_SHARED)), const((D_SHARED, D_MODEL)),
    ]
    args = [x2d, attn, pool, gate1, shift2, scale2, gate2, w_out_bf, g_ffn, w_router_hi, w_router_lo,
            ws_gate_bf, ws_up_bf, ws_down_bf]
    aliases = {}
    n_in = len(args)
    if h2_all is not None:
        in_specs.append(pl.BlockSpec(memory_space=pl.ANY))
        args.append(h2_all)
        aliases = {n_in: 1}

    def kernel(*refs):
        refs = refs[:n_in] + refs[len(args):]
        if n_fill == 0:
            _out_kernel(*refs)
            return
        step = pl.program_id(0)
        pl.when(step < n_main)(lambda: _out_kernel(*refs))

        @pl.when(step >= n_main)
        def _():
            h2_ref = refs[n_in + 1]
            h2_ref[...] = jnp.zeros(h2_ref.shape, F32)

    return pl.pallas_call(
        kernel,
        grid=(n_main + n_fill,),
        in_specs=in_specs,
        out_specs=[
            pl.BlockSpec((tm, D_MODEL), lambda i: (row(i), 0)),
            pl.BlockSpec((tm * ROW_CHUNKS, LANES), lambda i: (i + h2_block0, 0)),
            pl.BlockSpec((N_EXPERTS, tm), lambda i: (0, row(i))),
        ],
        out_shape=[
            jax.ShapeDtypeStruct((m, D_MODEL), F32),
            jax.ShapeDtypeStruct((h2_rows, LANES), F32),
            jax.ShapeDtypeStruct((N_EXPERTS, m), F32),
        ],
        input_output_aliases=aliases,
        compiler_params=_cparams(("arbitrary",)),
        name="out_proj",
    )(*args)


def _first_index_of_max(v, iota, size, axis):
    m = jnp.max(v, axis=axis, keepdims=True)
    idx = jnp.min(jnp.where(v == m, iota, size), axis=axis, keepdims=True)
    return m, idx


def _route_kernel(n_valid, lg_ref, b_ref, eidx_ref, wts_ref, cnt_ref, carry):
    i = pl.program_id(0)
    tn = lg_ref.shape[1]

    @pl.when(i == 0)
    def _():
        carry[...] = jnp.zeros(carry.shape, F32)

    scores = jax.nn.sigmoid(lg_ref[...])
    biased = scores + b_ref[...]
    grp = biased.reshape(N_EXPERT_GROUPS, GROUP_SIZE, tn)
    io_g = lax.broadcasted_iota(I32, grp.shape, 1)
    m1, i1 = _first_index_of_max(grp, io_g, GROUP_SIZE, 1)
    m2 = jnp.max(jnp.where(io_g == i1, -jnp.inf, grp), axis=1, keepdims=True)
    gscore = (m1 + m2)[:, 0, :]
    io_n = lax.broadcasted_iota(I32, gscore.shape, 0)
    gsel = jnp.zeros(gscore.shape, jnp.bool_)
    for _ in range(TOPK_GROUPS):
        _, gi = _first_index_of_max(gscore, io_n, N_EXPERT_GROUPS, 0)
        hit = io_n == gi
        gsel = jnp.logical_or(gsel, hit)
        gscore = jnp.where(hit, -jnp.inf, gscore)
    emask = jnp.broadcast_to(gsel[:, None, :], grp.shape).reshape(N_EXPERTS, tn)
    cand = jnp.where(emask, biased, -jnp.inf)
    io_e = lax.broadcasted_iota(I32, cand.shape, 0)
    picked = jnp.zeros(cand.shape, jnp.bool_)
    eidx, sel = [], []
    for _ in range(TOP_K):
        _, ei = _first_index_of_max(cand, io_e, N_EXPERTS, 0)
        hit = io_e == ei
        eidx.append(ei)
        sel.append(jnp.sum(jnp.where(hit, scores, 0.0), axis=0, keepdims=True))
        picked = jnp.logical_or(picked, hit)
        cand = jnp.where(hit, -jnp.inf, cand)
    sel = jnp.concatenate(sel, axis=0)
    eidx = jnp.concatenate(eidx, axis=0)
    wts_ref[...] = sel / jnp.sum(sel, axis=0, keepdims=True) * ROUTED_SCALE
    eidx_ref[...] = eidx

    tok = i * tn + lax.broadcasted_iota(I32, cand.shape, 1)
    mask = jnp.logical_and(picked, tok < n_valid).astype(F32)
    total = carry[...] + jnp.sum(mask, axis=1, keepdims=True)
    carry[...] = total
    cnt_ref[...] = total.astype(I32)


def _route(logits_t, b_router, n_valid):
    mp = logits_t.shape[1]
    tn = ROUTE_TILE
    tok_spec = pl.BlockSpec((TOP_K, tn), lambda i: (0, i))
    return pl.pallas_call(
        functools.partial(_route_kernel, n_valid),
        grid=(mp // tn,),
        in_specs=[
            pl.BlockSpec((N_EXPERTS, tn), lambda i: (0, i)),
            pl.BlockSpec((N_EXPERTS, 1), lambda i: (0, 0)),
        ],
        out_specs=[tok_spec, tok_spec, pl.BlockSpec((N_EXPERTS, LANES), lambda i: (0, 0))],
        out_shape=[
            jax.ShapeDtypeStruct((TOP_K, mp), I32),
            jax.ShapeDtypeStruct((TOP_K, mp), F32),
            jax.ShapeDtypeStruct((N_EXPERTS, LANES), I32),
        ],
        scratch_shapes=[pltpu.VMEM((N_EXPERTS, LANES), F32)],
        compiler_params=_cparams(("arbitrary",)),
        name="route",
    )(logits_t, b_router.reshape(N_EXPERTS, 1))


TILE_USED = 1
TILE_FIRST = 2
TILE_WSLOT = 4
TILE_PREV_USED = 8
PAIR_BITS = 3
assert 1 << PAIR_BITS == TOP_K
DUMMY_TOKENS = 2 * EXPERT_TILE // TOP_K
GATHER_SLOTS = 3


def _experts_kernel(n_tok, te_ref, ts_ref, tn_ref, tf_ref, ne_ref, g_ref, s_ref,
                    h_hbm, wg_hbm, wu_hbm, wd_hbm, y_hbm,
                    xbuf, ybuf, xs, wg_f, wu_f, wd_f, wg_bf, wu_bf, wd_bf, gsem, ssem, wsem, zsem):
    j = pl.program_id(0)
    last = pl.num_programs(0) - 1
    tm = EXPERT_TILE
    rc = ROW_CHUNKS
    plane_tokens = n_tok + DUMMY_TOKENS
    slot = j % 2
    other = 1 - slot
    gslot = j % GATHER_SLOTS
    flags = tf_ref[j]
    used = (flags & TILE_USED) != 0
    first = (flags & TILE_FIRST) != 0
    prev_used = (flags & TILE_PREV_USED) != 0
    wslot = (flags // TILE_WSLOT) & 1
    prv = jnp.maximum(j - 1, 0)
    ahead = jnp.minimum(j + GATHER_SLOTS - 1, last)

    def buf_rows(i):
        return pl.ds(i * rc, rc) if isinstance(i, int) else pl.ds(pl.multiple_of(i * rc, rc), rc)

    def gather_row(i, start, dst):
        row = g_ref[start + i]
        pltpu.make_async_copy(h_hbm.at[pl.ds(pl.multiple_of(row, rc), rc), :],
                              xbuf.at[dst, buf_rows(i), :], gsem.at[dst]).start(priority=0)

    def scatter_row(i, start, n_valid, src):
        spare = ((i & (TOP_K - 1)) * plane_tokens + n_tok + (i >> PAIR_BITS)) * rc + src * ((tm >> PAIR_BITS) * rc)
        row = jnp.where(i < n_valid, s_ref[start + i], spare)
        pltpu.make_async_copy(ybuf.at[src, buf_rows(i), :],
                              y_hbm.at[pl.ds(pl.multiple_of(row, rc), rc), :],
                              ssem.at[src]).start(priority=i % 2 if isinstance(i, int) else 0)

    def scatter_tile_loop(tile, src):
        start, n_valid = ts_ref[tile], tn_ref[tile]
        lax.fori_loop(0, tm, lambda i, c: (scatter_row(i, start, n_valid, src), c)[1], 0)

    def wait_gather(dst):
        pltpu.make_async_copy(h_hbm.at[pl.ds(0, tm * rc), :], xbuf.at[dst], gsem.at[dst]).wait()

    def wait_scatter(src):
        pltpu.make_async_copy(ybuf.at[src], y_hbm.at[pl.ds(0, tm * rc), :], ssem.at[src]).wait()

    def weight_copies(e, ws):
        return [pltpu.make_async_copy(src.at[e], dst.at[ws], wsem.at[ws])
                for src, dst in ((wg_hbm, wg_f), (wu_hbm, wu_f), (wd_hbm, wd_f))]

    @pl.when(j == 0)
    def _():
        ybuf[1] = jnp.zeros(ybuf.shape[1:], ybuf.dtype)
        fills = [pltpu.make_async_copy(ybuf.at[1, pl.ds(0, DUMMY_TOKENS * rc), :],
                                       y_hbm.at[pl.ds((k * plane_tokens + n_tok) * rc, DUMMY_TOKENS * rc), :],
                                       zsem.at[0]) for k in range(TOP_K)]
        for c in fills:
            c.start()
        for c in fills:
            c.wait()
        for c in weight_copies(te_ref[0], 0):
            c.start(priority=1)
        for t in range(GATHER_SLOTS - 1):
            start_t = ts_ref[jnp.minimum(t, last)]
            lax.fori_loop(0, tm, lambda i, c, start_t=start_t, t=t: (gather_row(i, start_t, t), c)[1], 0)

    @pl.when(first)
    def _():
        for c in weight_copies(te_ref[j], wslot):
            c.wait()
        wg_bf[...] = wg_f[wslot].astype(BF16)
        wu_bf[...] = wu_f[wslot].astype(BF16)
        wd_bf[...] = wd_f[wslot].astype(BF16)
        nxt_e = ne_ref[j]

        @pl.when(nxt_e >= 0)
        def _():
            for c in weight_copies(nxt_e, 1 - wslot):
                c.start(priority=1)

    @pl.when(jnp.logical_and(used, j > 0))
    def _():
        wait_scatter(slot)

    @pl.when(used)
    def _():
        wait_gather(gslot)
        xs[...] = jnp.concatenate(
            [xbuf[gslot, pl.ds(c, tm, stride=rc), :] for c in range(rc)], axis=1).astype(BF16)
        start_n = ts_ref[ahead]
        ahead_slot = (j + GATHER_SLOTS - 1) % GATHER_SLOTS
        for i in range(tm):
            gather_row(i, start_n, ahead_slot)
        start_p = ts_ref[prv]
        n_valid_p = jnp.where(j > 0, tn_ref[prv], 0)
        for i in range(tm):
            scatter_row(i, start_p, n_valid_p, other)
        x = xs[...]
        hg = jnp.dot(x, wg_bf[...], preferred_element_type=F32)
        hu = jnp.dot(x, wu_bf[...], preferred_element_type=F32)
        h = (hg * jax.nn.sigmoid(hg) * hu).astype(BF16)
        y = jnp.dot(h, wd_bf[...], preferred_element_type=F32)
        for c in range(rc):
            ybuf[slot, pl.ds(c, tm, stride=rc), :] = y[:, c * LANES:(c + 1) * LANES]

    @pl.when(jnp.logical_and(jnp.logical_not(used), prev_used))
    def _():
        for t in range(GATHER_SLOTS - 1):
            wait_gather((j + t) % GATHER_SLOTS)
        wait_scatter(slot)
        scatter_tile_loop(prv, other)
        wait_scatter(other)

    @pl.when(jnp.logical_and(used, j == last))
    def _():
        for t in range(1, GATHER_SLOTS):
            wait_gather((j + t) % GATHER_SLOTS)
        wait_scatter(other)
        scatter_tile_loop(j, slot)
        wait_scatter(slot)


def _experts(tile_expert, tile_start, tile_rows, tile_flags, next_expert, gather_rows, scatter_rows, h2_all,
             w_gate, w_up, w_down, n_tok):
    n_tiles = tile_expert.shape[0]
    tm = EXPERT_TILE
    any_spec = pl.BlockSpec(memory_space=pl.ANY)
    grid_spec = pltpu.PrefetchScalarGridSpec(
        num_scalar_prefetch=7,
        grid=(n_tiles,),
        in_specs=[any_spec] * 4,
        out_specs=any_spec,
        scratch_shapes=[
            pltpu.VMEM((GATHER_SLOTS, tm * ROW_CHUNKS, LANES), F32),
            pltpu.VMEM((2, tm * ROW_CHUNKS, LANES), F32),
            pltpu.VMEM((tm, D_MODEL), BF16),
            pltpu.VMEM((2, D_MODEL, D_EXPERT), F32),
            pltpu.VMEM((2, D_MODEL, D_EXPERT), F32),
            pltpu.VMEM((2, D_EXPERT, D_MODEL), F32),
            pltpu.VMEM((D_MODEL, D_EXPERT), BF16),
            pltpu.VMEM((D_MODEL, D_EXPERT), BF16),
            pltpu.VMEM((D_EXPERT, D_MODEL), BF16),
            pltpu.SemaphoreType.DMA((GATHER_SLOTS,)),
            pltpu.SemaphoreType.DMA((2,)),
            pltpu.SemaphoreType.DMA((2,)),
            pltpu.SemaphoreType.DMA((1,)),
        ],
    )
    return pl.pallas_call(
        functools.partial(_experts_kernel, n_tok),
        grid_spec=grid_spec,
        out_shape=jax.ShapeDtypeStruct((TOP_K * (n_tok + DUMMY_TOKENS) * ROW_CHUNKS, LANES), F32),
        compiler_params=_cparams(("arbitrary",)),
        name="experts",
    )(tile_expert, tile_start, tile_rows, tile_flags, next_expert, gather_rows, scatter_rows, h2_all,
      w_gate, w_up, w_down)


COMBINE_UNROLL = 4


def _combine_kernel(w_ref, y_ref, base_ref, g2_ref, o_ref, acc_rows):
    i = pl.program_id(0)
    tc = base_ref.shape[0]
    rc = ROW_CHUNKS

    def body(g, carry):
        for u in range(COMBINE_UNROLL):
            t = g * COMBINE_UNROLL + u
            rows = pl.ds(pl.multiple_of(t * rc, rc), rc)
            w0 = (i * tc + t) * TOP_K
            acc = w_ref[w0] * y_ref[0, rows, :]
            for k in range(1, TOP_K):
                acc = acc + w_ref[w0 + k] * y_ref[k, rows, :]
            acc_rows[rows, :] = acc
        return carry

    lax.fori_loop(0, tc // COMBINE_UNROLL, body, 0)
    g2 = g2_ref[0]
    for c in range(rc):
        sl = slice(c * LANES, (c + 1) * LANES)
        o_ref[:, sl] = base_ref[:, sl] + g2[:, sl] * acc_rows[pl.ds(c, tc, stride=rc), :]


def _combine(y_planes, wts, base, gate2, tc, rows_per_mod, block0):
    m = base.shape[0]
    r = gate2.shape[1]
    assert tc % COMBINE_UNROLL == 0
    grid_spec = pltpu.PrefetchScalarGridSpec(
        num_scalar_prefetch=1,
        grid=(m // tc,),
        in_specs=[
            pl.BlockSpec((TOP_K, tc * ROW_CHUNKS, LANES), lambda i, w: (0, i + block0, 0)),
            pl.BlockSpec((tc, D_MODEL), lambda i, w: (i, 0)),
            pl.BlockSpec((1, r, D_MODEL), lambda i, w: (i // rows_per_mod, 0, 0)),
        ],
        out_specs=pl.BlockSpec((tc, D_MODEL), lambda i, w: (i, 0)),
        scratch_shapes=[pltpu.VMEM((tc * ROW_CHUNKS, LANES), F32)],
    )
    return pl.pallas_call(
        _combine_kernel,
        grid_spec=grid_spec,
        out_shape=jax.ShapeDtypeStruct((m, D_MODEL), F32),
        compiler_params=_cparams(("parallel",)),
        name="combine",
    )(wts.reshape(-1), y_planes, base, gate2)


def _moe_routed(h2_all, logits_t, b_router, w_gate, w_up, w_down):
    n_tok = logits_t.shape[1]
    mp = -(-n_tok // ROUTE_TILE) * ROUTE_TILE
    logits_t = jnp.pad(logits_t, ((0, 0), (0, mp - n_tok)))
    eidx, wts, counts = _route(logits_t, b_router, n_tok)
    eidx, wts, counts = eidx[:, :n_tok], wts[:, :n_tok], counts[:, 0]

    tm = EXPERT_TILE
    n_pairs = n_tok * TOP_K
    n_tiles = n_pairs // tm + N_EXPERTS
    pair_code = jnp.arange(n_tok, dtype=I32)[None, :] * TOP_K + jnp.arange(TOP_K, dtype=I32)[:, None]
    (sorted_pairs,) = lax.sort(((eidx * n_pairs + pair_code).reshape(-1),), is_stable=False)
    sorted_pairs = jnp.concatenate([sorted_pairs % n_pairs, jnp.zeros((tm,), I32)])
    pair_tok, pair_choice = sorted_pairs >> PAIR_BITS, sorted_pairs & (TOP_K - 1)
    gather_rows = pair_tok * ROW_CHUNKS
    scatter_rows = (pair_choice * (n_tok + DUMMY_TOKENS) + pair_tok) * ROW_CHUNKS
    tiles_e = (counts + tm - 1) // tm
    tile_end = jnp.cumsum(tiles_e)
    tile_begin = tile_end - tiles_e
    dense_begin = jnp.cumsum(counts) - counts
    tj = jnp.arange(n_tiles, dtype=I32)
    total_tiles = tile_end[-1]
    used = tj < total_tiles
    tj_used = jnp.minimum(tj, total_tiles - 1)[:, None]
    member = jnp.logical_and(tile_begin[None, :] <= tj_used, tj_used < tile_end[None, :])
    of_tile = lambda per_expert: jnp.sum(jnp.where(member, per_expert[None, :], 0), axis=1)
    e_ids = jnp.arange(N_EXPERTS, dtype=I32)
    te = of_tile(e_ids)
    local = tj - of_tile(tile_begin)
    ts = jnp.where(used, of_tile(dense_begin) + local * tm, 0).astype(I32)
    tn = jnp.where(used, jnp.clip(of_tile(counts) - local * tm, 0, tm), 0).astype(I32)
    first = jnp.logical_and(used, local == 0)
    busy = counts > 0
    wslot = of_tile((jnp.cumsum(busy.astype(I32)) - 1) % 2)
    later = jnp.logical_and(e_ids[None, :] > e_ids[:, None], busy[None, :])
    succ = jnp.min(jnp.where(later, e_ids[None, :], N_EXPERTS), axis=1)
    succ = jnp.where(succ == N_EXPERTS, -1, succ)
    ne = jnp.where(first, of_tile(succ), -1).astype(I32)
    prev_used = jnp.concatenate([jnp.zeros((1,), jnp.bool_), used[:-1]])
    tf = (used.astype(I32) * TILE_USED + first.astype(I32) * TILE_FIRST + wslot * TILE_WSLOT
          + prev_used.astype(I32) * TILE_PREV_USED)
    y_all = _experts(te, ts, tn, tf, ne, gather_rows, scatter_rows, h2_all, w_gate, w_up, w_down, n_tok)
    return y_all.reshape(TOP_K, (n_tok + DUMMY_TOKENS) * ROW_CHUNKS, LANES), wts.T


def kernel(x_prompt, x_sample, cache_k_win, cache_v_win, state_pool, c_prompt, c_sample, w_ada, b_ada, g_mix, w_in, q_norm, k_norm, w_pool, pool_scale, w_out, g_ffn, w_router, b_router, w_gate, w_up, w_down, ws_gate, ws_up, ws_down):
    depth = w_ada.shape[0]
    assert depth == 1
    l = 0
    nb, seq, _ = x_prompt.shape
    bd, dec_seq, _ = x_sample.shape
    assert dec_seq == 1 and seq % (TOKEN_TILE * 16) == 0
    n_prompt = nb * seq
    n_tok = n_prompt + bd
    n_keep = min(MAX_WINDOW, seq)
    tiles_per_batch = seq // TOKEN_TILE

    n_mod_rows = -(-(nb + bd) // 8) * 8
    c_all = jnp.concatenate([c_prompt, c_sample, jnp.zeros((n_mod_rows - nb - bd, D_MODEL), F32)], axis=0)
    mods = _ada(c_all, w_ada[l], b_ada[l])
    mods_p = [mods[:nb, j * D_MODEL:(j + 1) * D_MODEL].reshape(nb, 1, D_MODEL) for j in range(6)]
    mods_s = [mods[nb:nb + bd, j * D_MODEL:(j + 1) * D_MODEL].reshape(1, bd, D_MODEL) for j in range(6)]

    g_mix_l = g_mix[l].reshape(1, D_MODEL)
    g_ffn_l = g_ffn[l].reshape(1, D_MODEL)
    qn = q_norm[l].reshape(1, HEAD_DIM)
    kn = k_norm[l].reshape(1, HEAD_DIM)
    w_in_bf = w_in[l].astype(BF16)
    w_out_bf = w_out[l].astype(BF16)
    w_pool_bf = w_pool[l].astype(BF16)
    ps = pool_scale[l].reshape(1, D_POOL)
    w_router_pad = jnp.pad(w_router[l], ((0, 0), (0, LANES - N_EXPERTS)))
    w_router_hi = w_router_pad.astype(BF16)
    w_router_lo = (w_router_pad - w_router_hi.astype(F32)).astype(BF16)
    wsg_bf, wsu_bf, wsd_bf = ws_gate[l].astype(BF16), ws_up[l].astype(BF16), ws_down[l].astype(BF16)

    rope_p = _rope_tables(jnp.arange(seq, dtype=I32))
    rope_s = _rope_tables(jnp.full((bd,), PAST_LEN, I32))
    xp2 = x_prompt.reshape(n_prompt, D_MODEL)
    xs2 = x_sample.reshape(bd, D_MODEL)
    assert n_keep % TOKEN_TILE == 0
    qp, kp, vp, up, kwin_p, vwin_p = _in_proj(xp2, mods_p[0], mods_p[1], g_mix_l, w_in_bf, qn, kn, rope_p,
                                              TOKEN_TILE, tiles_per_batch, tiles_per_batch, n_keep // TOKEN_TILE)
    qs, _, _, us, k_new, v_new = _in_proj(xs2, mods_s[0], mods_s[1], g_mix_l, w_in_bf, qn, kn, rope_s, bd, 1, 1, 1)

    to_seq = lambda t: t.reshape(nb, seq, -1)
    heads = lambda t: t.reshape(bd, N_HEADS, HEAD_DIM)
    ck, cv = cache_k_win[l], cache_v_win[l]
    k_new, v_new = heads(k_new), heads(v_new)
    attn_p, k_win_s, v_win_s = _attn_prompt(to_seq(qp), to_seq(kp), to_seq(vp), ck, cv, k_new, v_new)
    attn_p = attn_p.reshape(n_prompt, D_ATTN)
    pool_p = _pool_prompt(to_seq(up), w_pool_bf, ps).reshape(n_prompt, D_POOL)
    attn_s = _attn_decode(heads(qs), k_new, v_new, ck, cv).reshape(bd, D_ATTN)
    pool_s = _pool_decode(us, jnp.swapaxes(state_pool[l], 0, 1), w_pool_bf, ps)

    base_p, h2_all, lg_p = _out_proj(xp2, attn_p, pool_p, mods_p[2], mods_p[3], mods_p[4], mods_p[5],
                                     w_out_bf, g_ffn_l, w_router_hi, w_router_lo, wsg_bf, wsu_bf, wsd_bf,
                                     TOKEN_TILE, tiles_per_batch, n_tok, None, 0)
    base_s, h2_all, lg_s = _out_proj(xs2, attn_s, pool_s, mods_s[2], mods_s[3], mods_s[4], mods_s[5],
                                     w_out_bf, g_ffn_l, w_router_hi, w_router_lo, wsg_bf, wsu_bf, wsd_bf,
                                     bd, 1, n_tok, h2_all, n_prompt // bd)

    y_planes, wts_t = _moe_routed(h2_all, jnp.concatenate([lg_p, lg_s], axis=1), b_router[l],
                                  w_gate[l], w_up[l], w_down[l])

    yp = _combine(y_planes, wts_t[:n_prompt], base_p, mods_p[5], COMBINE_TILE, seq // COMBINE_TILE, 0)
    ys = _combine(y_planes, wts_t[n_prompt:n_tok], base_s, mods_s[5], bd, 1, n_prompt // bd)

    y_prompt = yp.reshape(nb, seq, D_MODEL)
    y_sample = ys.reshape(bd, dec_seq, D_MODEL)
    win = lambda t: t.reshape(1, nb, n_keep, N_HEADS, HEAD_DIM)
    pool_p_state = to_seq(up)[:, seq - POOL_STATE:][None]
    pool_s_state = jnp.concatenate([state_pool[l][:, 1:], us[:, None, :]], axis=1)[None]
    return (y_prompt, y_sample, win(kwin_p), win(vwin_p), pool_p_state,
            k_win_s[None], v_win_s[None], pool_s_state)
```

```python
import functools

import jax
import jax.numpy as jnp
from jax import lax
from jax.experimental import pallas as pl
from jax.experimental.pallas import tpu as pltpu

F32 = jnp.float32
BF16 = jnp.bfloat16
I32 = jnp.int32

D_MODEL = 2048
N_HEADS = 8
HEAD_DIM = 128
D_ATTN = N_HEADS * HEAD_DIM
D_POOL = D_MODEL - D_ATTN
D_IN = 3 * D_ATTN + D_POOL
ROPE_DIM = HEAD_DIM // 4
ROPE_HALF = ROPE_DIM // 2
ROPE_THETA = 500000.0
DILATED_GROUPS = ((128, 1), (512, 4), (2048, 16))
QBLOCK = 128
ATTN_SCALE = HEAD_DIM ** -0.5
POOL_WINDOWS = (2, 4, 8, 16)
POOL_GROUP_DIM = D_POOL // len(POOL_WINDOWS)
POOL_STATE = max(POOL_WINDOWS) - 1
POOL_HALO = 16
N_EXPERTS = 64
TOP_K = 8
N_EXPERT_GROUPS = 8
GROUP_SIZE = N_EXPERTS // N_EXPERT_GROUPS
TOPK_GROUPS = 4
D_EXPERT = 512
D_SHARED = 512
ROUTED_SCALE = 2.5
EPS = 1e-6
PAST_LEN = 16384
MAX_WINDOW = 2048

LANES = 128
ROW_CHUNKS = D_MODEL // LANES
NEG_BIG = -1e30
LOG2_E = 1.4426950408889634

TOKEN_TILE = 256
EXPERT_TILE = 256
COMBINE_TILE = 128
ROUTE_TILE = 256
VMEM_LIMIT = 60 * 1024 * 1024


def _cparams(sem, vmem=VMEM_LIMIT):
    return pltpu.CompilerParams(dimension_semantics=sem, vmem_limit_bytes=vmem)


def _ada_kernel(c_ref, w_ref, b_ref, o_ref):
    c = c_ref[...]
    s = (c * jax.nn.sigmoid(c)).astype(BF16)
    o_ref[...] = jnp.dot(s, w_ref[...].astype(BF16), preferred_element_type=F32) + b_ref[...]


def _ada(c_all, w_ada, b_ada):
    rows = c_all.shape[0]
    n = w_ada.shape[1]
    tn = 1024
    return pl.pallas_call(
        _ada_kernel,
        grid=(n // tn,),
        in_specs=[
            pl.BlockSpec((rows, D_MODEL), lambda j: (0, 0)),
            pl.BlockSpec((D_MODEL, tn), lambda j: (0, j)),
            pl.BlockSpec((1, tn), lambda j: (0, j)),
        ],
        out_specs=pl.BlockSpec((rows, tn), lambda j: (0, j)),
        out_shape=jax.ShapeDtypeStruct((rows, n), F32),
        compiler_params=_cparams(("parallel",)),
        name="ada_modulation",
    )(c_all, w_ada, b_ada.reshape(1, n))


def _rms(x):
    return x * lax.rsqrt(jnp.mean(x * x, axis=-1, keepdims=True) + EPS)


def _in_kernel(x_ref, sh_ref, sc_ref, g_ref, w_ref, qn_ref, kn_ref, c_ref, s1_ref, s2_ref,
               q_ref, k_ref, v_ref, u_ref, kw_ref, vw_ref):
    tm = x_ref.shape[0]
    x = x_ref[...]
    h = _rms(x) * g_ref[...] * (1.0 + sc_ref[0]) + sh_ref[0]
    hb = h.astype(BF16)
    cos = c_ref[...]
    s1 = s1_ref[...]
    s2 = s2_ref[...]

    def head_rows(win_ref, hd, val):
        win_ref[pl.ds(hd, tm, stride=N_HEADS), :] = val

    def qk(sec, nrm, out_ref, win_ref):
        z = jnp.dot(hb, w_ref[:, sec * D_ATTN:(sec + 1) * D_ATTN], preferred_element_type=F32)
        for hd in range(N_HEADS):
            sl = slice(hd * HEAD_DIM, (hd + 1) * HEAD_DIM)
            r = _rms(z[:, sl]) * nrm
            r = r * cos + pltpu.roll(r, HEAD_DIM - ROPE_HALF, 1) * s1 + pltpu.roll(r, ROPE_HALF, 1) * s2
            out_ref[:, sl] = r
            if win_ref is not None:
                head_rows(win_ref, hd, r)

    qk(0, qn_ref[...], q_ref, None)
    qk(1, kn_ref[...], k_ref, kw_ref)
    v = jnp.dot(hb, w_ref[:, 2 * D_ATTN:3 * D_ATTN], preferred_element_type=F32)
    v_ref[...] = v
    for hd in range(N_HEADS):
        head_rows(vw_ref, hd, v[:, hd * HEAD_DIM:(hd + 1) * HEAD_DIM])
    u_ref[...] = jnp.dot(hb, w_ref[:, 3 * D_ATTN:], preferred_element_type=F32)


def _in_proj(x2d, shift, scale, g_mix, w_in_bf, q_norm, k_norm, rope, tm, rows_per_mod, rope_tiles, keep_tiles):
    m = x2d.shape[0]
    r = shift.shape[1]
    n_groups = m // tm // rows_per_mod
    skip = rows_per_mod - keep_tiles
    mod_spec = pl.BlockSpec((1, r, D_MODEL), lambda i: (i // rows_per_mod, 0, 0))
    rope_spec = pl.BlockSpec((tm, HEAD_DIM), lambda i: (i % rope_tiles, 0))
    out_spec = pl.BlockSpec((tm, D_ATTN), lambda i: (i, 0))
    out_sd = jax.ShapeDtypeStruct((m, D_ATTN), F32)
    win_spec = pl.BlockSpec(
        (tm * N_HEADS, HEAD_DIM),
        lambda i: ((i // rows_per_mod) * keep_tiles + jnp.maximum(i % rows_per_mod - skip, 0), 0))
    win_sd = jax.ShapeDtypeStruct((n_groups * keep_tiles * tm * N_HEADS, HEAD_DIM), F32)
    return pl.pallas_call(
        _in_kernel,
        grid=(m // tm,),
        in_specs=[
            pl.BlockSpec((tm, D_MODEL), lambda i: (i, 0)),
            mod_spec, mod_spec,
            pl.BlockSpec((1, D_MODEL), lambda i: (0, 0)),
            pl.BlockSpec((D_MODEL, D_IN), lambda i: (0, 0)),
            pl.BlockSpec((1, HEAD_DIM), lambda i: (0, 0)),
            pl.BlockSpec((1, HEAD_DIM), lambda i: (0, 0)),
            rope_spec, rope_spec, rope_spec,
        ],
        out_specs=[out_spec] * 4 + [win_spec] * 2,
        out_shape=[out_sd] * 4 + [win_sd] * 2,
        compiler_params=_cparams(("arbitrary",)),
        name="in_proj",
    )(x2d, shift, scale, g_mix, w_in_bf, q_norm, k_norm, *rope)


def _rope_tables(pos):
    inv_freq = jnp.float32(ROPE_THETA) ** (-jnp.arange(ROPE_HALF, dtype=F32) / ROPE_HALF)
    ang = pos.astype(F32)[:, None] * inv_freq[None, :]
    cos, sin = jnp.cos(ang), jnp.sin(ang)
    t = pos.shape[0]
    rest = HEAD_DIM - ROPE_DIM
    c = jnp.concatenate([cos, cos, jnp.ones((t, rest), F32)], axis=1)
    s1 = jnp.concatenate([-sin, jnp.zeros((t, ROPE_HALF + rest), F32)], axis=1)
    s2 = jnp.concatenate([jnp.zeros((t, ROPE_HALF), F32), sin, jnp.zeros((t, rest), F32)], axis=1)
    return c, s1, s2


ATTN_INTERLEAVE = 4
WINDOW_ROWS_PER_STEP = 2
REGROUP_MIN_DILATION = 16


REGROUP_STRIDE = 4


def _to_residue_major(read_rows, dst, tmp, seq, sub_len):
    n, part = REGROUP_STRIDE, seq // REGROUP_STRIDE
    for a in range(n):
        tmp[pl.ds(a * part, part), :] = read_rows(pl.ds(a, part, stride=n))
    for a in range(n):
        for s in range(n):
            dst[pl.ds((s * n + a) * sub_len, sub_len), :] = tmp[pl.ds(a * part + s, sub_len, stride=n), :]


def _from_residue_major(src, dst, tmp, seq, sub_len):
    n, part = REGROUP_STRIDE, seq // REGROUP_STRIDE
    for a in range(n):
        for s in range(n):
            tmp[pl.ds(a * part + s, sub_len, stride=n), :] = src[pl.ds((s * n + a) * sub_len, sub_len), :]
    for a in range(n):
        dst[pl.ds(a, part, stride=n), :] = tmp[pl.ds(a * part, part), :]


def _attn_prompt_kernel(q_ref, k_ref, v_ref, ck_hbm, cv_hbm, kn_ref, vn_ref, o_ref, ok_hbm, ov_hbm,
                        m_scr, l_scr, acc_scr, q_rm, k_rm, v_rm, m_rm, l_rm, a_rm, wbuf, isem, osem):
    seq = q_ref.shape[1]
    half = wbuf.shape[2]
    step = pl.program_id(0) * pl.num_programs(1) + pl.program_id(1)
    last_step = pl.num_programs(0) * pl.num_programs(1) - 1
    windows = ((ck_hbm, kn_ref, ok_hbm), (cv_hbm, vn_ref, ov_hbm))
    row0 = step * WINDOW_ROWS_PER_STEP
    n_bounds = 2 * WINDOW_ROWS_PER_STEP
    assert n_bounds == len(DILATED_GROUPS) + 1

    def chunk_in(row, h, a):
        n_rows = half - h
        return pltpu.make_async_copy(windows[a][0].at[row, pl.ds(h * half + 1, n_rows)],
                                     wbuf.at[h, a, pl.ds(0, n_rows)], isem.at[h, a])

    def chunk_out(row, h, a):
        return pltpu.make_async_copy(wbuf.at[h, a], windows[a][2].at[row, pl.ds(h * half, half)], osem.at[h, a])

    def finish_in_start_out(row, h):
        for a in range(2):
            chunk_in(row, h, a).wait()
            if h == 1:
                wbuf[h, a, half - 1] = windows[a][1][row]
            chunk_out(row, h, a).start()

    def finish_out(row, h):
        for a in range(2):
            chunk_out(row, h, a).wait()

    def window_boundary(b):
        h = b % 2
        if b == 0:
            @pl.when(step > 0)
            def _():
                finish_in_start_out(row0 - 1, 1)
                finish_out(row0 - 1, 0)
        elif b == 1:
            finish_in_start_out(row0, 0)
            pl.when(step > 0)(lambda: finish_out(row0 - 1, 1))
        else:
            finish_in_start_out(row0 + (b - 1) // 2, (b - 1) % 2)
            finish_out(row0 + (b - 2) // 2, h)
        for a in range(2):
            chunk_in(row0 + b // 2, h, a).start()

    window_phases = [functools.partial(window_boundary, b) for b in range(n_bounds)]
    window_phases[0]()
    m_scr[...] = jnp.full(m_scr.shape, NEG_BIG, F32)
    l_scr[...] = jnp.zeros(l_scr.shape, F32)
    acc_scr[...] = jnp.zeros(acc_scr.shape, F32)
    qi = lax.broadcasted_iota(I32, (QBLOCK, 2 * QBLOCK), 0)
    kj = lax.broadcasted_iota(I32, (QBLOCK, 2 * QBLOCK), 1)
    dist = qi + QBLOCK - kj
    band_mask = jnp.logical_and(dist >= 0, dist <= QBLOCK)
    bias_band = jnp.where(band_mask, 0.0, NEG_BIG)
    bias_first = jnp.where(jnp.logical_and(band_mask, kj >= QBLOCK), 0.0, NEG_BIG)
    ones = jnp.ones((2 * QBLOCK, HEAD_DIM), BF16)
    dims = (((1,), (1,)), ((), ()))
    q_scale = ATTN_SCALE * LOG2_E

    regrouped = None
    for window, dil in DILATED_GROUPS:
        assert window // dil == QBLOCK
        n_units = seq // QBLOCK
        assert n_units % ATTN_INTERLEAVE == 0 and (dil % ATTN_INTERLEAVE == 0 or dil == 1)

        def rows(start, dil=dil):
            return pl.ds(start, QBLOCK) if dil == 1 else pl.ds(start, QBLOCK, stride=dil)

        sub_len = seq // dil
        regroup = dil >= REGROUP_MIN_DILATION
        if regroup:
            assert dil == REGROUP_STRIDE ** 2 and regrouped is None
            regrouped = (dil, sub_len)
            for src, dst in ((q_ref, q_rm), (k_ref, k_rm), (v_ref, v_rm)):
                _to_residue_major(lambda rows_, src=src: src[0, rows_, :], dst, m_rm, seq, sub_len)
            m_rm[...] = jnp.full(m_rm.shape, NEG_BIG, F32)
            l_rm[...] = jnp.zeros(l_rm.shape, F32)
            a_rm[...] = jnp.zeros(a_rm.shape, F32)

        def qkv_blocks(blk, res, start, prev, rows=rows, regroup=regroup, sub_len=sub_len):
            if regroup:
                cur = pl.ds(res * sub_len + blk * QBLOCK, QBLOCK)
                prv = pl.ds(res * sub_len + jnp.maximum(blk - 1, 0) * QBLOCK, QBLOCK)
                return (q_rm[cur, :], k_rm[prv, :], k_rm[cur, :], v_rm[prv, :], v_rm[cur, :])
            return (q_ref[0, rows(start), :], k_ref[0, rows(prev), :], k_ref[0, rows(start), :],
                    v_ref[0, rows(prev), :], v_ref[0, rows(start), :])

        state_refs = (m_rm, l_rm, a_rm) if regroup else (m_scr, l_scr, acc_scr)

        def state_rows(blk, res, start, rows=rows, regroup=regroup, sub_len=sub_len):
            return pl.ds(res * sub_len + blk * QBLOCK, QBLOCK) if regroup else rows(start)

        def body(it, carry, dil=dil, rows=rows, qkv_blocks=qkv_blocks, state_refs=state_refs, state_rows=state_rows):
            m_st, l_st, a_st = state_refs
            loaded = []
            for u in range(ATTN_INTERLEAVE):
                f = it * ATTN_INTERLEAVE + u
                blk, res = f // dil, f % dil
                start = blk * (QBLOCK * dil) + res
                prev = jnp.maximum(start - QBLOCK * dil, res)
                q_blk, k_prev, k_cur, v_prev, v_cur = qkv_blocks(blk, res, start, prev)
                q = (q_blk * q_scale).astype(BF16)
                k = jnp.concatenate([k_prev, k_cur], axis=0).astype(BF16)
                v = jnp.concatenate([v_prev, v_cur], axis=0).astype(BF16)
                v_ones = jnp.concatenate([v, ones], axis=1)
                st = state_rows(blk, res, start)
                state = (m_st[st, :], l_st[st, :], a_st[st, :])
                loaded.append((st, blk, q, k, v_ones, state))
            scores = [lax.dot_general(q, k, dims, preferred_element_type=F32)
                      + jnp.where(blk > 0, bias_band, bias_first)
                      for _, blk, q, k, _, _ in loaded]
            probs = []
            for s, (_, _, _, _, _, (m_old, _, _)) in zip(scores, loaded):
                m_new = jnp.maximum(m_old, jnp.max(s, axis=1, keepdims=True))
                p = jnp.exp2(s - jnp.concatenate([m_new, m_new], axis=1)).astype(BF16)
                probs.append((m_new, jnp.exp2(m_old - m_new), p))
            updated = []
            for (m_new, alpha, p), (start, _, _, _, v_ones, (_, l_old, a_old)) in zip(probs, loaded):
                pv = jnp.dot(p, v_ones, preferred_element_type=F32)
                updated.append((start, m_new, alpha * l_old + pv[:, HEAD_DIM:], alpha * a_old + pv[:, :HEAD_DIM]))
            for st, m_new, l_new, a_new in updated:
                m_st[st, :] = m_new
                l_st[st, :] = l_new
                a_st[st, :] = a_new
            return carry

        trips = n_units // ATTN_INTERLEAVE
        phase = window_phases[DILATED_GROUPS.index((window, dil)) + 1]
        if (window, dil) == DILATED_GROUPS[-1]:
            lax.fori_loop(0, trips // 2, body, 0)
            phase()
            lax.fori_loop(trips // 2, trips, body, 0)
        else:
            lax.fori_loop(0, trips, body, 0)
            phase()

    @pl.when(step == last_step)
    def _():
        last_row = row0 + WINDOW_ROWS_PER_STEP - 1
        finish_in_start_out(last_row, 1)
        finish_out(last_row, 0)
        finish_out(last_row, 1)

    if regrouped is None:
        o_ref[0] = acc_scr[...] / l_scr[...]
    else:
        _, sub_len = regrouped
        _from_residue_major(a_rm, q_rm, k_rm, seq, sub_len)
        _from_residue_major(l_rm, k_rm, a_rm, seq, sub_len)
        _from_residue_major(m_rm, v_rm, a_rm, seq, sub_len)
        m_a, m_b = m_scr[...], v_rm[...]
        m_all = jnp.maximum(m_a, m_b)
        w_a, w_b = jnp.exp2(m_a - m_all), jnp.exp2(m_b - m_all)
        o_ref[0] = (acc_scr[...] * w_a + q_rm[...] * w_b) / (l_scr[...] * w_a + k_rm[...] * w_b)


def _attn_prompt(q, k, v, cache_k, cache_v, k_new, v_new):
    b, s, _ = q.shape
    bd, n_buf = cache_k.shape[0], cache_k.shape[1]
    assert bd == b * N_HEADS * WINDOW_ROWS_PER_STEP
    spec = pl.BlockSpec((1, s, HEAD_DIM), lambda bi, hi: (bi, 0, hi))
    any_spec = pl.BlockSpec(memory_space=pl.ANY)
    new_spec = pl.BlockSpec(k_new.shape, lambda bi, hi: (0, 0, 0))
    win_sd = jax.ShapeDtypeStruct(cache_k.shape, cache_k.dtype)
    return pl.pallas_call(
        _attn_prompt_kernel,
        grid=(b, N_HEADS),
        in_specs=[spec, spec, spec, any_spec, any_spec, new_spec, new_spec],
        out_specs=[spec, any_spec, any_spec],
        out_shape=[jax.ShapeDtypeStruct((b, s, D_ATTN), F32), win_sd, win_sd],
        scratch_shapes=[pltpu.VMEM((s, HEAD_DIM), F32)] * 9 + [
            pltpu.VMEM((2, 2, n_buf // 2) + cache_k.shape[2:], F32),
            pltpu.SemaphoreType.DMA((2, 2)),
            pltpu.SemaphoreType.DMA((2, 2)),
        ],
        compiler_params=_cparams(("arbitrary", "arbitrary")),
        name="attn_prompt",
    )(q, k, v, cache_k, cache_v, k_new, v_new)


def _attn_decode_kernel(q_ref, kn_ref, vn_ref, k1_ref, k2_ref, k3_ref, v1_ref, v2_ref, v3_ref, o_ref):
    q = q_ref[0]
    k_new = kn_ref[0]
    v_new = vn_ref[0]
    n_grp = len(DILATED_GROUPS)

    def scores(kb):
        return jnp.sum(kb * q[None], axis=-1, keepdims=True) * ATTN_SCALE

    kcs = [k1_ref[0], k2_ref[0][:, 0], k3_ref[0][:, 0]]
    vcs = [v1_ref[0], v2_ref[0][:, 0], v3_ref[0][:, 0]]
    s_new = jnp.sum(k_new * q, axis=-1, keepdims=True) * ATTN_SCALE
    s_grp = [scores(kb) for kb in kcs]
    m = s_new
    for s in s_grp:
        m = jnp.maximum(m, jnp.max(s, axis=0))
    p_new = jnp.exp(s_new - m)
    den = n_grp * p_new
    num = n_grp * p_new * v_new
    for s, vb in zip(s_grp, vcs):
        p = jnp.exp(s - m[None])
        den = den + jnp.sum(p, axis=0)
        num = num + jnp.sum(p * vb, axis=0)
    o_ref[0] = num / den


def _attn_decode(q, k_new, v_new, cache_k, cache_v):
    bd, n_buf = cache_k.shape[0], cache_k.shape[1]
    assert n_buf == MAX_WINDOW

    def views(cache):
        out, specs = [], []
        for window, dil in DILATED_GROUPS:
            band = window // dil
            assert band == QBLOCK and n_buf % dil == 0 and n_buf // dil >= band
            if dil == 1:
                out.append(cache)
                specs.append(pl.BlockSpec((1, band, N_HEADS, HEAD_DIM),
                                          lambda b, nb=n_buf // band: (b, nb - 1, 0, 0)))
            else:
                out.append(cache.reshape(bd, n_buf // dil, dil, N_HEADS, HEAD_DIM))
                specs.append(pl.BlockSpec((1, band, 1, N_HEADS, HEAD_DIM),
                                          lambda b, nb=n_buf // dil // band: (b, nb - 1, 0, 0, 0)))
        return out, specs

    kv, kspecs = views(cache_k)
    vv, vspecs = views(cache_v)
    tok = pl.BlockSpec((1, N_HEADS, HEAD_DIM), lambda b: (b, 0, 0))
    return pl.pallas_call(
        _attn_decode_kernel,
        grid=(bd,),
        in_specs=[tok, tok, tok] + kspecs + vspecs,
        out_specs=tok,
        out_shape=jax.ShapeDtypeStruct((bd, N_HEADS, HEAD_DIM), F32),
        compiler_params=_cparams(("parallel",)),
        name="attn_decode",
    )(q, k_new, v_new, *kv, *vv)


def _pool_project(d_groups, wp_ref, ps_ref, out_ref):
    for g, d in enumerate(d_groups):
        sl = slice(g * POOL_GROUP_DIM, (g + 1) * POOL_GROUP_DIM)
        out_ref[:, sl] = jnp.dot(d.astype(BF16), wp_ref[g], preferred_element_type=F32) * ps_ref[:, sl]


def _pool_prompt_kernel(u_ref, prev_ref, wp_ref, ps_ref, o_ref):
    i = pl.program_id(1)
    tp = u_ref.shape[1]
    u = u_ref[0]
    prev = jnp.where(i > 0, prev_ref[0], 0.0)
    ext = jnp.concatenate([prev, u], axis=0)
    pos = i * tp + lax.broadcasted_iota(I32, (tp, 1), 0)
    d_groups = []
    for g, w in enumerate(POOL_WINDOWS):
        sl = slice(g * POOL_GROUP_DIM, (g + 1) * POOL_GROUP_DIM)
        a = ext[:, sl]
        span = 1
        while span < w:
            a = a + pltpu.roll(a, span, 0)
            span *= 2
        win = a[POOL_HALO:, :]
        cnt = jnp.minimum(w, pos + 1).astype(F32)
        d_groups.append(win / cnt - u[:, sl])
    _pool_project(d_groups, wp_ref, ps_ref, o_ref.at[0])


def _pool_prompt(u, w_pool_bf, pool_scale):
    b, s, _ = u.shape
    tp = 512
    halo_blocks = tp // POOL_HALO
    return pl.pallas_call(
        _pool_prompt_kernel,
        grid=(b, s // tp),
        in_specs=[
            pl.BlockSpec((1, tp, D_POOL), lambda bi, i: (bi, i, 0)),
            pl.BlockSpec((1, POOL_HALO, D_POOL), lambda bi, i: (bi, jnp.maximum(i * halo_blocks - 1, 0), 0)),
            pl.BlockSpec(w_pool_bf.shape, lambda bi, i: (0, 0, 0)),
            pl.BlockSpec((1, D_POOL), lambda bi, i: (0, 0)),
        ],
        out_specs=pl.BlockSpec((1, tp, D_POOL), lambda bi, i: (bi, i, 0)),
        out_shape=jax.ShapeDtypeStruct((b, s, D_POOL), F32),
        compiler_params=_cparams(("parallel", "parallel")),
        name="pool_prompt",
    )(u, u, w_pool_bf, pool_scale)


def _pool_decode_kernel(u_ref, st_ref, wp_ref, ps_ref, o_ref):
    u = u_ref[...]
    d_groups = []
    for g, w in enumerate(POOL_WINDOWS):
        sl = slice(g * POOL_GROUP_DIM, (g + 1) * POOL_GROUP_DIM)
        win = u[:, sl]
        for j in range(1, w):
            win = win + st_ref[POOL_STATE - j][:, sl]
        d_groups.append(win / float(w) - u[:, sl])
    _pool_project(d_groups, wp_ref, ps_ref, o_ref)


def _pool_decode(u, state_t, w_pool_bf, pool_scale):
    assert PAST_LEN + 1 >= max(POOL_WINDOWS)
    return pl.pallas_call(
        _pool_decode_kernel,
        out_shape=jax.ShapeDtypeStruct(u.shape, F32),
        compiler_params=_cparams(None),
        name="pool_decode",
    )(u, state_t, w_pool_bf, pool_scale)


def _split_bf16(x):
    hi = x.astype(BF16)
    lo = (x - hi.astype(F32)).astype(BF16)
    return hi, lo


def _out_kernel(x_ref, a_ref, p_ref, g1_ref, sh_ref, sc_ref, g2_ref, wo_ref, gf_ref, wrh_ref, wrl_ref,
                wsg_ref, wsu_ref, wsd_ref, base_ref, h2_ref, lg_ref):
    tm = x_ref.shape[0]
    mix = (jnp.dot(a_ref[...].astype(BF16), wo_ref[:D_ATTN, :], preferred_element_type=F32)
           + jnp.dot(p_ref[...].astype(BF16), wo_ref[D_ATTN:, :], preferred_element_type=F32))
    x1 = x_ref[...] + g1_ref[0] * mix
    h2 = _rms(x1) * gf_ref[...] * (1.0 + sc_ref[0]) + sh_ref[0]
    for c in range(ROW_CHUNKS):
        h2_ref[pl.ds(c, tm, stride=ROW_CHUNKS), :] = h2[:, c * LANES:(c + 1) * LANES]
    h_hi, h_lo = _split_bf16(h2)
    w_hi, w_lo = wrh_ref[...], wrl_ref[...]
    lg = (jnp.dot(h_hi, w_hi, preferred_element_type=F32) + jnp.dot(h_lo, w_hi, preferred_element_type=F32)
          + jnp.dot(h_hi, w_lo, preferred_element_type=F32))
    if tm < LANES:
        lg = jnp.concatenate([lg, jnp.zeros((LANES - tm, LANES), F32)], axis=0)
    lg_ref[...] = lg.T[:N_EXPERTS, :tm]
    sg = jnp.dot(h_hi, wsg_ref[...], preferred_element_type=F32)
    su = jnp.dot(h_hi, wsu_ref[...], preferred_element_type=F32)
    hs = (sg * jax.nn.sigmoid(sg) * su).astype(BF16)
    base_ref[...] = x1 + g2_ref[0] * jnp.dot(hs, wsd_ref[...], preferred_element_type=F32)


def _out_proj(x2d, attn, pool, gate1, shift2, scale2, gate2, w_out_bf, g_ffn, w_router_hi, w_router_lo,
              ws_gate_bf, ws_up_bf, ws_down_bf, tm, rows_per_mod, h2_tokens, h2_all, h2_block0):
    m = x2d.shape[0]
    r = gate1.shape[1]
    n_main = m // tm
    h2_rows = h2_tokens * ROW_CHUNKS
    n_fill = 0 if h2_all is not None else pl.cdiv(h2_rows - m * ROW_CHUNKS, tm * ROW_CHUNKS)
    last = n_main - 1
    row = lambda i: jnp.minimum(i, last)
    mod_spec = pl.BlockSpec((1, r, D_MODEL), lambda i: (row(i) // rows_per_mod, 0, 0))
    const = lambda shape: pl.BlockSpec(shape, lambda i: (0,) * len(shape))
    in_specs = [
        pl.BlockSpec((tm, D_MODEL), lambda i: (row(i), 0)),
        pl.BlockSpec((tm, D_ATTN), lambda i: (row(i), 0)),
        pl.BlockSpec((tm, D_POOL), lambda i: (row(i), 0)),
        mod_spec, mod_spec, mod_spec, mod_spec,
        const((D_MODEL, D_MODEL)),
        const((1, D_MODEL)),
        const((D_MODEL, LANES)), const((D_MODEL, LANES)),
        const((D_MODEL, D_SHARED)), const((D_MODEL, D_SHARED)), const((D_SHARED, D_MODEL)),
    ]
    args = [x2d, attn, pool, gate1, shift2, scale2, gate2, w_out_bf, g_ffn, w_router_hi, w_router_lo,
            ws_gate_bf, ws_up_bf, ws_down_bf]
    aliases = {}
    n_in = len(args)
    if h2_all is not None:
        in_specs.append(pl.BlockSpec(memory_space=pl.ANY))
        args.append(h2_all)
        aliases = {n_in: 1}

    def kernel(*refs):
        refs = refs[:n_in] + refs[len(args):]
        if n_fill == 0:
            _out_kernel(*refs)
            return
        step = pl.program_id(0)
        pl.when(step < n_main)(lambda: _out_kernel(*refs))

        @pl.when(step >= n_main)
        def _():
            h2_ref = refs[n_in + 1]
            h2_ref[...] = jnp.zeros(h2_ref.shape, F32)

    return pl.pallas_call(
        kernel,
        grid=(n_main + n_fill,),
        in_specs=in_specs,
        out_specs=[
            pl.BlockSpec((tm, D_MODEL), lambda i: (row(i), 0)),
            pl.BlockSpec((tm * ROW_CHUNKS, LANES), lambda i: (i + h2_block0, 0)),
            pl.BlockSpec((N_EXPERTS, tm), lambda i: (0, row(i))),
        ],
        out_shape=[
            jax.ShapeDtypeStruct((m, D_MODEL), F32),
            jax.ShapeDtypeStruct((h2_rows, LANES), F32),
            jax.ShapeDtypeStruct((N_EXPERTS, m), F32),
        ],
        input_output_aliases=aliases,
        compiler_params=_cparams(("arbitrary",)),
        name="out_proj",
    )(*args)


def _first_index_of_max(v, iota, size, axis):
    m = jnp.max(v, axis=axis, keepdims=True)
    idx = jnp.min(jnp.where(v == m, iota, size), axis=axis, keepdims=True)
    return m, idx


def _route_kernel(n_valid, lg_ref, b_ref, eidx_ref, wts_ref, cnt_ref, carry):
    i = pl.program_id(0)
    tn = lg_ref.shape[1]

    @pl.when(i == 0)
    def _():
        carry[...] = jnp.zeros(carry.shape, F32)

    scores = jax.nn.sigmoid(lg_ref[...])
    biased = scores + b_ref[...]
    grp = biased.reshape(N_EXPERT_GROUPS, GROUP_SIZE, tn)
    io_g = lax.broadcasted_iota(I32, grp.shape, 1)
    m1, i1 = _first_index_of_max(grp, io_g, GROUP_SIZE, 1)
    m2 = jnp.max(jnp.where(io_g == i1, -jnp.inf, grp), axis=1, keepdims=True)
    gscore = (m1 + m2)[:, 0, :]
    io_n = lax.broadcasted_iota(I32, gscore.shape, 0)
    gsel = jnp.zeros(gscore.shape, jnp.bool_)
    for _ in range(TOPK_GROUPS):
        _, gi = _first_index_of_max(gscore, io_n, N_EXPERT_GROUPS, 0)
        hit = io_n == gi
        gsel = jnp.logical_or(gsel, hit)
        gscore = jnp.where(hit, -jnp.inf, gscore)
    emask = jnp.broadcast_to(gsel[:, None, :], grp.shape).reshape(N_EXPERTS, tn)
    cand = jnp.where(emask, biased, -jnp.inf)
    io_e = lax.broadcasted_iota(I32, cand.shape, 0)
    picked = jnp.zeros(cand.shape, jnp.bool_)
    eidx, sel = [], []
    for _ in range(TOP_K):
        _, ei = _first_index_of_max(cand, io_e, N_EXPERTS, 0)
        hit = io_e == ei
        eidx.append(ei)
        sel.append(jnp.sum(jnp.where(hit, scores, 0.0), axis=0, keepdims=True))
        picked = jnp.logical_or(picked, hit)
        cand = jnp.where(hit, -jnp.inf, cand)
    sel = jnp.concatenate(sel, axis=0)
    eidx = jnp.concatenate(eidx, axis=0)
    wts_ref[...] = sel / jnp.sum(sel, axis=0, keepdims=True) * ROUTED_SCALE
    eidx_ref[...] = eidx

    tok = i * tn + lax.broadcasted_iota(I32, cand.shape, 1)
    mask = jnp.logical_and(picked, tok < n_valid).astype(F32)
    total = carry[...] + jnp.sum(mask, axis=1, keepdims=True)
    carry[...] = total
    cnt_ref[...] = total.astype(I32)


def _route(logits_t, b_router, n_valid):
    mp = logits_t.shape[1]
    tn = ROUTE_TILE
    tok_spec = pl.BlockSpec((TOP_K, tn), lambda i: (0, i))
    return pl.pallas_call(
        functools.partial(_route_kernel, n_valid),
        grid=(mp // tn,),
        in_specs=[
            pl.BlockSpec((N_EXPERTS, tn), lambda i: (0, i)),
            pl.BlockSpec((N_EXPERTS, 1), lambda i: (0, 0)),
        ],
        out_specs=[tok_spec, tok_spec, pl.BlockSpec((N_EXPERTS, LANES), lambda i: (0, 0))],
        out_shape=[
            jax.ShapeDtypeStruct((TOP_K, mp), I32),
            jax.ShapeDtypeStruct((TOP_K, mp), F32),
            jax.ShapeDtypeStruct((N_EXPERTS, LANES), I32),
        ],
        scratch_shapes=[pltpu.VMEM((N_EXPERTS, LANES), F32)],
        compiler_params=_cparams(("arbitrary",)),
        name="route",
    )(logits_t, b_router.reshape(N_EXPERTS, 1))


TILE_USED = 1
TILE_FIRST = 2
TILE_WSLOT = 4
TILE_PREV_USED = 8
PAIR_BITS = 3
assert 1 << PAIR_BITS == TOP_K
DUMMY_TOKENS = 2 * EXPERT_TILE // TOP_K
GATHER_SLOTS = 3


def _experts_kernel(n_tok, te_ref, ts_ref, tn_ref, tf_ref, ne_ref, g_ref, s_ref,
                    h_hbm, wg_hbm, wu_hbm, wd_hbm, y_hbm,
                    xbuf, ybuf, xs, wg_f, wu_f, wd_f, wg_bf, wu_bf, wd_bf, gsem, ssem, wsem, zsem):
    j = pl.program_id(0)
    last = pl.num_programs(0) - 1
    tm = EXPERT_TILE
    rc = ROW_CHUNKS
    plane_tokens = n_tok + DUMMY_TOKENS
    slot = j % 2
    other = 1 - slot
    gslot = j % GATHER_SLOTS
    flags = tf_ref[j]
    used = (flags & TILE_USED) != 0
    first = (flags & TILE_FIRST) != 0
    prev_used = (flags & TILE_PREV_USED) != 0
    wslot = (flags // TILE_WSLOT) & 1
    prv = jnp.maximum(j - 1, 0)
    ahead = jnp.minimum(j + GATHER_SLOTS - 1, last)

    def buf_rows(i):
        return pl.ds(i * rc, rc) if isinstance(i, int) else pl.ds(pl.multiple_of(i * rc, rc), rc)

    def gather_row(i, start, dst):
        row = g_ref[start + i]
        pltpu.make_async_copy(h_hbm.at[pl.ds(pl.multiple_of(row, rc), rc), :],
                              xbuf.at[dst, buf_rows(i), :], gsem.at[dst]).start(priority=0)

    def scatter_row(i, start, n_valid, src):
        spare = ((i & (TOP_K - 1)) * plane_tokens + n_tok + (i >> PAIR_BITS)) * rc + src * ((tm >> PAIR_BITS) * rc)
        row = jnp.where(i < n_valid, s_ref[start + i], spare)
        pltpu.make_async_copy(ybuf.at[src, buf_rows(i), :],
                              y_hbm.at[pl.ds(pl.multiple_of(row, rc), rc), :],
                              ssem.at[src]).start(priority=i % 2 if isinstance(i, int) else 0)

    def scatter_tile_loop(tile, src):
        start, n_valid = ts_ref[tile], tn_ref[tile]
        lax.fori_loop(0, tm, lambda i, c: (scatter_row(i, start, n_valid, src), c)[1], 0)

    def wait_gather(dst):
        pltpu.make_async_copy(h_hbm.at[pl.ds(0, tm * rc), :], xbuf.at[dst], gsem.at[dst]).wait()

    def wait_scatter(src):
        pltpu.make_async_copy(ybuf.at[src], y_hbm.at[pl.ds(0, tm * rc), :], ssem.at[src]).wait()

    def weight_copies(e, ws):
        return [pltpu.make_async_copy(src.at[e], dst.at[ws], wsem.at[ws])
                for src, dst in ((wg_hbm, wg_f), (wu_hbm, wu_f), (wd_hbm, wd_f))]

    @pl.when(j == 0)
    def _():
        ybuf[1] = jnp.zeros(ybuf.shape[1:], ybuf.dtype)
        fills = [pltpu.make_async_copy(ybuf.at[1, pl.ds(0, DUMMY_TOKENS * rc), :],
                                       y_hbm.at[pl.ds((k * plane_tokens + n_tok) * rc, DUMMY_TOKENS * rc), :],
                                       zsem.at[0]) for k in range(TOP_K)]
        for c in fills:
            c.start()
        for c in fills:
            c.wait()
        for c in weight_copies(te_ref[0], 0):
            c.start(priority=1)
        for t in range(GATHER_SLOTS - 1):
            start_t = ts_ref[jnp.minimum(t, last)]
            lax.fori_loop(0, tm, lambda i, c, start_t=start_t, t=t: (gather_row(i, start_t, t), c)[1], 0)

    @pl.when(first)
    def _():
        for c in weight_copies(te_ref[j], wslot):
            c.wait()
        wg_bf[...] = wg_f[wslot].astype(BF16)
        wu_bf[...] = wu_f[wslot].astype(BF16)
        wd_bf[...] = wd_f[wslot].astype(BF16)
        nxt_e = ne_ref[j]

        @pl.when(nxt_e >= 0)
        def _():
            for c in weight_copies(nxt_e, 1 - wslot):
                c.start(priority=1)

    @pl.when(jnp.logical_and(used, j > 0))
    def _():
        wait_scatter(slot)

    @pl.when(used)
    def _():
        wait_gather(gslot)
        xs[...] = jnp.concatenate(
            [xbuf[gslot, pl.ds(c, tm, stride=rc), :] for c in range(rc)], axis=1).astype(BF16)
        start_n = ts_ref[ahead]
        ahead_slot = (j + GATHER_SLOTS - 1) % GATHER_SLOTS
        for i in range(tm):
            gather_row(i, start_n, ahead_slot)
        start_p = ts_ref[prv]
        n_valid_p = jnp.where(j > 0, tn_ref[prv], 0)
        for i in range(tm):
            scatter_row(i, start_p, n_valid_p, other)
        x = xs[...]
        hg = jnp.dot(x, wg_bf[...], preferred_element_type=F32)
        hu = jnp.dot(x, wu_bf[...], preferred_element_type=F32)
        h = (hg * jax.nn.sigmoid(hg) * hu).astype(BF16)
        y = jnp.dot(h, wd_bf[...], preferred_element_type=F32)
        for c in range(rc):
            ybuf[slot, pl.ds(c, tm, stride=rc), :] = y[:, c * LANES:(c + 1) * LANES]

    @pl.when(jnp.logical_and(jnp.logical_not(used), prev_used))
    def _():
        for t in range(GATHER_SLOTS - 1):
            wait_gather((j + t) % GATHER_SLOTS)
        wait_scatter(slot)
        scatter_tile_loop(prv, other)
        wait_scatter(other)

    @pl.when(jnp.logical_and(used, j == last))
    def _():
        for t in range(1, GATHER_SLOTS):
            wait_gather((j + t) % GATHER_SLOTS)
        wait_scatter(other)
        scatter_tile_loop(j, slot)
        wait_scatter(slot)


def _experts(tile_expert, tile_start, tile_rows, tile_flags, next_expert, gather_rows, scatter_rows, h2_all,
             w_gate, w_up, w_down, n_tok):
    n_tiles = tile_expert.shape[0]
    tm = EXPERT_TILE
    any_spec = pl.BlockSpec(memory_space=pl.ANY)
    grid_spec = pltpu.PrefetchScalarGridSpec(
        num_scalar_prefetch=7,
        grid=(n_tiles,),
        in_specs=[any_spec] * 4,
        out_specs=any_spec,
        scratch_shapes=[
            pltpu.VMEM((GATHER_SLOTS, tm * ROW_CHUNKS, LANES), F32),
            pltpu.VMEM((2, tm * ROW_CHUNKS, LANES), F32),
            pltpu.VMEM((tm, D_MODEL), BF16),
            pltpu.VMEM((2, D_MODEL, D_EXPERT), F32),
            pltpu.VMEM((2, D_MODEL, D_EXPERT), F32),
            pltpu.VMEM((2, D_EXPERT, D_MODEL), F32),
            pltpu.VMEM((D_MODEL, D_EXPERT), BF16),
            pltpu.VMEM((D_MODEL, D_EXPERT), BF16),
            pltpu.VMEM((D_EXPERT, D_MODEL), BF16),
            pltpu.SemaphoreType.DMA((GATHER_SLOTS,)),
            pltpu.SemaphoreType.DMA((2,)),
            pltpu.SemaphoreType.DMA((2,)),
            pltpu.SemaphoreType.DMA((1,)),
        ],
    )
    return pl.pallas_call(
        functools.partial(_experts_kernel, n_tok),
        grid_spec=grid_spec,
        out_shape=jax.ShapeDtypeStruct((TOP_K * (n_tok + DUMMY_TOKENS) * ROW_CHUNKS, LANES), F32),
        compiler_params=_cparams(("arbitrary",)),
        name="experts",
    )(tile_expert, tile_start, tile_rows, tile_flags, next_expert, gather_rows, scatter_rows, h2_all,
      w_gate, w_up, w_down)


COMBINE_UNROLL = 4


def _combine_kernel(w_ref, y_ref, base_ref, g2_ref, o_ref, acc_rows):
    i = pl.program_id(0)
    tc = base_ref.shape[0]
    rc = ROW_CHUNKS

    def body(g, carry):
        for u in range(COMBINE_UNROLL):
            t = g * COMBINE_UNROLL + u
            rows = pl.ds(pl.multiple_of(t * rc, rc), rc)
            w0 = (i * tc + t) * TOP_K
            acc = w_ref[w0] * y_ref[0, rows, :]
            for k in range(1, TOP_K):
                acc = acc + w_ref[w0 + k] * y_ref[k, rows, :]
            acc_rows[rows, :] = acc
        return carry

    lax.fori_loop(0, tc // COMBINE_UNROLL, body, 0)
    g2 = g2_ref[0]
    for c in range(rc):
        sl = slice(c * LANES, (c + 1) * LANES)
        o_ref[:, sl] = base_ref[:, sl] + g2[:, sl] * acc_rows[pl.ds(c, tc, stride=rc), :]


def _combine(y_planes, wts, base, gate2, tc, rows_per_mod, block0):
    m = base.shape[0]
    r = gate2.shape[1]
    assert tc % COMBINE_UNROLL == 0
    grid_spec = pltpu.PrefetchScalarGridSpec(
        num_scalar_prefetch=1,
        grid=(m // tc,),
        in_specs=[
            pl.BlockSpec((TOP_K, tc * ROW_CHUNKS, LANES), lambda i, w: (0, i + block0, 0)),
            pl.BlockSpec((tc, D_MODEL), lambda i, w: (i, 0)),
            pl.BlockSpec((1, r, D_MODEL), lambda i, w: (i // rows_per_mod, 0, 0)),
        ],
        out_specs=pl.BlockSpec((tc, D_MODEL), lambda i, w: (i, 0)),
        scratch_shapes=[pltpu.VMEM((tc * ROW_CHUNKS, LANES), F32)],
    )
    return pl.pallas_call(
        _combine_kernel,
        grid_spec=grid_spec,
        out_shape=jax.ShapeDtypeStruct((m, D_MODEL), F32),
        compiler_params=_cparams(("parallel",)),
        name="combine",
    )(wts.reshape(-1), y_planes, base, gate2)


def _moe_routed(h2_all, logits_t, b_router, w_gate, w_up, w_down):
    n_tok = logits_t.shape[1]
    mp = -(-n_tok // ROUTE_TILE) * ROUTE_TILE
    logits_t = jnp.pad(logits_t, ((0, 0), (0, mp - n_tok)))
    eidx, wts, counts = _route(logits_t, b_router, n_tok)
    eidx, wts, counts = eidx[:, :n_tok], wts[:, :n_tok], counts[:, 0]

    tm = EXPERT_TILE
    n_pairs = n_tok * TOP_K
    n_tiles = n_pairs // tm + N_EXPERTS
    pair_code = jnp.arange(n_tok, dtype=I32)[None, :] * TOP_K + jnp.arange(TOP_K, dtype=I32)[:, None]
    (sorted_pairs,) = lax.sort(((eidx * n_pairs + pair_code).reshape(-1),), is_stable=False)
    sorted_pairs = jnp.concatenate([sorted_pairs % n_pairs, jnp.zeros((tm,), I32)])
    pair_tok, pair_choice = sorted_pairs >> PAIR_BITS, sorted_pairs & (TOP_K - 1)
    gather_rows = pair_tok * ROW_CHUNKS
    scatter_rows = (pair_choice * (n_tok + DUMMY_TOKENS) + pair_tok) * ROW_CHUNKS
    tiles_e = (counts + tm - 1) // tm
    tile_end = jnp.cumsum(tiles_e)
    tile_begin = tile_end - tiles_e
    dense_begin = jnp.cumsum(counts) - counts
    tj = jnp.arange(n_tiles, dtype=I32)
    total_tiles = tile_end[-1]
    used = tj < total_tiles
    tj_used = jnp.minimum(tj, total_tiles - 1)[:, None]
    member = jnp.logical_and(tile_begin[None, :] <= tj_used, tj_used < tile_end[None, :])
    of_tile = lambda per_expert: jnp.sum(jnp.where(member, per_expert[None, :], 0), axis=1)
    e_ids = jnp.arange(N_EXPERTS, dtype=I32)
    te = of_tile(e_ids)
    local = tj - of_tile(tile_begin)
    ts = jnp.where(used, of_tile(dense_begin) + local * tm, 0).astype(I32)
    tn = jnp.where(used, jnp.clip(of_tile(counts) - local * tm, 0, tm), 0).astype(I32)
    first = jnp.logical_and(used, local == 0)
    busy = counts > 0
    wslot = of_tile((jnp.cumsum(busy.astype(I32)) - 1) % 2)
    later = jnp.logical_and(e_ids[None, :] > e_ids[:, None], busy[None, :])
    succ = jnp.min(jnp.where(later, e_ids[None, :], N_EXPERTS), axis=1)
    succ = jnp.where(succ == N_EXPERTS, -1, succ)
    ne = jnp.where(first, of_tile(succ), -1).astype(I32)
    prev_used = jnp.concatenate([jnp.zeros((1,), jnp.bool_), used[:-1]])
    tf = (used.astype(I32) * TILE_USED + first.astype(I32) * TILE_FIRST + wslot * TILE_WSLOT
          + prev_used.astype(I32) * TILE_PREV_USED)
    y_all = _experts(te, ts, tn, tf, ne, gather_rows, scatter_rows, h2_all, w_gate, w_up, w_down, n_tok)
    return y_all.reshape(TOP_K, (n_tok + DUMMY_TOKENS) * ROW_CHUNKS, LANES), wts.T


def kernel(x_prompt, x_sample, cache_k_win, cache_v_win, state_pool, c_prompt, c_sample, w_ada, b_ada, g_mix, w_in, q_norm, k_norm, w_pool, pool_scale, w_out, g_ffn, w_router, b_router, w_gate, w_up, w_down, ws_gate, ws_up, ws_down):
    depth = w_ada.shape[0]
    assert depth == 1
    l = 0
    nb, seq, _ = x_prompt.shape
    bd, dec_seq, _ = x_sample.shape
    assert dec_seq == 1 and seq % (TOKEN_TILE * 16) == 0
    n_prompt = nb * seq
    n_tok = n_prompt + bd
    n_keep = min(MAX_WINDOW, seq)
    tiles_per_batch = seq // TOKEN_TILE

    n_mod_rows = -(-(nb + bd) // 8) * 8
    c_all = jnp.concatenate([c_prompt, c_sample, jnp.zeros((n_mod_rows - nb - bd, D_MODEL), F32)], axis=0)
    mods = _ada(c_all, w_ada[l], b_ada[l])
    mods_p = [mods[:nb, j * D_MODEL:(j + 1) * D_MODEL].reshape(nb, 1, D_MODEL) for j in range(6)]
    mods_s = [mods[nb:nb + bd, j * D_MODEL:(j + 1) * D_MODEL].reshape(1, bd, D_MODEL) for j in range(6)]

    g_mix_l = g_mix[l].reshape(1, D_MODEL)
    g_ffn_l = g_ffn[l].reshape(1, D_MODEL)
    qn = q_norm[l].reshape(1, HEAD_DIM)
    kn = k_norm[l].reshape(1, HEAD_DIM)
    w_in_bf = w_in[l].astype(BF16)
    w_out_bf = w_out[l].astype(BF16)
    w_pool_bf = w_pool[l].astype(BF16)
    ps = pool_scale[l].reshape(1, D_POOL)
    w_router_pad = jnp.pad(w_router[l], ((0, 0), (0, LANES - N_EXPERTS)))
    w_router_hi = w_router_pad.astype(BF16)
    w_router_lo = (w_router_pad - w_router_hi.astype(F32)).astype(BF16)
    wsg_bf, wsu_bf, wsd_bf = ws_gate[l].astype(BF16), ws_up[l].astype(BF16), ws_down[l].astype(BF16)

    rope_p = _rope_tables(jnp.arange(seq, dtype=I32))
    rope_s = _rope_tables(jnp.full((bd,), PAST_LEN, I32))
    xp2 = x_prompt.reshape(n_prompt, D_MODEL)
    xs2 = x_sample.reshape(bd, D_MODEL)
    assert n_keep % TOKEN_TILE == 0
    qp, kp, vp, up, kwin_p, vwin_p = _in_proj(xp2, mods_p[0], mods_p[1], g_mix_l, w_in_bf, qn, kn, rope_p,
                                              TOKEN_TILE, tiles_per_batch, tiles_per_batch, n_keep // TOKEN_TILE)
    qs, _, _, us, k_new, v_new = _in_proj(xs2, mods_s[0], mods_s[1], g_mix_l, w_in_bf, qn, kn, rope_s, bd, 1, 1, 1)

    to_seq = lambda t: t.reshape(nb, seq, -1)
    heads = lambda t: t.reshape(bd, N_HEADS, HEAD_DIM)
    ck, cv = cache_k_win[l], cache_v_win[l]
    k_new, v_new = heads(k_new), heads(v_new)
    attn_p, k_win_s, v_win_s = _attn_prompt(to_seq(qp), to_seq(kp), to_seq(vp), ck, cv, k_new, v_new)
    attn_p = attn_p.reshape(n_prompt, D_ATTN)
    pool_p = _pool_prompt(to_seq(up), w_pool_bf, ps).reshape(n_prompt, D_POOL)
    attn_s = _attn_decode(heads(qs), k_new, v_new, ck, cv).reshape(bd, D_ATTN)
    pool_s = _pool_decode(us, jnp.swapaxes(state_pool[l], 0, 1), w_pool_bf, ps)

    base_p, h2_all, lg_p = _out_proj(xp2, attn_p, pool_p, mods_p[2], mods_p[3], mods_p[4], mods_p[5],
                                     w_out_bf, g_ffn_l, w_router_hi, w_router_lo, wsg_bf, wsu_bf, wsd_bf,
                                     TOKEN_TILE, tiles_per_batch, n_tok, None, 0)
    base_s, h2_all, lg_s = _out_proj(xs2, attn_s, pool_s, mods_s[2], mods_s[3], mods_s[4], mods_s[5],
                                     w_out_bf, g_ffn_l, w_router_hi, w_router_lo, wsg_bf, wsu_bf, wsd_bf,
                                     bd, 1, n_tok, h2_all, n_prompt // bd)

    y_planes, wts_t = _moe_routed(h2_all, jnp.concatenate([lg_p, lg_s], axis=1), b_router[l],
                                  w_gate[l], w_up[l], w_down[l])

    yp = _combine(y_planes, wts_t[:n_prompt], base_p, mods_p[5], COMBINE_TILE, seq // COMBINE_TILE, 0)
    ys = _combine(y_planes, wts_t[n_prompt:n_tok], base_s, mods_s[5], bd, 1, n_prompt // bd)

    y_prompt = yp.reshape(nb, seq, D_MODEL)
    y_sample = ys.reshape(bd, dec_seq, D_MODEL)
    win = lambda t: t.reshape(1, nb, n_keep, N_HEADS, HEAD_DIM)
    pool_p_state = to_seq(up)[:, seq - POOL_STATE:][None]
    pool_s_state = jnp.concatenate([state_pool[l][:, 1:], us[:, None, :]], axis=1)[None]
    return (y_prompt, y_sample, win(kwin_p), win(vwin_p), pool_p_state,
            k_win_s[None], v_win_s[None], pool_s_state)
```

```python
import functools

import jax
import jax.numpy as jnp
from jax import lax
from jax.experimental import pallas as pl
from jax.experimental.pallas import tpu as pltpu

F32 = jnp.float32
BF16 = jnp.bfloat16
I32 = jnp.int32

D_MODEL = 2048
N_HEADS = 8
HEAD_DIM = 128
D_ATTN = N_HEADS * HEAD_DIM
D_POOL = D_MODEL - D_ATTN
D_IN = 3 * D_ATTN + D_POOL
ROPE_DIM = HEAD_DIM // 4
ROPE_HALF = ROPE_DIM // 2
ROPE_THETA = 500000.0
DILATED_GROUPS = ((128, 1), (512, 4), (2048, 16))
QBLOCK = 128
ATTN_SCALE = HEAD_DIM ** -0.5
POOL_WINDOWS = (2, 4, 8, 16)
POOL_GROUP_DIM = D_POOL // len(POOL_WINDOWS)
POOL_STATE = max(POOL_WINDOWS) - 1
POOL_HALO = 16
N_EXPERTS = 64
TOP_K = 8
N_EXPERT_GROUPS = 8
GROUP_SIZE = N_EXPERTS // N_EXPERT_GROUPS
TOPK_GROUPS = 4
D_EXPERT = 512
D_SHARED = 512
ROUTED_SCALE = 2.5
EPS = 1e-6
PAST_LEN = 16384
MAX_WINDOW = 2048

LANES = 128
ROW_CHUNKS = D_MODEL // LANES
NEG_BIG = -1e30
LOG2_E = 1.4426950408889634

TOKEN_TILE = 256
EXPERT_TILE = 256
COMBINE_TILE = 128
ROUTE_TILE = 256
ADA_COL_TILE = 1024
POOL_TILE = 512
VMEM_LIMIT = 60 * 1024 * 1024


def _cparams(sem, vmem=VMEM_LIMIT):
    return pltpu.CompilerParams(dimension_semantics=sem, vmem_limit_bytes=vmem)


def _ada_kernel(c_ref, w_ref, b_ref, o_ref):
    c = c_ref[...]
    s = (c * jax.nn.sigmoid(c)).astype(BF16)
    o_ref[...] = jnp.dot(s, w_ref[...].astype(BF16), preferred_element_type=F32) + b_ref[...]


def _ada(c_all, w_ada, b_ada):
    rows = c_all.shape[0]
    n = w_ada.shape[1]
    tn = ADA_COL_TILE
    return pl.pallas_call(
        _ada_kernel,
        grid=(n // tn,),
        in_specs=[
            pl.BlockSpec((rows, D_MODEL), lambda j: (0, 0)),
            pl.BlockSpec((D_MODEL, tn), lambda j: (0, j)),
            pl.BlockSpec((1, tn), lambda j: (0, j)),
        ],
        out_specs=pl.BlockSpec((rows, tn), lambda j: (0, j)),
        out_shape=jax.ShapeDtypeStruct((rows, n), F32),
        compiler_params=_cparams(("parallel",)),
        name="ada_modulation",
    )(c_all, w_ada, b_ada.reshape(1, n))


def _rms(x):
    return x * lax.rsqrt(jnp.mean(x * x, axis=-1, keepdims=True) + EPS)


def _in_kernel(x_ref, sh_ref, sc_ref, g_ref, w_ref, qn_ref, kn_ref, c_ref, s1_ref, s2_ref,
               q_ref, k_ref, v_ref, u_ref, kw_ref, vw_ref):
    tm = x_ref.shape[0]
    x = x_ref[...]
    h = _rms(x) * g_ref[...] * (1.0 + sc_ref[0]) + sh_ref[0]
    hb = h.astype(BF16)
    cos = c_ref[...]
    s1 = s1_ref[...]
    s2 = s2_ref[...]

    def head_rows(win_ref, hd, val):
        win_ref[pl.ds(hd, tm, stride=N_HEADS), :] = val

    def qk(sec, nrm, out_ref, win_ref):
        z = jnp.dot(hb, w_ref[:, sec * D_ATTN:(sec + 1) * D_ATTN], preferred_element_type=F32)
        for hd in range(N_HEADS):
            sl = slice(hd * HEAD_DIM, (hd + 1) * HEAD_DIM)
            r = _rms(z[:, sl]) * nrm
            r = r * cos + pltpu.roll(r, HEAD_DIM - ROPE_HALF, 1) * s1 + pltpu.roll(r, ROPE_HALF, 1) * s2
            out_ref[:, sl] = r
            if win_ref is not None:
                head_rows(win_ref, hd, r)

    qk(0, qn_ref[...], q_ref, None)
    qk(1, kn_ref[...], k_ref, kw_ref)
    v = jnp.dot(hb, w_ref[:, 2 * D_ATTN:3 * D_ATTN], preferred_element_type=F32)
    v_ref[...] = v
    for hd in range(N_HEADS):
        head_rows(vw_ref, hd, v[:, hd * HEAD_DIM:(hd + 1) * HEAD_DIM])
    u_ref[...] = jnp.dot(hb, w_ref[:, 3 * D_ATTN:], preferred_element_type=F32)


def _in_proj(x2d, shift, scale, g_mix, w_in_bf, q_norm, k_norm, rope, tm, rows_per_mod, rope_tiles, keep_tiles):
    m = x2d.shape[0]
    r = shift.shape[1]
    n_groups = m // tm // rows_per_mod
    skip = rows_per_mod - keep_tiles
    mod_spec = pl.BlockSpec((1, r, D_MODEL), lambda i: (i // rows_per_mod, 0, 0))
    rope_spec = pl.BlockSpec((tm, HEAD_DIM), lambda i: (i % rope_tiles, 0))
    out_spec = pl.BlockSpec((tm, D_ATTN), lambda i: (i, 0))
    out_sd = jax.ShapeDtypeStruct((m, D_ATTN), F32)
    win_spec = pl.BlockSpec(
        (tm * N_HEADS, HEAD_DIM),
        lambda i: ((i // rows_per_mod) * keep_tiles + jnp.maximum(i % rows_per_mod - skip, 0), 0))
    win_sd = jax.ShapeDtypeStruct((n_groups * keep_tiles * tm * N_HEADS, HEAD_DIM), F32)
    return pl.pallas_call(
        _in_kernel,
        grid=(m // tm,),
        in_specs=[
            pl.BlockSpec((tm, D_MODEL), lambda i: (i, 0)),
            mod_spec, mod_spec,
            pl.BlockSpec((1, D_MODEL), lambda i: (0, 0)),
            pl.BlockSpec((D_MODEL, D_IN), lambda i: (0, 0)),
            pl.BlockSpec((1, HEAD_DIM), lambda i: (0, 0)),
            pl.BlockSpec((1, HEAD_DIM), lambda i: (0, 0)),
            rope_spec, rope_spec, rope_spec,
        ],
        out_specs=[out_spec] * 4 + [win_spec] * 2,
        out_shape=[out_sd] * 4 + [win_sd] * 2,
        compiler_params=_cparams(("arbitrary",)),
        name="in_proj",
    )(x2d, shift, scale, g_mix, w_in_bf, q_norm, k_norm, *rope)


def _rope_tables(pos):
    inv_freq = jnp.float32(ROPE_THETA) ** (-jnp.arange(ROPE_HALF, dtype=F32) / ROPE_HALF)
    ang = pos.astype(F32)[:, None] * inv_freq[None, :]
    cos, sin = jnp.cos(ang), jnp.sin(ang)
    t = pos.shape[0]
    rest = HEAD_DIM - ROPE_DIM
    c = jnp.concatenate([cos, cos, jnp.ones((t, rest), F32)], axis=1)
    s1 = jnp.concatenate([-sin, jnp.zeros((t, ROPE_HALF + rest), F32)], axis=1)
    s2 = jnp.concatenate([jnp.zeros((t, ROPE_HALF), F32), sin, jnp.zeros((t, rest), F32)], axis=1)
    return c, s1, s2


ATTN_INTERLEAVE = 4
WINDOW_ROWS_PER_STEP = 2
REGROUP_MIN_DILATION = 16


REGROUP_STRIDE = 4


def _to_residue_major(read_rows, dst, tmp, seq, sub_len):
    n, part = REGROUP_STRIDE, seq // REGROUP_STRIDE
    for a in range(n):
        tmp[pl.ds(a * part, part), :] = read_rows(pl.ds(a, part, stride=n))
    for a in range(n):
        for s in range(n):
            dst[pl.ds((s * n + a) * sub_len, sub_len), :] = tmp[pl.ds(a * part + s, sub_len, stride=n), :]


def _from_residue_major(src, dst, tmp, seq, sub_len):
    n, part = REGROUP_STRIDE, seq // REGROUP_STRIDE
    for a in range(n):
        for s in range(n):
            tmp[pl.ds(a * part + s, sub_len, stride=n), :] = src[pl.ds((s * n + a) * sub_len, sub_len), :]
    for a in range(n):
        dst[pl.ds(a, part, stride=n), :] = tmp[pl.ds(a * part, part), :]


def _attn_prompt_kernel(q_ref, k_ref, v_ref, ck_hbm, cv_hbm, kn_ref, vn_ref, o_ref, ok_hbm, ov_hbm,
                        m_scr, l_scr, acc_scr, q_rm, k_rm, v_rm, m_rm, l_rm, a_rm, wbuf, isem, osem):
    seq = q_ref.shape[1]
    half = wbuf.shape[2]
    step = pl.program_id(0) * pl.num_programs(1) + pl.program_id(1)
    last_step = pl.num_programs(0) * pl.num_programs(1) - 1
    windows = ((ck_hbm, kn_ref, ok_hbm), (cv_hbm, vn_ref, ov_hbm))
    row0 = step * WINDOW_ROWS_PER_STEP
    n_bounds = 2 * WINDOW_ROWS_PER_STEP
    assert n_bounds == len(DILATED_GROUPS) + 1

    def chunk_in(row, h, a):
        n_rows = half - h
        return pltpu.make_async_copy(windows[a][0].at[row, pl.ds(h * half + 1, n_rows)],
                                     wbuf.at[h, a, pl.ds(0, n_rows)], isem.at[h, a])

    def chunk_out(row, h, a):
        return pltpu.make_async_copy(wbuf.at[h, a], windows[a][2].at[row, pl.ds(h * half, half)], osem.at[h, a])

    def finish_in_start_out(row, h):
        for a in range(2):
            chunk_in(row, h, a).wait()
            if h == 1:
                wbuf[h, a, half - 1] = windows[a][1][row]
            chunk_out(row, h, a).start()

    def finish_out(row, h):
        for a in range(2):
            chunk_out(row, h, a).wait()

    def window_boundary(b):
        h = b % 2
        if b == 0:
            @pl.when(step > 0)
            def _():
                finish_in_start_out(row0 - 1, 1)
                finish_out(row0 - 1, 0)
        elif b == 1:
            finish_in_start_out(row0, 0)
            pl.when(step > 0)(lambda: finish_out(row0 - 1, 1))
        else:
            finish_in_start_out(row0 + (b - 1) // 2, (b - 1) % 2)
            finish_out(row0 + (b - 2) // 2, h)
        for a in range(2):
            chunk_in(row0 + b // 2, h, a).start()

    window_phases = [functools.partial(window_boundary, b) for b in range(n_bounds)]
    window_phases[0]()
    m_scr[...] = jnp.full(m_scr.shape, NEG_BIG, F32)
    l_scr[...] = jnp.zeros(l_scr.shape, F32)
    acc_scr[...] = jnp.zeros(acc_scr.shape, F32)
    qi = lax.broadcasted_iota(I32, (QBLOCK, 2 * QBLOCK), 0)
    kj = lax.broadcasted_iota(I32, (QBLOCK, 2 * QBLOCK), 1)
    dist = qi + QBLOCK - kj
    band_mask = jnp.logical_and(dist >= 0, dist <= QBLOCK)
    bias_band = jnp.where(band_mask, 0.0, NEG_BIG)
    bias_first = jnp.where(jnp.logical_and(band_mask, kj >= QBLOCK), 0.0, NEG_BIG)
    ones = jnp.ones((2 * QBLOCK, HEAD_DIM), BF16)
    dims = (((1,), (1,)), ((), ()))
    q_scale = ATTN_SCALE * LOG2_E

    regrouped = None
    for window, dil in DILATED_GROUPS:
        assert window // dil == QBLOCK
        n_units = seq // QBLOCK
        assert n_units % ATTN_INTERLEAVE == 0 and (dil % ATTN_INTERLEAVE == 0 or dil == 1)

        def rows(start, dil=dil):
            return pl.ds(start, QBLOCK) if dil == 1 else pl.ds(start, QBLOCK, stride=dil)

        sub_len = seq // dil
        regroup = dil >= REGROUP_MIN_DILATION
        if regroup:
            assert dil == REGROUP_STRIDE ** 2 and regrouped is None
            regrouped = (dil, sub_len)
            for src, dst in ((q_ref, q_rm), (k_ref, k_rm), (v_ref, v_rm)):
                _to_residue_major(lambda rows_, src=src: src[0, rows_, :], dst, m_rm, seq, sub_len)
            m_rm[...] = jnp.full(m_rm.shape, NEG_BIG, F32)
            l_rm[...] = jnp.zeros(l_rm.shape, F32)
            a_rm[...] = jnp.zeros(a_rm.shape, F32)

        def qkv_blocks(blk, res, start, prev, rows=rows, regroup=regroup, sub_len=sub_len):
            if regroup:
                cur = pl.ds(res * sub_len + blk * QBLOCK, QBLOCK)
                prv = pl.ds(res * sub_len + jnp.maximum(blk - 1, 0) * QBLOCK, QBLOCK)
                return (q_rm[cur, :], k_rm[prv, :], k_rm[cur, :], v_rm[prv, :], v_rm[cur, :])
            return (q_ref[0, rows(start), :], k_ref[0, rows(prev), :], k_ref[0, rows(start), :],
                    v_ref[0, rows(prev), :], v_ref[0, rows(start), :])

        state_refs = (m_rm, l_rm, a_rm) if regroup else (m_scr, l_scr, acc_scr)

        def state_rows(blk, res, start, rows=rows, regroup=regroup, sub_len=sub_len):
            return pl.ds(res * sub_len + blk * QBLOCK, QBLOCK) if regroup else rows(start)

        def body(it, carry, dil=dil, rows=rows, qkv_blocks=qkv_blocks, state_refs=state_refs, state_rows=state_rows):
            m_st, l_st, a_st = state_refs
            loaded = []
            for u in range(ATTN_INTERLEAVE):
                f = it * ATTN_INTERLEAVE + u
                blk, res = f // dil, f % dil
                start = blk * (QBLOCK * dil) + res
                prev = jnp.maximum(start - QBLOCK * dil, res)
                q_blk, k_prev, k_cur, v_prev, v_cur = qkv_blocks(blk, res, start, prev)
                q = (q_blk * q_scale).astype(BF16)
                k = jnp.concatenate([k_prev, k_cur], axis=0).astype(BF16)
                v = jnp.concatenate([v_prev, v_cur], axis=0).astype(BF16)
                v_ones = jnp.concatenate([v, ones], axis=1)
                st = state_rows(blk, res, start)
                state = (m_st[st, :], l_st[st, :], a_st[st, :])
                loaded.append((st, blk, q, k, v_ones, state))
            scores = [lax.dot_general(q, k, dims, preferred_element_type=F32)
                      + jnp.where(blk > 0, bias_band, bias_first)
                      for _, blk, q, k, _, _ in loaded]
            probs = []
            for s, (_, _, _, _, _, (m_old, _, _)) in zip(scores, loaded):
                m_new = jnp.maximum(m_old, jnp.max(s, axis=1, keepdims=True))
                p = jnp.exp2(s - jnp.concatenate([m_new, m_new], axis=1)).astype(BF16)
                probs.append((m_new, jnp.exp2(m_old - m_new), p))
            updated = []
            for (m_new, alpha, p), (start, _, _, _, v_ones, (_, l_old, a_old)) in zip(probs, loaded):
                pv = jnp.dot(p, v_ones, preferred_element_type=F32)
                updated.append((start, m_new, alpha * l_old + pv[:, HEAD_DIM:], alpha * a_old + pv[:, :HEAD_DIM]))
            for st, m_new, l_new, a_new in updated:
                m_st[st, :] = m_new
                l_st[st, :] = l_new
                a_st[st, :] = a_new
            return carry

        trips = n_units // ATTN_INTERLEAVE
        phase = window_phases[DILATED_GROUPS.index((window, dil)) + 1]
        if (window, dil) == DILATED_GROUPS[-1]:
            lax.fori_loop(0, trips // 2, body, 0)
            phase()
            lax.fori_loop(trips // 2, trips, body, 0)
        else:
            lax.fori_loop(0, trips, body, 0)
            phase()

    @pl.when(step == last_step)
    def _():
        last_row = row0 + WINDOW_ROWS_PER_STEP - 1
        finish_in_start_out(last_row, 1)
        finish_out(last_row, 0)
        finish_out(last_row, 1)

    if regrouped is None:
        o_ref[0] = acc_scr[...] / l_scr[...]
    else:
        _, sub_len = regrouped
        _from_residue_major(a_rm, q_rm, k_rm, seq, sub_len)
        _from_residue_major(l_rm, k_rm, a_rm, seq, sub_len)
        _from_residue_major(m_rm, v_rm, a_rm, seq, sub_len)
        m_a, m_b = m_scr[...], v_rm[...]
        m_all = jnp.maximum(m_a, m_b)
        w_a, w_b = jnp.exp2(m_a - m_all), jnp.exp2(m_b - m_all)
        o_ref[0] = (acc_scr[...] * w_a + q_rm[...] * w_b) / (l_scr[...] * w_a + k_rm[...] * w_b)


def _attn_prompt(q, k, v, cache_k, cache_v, k_new, v_new):
    b, s, _ = q.shape
    bd, n_buf = cache_k.shape[0], cache_k.shape[1]
    assert bd == b * N_HEADS * WINDOW_ROWS_PER_STEP
    spec = pl.BlockSpec((1, s, HEAD_DIM), lambda bi, hi: (bi, 0, hi))
    any_spec = pl.BlockSpec(memory_space=pl.ANY)
    new_spec = pl.BlockSpec(k_new.shape, lambda bi, hi: (0, 0, 0))
    win_sd = jax.ShapeDtypeStruct(cache_k.shape, cache_k.dtype)
    return pl.pallas_call(
        _attn_prompt_kernel,
        grid=(b, N_HEADS),
        in_specs=[spec, spec, spec, any_spec, any_spec, new_spec, new_spec],
        out_specs=[spec, any_spec, any_spec],
        out_shape=[jax.ShapeDtypeStruct((b, s, D_ATTN), F32), win_sd, win_sd],
        scratch_shapes=[pltpu.VMEM((s, HEAD_DIM), F32)] * 9 + [
            pltpu.VMEM((2, 2, n_buf // 2) + cache_k.shape[2:], F32),
            pltpu.SemaphoreType.DMA((2, 2)),
            pltpu.SemaphoreType.DMA((2, 2)),
        ],
        compiler_params=_cparams(("arbitrary", "arbitrary")),
        name="attn_prompt",
    )(q, k, v, cache_k, cache_v, k_new, v_new)


def _attn_decode_kernel(q_ref, kn_ref, vn_ref, k1_ref, k2_ref, k3_ref, v1_ref, v2_ref, v3_ref, o_ref):
    q = q_ref[0]
    k_new = kn_ref[0]
    v_new = vn_ref[0]
    n_grp = len(DILATED_GROUPS)

    def scores(kb):
        return jnp.sum(kb * q[None], axis=-1, keepdims=True) * ATTN_SCALE

    kcs = [k1_ref[0], k2_ref[0][:, 0], k3_ref[0][:, 0]]
    vcs = [v1_ref[0], v2_ref[0][:, 0], v3_ref[0][:, 0]]
    s_new = jnp.sum(k_new * q, axis=-1, keepdims=True) * ATTN_SCALE
    s_grp = [scores(kb) for kb in kcs]
    m = s_new
    for s in s_grp:
        m = jnp.maximum(m, jnp.max(s, axis=0))
    p_new = jnp.exp(s_new - m)
    den = n_grp * p_new
    num = n_grp * p_new * v_new
    for s, vb in zip(s_grp, vcs):
        p = jnp.exp(s - m[None])
        den = den + jnp.sum(p, axis=0)
        num = num + jnp.sum(p * vb, axis=0)
    o_ref[0] = num / den


def _attn_decode(q, k_new, v_new, cache_k, cache_v):
    bd, n_buf = cache_k.shape[0], cache_k.shape[1]
    assert n_buf == MAX_WINDOW

    def views(cache):
        out, specs = [], []
        for window, dil in DILATED_GROUPS:
            band = window // dil
            assert band == QBLOCK and n_buf % dil == 0 and n_buf // dil >= band
            if dil == 1:
                out.append(cache)
                specs.append(pl.BlockSpec((1, band, N_HEADS, HEAD_DIM),
                                          lambda b, nb=n_buf // band: (b, nb - 1, 0, 0)))
            else:
                out.append(cache.reshape(bd, n_buf // dil, dil, N_HEADS, HEAD_DIM))
                specs.append(pl.BlockSpec((1, band, 1, N_HEADS, HEAD_DIM),
                                          lambda b, nb=n_buf // dil // band: (b, nb - 1, 0, 0, 0)))
        return out, specs

    kv, kspecs = views(cache_k)
    vv, vspecs = views(cache_v)
    tok = pl.BlockSpec((1, N_HEADS, HEAD_DIM), lambda b: (b, 0, 0))
    return pl.pallas_call(
        _attn_decode_kernel,
        grid=(bd,),
        in_specs=[tok, tok, tok] + kspecs + vspecs,
        out_specs=tok,
        out_shape=jax.ShapeDtypeStruct((bd, N_HEADS, HEAD_DIM), F32),
        compiler_params=_cparams(("parallel",)),
        name="attn_decode",
    )(q, k_new, v_new, *kv, *vv)


def _pool_project(d_groups, wp_ref, ps_ref, out_ref):
    for g, d in enumerate(d_groups):
        sl = slice(g * POOL_GROUP_DIM, (g + 1) * POOL_GROUP_DIM)
        out_ref[:, sl] = jnp.dot(d.astype(BF16), wp_ref[g], preferred_element_type=F32) * ps_ref[:, sl]


def _pool_prompt_kernel(u_ref, prev_ref, wp_ref, ps_ref, o_ref):
    i = pl.program_id(1)
    tp = u_ref.shape[1]
    u = u_ref[0]
    prev = jnp.where(i > 0, prev_ref[0], 0.0)
    ext = jnp.concatenate([prev, u], axis=0)
    pos = i * tp + lax.broadcasted_iota(I32, (tp, 1), 0)
    d_groups = []
    for g, w in enumerate(POOL_WINDOWS):
        sl = slice(g * POOL_GROUP_DIM, (g + 1) * POOL_GROUP_DIM)
        a = ext[:, sl]
        span = 1
        while span < w:
            a = a + pltpu.roll(a, span, 0)
            span *= 2
        win = a[POOL_HALO:, :]
        cnt = jnp.minimum(w, pos + 1).astype(F32)
        d_groups.append(win / cnt - u[:, sl])
    _pool_project(d_groups, wp_ref, ps_ref, o_ref.at[0])


def _pool_prompt(u, w_pool_bf, pool_scale):
    b, s, _ = u.shape
    tp = POOL_TILE
    halo_blocks = tp // POOL_HALO
    return pl.pallas_call(
        _pool_prompt_kernel,
        grid=(b, s // tp),
        in_specs=[
            pl.BlockSpec((1, tp, D_POOL), lambda bi, i: (bi, i, 0)),
            pl.BlockSpec((1, POOL_HALO, D_POOL), lambda bi, i: (bi, jnp.maximum(i * halo_blocks - 1, 0), 0)),
            pl.BlockSpec(w_pool_bf.shape, lambda bi, i: (0, 0, 0)),
            pl.BlockSpec((1, D_POOL), lambda bi, i: (0, 0)),
        ],
        out_specs=pl.BlockSpec((1, tp, D_POOL), lambda bi, i: (bi, i, 0)),
        out_shape=jax.ShapeDtypeStruct((b, s, D_POOL), F32),
        compiler_params=_cparams(("parallel", "parallel")),
        name="pool_prompt",
    )(u, u, w_pool_bf, pool_scale)


def _pool_decode_kernel(u_ref, st_ref, wp_ref, ps_ref, o_ref):
    u = u_ref[...]
    d_groups = []
    for g, w in enumerate(POOL_WINDOWS):
        sl = slice(g * POOL_GROUP_DIM, (g + 1) * POOL_GROUP_DIM)
        win = u[:, sl]
        for j in range(1, w):
            win = win + st_ref[POOL_STATE - j][:, sl]
        d_groups.append(win / float(w) - u[:, sl])
    _pool_project(d_groups, wp_ref, ps_ref, o_ref)


def _pool_decode(u, state_t, w_pool_bf, pool_scale):
    assert PAST_LEN + 1 >= max(POOL_WINDOWS)
    return pl.pallas_call(
        _pool_decode_kernel,
        out_shape=jax.ShapeDtypeStruct(u.shape, F32),
        compiler_params=_cparams(None),
        name="pool_decode",
    )(u, state_t, w_pool_bf, pool_scale)


def _split_bf16(x):
    hi = x.astype(BF16)
    lo = (x - hi.astype(F32)).astype(BF16)
    return hi, lo


def _out_kernel(x_ref, a_ref, p_ref, g1_ref, sh_ref, sc_ref, g2_ref, wo_ref, gf_ref, wrh_ref, wrl_ref,
                wsg_ref, wsu_ref, wsd_ref, base_ref, h2_ref, lg_ref):
    tm = x_ref.shape[0]
    mix = (jnp.dot(a_ref[...].astype(BF16), wo_ref[:D_ATTN, :], preferred_element_type=F32)
           + jnp.dot(p_ref[...].astype(BF16), wo_ref[D_ATTN:, :], preferred_element_type=F32))
    x1 = x_ref[...] + g1_ref[0] * mix
    h2 = _rms(x1) * gf_ref[...] * (1.0 + sc_ref[0]) + sh_ref[0]
    for c in range(ROW_CHUNKS):
        h2_ref[pl.ds(c, tm, stride=ROW_CHUNKS), :] = h2[:, c * LANES:(c + 1) * LANES]
    h_hi, h_lo = _split_bf16(h2)
    w_hi, w_lo = wrh_ref[...], wrl_ref[...]
    lg = (jnp.dot(h_hi, w_hi, preferred_element_type=F32) + jnp.dot(h_lo, w_hi, preferred_element_type=F32)
          + jnp.dot(h_hi, w_lo, preferred_element_type=F32))
    if tm < LANES:
        lg = jnp.concatenate([lg, jnp.zeros((LANES - tm, LANES), F32)], axis=0)
    lg_ref[...] = lg.T[:N_EXPERTS, :tm]
    sg = jnp.dot(h_hi, wsg_ref[...], preferred_element_type=F32)
    su = jnp.dot(h_hi, wsu_ref[...], preferred_element_type=F32)
    hs = (sg * jax.nn.sigmoid(sg) * su).astype(BF16)
    base_ref[...] = x1 + g2_ref[0] * jnp.dot(hs, wsd_ref[...], preferred_element_type=F32)


def _out_proj(x2d, attn, pool, gate1, shift2, scale2, gate2, w_out_bf, g_ffn, w_router_hi, w_router_lo,
              ws_gate_bf, ws_up_bf, ws_down_bf, tm, rows_per_mod, h2_tokens, h2_all, h2_block0):
    m = x2d.shape[0]
    r = gate1.shape[1]
    n_main = m // tm
    h2_rows = h2_tokens * ROW_CHUNKS
    n_fill = 0 if h2_all is not None else pl.cdiv(h2_rows - m * ROW_CHUNKS, tm * ROW_CHUNKS)
    last = n_main - 1
    row = lambda i: jnp.minimum(i, last)
    mod_spec = pl.BlockSpec((1, r, D_MODEL), lambda i: (row(i) // rows_per_mod, 0, 0))
    const = lambda shape: pl.BlockSpec(shape, lambda i: (0,) * len(shape))
    in_specs = [
        pl.BlockSpec((tm, D_MODEL), lambda i: (row(i), 0)),
        pl.BlockSpec((tm, D_ATTN), lambda i: (row(i), 0)),
        pl.BlockSpec((tm, D_POOL), lambda i: (row(i), 0)),
        mod_spec, mod_spec, mod_spec, mod_spec,
        const((D_MODEL, D_MODEL)),
        const((1, D_MODEL)),
        const((D_MODEL, LANES)), const((D_MODEL, LANES)),
        const((D_MODEL, D_SHARED)), const((D_MODEL, D_SHARED)), const((D_SHARED, D_MODEL)),
    ]
    args = [x2d, attn, pool, gate1, shift2, scale2, gate2, w_out_bf, g_ffn, w_router_hi, w_router_lo,
            ws_gate_bf, ws_up_bf, ws_down_bf]
    aliases = {}
    n_in = len(args)
    if h2_all is not None:
        in_specs.append(pl.BlockSpec(memory_space=pl.ANY))
        args.append(h2_all)
        aliases = {n_in: 1}

    def kernel(*refs):
        refs = refs[:n_in] + refs[len(args):]
        if n_fill == 0:
            _out_kernel(*refs)
            return
        step = pl.program_id(0)
        pl.when(step < n_main)(lambda: _out_kernel(*refs))

        @pl.when(step >= n_main)
        def _():
            h2_ref = refs[n_in + 1]
            h2_ref[...] = jnp.zeros(h2_ref.shape, F32)

    return pl.pallas_call(
        kernel,
        grid=(n_main + n_fill,),
        in_specs=in_specs,
        out_specs=[
            pl.BlockSpec((tm, D_MODEL), lambda i: (row(i), 0)),
            pl.BlockSpec((tm * ROW_CHUNKS, LANES), lambda i: (i + h2_block0, 0)),
            pl.BlockSpec((N_EXPERTS, tm), lambda i: (0, row(i))),
        ],
        out_shape=[
            jax.ShapeDtypeStruct((m, D_MODEL), F32),
            jax.ShapeDtypeStruct((h2_rows, LANES), F32),
            jax.ShapeDtypeStruct((N_EXPERTS, m), F32),
        ],
        input_output_aliases=aliases,
        compiler_params=_cparams(("arbitrary",)),
        name="out_proj",
    )(*args)


def _first_index_of_max(v, iota, size, axis):
    m = jnp.max(v, axis=axis, keepdims=True)
    idx = jnp.min(jnp.where(v == m, iota, size), axis=axis, keepdims=True)
    return m, idx


def _route_kernel(n_valid, lg_ref, b_ref, eidx_ref, wts_ref, cnt_ref, carry):
    i = pl.program_id(0)
    tn = lg_ref.shape[1]

    @pl.when(i == 0)
    def _():
        carry[...] = jnp.zeros(carry.shape, F32)

    scores = jax.nn.sigmoid(lg_ref[...])
    biased = scores + b_ref[...]
    grp = biased.reshape(N_EXPERT_GROUPS, GROUP_SIZE, tn)
    io_g = lax.broadcasted_iota(I32, grp.shape, 1)
    m1, i1 = _first_index_of_max(grp, io_g, GROUP_SIZE, 1)
    m2 = jnp.max(jnp.where(io_g == i1, -jnp.inf, grp), axis=1, keepdims=True)
    gscore = (m1 + m2)[:, 0, :]
    io_n = lax.broadcasted_iota(I32, gscore.shape, 0)
    gsel = jnp.zeros(gscore.shape, jnp.bool_)
    for _ in range(TOPK_GROUPS):
        _, gi = _first_index_of_max(gscore, io_n, N_EXPERT_GROUPS, 0)
        hit = io_n == gi
        gsel = jnp.logical_or(gsel, hit)
        gscore = jnp.where(hit, -jnp.inf, gscore)
    emask = jnp.broadcast_to(gsel[:, None, :], grp.shape).reshape(N_EXPERTS, tn)
    cand = jnp.where(emask, biased, -jnp.inf)
    io_e = lax.broadcasted_iota(I32, cand.shape, 0)
    picked = jnp.zeros(cand.shape, jnp.bool_)
    eidx, sel = [], []
    for _ in range(TOP_K):
        _, ei = _first_index_of_max(cand, io_e, N_EXPERTS, 0)
        hit = io_e == ei
        eidx.append(ei)
        sel.append(jnp.sum(jnp.where(hit, scores, 0.0), axis=0, keepdims=True))
        picked = jnp.logical_or(picked, hit)
        cand = jnp.where(hit, -jnp.inf, cand)
    sel = jnp.concatenate(sel, axis=0)
    eidx = jnp.concatenate(eidx, axis=0)
    wts_ref[...] = sel / jnp.sum(sel, axis=0, keepdims=True) * ROUTED_SCALE
    eidx_ref[...] = eidx

    tok = i * tn + lax.broadcasted_iota(I32, cand.shape, 1)
    mask = jnp.logical_and(picked, tok < n_valid).astype(F32)
    total = carry[...] + jnp.sum(mask, axis=1, keepdims=True)
    carry[...] = total
    cnt_ref[...] = total.astype(I32)


def _route(logits_t, b_router, n_valid):
    mp = logits_t.shape[1]
    tn = ROUTE_TILE
    tok_spec = pl.BlockSpec((TOP_K, tn), lambda i: (0, i))
    return pl.pallas_call(
        functools.partial(_route_kernel, n_valid),
        grid=(mp // tn,),
        in_specs=[
            pl.BlockSpec((N_EXPERTS, tn), lambda i: (0, i)),
            pl.BlockSpec((N_EXPERTS, 1), lambda i: (0, 0)),
        ],
        out_specs=[tok_spec, tok_spec, pl.BlockSpec((N_EXPERTS, LANES), lambda i: (0, 0))],
        out_shape=[
            jax.ShapeDtypeStruct((TOP_K, mp), I32),
            jax.ShapeDtypeStruct((TOP_K, mp), F32),
            jax.ShapeDtypeStruct((N_EXPERTS, LANES), I32),
        ],
        scratch_shapes=[pltpu.VMEM((N_EXPERTS, LANES), F32)],
        compiler_params=_cparams(("arbitrary",)),
        name="route",
    )(logits_t, b_router.reshape(N_EXPERTS, 1))


TILE_USED = 1
TILE_FIRST = 2
TILE_WSLOT = 4
TILE_PREV_USED = 8
PAIR_BITS = 3
assert 1 << PAIR_BITS == TOP_K
DUMMY_TOKENS = 2 * EXPERT_TILE // TOP_K
GATHER_SLOTS = 3


def _experts_kernel(n_tok, te_ref, ts_ref, tn_ref, tf_ref, ne_ref, g_ref, s_ref,
                    h_hbm, wg_hbm, wu_hbm, wd_hbm, y_hbm,
                    xbuf, ybuf, xs, wg_f, wu_f, wd_f, wg_bf, wu_bf, wd_bf, gsem, ssem, wsem, zsem):
    j = pl.program_id(0)
    last = pl.num_programs(0) - 1
    tm = EXPERT_TILE
    rc = ROW_CHUNKS
    plane_tokens = n_tok + DUMMY_TOKENS
    slot = j % 2
    other = 1 - slot
    gslot = j % GATHER_SLOTS
    flags = tf_ref[j]
    used = (flags & TILE_USED) != 0
    first = (flags & TILE_FIRST) != 0
    prev_used = (flags & TILE_PREV_USED) != 0
    wslot = (flags // TILE_WSLOT) & 1
    prv = jnp.maximum(j - 1, 0)
    ahead = jnp.minimum(j + GATHER_SLOTS - 1, last)

    def buf_rows(i):
        return pl.ds(i * rc, rc) if isinstance(i, int) else pl.ds(pl.multiple_of(i * rc, rc), rc)

    def gather_row(i, start, dst):
        row = g_ref[start + i]
        pltpu.make_async_copy(h_hbm.at[pl.ds(pl.multiple_of(row, rc), rc), :],
                              xbuf.at[dst, buf_rows(i), :], gsem.at[dst]).start(priority=0)

    def scatter_row(i, start, n_valid, src):
        spare = ((i & (TOP_K - 1)) * plane_tokens + n_tok + (i >> PAIR_BITS)) * rc + src * ((tm >> PAIR_BITS) * rc)
        row = jnp.where(i < n_valid, s_ref[start + i], spare)
        pltpu.make_async_copy(ybuf.at[src, buf_rows(i), :],
                              y_hbm.at[pl.ds(pl.multiple_of(row, rc), rc), :],
                              ssem.at[src]).start(priority=i % 2 if isinstance(i, int) else 0)

    def scatter_tile_loop(tile, src):
        start, n_valid = ts_ref[tile], tn_ref[tile]
        lax.fori_loop(0, tm, lambda i, c: (scatter_row(i, start, n_valid, src), c)[1], 0)

    def wait_gather(dst):
        pltpu.make_async_copy(h_hbm.at[pl.ds(0, tm * rc), :], xbuf.at[dst], gsem.at[dst]).wait()

    def wait_scatter(src):
        pltpu.make_async_copy(ybuf.at[src], y_hbm.at[pl.ds(0, tm * rc), :], ssem.at[src]).wait()

    def weight_copies(e, ws):
        return [pltpu.make_async_copy(src.at[e], dst.at[ws], wsem.at[ws])
                for src, dst in ((wg_hbm, wg_f), (wu_hbm, wu_f), (wd_hbm, wd_f))]

    @pl.when(j == 0)
    def _():
        ybuf[1] = jnp.zeros(ybuf.shape[1:], ybuf.dtype)
        fills = [pltpu.make_async_copy(ybuf.at[1, pl.ds(0, DUMMY_TOKENS * rc), :],
                                       y_hbm.at[pl.ds((k * plane_tokens + n_tok) * rc, DUMMY_TOKENS * rc), :],
                                       zsem.at[0]) for k in range(TOP_K)]
        for c in fills:
            c.start()
        for c in fills:
            c.wait()
        for c in weight_copies(te_ref[0], 0):
            c.start(priority=1)
        for t in range(GATHER_SLOTS - 1):
            start_t = ts_ref[jnp.minimum(t, last)]
            lax.fori_loop(0, tm, lambda i, c, start_t=start_t, t=t: (gather_row(i, start_t, t), c)[1], 0)

    @pl.when(first)
    def _():
        for c in weight_copies(te_ref[j], wslot):
            c.wait()
        wg_bf[...] = wg_f[wslot].astype(BF16)
        wu_bf[...] = wu_f[wslot].astype(BF16)
        wd_bf[...] = wd_f[wslot].astype(BF16)
        nxt_e = ne_ref[j]

        @pl.when(nxt_e >= 0)
        def _():
            for c in weight_copies(nxt_e, 1 - wslot):
                c.start(priority=1)

    @pl.when(jnp.logical_and(used, j > 0))
    def _():
        wait_scatter(slot)

    @pl.when(used)
    def _():
        wait_gather(gslot)
        xs[...] = jnp.concatenate(
            [xbuf[gslot, pl.ds(c, tm, stride=rc), :] for c in range(rc)], axis=1).astype(BF16)
        start_n = ts_ref[ahead]
        ahead_slot = (j + GATHER_SLOTS - 1) % GATHER_SLOTS
        for i in range(tm):
            gather_row(i, start_n, ahead_slot)
        start_p = ts_ref[prv]
        n_valid_p = jnp.where(j > 0, tn_ref[prv], 0)
        for i in range(tm):
            scatter_row(i, start_p, n_valid_p, other)
        x = xs[...]
        hg = jnp.dot(x, wg_bf[...], preferred_element_type=F32)
        hu = jnp.dot(x, wu_bf[...], preferred_element_type=F32)
        h = (hg * jax.nn.sigmoid(hg) * hu).astype(BF16)
        y = jnp.dot(h, wd_bf[...], preferred_element_type=F32)
        for c in range(rc):
            ybuf[slot, pl.ds(c, tm, stride=rc), :] = y[:, c * LANES:(c + 1) * LANES]

    @pl.when(jnp.logical_and(jnp.logical_not(used), prev_used))
    def _():
        for t in range(GATHER_SLOTS - 1):
            wait_gather((j + t) % GATHER_SLOTS)
        wait_scatter(slot)
        scatter_tile_loop(prv, other)
        wait_scatter(other)

    @pl.when(jnp.logical_and(used, j == last))
    def _():
        for t in range(1, GATHER_SLOTS):
            wait_gather((j + t) % GATHER_SLOTS)
        wait_scatter(other)
        scatter_tile_loop(j, slot)
        wait_scatter(slot)


def _experts(tile_expert, tile_start, tile_rows, tile_flags, next_expert, gather_rows, scatter_rows, h2_all,
             w_gate, w_up, w_down, n_tok):
    n_tiles = tile_expert.shape[0]
    tm = EXPERT_TILE
    any_spec = pl.BlockSpec(memory_space=pl.ANY)
    grid_spec = pltpu.PrefetchScalarGridSpec(
        num_scalar_prefetch=7,
        grid=(n_tiles,),
        in_specs=[any_spec] * 4,
        out_specs=any_spec,
        scratch_shapes=[
            pltpu.VMEM((GATHER_SLOTS, tm * ROW_CHUNKS, LANES), F32),
            pltpu.VMEM((2, tm * ROW_CHUNKS, LANES), F32),
            pltpu.VMEM((tm, D_MODEL), BF16),
            pltpu.VMEM((2, D_MODEL, D_EXPERT), F32),
            pltpu.VMEM((2, D_MODEL, D_EXPERT), F32),
            pltpu.VMEM((2, D_EXPERT, D_MODEL), F32),
            pltpu.VMEM((D_MODEL, D_EXPERT), BF16),
            pltpu.VMEM((D_MODEL, D_EXPERT), BF16),
            pltpu.VMEM((D_EXPERT, D_MODEL), BF16),
            pltpu.SemaphoreType.DMA((GATHER_SLOTS,)),
            pltpu.SemaphoreType.DMA((2,)),
            pltpu.SemaphoreType.DMA((2,)),
            pltpu.SemaphoreType.DMA((1,)),
        ],
    )
    return pl.pallas_call(
        functools.partial(_experts_kernel, n_tok),
        grid_spec=grid_spec,
        out_shape=jax.ShapeDtypeStruct((TOP_K * (n_tok + DUMMY_TOKENS) * ROW_CHUNKS, LANES), F32),
        compiler_params=_cparams(("arbitrary",)),
        name="experts",
    )(tile_expert, tile_start, tile_rows, tile_flags, next_expert, gather_rows, scatter_rows, h2_all,
      w_gate, w_up, w_down)


COMBINE_UNROLL = 4


def _combine_kernel(w_ref, y_ref, base_ref, g2_ref, o_ref, acc_rows):
    i = pl.program_id(0)
    tc = base_ref.shape[0]
    rc = ROW_CHUNKS

    def body(g, carry):
        for u in range(COMBINE_UNROLL):
            t = g * COMBINE_UNROLL + u
            rows = pl.ds(pl.multiple_of(t * rc, rc), rc)
            w0 = (i * tc + t) * TOP_K
            acc = w_ref[w0] * y_ref[0, rows, :]
            for k in range(1, TOP_K):
                acc = acc + w_ref[w0 + k] * y_ref[k, rows, :]
            acc_rows[rows, :] = acc
        return carry

    lax.fori_loop(0, tc // COMBINE_UNROLL, body, 0)
    g2 = g2_ref[0]
    for c in range(rc):
        sl = slice(c * LANES, (c + 1) * LANES)
        o_ref[:, sl] = base_ref[:, sl] + g2[:, sl] * acc_rows[pl.ds(c, tc, stride=rc), :]


def _combine(y_planes, wts, base, gate2, tc, rows_per_mod, block0):
    m = base.shape[0]
    r = gate2.shape[1]
    assert tc % COMBINE_UNROLL == 0
    grid_spec = pltpu.PrefetchScalarGridSpec(
        num_scalar_prefetch=1,
        grid=(m // tc,),
        in_specs=[
            pl.BlockSpec((TOP_K, tc * ROW_CHUNKS, LANES), lambda i, w: (0, i + block0, 0)),
            pl.BlockSpec((tc, D_MODEL), lambda i, w: (i, 0)),
            pl.BlockSpec((1, r, D_MODEL), lambda i, w: (i // rows_per_mod, 0, 0)),
        ],
        out_specs=pl.BlockSpec((tc, D_MODEL), lambda i, w: (i, 0)),
        scratch_shapes=[pltpu.VMEM((tc * ROW_CHUNKS, LANES), F32)],
    )
    return pl.pallas_call(
        _combine_kernel,
        grid_spec=grid_spec,
        out_shape=jax.ShapeDtypeStruct((m, D_MODEL), F32),
        compiler_params=_cparams(("parallel",)),
        name="combine",
    )(wts.reshape(-1), y_planes, base, gate2)


def _moe_routed(h2_all, logits_t, b_router, w_gate, w_up, w_down):
    n_tok = logits_t.shape[1]
    mp = -(-n_tok // ROUTE_TILE) * ROUTE_TILE
    logits_t = jnp.pad(logits_t, ((0, 0), (0, mp - n_tok)))
    eidx, wts, counts = _route(logits_t, b_router, n_tok)
    eidx, wts, counts = eidx[:, :n_tok], wts[:, :n_tok], counts[:, 0]

    tm = EXPERT_TILE
    n_pairs = n_tok * TOP_K
    n_tiles = n_pairs // tm + N_EXPERTS
    pair_code = jnp.arange(n_tok, dtype=I32)[None, :] * TOP_K + jnp.arange(TOP_K, dtype=I32)[:, None]
    (sorted_pairs,) = lax.sort(((eidx * n_pairs + pair_code).reshape(-1),), is_stable=False)
    sorted_pairs = jnp.concatenate([sorted_pairs % n_pairs, jnp.zeros((tm,), I32)])
    pair_tok, pair_choice = sorted_pairs >> PAIR_BITS, sorted_pairs & (TOP_K - 1)
    gather_rows = pair_tok * ROW_CHUNKS
    scatter_rows = (pair_choice * (n_tok + DUMMY_TOKENS) + pair_tok) * ROW_CHUNKS
    tiles_e = (counts + tm - 1) // tm
    tile_end = jnp.cumsum(tiles_e)
    tile_begin = tile_end - tiles_e
    dense_begin = jnp.cumsum(counts) - counts
    tj = jnp.arange(n_tiles, dtype=I32)
    total_tiles = tile_end[-1]
    used = tj < total_tiles
    tj_used = jnp.minimum(tj, total_tiles - 1)[:, None]
    member = jnp.logical_and(tile_begin[None, :] <= tj_used, tj_used < tile_end[None, :])
    of_tile = lambda per_expert: jnp.sum(jnp.where(member, per_expert[None, :], 0), axis=1)
    e_ids = jnp.arange(N_EXPERTS, dtype=I32)
    te = of_tile(e_ids)
    local = tj - of_tile(tile_begin)
    ts = jnp.where(used, of_tile(dense_begin) + local * tm, 0).astype(I32)
    tn = jnp.where(used, jnp.clip(of_tile(counts) - local * tm, 0, tm), 0).astype(I32)
    first = jnp.logical_and(used, local == 0)
    busy = counts > 0
    wslot = of_tile((jnp.cumsum(busy.astype(I32)) - 1) % 2)
    later = jnp.logical_and(e_ids[None, :] > e_ids[:, None], busy[None, :])
    succ = jnp.min(jnp.where(later, e_ids[None, :], N_EXPERTS), axis=1)
    succ = jnp.where(succ == N_EXPERTS, -1, succ)
    ne = jnp.where(first, of_tile(succ), -1).astype(I32)
    prev_used = jnp.concatenate([jnp.zeros((1,), jnp.bool_), used[:-1]])
    tf = (used.astype(I32) * TILE_USED + first.astype(I32) * TILE_FIRST + wslot * TILE_WSLOT
          + prev_used.astype(I32) * TILE_PREV_USED)
    y_all = _experts(te, ts, tn, tf, ne, gather_rows, scatter_rows, h2_all, w_gate, w_up, w_down, n_tok)
    return y_all.reshape(TOP_K, (n_tok + DUMMY_TOKENS) * ROW_CHUNKS, LANES), wts.T


def kernel(x_prompt, x_sample, cache_k_win, cache_v_win, state_pool, c_prompt, c_sample, w_ada, b_ada, g_mix, w_in, q_norm, k_norm, w_pool, pool_scale, w_out, g_ffn, w_router, b_router, w_gate, w_up, w_down, ws_gate, ws_up, ws_down):
    depth = w_ada.shape[0]
    assert depth == 1
    l = 0
    nb, seq, _ = x_prompt.shape
    bd, dec_seq, _ = x_sample.shape
    assert dec_seq == 1 and seq % (TOKEN_TILE * 16) == 0
    n_prompt = nb * seq
    n_tok = n_prompt + bd
    n_keep = min(MAX_WINDOW, seq)
    tiles_per_batch = seq // TOKEN_TILE

    n_mod_rows = -(-(nb + bd) // 8) * 8
    c_all = jnp.concatenate([c_prompt, c_sample, jnp.zeros((n_mod_rows - nb - bd, D_MODEL), F32)], axis=0)
    mods = _ada(c_all, w_ada[l], b_ada[l])
    mods_p = [mods[:nb, j * D_MODEL:(j + 1) * D_MODEL].reshape(nb, 1, D_MODEL) for j in range(6)]
    mods_s = [mods[nb:nb + bd, j * D_MODEL:(j + 1) * D_MODEL].reshape(1, bd, D_MODEL) for j in range(6)]

    g_mix_l = g_mix[l].reshape(1, D_MODEL)
    g_ffn_l = g_ffn[l].reshape(1, D_MODEL)
    qn = q_norm[l].reshape(1, HEAD_DIM)
    kn = k_norm[l].reshape(1, HEAD_DIM)
    w_in_bf = w_in[l].astype(BF16)
    w_out_bf = w_out[l].astype(BF16)
    w_pool_bf = w_pool[l].astype(BF16)
    ps = pool_scale[l].reshape(1, D_POOL)
    w_router_pad = jnp.pad(w_router[l], ((0, 0), (0, LANES - N_EXPERTS)))
    w_router_hi = w_router_pad.astype(BF16)
    w_router_lo = (w_router_pad - w_router_hi.astype(F32)).astype(BF16)
    wsg_bf, wsu_bf, wsd_bf = ws_gate[l].astype(BF16), ws_up[l].astype(BF16), ws_down[l].astype(BF16)

    rope_p = _rope_tables(jnp.arange(seq, dtype=I32))
    rope_s = _rope_tables(jnp.full((bd,), PAST_LEN, I32))
    xp2 = x_prompt.reshape(n_prompt, D_MODEL)
    xs2 = x_sample.reshape(bd, D_MODEL)
    assert n_keep % TOKEN_TILE == 0
    qp, kp, vp, up, kwin_p, vwin_p = _in_proj(xp2, mods_p[0], mods_p[1], g_mix_l, w_in_bf, qn, kn, rope_p,
                                              TOKEN_TILE, tiles_per_batch, tiles_per_batch, n_keep // TOKEN_TILE)
    qs, _, _, us, k_new, v_new = _in_proj(xs2, mods_s[0], mods_s[1], g_mix_l, w_in_bf, qn, kn, rope_s, bd, 1, 1, 1)

    to_seq = lambda t: t.reshape(nb, seq, -1)
    heads = lambda t: t.reshape(bd, N_HEADS, HEAD_DIM)
    ck, cv = cache_k_win[l], cache_v_win[l]
    k_new, v_new = heads(k_new), heads(v_new)
    attn_p, k_win_s, v_win_s = _attn_prompt(to_seq(qp), to_seq(kp), to_seq(vp), ck, cv, k_new, v_new)
    attn_p = attn_p.reshape(n_prompt, D_ATTN)
    pool_p = _pool_prompt(to_seq(up), w_pool_bf, ps).reshape(n_prompt, D_POOL)
    attn_s = _attn_decode(heads(qs), k_new, v_new, ck, cv).reshape(bd, D_ATTN)
    pool_s = _pool_decode(us, jnp.swapaxes(state_pool[l], 0, 1), w_pool_bf, ps)

    base_p, h2_all, lg_p = _out_proj(xp2, attn_p, pool_p, mods_p[2], mods_p[3], mods_p[4], mods_p[5],
                                     w_out_bf, g_ffn_l, w_router_hi, w_router_lo, wsg_bf, wsu_bf, wsd_bf,
                                     TOKEN_TILE, tiles_per_batch, n_tok, None, 0)
    base_s, h2_all, lg_s = _out_proj(xs2, attn_s, pool_s, mods_s[2], mods_s[3], mods_s[4], mods_s[5],
                                     w_out_bf, g_ffn_l, w_router_hi, w_router_lo, wsg_bf, wsu_bf, wsd_bf,
                                     bd, 1, n_tok, h2_all, n_prompt // bd)

    y_planes, wts_t = _moe_routed(h2_all, jnp.concatenate([lg_p, lg_s], axis=1), b_router[l],
                                  w_gate[l], w_up[l], w_down[l])

    yp = _combine(y_planes, wts_t[:n_prompt], base_p, mods_p[5], COMBINE_TILE, seq // COMBINE_TILE, 0)
    ys = _combine(y_planes, wts_t[n_prompt:n_tok], base_s, mods_s[5], bd, 1, n_prompt // bd)

    y_prompt = yp.reshape(nb, seq, D_MODEL)
    y_sample = ys.reshape(bd, dec_seq, D_MODEL)
    win = lambda t: t.reshape(1, nb, n_keep, N_HEADS, HEAD_DIM)
    pool_p_state = to_seq(up)[:, seq - POOL_STATE:][None]
    pool_s_state = jnp.concatenate([state_pool[l][:, 1:], us[:, None, :]], axis=1)[None]
    return (y_prompt, y_sample, win(kwin_p), win(vwin_p), pool_p_state,
            k_win_s[None], v_win_s[None], pool_s_state)
```

```python
import functools

import jax
import jax.numpy as jnp
from jax import lax
from jax.experimental import pallas as pl
from jax.experimental.pallas import tpu as pltpu

F32 = jnp.float32
BF16 = jnp.bfloat16
I32 = jnp.int32

D_MODEL = 2048
N_HEADS = 8
HEAD_DIM = 128
D_ATTN = N_HEADS * HEAD_DIM
D_POOL = D_MODEL - D_ATTN
D_IN = 3 * D_ATTN + D_POOL
ROPE_DIM = HEAD_DIM // 4
ROPE_HALF = ROPE_DIM // 2
ROPE_THETA = 500000.0
DILATED_GROUPS = ((128, 1), (512, 4), (2048, 16))
QBLOCK = 128
ATTN_SCALE = HEAD_DIM ** -0.5
POOL_WINDOWS = (2, 4, 8, 16)
POOL_GROUP_DIM = D_POOL // len(POOL_WINDOWS)
POOL_STATE = max(POOL_WINDOWS) - 1
POOL_HALO = 16
N_EXPERTS = 64
TOP_K = 8
N_EXPERT_GROUPS = 8
GROUP_SIZE = N_EXPERTS // N_EXPERT_GROUPS
TOPK_GROUPS = 4
D_EXPERT = 512
D_SHARED = 512
ROUTED_SCALE = 2.5
EPS = 1e-6
PAST_LEN = 16384
MAX_WINDOW = 2048

LANES = 128
ROW_CHUNKS = D_MODEL // LANES
NEG_BIG = -1e30
LOG2_E = 1.4426950408889634

TOKEN_TILE = 256
EXPERT_TILE = 256
COMBINE_TILE = 128
ROUTE_TILE = 256
ADA_COL_TILE = 1024
POOL_TILE = 512
VMEM_LIMIT = 60 * 1024 * 1024


def _cparams(sem, vmem=VMEM_LIMIT):
    return pltpu.CompilerParams(dimension_semantics=sem, vmem_limit_bytes=vmem)


def _ada_kernel(c_ref, w_ref, b_ref, o_ref):
    c = c_ref[...]
    s = (c * jax.nn.sigmoid(c)).astype(BF16)
    o_ref[...] = jnp.dot(s, w_ref[...].astype(BF16), preferred_element_type=F32) + b_ref[...]


def _ada(c_all, w_ada, b_ada):
    rows = c_all.shape[0]
    n = w_ada.shape[1]
    tn = ADA_COL_TILE
    return pl.pallas_call(
        _ada_kernel,
        grid=(n // tn,),
        in_specs=[
            pl.BlockSpec((rows, D_MODEL), lambda j: (0, 0)),
            pl.BlockSpec((D_MODEL, tn), lambda j: (0, j)),
            pl.BlockSpec((1, tn), lambda j: (0, j)),
        ],
        out_specs=pl.BlockSpec((rows, tn), lambda j: (0, j)),
        out_shape=jax.ShapeDtypeStruct((rows, n), F32),
        compiler_params=_cparams(("parallel",)),
        name="ada_modulation",
    )(c_all, w_ada, b_ada.reshape(1, n))


def _rms(x):
    return x * lax.rsqrt(jnp.mean(x * x, axis=-1, keepdims=True) + EPS)


def _in_kernel(x_ref, sh_ref, sc_ref, g_ref, w_ref, qn_ref, kn_ref, c_ref, s1_ref, s2_ref,
               q_ref, k_ref, v_ref, u_ref, kw_ref, vw_ref):
    tm = x_ref.shape[0]
    x = x_ref[...]
    h = _rms(x) * g_ref[...] * (1.0 + sc_ref[0]) + sh_ref[0]
    hb = h.astype(BF16)
    cos = c_ref[...]
    s1 = s1_ref[...]
    s2 = s2_ref[...]

    def head_rows(win_ref, hd, val):
        win_ref[pl.ds(hd, tm, stride=N_HEADS), :] = val

    def qk(sec, nrm, out_ref, win_ref):
        z = jnp.dot(hb, w_ref[:, sec * D_ATTN:(sec + 1) * D_ATTN], preferred_element_type=F32)
        for hd in range(N_HEADS):
            sl = slice(hd * HEAD_DIM, (hd + 1) * HEAD_DIM)
            r = _rms(z[:, sl]) * nrm
            r = r * cos + pltpu.roll(r, HEAD_DIM - ROPE_HALF, 1) * s1 + pltpu.roll(r, ROPE_HALF, 1) * s2
            out_ref[:, sl] = r
            if win_ref is not None:
                head_rows(win_ref, hd, r)

    qk(0, qn_ref[...], q_ref, None)
    qk(1, kn_ref[...], k_ref, kw_ref)
    v = jnp.dot(hb, w_ref[:, 2 * D_ATTN:3 * D_ATTN], preferred_element_type=F32)
    v_ref[...] = v
    for hd in range(N_HEADS):
        head_rows(vw_ref, hd, v[:, hd * HEAD_DIM:(hd + 1) * HEAD_DIM])
    u_ref[...] = jnp.dot(hb, w_ref[:, 3 * D_ATTN:], preferred_element_type=F32)


def _in_proj(x2d, shift, scale, g_mix, w_in_bf, q_norm, k_norm, rope, tm, rows_per_mod, rope_tiles, keep_tiles):
    m = x2d.shape[0]
    r = shift.shape[1]
    n_groups = m // tm // rows_per_mod
    skip = rows_per_mod - keep_tiles
    mod_spec = pl.BlockSpec((1, r, D_MODEL), lambda i: (i // rows_per_mod, 0, 0))
    rope_spec = pl.BlockSpec((tm, HEAD_DIM), lambda i: (i % rope_tiles, 0))
    out_spec = pl.BlockSpec((tm, D_ATTN), lambda i: (i, 0))
    out_sd = jax.ShapeDtypeStruct((m, D_ATTN), F32)
    win_spec = pl.BlockSpec(
        (tm * N_HEADS, HEAD_DIM),
        lambda i: ((i // rows_per_mod) * keep_tiles + jnp.maximum(i % rows_per_mod - skip, 0), 0))
    win_sd = jax.ShapeDtypeStruct((n_groups * keep_tiles * tm * N_HEADS, HEAD_DIM), F32)
    return pl.pallas_call(
        _in_kernel,
        grid=(m // tm,),
        in_specs=[
            pl.BlockSpec((tm, D_MODEL), lambda i: (i, 0)),
            mod_spec, mod_spec,
            pl.BlockSpec((1, D_MODEL), lambda i: (0, 0)),
            pl.BlockSpec((D_MODEL, D_IN), lambda i: (0, 0)),
            pl.BlockSpec((1, HEAD_DIM), lambda i: (0, 0)),
            pl.BlockSpec((1, HEAD_DIM), lambda i: (0, 0)),
            rope_spec, rope_spec, rope_spec,
        ],
        out_specs=[out_spec] * 4 + [win_spec] * 2,
        out_shape=[out_sd] * 4 + [win_sd] * 2,
        compiler_params=_cparams(("arbitrary",)),
        name="in_proj",
    )(x2d, shift, scale, g_mix, w_in_bf, q_norm, k_norm, *rope)


def _rope_tables(pos):
    inv_freq = jnp.float32(ROPE_THETA) ** (-jnp.arange(ROPE_HALF, dtype=F32) / ROPE_HALF)
    ang = pos.astype(F32)[:, None] * inv_freq[None, :]
    cos, sin = jnp.cos(ang), jnp.sin(ang)
    t = pos.shape[0]
    rest = HEAD_DIM - ROPE_DIM
    c = jnp.concatenate([cos, cos, jnp.ones((t, rest), F32)], axis=1)
    s1 = jnp.concatenate([-sin, jnp.zeros((t, ROPE_HALF + rest), F32)], axis=1)
    s2 = jnp.concatenate([jnp.zeros((t, ROPE_HALF), F32), sin, jnp.zeros((t, rest), F32)], axis=1)
    return c, s1, s2


ATTN_INTERLEAVE = 4
WINDOW_ROWS_PER_STEP = 2
REGROUP_MIN_DILATION = 16


REGROUP_STRIDE = 4


def _to_residue_major(read_rows, dst, tmp, seq, sub_len):
    n, part = REGROUP_STRIDE, seq // REGROUP_STRIDE
    for a in range(n):
        tmp[pl.ds(a * part, part), :] = read_rows(pl.ds(a, part, stride=n))
    for a in range(n):
        for s in range(n):
            dst[pl.ds((s * n + a) * sub_len, sub_len), :] = tmp[pl.ds(a * part + s, sub_len, stride=n), :]


def _from_residue_major(src, dst, tmp, seq, sub_len):
    n, part = REGROUP_STRIDE, seq // REGROUP_STRIDE
    for a in range(n):
        for s in range(n):
            tmp[pl.ds(a * part + s, sub_len, stride=n), :] = src[pl.ds((s * n + a) * sub_len, sub_len), :]
    for a in range(n):
        dst[pl.ds(a, part, stride=n), :] = tmp[pl.ds(a * part, part), :]


def _attn_prompt_kernel(q_ref, k_ref, v_ref, ck_hbm, cv_hbm, kn_ref, vn_ref, o_ref, ok_hbm, ov_hbm,
                        m_scr, l_scr, acc_scr, q_rm, k_rm, v_rm, m_rm, l_rm, a_rm, wbuf, isem, osem):
    seq = q_ref.shape[1]
    half = wbuf.shape[2]
    step = pl.program_id(0) * pl.num_programs(1) + pl.program_id(1)
    last_step = pl.num_programs(0) * pl.num_programs(1) - 1
    windows = ((ck_hbm, kn_ref, ok_hbm), (cv_hbm, vn_ref, ov_hbm))
    row0 = step * WINDOW_ROWS_PER_STEP
    n_bounds = 2 * WINDOW_ROWS_PER_STEP
    assert n_bounds == len(DILATED_GROUPS) + 1

    def chunk_in(row, h, a):
        n_rows = half - h
        return pltpu.make_async_copy(windows[a][0].at[row, pl.ds(h * half + 1, n_rows)],
                                     wbuf.at[h, a, pl.ds(0, n_rows)], isem.at[h, a])

    def chunk_out(row, h, a):
        return pltpu.make_async_copy(wbuf.at[h, a], windows[a][2].at[row, pl.ds(h * half, half)], osem.at[h, a])

    def finish_in_start_out(row, h):
        for a in range(2):
            chunk_in(row, h, a).wait()
            if h == 1:
                wbuf[h, a, half - 1] = windows[a][1][row]
            chunk_out(row, h, a).start(priority=1)

    def finish_out(row, h):
        for a in range(2):
            chunk_out(row, h, a).wait()

    def window_boundary(b):
        h = b % 2
        if b == 0:
            @pl.when(step > 0)
            def _():
                finish_in_start_out(row0 - 1, 1)
                finish_out(row0 - 1, 0)
        elif b == 1:
            finish_in_start_out(row0, 0)
            pl.when(step > 0)(lambda: finish_out(row0 - 1, 1))
        else:
            finish_in_start_out(row0 + (b - 1) // 2, (b - 1) % 2)
            finish_out(row0 + (b - 2) // 2, h)
        for a in range(2):
            chunk_in(row0 + b // 2, h, a).start()

    window_phases = [functools.partial(window_boundary, b) for b in range(n_bounds)]
    window_phases[0]()
    m_scr[...] = jnp.full(m_scr.shape, NEG_BIG, F32)
    l_scr[...] = jnp.zeros(l_scr.shape, F32)
    acc_scr[...] = jnp.zeros(acc_scr.shape, F32)
    qi = lax.broadcasted_iota(I32, (QBLOCK, 2 * QBLOCK), 0)
    kj = lax.broadcasted_iota(I32, (QBLOCK, 2 * QBLOCK), 1)
    dist = qi + QBLOCK - kj
    band_mask = jnp.logical_and(dist >= 0, dist <= QBLOCK)
    bias_band = jnp.where(band_mask, 0.0, NEG_BIG)
    bias_first = jnp.where(jnp.logical_and(band_mask, kj >= QBLOCK), 0.0, NEG_BIG)
    ones = jnp.ones((2 * QBLOCK, HEAD_DIM), BF16)
    dims = (((1,), (1,)), ((), ()))
    q_scale = ATTN_SCALE * LOG2_E

    regrouped = None
    for window, dil in DILATED_GROUPS:
        assert window // dil == QBLOCK
        n_units = seq // QBLOCK
        assert n_units % ATTN_INTERLEAVE == 0 and (dil % ATTN_INTERLEAVE == 0 or dil == 1)

        def rows(start, dil=dil):
            return pl.ds(start, QBLOCK) if dil == 1 else pl.ds(start, QBLOCK, stride=dil)

        sub_len = seq // dil
        regroup = dil >= REGROUP_MIN_DILATION
        if regroup:
            assert dil == REGROUP_STRIDE ** 2 and regrouped is None
            regrouped = (dil, sub_len)
            for src, dst in ((q_ref, q_rm), (k_ref, k_rm), (v_ref, v_rm)):
                _to_residue_major(lambda rows_, src=src: src[0, rows_, :], dst, m_rm, seq, sub_len)
            m_rm[...] = jnp.full(m_rm.shape, NEG_BIG, F32)
            l_rm[...] = jnp.zeros(l_rm.shape, F32)
            a_rm[...] = jnp.zeros(a_rm.shape, F32)

        def qkv_blocks(blk, res, start, prev, rows=rows, regroup=regroup, sub_len=sub_len):
            if regroup:
                cur = pl.ds(res * sub_len + blk * QBLOCK, QBLOCK)
                prv = pl.ds(res * sub_len + jnp.maximum(blk - 1, 0) * QBLOCK, QBLOCK)
                return (q_rm[cur, :], k_rm[prv, :], k_rm[cur, :], v_rm[prv, :], v_rm[cur, :])
            return (q_ref[0, rows(start), :], k_ref[0, rows(prev), :], k_ref[0, rows(start), :],
                    v_ref[0, rows(prev), :], v_ref[0, rows(start), :])

        state_refs = (m_rm, l_rm, a_rm) if regroup else (m_scr, l_scr, acc_scr)

        def state_rows(blk, res, start, rows=rows, regroup=regroup, sub_len=sub_len):
            return pl.ds(res * sub_len + blk * QBLOCK, QBLOCK) if regroup else rows(start)

        def body(it, carry, dil=dil, rows=rows, qkv_blocks=qkv_blocks, state_refs=state_refs, state_rows=state_rows):
            m_st, l_st, a_st = state_refs
            loaded = []
            for u in range(ATTN_INTERLEAVE):
                f = it * ATTN_INTERLEAVE + u
                blk, res = f // dil, f % dil
                start = blk * (QBLOCK * dil) + res
                prev = jnp.maximum(start - QBLOCK * dil, res)
                q_blk, k_prev, k_cur, v_prev, v_cur = qkv_blocks(blk, res, start, prev)
                q = (q_blk * q_scale).astype(BF16)
                k = jnp.concatenate([k_prev, k_cur], axis=0).astype(BF16)
                v = jnp.concatenate([v_prev, v_cur], axis=0).astype(BF16)
                v_ones = jnp.concatenate([v, ones], axis=1)
                st = state_rows(blk, res, start)
                state = (m_st[st, :], l_st[st, :], a_st[st, :])
                loaded.append((st, blk, q, k, v_ones, state))
            scores = [lax.dot_general(q, k, dims, preferred_element_type=F32)
                      + jnp.where(blk > 0, bias_band, bias_first)
                      for _, blk, q, k, _, _ in loaded]
            probs = []
            for s, (_, _, _, _, _, (m_old, _, _)) in zip(scores, loaded):
                m_new = jnp.maximum(m_old, jnp.max(s, axis=1, keepdims=True))
                p = jnp.exp2(s - jnp.concatenate([m_new, m_new], axis=1)).astype(BF16)
                probs.append((m_new, jnp.exp2(m_old - m_new), p))
            updated = []
            for (m_new, alpha, p), (start, _, _, _, v_ones, (_, l_old, a_old)) in zip(probs, loaded):
                pv = jnp.dot(p, v_ones, preferred_element_type=F32)
                updated.append((start, m_new, alpha * l_old + pv[:, HEAD_DIM:], alpha * a_old + pv[:, :HEAD_DIM]))
            for st, m_new, l_new, a_new in updated:
                m_st[st, :] = m_new
                l_st[st, :] = l_new
                a_st[st, :] = a_new
            return carry

        trips = n_units // ATTN_INTERLEAVE
        phase = window_phases[DILATED_GROUPS.index((window, dil)) + 1]
        if (window, dil) == DILATED_GROUPS[-1]:
            lax.fori_loop(0, trips // 2, body, 0)
            phase()
            lax.fori_loop(trips // 2, trips, body, 0)
        else:
            lax.fori_loop(0, trips, body, 0)
            phase()

    @pl.when(step == last_step)
    def _():
        last_row = row0 + WINDOW_ROWS_PER_STEP - 1
        finish_in_start_out(last_row, 1)
        finish_out(last_row, 0)
        finish_out(last_row, 1)

    if regrouped is None:
        o_ref[0] = acc_scr[...] / l_scr[...]
    else:
        _, sub_len = regrouped
        _from_residue_major(a_rm, q_rm, k_rm, seq, sub_len)
        _from_residue_major(l_rm, k_rm, a_rm, seq, sub_len)
        _from_residue_major(m_rm, v_rm, a_rm, seq, sub_len)
        m_a, m_b = m_scr[...], v_rm[...]
        m_all = jnp.maximum(m_a, m_b)
        w_a, w_b = jnp.exp2(m_a - m_all), jnp.exp2(m_b - m_all)
        o_ref[0] = (acc_scr[...] * w_a + q_rm[...] * w_b) / (l_scr[...] * w_a + k_rm[...] * w_b)


def _attn_prompt(q, k, v, cache_k, cache_v, k_new, v_new):
    b, s, _ = q.shape
    bd, n_buf = cache_k.shape[0], cache_k.shape[1]
    assert bd == b * N_HEADS * WINDOW_ROWS_PER_STEP
    spec = pl.BlockSpec((1, s, HEAD_DIM), lambda bi, hi: (bi, 0, hi))
    any_spec = pl.BlockSpec(memory_space=pl.ANY)
    new_spec = pl.BlockSpec(k_new.shape, lambda bi, hi: (0, 0, 0))
    win_sd = jax.ShapeDtypeStruct(cache_k.shape, cache_k.dtype)
    return pl.pallas_call(
        _attn_prompt_kernel,
        grid=(b, N_HEADS),
        in_specs=[spec, spec, spec, any_spec, any_spec, new_spec, new_spec],
        out_specs=[spec, any_spec, any_spec],
        out_shape=[jax.ShapeDtypeStruct((b, s, D_ATTN), F32), win_sd, win_sd],
        scratch_shapes=[pltpu.VMEM((s, HEAD_DIM), F32)] * 9 + [
            pltpu.VMEM((2, 2, n_buf // 2) + cache_k.shape[2:], F32),
            pltpu.SemaphoreType.DMA((2, 2)),
            pltpu.SemaphoreType.DMA((2, 2)),
        ],
        compiler_params=_cparams(("arbitrary", "arbitrary")),
        name="attn_prompt",
    )(q, k, v, cache_k, cache_v, k_new, v_new)


def _attn_decode_kernel(q_ref, kn_ref, vn_ref, k1_ref, k2_ref, k3_ref, v1_ref, v2_ref, v3_ref, o_ref):
    q = q_ref[0]
    k_new = kn_ref[0]
    v_new = vn_ref[0]
    n_grp = len(DILATED_GROUPS)

    def scores(kb):
        return jnp.sum(kb * q[None], axis=-1, keepdims=True) * ATTN_SCALE

    kcs = [k1_ref[0], k2_ref[0][:, 0], k3_ref[0][:, 0]]
    vcs = [v1_ref[0], v2_ref[0][:, 0], v3_ref[0][:, 0]]
    s_new = jnp.sum(k_new * q, axis=-1, keepdims=True) * ATTN_SCALE
    s_grp = [scores(kb) for kb in kcs]
    m = s_new
    for s in s_grp:
        m = jnp.maximum(m, jnp.max(s, axis=0))
    p_new = jnp.exp(s_new - m)
    den = n_grp * p_new
    num = n_grp * p_new * v_new
    for s, vb in zip(s_grp, vcs):
        p = jnp.exp(s - m[None])
        den = den + jnp.sum(p, axis=0)
        num = num + jnp.sum(p * vb, axis=0)
    o_ref[0] = num / den


def _attn_decode(q, k_new, v_new, cache_k, cache_v):
    bd, n_buf = cache_k.shape[0], cache_k.shape[1]
    assert n_buf == MAX_WINDOW

    def views(cache):
        out, specs = [], []
        for window, dil in DILATED_GROUPS:
            band = window // dil
            assert band == QBLOCK and n_buf % dil == 0 and n_buf // dil >= band
            if dil == 1:
                out.append(cache)
                specs.append(pl.BlockSpec((1, band, N_HEADS, HEAD_DIM),
                                          lambda b, nb=n_buf // band: (b, nb - 1, 0, 0)))
            else:
                out.append(cache.reshape(bd, n_buf // dil, dil, N_HEADS, HEAD_DIM))
                specs.append(pl.BlockSpec((1, band, 1, N_HEADS, HEAD_DIM),
                                          lambda b, nb=n_buf // dil // band: (b, nb - 1, 0, 0, 0)))
        return out, specs

    kv, kspecs = views(cache_k)
    vv, vspecs = views(cache_v)
    tok = pl.BlockSpec((1, N_HEADS, HEAD_DIM), lambda b: (b, 0, 0))
    return pl.pallas_call(
        _attn_decode_kernel,
        grid=(bd,),
        in_specs=[tok, tok, tok] + kspecs + vspecs,
        out_specs=tok,
        out_shape=jax.ShapeDtypeStruct((bd, N_HEADS, HEAD_DIM), F32),
        compiler_params=_cparams(("parallel",)),
        name="attn_decode",
    )(q, k_new, v_new, *kv, *vv)


def _pool_project(d_groups, wp_ref, ps_ref, out_ref):
    for g, d in enumerate(d_groups):
        sl = slice(g * POOL_GROUP_DIM, (g + 1) * POOL_GROUP_DIM)
        out_ref[:, sl] = jnp.dot(d.astype(BF16), wp_ref[g], preferred_element_type=F32) * ps_ref[:, sl]


def _pool_prompt_kernel(u_ref, prev_ref, wp_ref, ps_ref, o_ref):
    i = pl.program_id(1)
    tp = u_ref.shape[1]
    u = u_ref[0]
    prev = jnp.where(i > 0, prev_ref[0], 0.0)
    ext = jnp.concatenate([prev, u], axis=0)
    pos = i * tp + lax.broadcasted_iota(I32, (tp, 1), 0)
    d_groups = []
    for g, w in enumerate(POOL_WINDOWS):
        sl = slice(g * POOL_GROUP_DIM, (g + 1) * POOL_GROUP_DIM)
        a = ext[:, sl]
        span = 1
        while span < w:
            a = a + pltpu.roll(a, span, 0)
            span *= 2
        win = a[POOL_HALO:, :]
        cnt = jnp.minimum(w, pos + 1).astype(F32)
        d_groups.append(win / cnt - u[:, sl])
    _pool_project(d_groups, wp_ref, ps_ref, o_ref.at[0])


def _pool_prompt(u, w_pool_bf, pool_scale):
    b, s, _ = u.shape
    tp = POOL_TILE
    halo_blocks = tp // POOL_HALO
    return pl.pallas_call(
        _pool_prompt_kernel,
        grid=(b, s // tp),
        in_specs=[
            pl.BlockSpec((1, tp, D_POOL), lambda bi, i: (bi, i, 0)),
            pl.BlockSpec((1, POOL_HALO, D_POOL), lambda bi, i: (bi, jnp.maximum(i * halo_blocks - 1, 0), 0)),
            pl.BlockSpec(w_pool_bf.shape, lambda bi, i: (0, 0, 0)),
            pl.BlockSpec((1, D_POOL), lambda bi, i: (0, 0)),
        ],
        out_specs=pl.BlockSpec((1, tp, D_POOL), lambda bi, i: (bi, i, 0)),
        out_shape=jax.ShapeDtypeStruct((b, s, D_POOL), F32),
        compiler_params=_cparams(("parallel", "parallel")),
        name="pool_prompt",
    )(u, u, w_pool_bf, pool_scale)


def _pool_decode_kernel(u_ref, st_ref, wp_ref, ps_ref, o_ref):
    u = u_ref[...]
    d_groups = []
    for g, w in enumerate(POOL_WINDOWS):
        sl = slice(g * POOL_GROUP_DIM, (g + 1) * POOL_GROUP_DIM)
        win = u[:, sl]
        for j in range(1, w):
            win = win + st_ref[POOL_STATE - j][:, sl]
        d_groups.append(win / float(w) - u[:, sl])
    _pool_project(d_groups, wp_ref, ps_ref, o_ref)


def _pool_decode(u, state_t, w_pool_bf, pool_scale):
    assert PAST_LEN + 1 >= max(POOL_WINDOWS)
    return pl.pallas_call(
        _pool_decode_kernel,
        out_shape=jax.ShapeDtypeStruct(u.shape, F32),
        compiler_params=_cparams(None),
        name="pool_decode",
    )(u, state_t, w_pool_bf, pool_scale)


def _split_bf16(x):
    hi = x.astype(BF16)
    lo = (x - hi.astype(F32)).astype(BF16)
    return hi, lo


def _out_kernel(x_ref, a_ref, p_ref, g1_ref, sh_ref, sc_ref, g2_ref, wo_ref, gf_ref, wrh_ref, wrl_ref,
                wsg_ref, wsu_ref, wsd_ref, base_ref, h2_ref, lg_ref):
    tm = x_ref.shape[0]
    mix = (jnp.dot(a_ref[...].astype(BF16), wo_ref[:D_ATTN, :], preferred_element_type=F32)
           + jnp.dot(p_ref[...].astype(BF16), wo_ref[D_ATTN:, :], preferred_element_type=F32))
    x1 = x_ref[...] + g1_ref[0] * mix
    h2 = _rms(x1) * gf_ref[...] * (1.0 + sc_ref[0]) + sh_ref[0]
    for c in range(ROW_CHUNKS):
        h2_ref[pl.ds(c, tm, stride=ROW_CHUNKS), :] = h2[:, c * LANES:(c + 1) * LANES]
    h_hi, h_lo = _split_bf16(h2)
    w_hi, w_lo = wrh_ref[...], wrl_ref[...]
    lg = (jnp.dot(h_hi, w_hi, preferred_element_type=F32) + jnp.dot(h_lo, w_hi, preferred_element_type=F32)
          + jnp.dot(h_hi, w_lo, preferred_element_type=F32))
    if tm < LANES:
        lg = jnp.concatenate([lg, jnp.zeros((LANES - tm, LANES), F32)], axis=0)
    lg_ref[...] = lg.T[:N_EXPERTS, :tm]
    sg = jnp.dot(h_hi, wsg_ref[...], preferred_element_type=F32)
    su = jnp.dot(h_hi, wsu_ref[...], preferred_element_type=F32)
    hs = (sg * jax.nn.sigmoid(sg) * su).astype(BF16)
    base_ref[...] = x1 + g2_ref[0] * jnp.dot(hs, wsd_ref[...], preferred_element_type=F32)


def _out_proj(x2d, attn, pool, gate1, shift2, scale2, gate2, w_out_bf, g_ffn, w_router_hi, w_router_lo,
              ws_gate_bf, ws_up_bf, ws_down_bf, tm, rows_per_mod, h2_tokens, h2_all, h2_block0):
    m = x2d.shape[0]
    r = gate1.shape[1]
    n_main = m // tm
    h2_rows = h2_tokens * ROW_CHUNKS
    n_fill = 0 if h2_all is not None else pl.cdiv(h2_rows - m * ROW_CHUNKS, tm * ROW_CHUNKS)
    last = n_main - 1
    row = lambda i: jnp.minimum(i, last)
    mod_spec = pl.BlockSpec((1, r, D_MODEL), lambda i: (row(i) // rows_per_mod, 0, 0))
    const = lambda shape: pl.BlockSpec(shape, lambda i: (0,) * len(shape))
    in_specs = [
        pl.BlockSpec((tm, D_MODEL), lambda i: (row(i), 0)),
        pl.BlockSpec((tm, D_ATTN), lambda i: (row(i), 0)),
        pl.BlockSpec((tm, D_POOL), lambda i: (row(i), 0)),
        mod_spec, mod_spec, mod_spec, mod_spec,
        const((D_MODEL, D_MODEL)),
        const((1, D_MODEL)),
        const((D_MODEL, LANES)), const((D_MODEL, LANES)),
        const((D_MODEL, D_SHARED)), const((D_MODEL, D_SHARED)), const((D_SHARED, D_MODEL)),
    ]
    args = [x2d, attn, pool, gate1, shift2, scale2, gate2, w_out_bf, g_ffn, w_router_hi, w_router_lo,
            ws_gate_bf, ws_up_bf, ws_down_bf]
    aliases = {}
    n_in = len(args)
    if h2_all is not None:
        in_specs.append(pl.BlockSpec(memory_space=pl.ANY))
        args.append(h2_all)
        aliases = {n_in: 1}

    def kernel(*refs):
        refs = refs[:n_in] + refs[len(args):]
        if n_fill == 0:
            _out_kernel(*refs)
            return
        step = pl.program_id(0)
        pl.when(step < n_main)(lambda: _out_kernel(*refs))

        @pl.when(step >= n_main)
        def _():
            h2_ref = refs[n_in + 1]
            h2_ref[...] = jnp.zeros(h2_ref.shape, F32)

    return pl.pallas_call(
        kernel,
        grid=(n_main + n_fill,),
        in_specs=in_specs,
        out_specs=[
            pl.BlockSpec((tm, D_MODEL), lambda i: (row(i), 0)),
            pl.BlockSpec((tm * ROW_CHUNKS, LANES), lambda i: (i + h2_block0, 0)),
            pl.BlockSpec((N_EXPERTS, tm), lambda i: (0, row(i))),
        ],
        out_shape=[
            jax.ShapeDtypeStruct((m, D_MODEL), F32),
            jax.ShapeDtypeStruct((h2_rows, LANES), F32),
            jax.ShapeDtypeStruct((N_EXPERTS, m), F32),
        ],
        input_output_aliases=aliases,
        compiler_params=_cparams(("arbitrary",)),
        name="out_proj",
    )(*args)


def _first_index_of_max(v, iota, size, axis):
    m = jnp.max(v, axis=axis, keepdims=True)
    idx = jnp.min(jnp.where(v == m, iota, size), axis=axis, keepdims=True)
    return m, idx


def _route_kernel(n_valid, lg_ref, b_ref, eidx_ref, wts_ref, cnt_ref, carry):
    i = pl.program_id(0)
    tn = lg_ref.shape[1]

    @pl.when(i == 0)
    def _():
        carry[...] = jnp.zeros(carry.shape, F32)

    scores = jax.nn.sigmoid(lg_ref[...])
    biased = scores + b_ref[...]
    grp = biased.reshape(N_EXPERT_GROUPS, GROUP_SIZE, tn)
    io_g = lax.broadcasted_iota(I32, grp.shape, 1)
    m1, i1 = _first_index_of_max(grp, io_g, GROUP_SIZE, 1)
    m2 = jnp.max(jnp.where(io_g == i1, -jnp.inf, grp), axis=1, keepdims=True)
    gscore = (m1 + m2)[:, 0, :]
    io_n = lax.broadcasted_iota(I32, gscore.shape, 0)
    gsel = jnp.zeros(gscore.shape, jnp.bool_)
    for _ in range(TOPK_GROUPS):
        _, gi = _first_index_of_max(gscore, io_n, N_EXPERT_GROUPS, 0)
        hit = io_n == gi
        gsel = jnp.logical_or(gsel, hit)
        gscore = jnp.where(hit, -jnp.inf, gscore)
    emask = jnp.broadcast_to(gsel[:, None, :], grp.shape).reshape(N_EXPERTS, tn)
    cand = jnp.where(emask, biased, -jnp.inf)
    io_e = lax.broadcasted_iota(I32, cand.shape, 0)
    picked = jnp.zeros(cand.shape, jnp.bool_)
    eidx, sel = [], []
    for _ in range(TOP_K):
        _, ei = _first_index_of_max(cand, io_e, N_EXPERTS, 0)
        hit = io_e == ei
        eidx.append(ei)
        sel.append(jnp.sum(jnp.where(hit, scores, 0.0), axis=0, keepdims=True))
        picked = jnp.logical_or(picked, hit)
        cand = jnp.where(hit, -jnp.inf, cand)
    sel = jnp.concatenate(sel, axis=0)
    eidx = jnp.concatenate(eidx, axis=0)
    wts_ref[...] = sel / jnp.sum(sel, axis=0, keepdims=True) * ROUTED_SCALE
    eidx_ref[...] = eidx

    tok = i * tn + lax.broadcasted_iota(I32, cand.shape, 1)
    mask = jnp.logical_and(picked, tok < n_valid).astype(F32)
    total = carry[...] + jnp.sum(mask, axis=1, keepdims=True)
    carry[...] = total
    cnt_ref[...] = total.astype(I32)


def _route(logits_t, b_router, n_valid):
    mp = logits_t.shape[1]
    tn = ROUTE_TILE
    tok_spec = pl.BlockSpec((TOP_K, tn), lambda i: (0, i))
    return pl.pallas_call(
        functools.partial(_route_kernel, n_valid),
        grid=(mp // tn,),
        in_specs=[
            pl.BlockSpec((N_EXPERTS, tn), lambda i: (0, i)),
            pl.BlockSpec((N_EXPERTS, 1), lambda i: (0, 0)),
        ],
        out_specs=[tok_spec, tok_spec, pl.BlockSpec((N_EXPERTS, LANES), lambda i: (0, 0))],
        out_shape=[
            jax.ShapeDtypeStruct((TOP_K, mp), I32),
            jax.ShapeDtypeStruct((TOP_K, mp), F32),
            jax.ShapeDtypeStruct((N_EXPERTS, LANES), I32),
        ],
        scratch_shapes=[pltpu.VMEM((N_EXPERTS, LANES), F32)],
        compiler_params=_cparams(("arbitrary",)),
        name="route",
    )(logits_t, b_router.reshape(N_EXPERTS, 1))


TILE_USED = 1
TILE_FIRST = 2
TILE_WSLOT = 4
TILE_PREV_USED = 8
PAIR_BITS = 3
assert 1 << PAIR_BITS == TOP_K
DUMMY_TOKENS = 2 * EXPERT_TILE // TOP_K
GATHER_SLOTS = 3


def _experts_kernel(n_tok, te_ref, ts_ref, tn_ref, tf_ref, ne_ref, g_ref, s_ref,
                    h_hbm, wg_hbm, wu_hbm, wd_hbm, y_hbm,
                    xbuf, ybuf, xs, wg_f, wu_f, wd_f, wg_bf, wu_bf, wd_bf, gsem, ssem, wsem, zsem):
    j = pl.program_id(0)
    last = pl.num_programs(0) - 1
    tm = EXPERT_TILE
    rc = ROW_CHUNKS
    plane_tokens = n_tok + DUMMY_TOKENS
    slot = j % 2
    other = 1 - slot
    gslot = j % GATHER_SLOTS
    flags = tf_ref[j]
    used = (flags & TILE_USED) != 0
    first = (flags & TILE_FIRST) != 0
    prev_used = (flags & TILE_PREV_USED) != 0
    wslot = (flags // TILE_WSLOT) & 1
    prv = jnp.maximum(j - 1, 0)
    ahead = jnp.minimum(j + GATHER_SLOTS - 1, last)

    def buf_rows(i):
        return pl.ds(i * rc, rc) if isinstance(i, int) else pl.ds(pl.multiple_of(i * rc, rc), rc)

    def gather_row(i, start, dst):
        row = g_ref[start + i]
        pltpu.make_async_copy(h_hbm.at[pl.ds(pl.multiple_of(row, rc), rc), :],
                              xbuf.at[dst, buf_rows(i), :], gsem.at[dst]).start(priority=0)

    def scatter_row(i, start, n_valid, src):
        spare = ((i & (TOP_K - 1)) * plane_tokens + n_tok + (i >> PAIR_BITS)) * rc + src * ((tm >> PAIR_BITS) * rc)
        row = jnp.where(i < n_valid, s_ref[start + i], spare)
        pltpu.make_async_copy(ybuf.at[src, buf_rows(i), :],
                              y_hbm.at[pl.ds(pl.multiple_of(row, rc), rc), :],
                              ssem.at[src]).start(priority=i % 2 if isinstance(i, int) else 0)

    def scatter_tile_loop(tile, src):
        start, n_valid = ts_ref[tile], tn_ref[tile]
        lax.fori_loop(0, tm, lambda i, c: (scatter_row(i, start, n_valid, src), c)[1], 0)

    def wait_gather(dst):
        pltpu.make_async_copy(h_hbm.at[pl.ds(0, tm * rc), :], xbuf.at[dst], gsem.at[dst]).wait()

    def wait_scatter(src):
        pltpu.make_async_copy(ybuf.at[src], y_hbm.at[pl.ds(0, tm * rc), :], ssem.at[src]).wait()

    def weight_copies(e, ws):
        return [pltpu.make_async_copy(src.at[e], dst.at[ws], wsem.at[ws])
                for src, dst in ((wg_hbm, wg_f), (wu_hbm, wu_f), (wd_hbm, wd_f))]

    @pl.when(j == 0)
    def _():
        ybuf[1] = jnp.zeros(ybuf.shape[1:], ybuf.dtype)
        fills = [pltpu.make_async_copy(ybuf.at[1, pl.ds(0, DUMMY_TOKENS * rc), :],
                                       y_hbm.at[pl.ds((k * plane_tokens + n_tok) * rc, DUMMY_TOKENS * rc), :],
                                       zsem.at[0]) for k in range(TOP_K)]
        for c in fills:
            c.start()
        for c in fills:
            c.wait()
        for c in weight_copies(te_ref[0], 0):
            c.start(priority=1)
        for t in range(GATHER_SLOTS - 1):
            start_t = ts_ref[jnp.minimum(t, last)]
            lax.fori_loop(0, tm, lambda i, c, start_t=start_t, t=t: (gather_row(i, start_t, t), c)[1], 0)

    @pl.when(first)
    def _():
        for c in weight_copies(te_ref[j], wslot):
            c.wait()
        wg_bf[...] = wg_f[wslot].astype(BF16)
        wu_bf[...] = wu_f[wslot].astype(BF16)
        wd_bf[...] = wd_f[wslot].astype(BF16)
        nxt_e = ne_ref[j]

        @pl.when(nxt_e >= 0)
        def _():
            for c in weight_copies(nxt_e, 1 - wslot):
                c.start(priority=1)

    @pl.when(jnp.logical_and(used, j > 0))
    def _():
        wait_scatter(slot)

    @pl.when(used)
    def _():
        wait_gather(gslot)
        xs[...] = jnp.concatenate(
            [xbuf[gslot, pl.ds(c, tm, stride=rc), :] for c in range(rc)], axis=1).astype(BF16)
        start_n = ts_ref[ahead]
        ahead_slot = (j + GATHER_SLOTS - 1) % GATHER_SLOTS
        for i in range(tm):
            gather_row(i, start_n, ahead_slot)
        start_p = ts_ref[prv]
        n_valid_p = jnp.where(j > 0, tn_ref[prv], 0)
        for i in range(tm):
            scatter_row(i, start_p, n_valid_p, other)
        x = xs[...]
        hg = jnp.dot(x, wg_bf[...], preferred_element_type=F32)
        hu = jnp.dot(x, wu_bf[...], preferred_element_type=F32)
        h = (hg * jax.nn.sigmoid(hg) * hu).astype(BF16)
        y = jnp.dot(h, wd_bf[...], preferred_element_type=F32)
        for c in range(rc):
            ybuf[slot, pl.ds(c, tm, stride=rc), :] = y[:, c * LANES:(c + 1) * LANES]

    @pl.when(jnp.logical_and(jnp.logical_not(used), prev_used))
    def _():
        for t in range(GATHER_SLOTS - 1):
            wait_gather((j + t) % GATHER_SLOTS)
        wait_scatter(slot)
        scatter_tile_loop(prv, other)
        wait_scatter(other)

    @pl.when(jnp.logical_and(used, j == last))
    def _():
        for t in range(1, GATHER_SLOTS):
            wait_gather((j + t) % GATHER_SLOTS)
        wait_scatter(other)
        scatter_tile_loop(j, slot)
        wait_scatter(slot)


def _experts(tile_expert, tile_start, tile_rows, tile_flags, next_expert, gather_rows, scatter_rows, h2_all,
             w_gate, w_up, w_down, n_tok):
    n_tiles = tile_expert.shape[0]
    tm = EXPERT_TILE
    any_spec = pl.BlockSpec(memory_space=pl.ANY)
    grid_spec = pltpu.PrefetchScalarGridSpec(
        num_scalar_prefetch=7,
        grid=(n_tiles,),
        in_specs=[any_spec] * 4,
        out_specs=any_spec,
        scratch_shapes=[
            pltpu.VMEM((GATHER_SLOTS, tm * ROW_CHUNKS, LANES), F32),
            pltpu.VMEM((2, tm * ROW_CHUNKS, LANES), F32),
            pltpu.VMEM((tm, D_MODEL), BF16),
            pltpu.VMEM((2, D_MODEL, D_EXPERT), F32),
            pltpu.VMEM((2, D_MODEL, D_EXPERT), F32),
            pltpu.VMEM((2, D_EXPERT, D_MODEL), F32),
            pltpu.VMEM((D_MODEL, D_EXPERT), BF16),
            pltpu.VMEM((D_MODEL, D_EXPERT), BF16),
            pltpu.VMEM((D_EXPERT, D_MODEL), BF16),
            pltpu.SemaphoreType.DMA((GATHER_SLOTS,)),
            pltpu.SemaphoreType.DMA((2,)),
            pltpu.SemaphoreType.DMA((2,)),
            pltpu.SemaphoreType.DMA((1,)),
        ],
    )
    return pl.pallas_call(
        functools.partial(_experts_kernel, n_tok),
        grid_spec=grid_spec,
        out_shape=jax.ShapeDtypeStruct((TOP_K * (n_tok + DUMMY_TOKENS) * ROW_CHUNKS, LANES), F32),
        compiler_params=_cparams(("arbitrary",)),
        name="experts",
    )(tile_expert, tile_start, tile_rows, tile_flags, next_expert, gather_rows, scatter_rows, h2_all,
      w_gate, w_up, w_down)


COMBINE_UNROLL = 4


def _combine_kernel(w_ref, y_ref, base_ref, g2_ref, o_ref, acc_rows):
    i = pl.program_id(0)
    tc = base_ref.shape[0]
    rc = ROW_CHUNKS

    def body(g, carry):
        for u in range(COMBINE_UNROLL):
            t = g * COMBINE_UNROLL + u
            rows = pl.ds(pl.multiple_of(t * rc, rc), rc)
            w0 = (i * tc + t) * TOP_K
            acc = w_ref[w0] * y_ref[0, rows, :]
            for k in range(1, TOP_K):
                acc = acc + w_ref[w0 + k] * y_ref[k, rows, :]
            acc_rows[rows, :] = acc
        return carry

    lax.fori_loop(0, tc // COMBINE_UNROLL, body, 0)
    g2 = g2_ref[0]
    for c in range(rc):
        sl = slice(c * LANES, (c + 1) * LANES)
        o_ref[:, sl] = base_ref[:, sl] + g2[:, sl] * acc_rows[pl.ds(c, tc, stride=rc), :]


def _combine(y_planes, wts, base, gate2, tc, rows_per_mod, block0):
    m = base.shape[0]
    r = gate2.shape[1]
    assert tc % COMBINE_UNROLL == 0
    grid_spec = pltpu.PrefetchScalarGridSpec(
        num_scalar_prefetch=1,
        grid=(m // tc,),
        in_specs=[
            pl.BlockSpec((TOP_K, tc * ROW_CHUNKS, LANES), lambda i, w: (0, i + block0, 0)),
            pl.BlockSpec((tc, D_MODEL), lambda i, w: (i, 0)),
            pl.BlockSpec((1, r, D_MODEL), lambda i, w: (i // rows_per_mod, 0, 0)),
        ],
        out_specs=pl.BlockSpec((tc, D_MODEL), lambda i, w: (i, 0)),
        scratch_shapes=[pltpu.VMEM((tc * ROW_CHUNKS, LANES), F32)],
    )
    return pl.pallas_call(
        _combine_kernel,
        grid_spec=grid_spec,
        out_shape=jax.ShapeDtypeStruct((m, D_MODEL), F32),
        compiler_params=_cparams(("parallel",)),
        name="combine",
    )(wts.reshape(-1), y_planes, base, gate2)


def _moe_routed(h2_all, logits_t, b_router, w_gate, w_up, w_down):
    n_tok = logits_t.shape[1]
    mp = -(-n_tok // ROUTE_TILE) * ROUTE_TILE
    logits_t = jnp.pad(logits_t, ((0, 0), (0, mp - n_tok)))
    eidx, wts, counts = _route(logits_t, b_router, n_tok)
    eidx, wts, counts = eidx[:, :n_tok], wts[:, :n_tok], counts[:, 0]

    tm = EXPERT_TILE
    n_pairs = n_tok * TOP_K
    n_tiles = n_pairs // tm + N_EXPERTS
    pair_code = jnp.arange(n_tok, dtype=I32)[None, :] * TOP_K + jnp.arange(TOP_K, dtype=I32)[:, None]
    (sorted_pairs,) = lax.sort(((eidx * n_pairs + pair_code).reshape(-1),), is_stable=False)
    sorted_pairs = jnp.concatenate([sorted_pairs % n_pairs, jnp.zeros((tm,), I32)])
    pair_tok, pair_choice = sorted_pairs >> PAIR_BITS, sorted_pairs & (TOP_K - 1)
    gather_rows = pair_tok * ROW_CHUNKS
    scatter_rows = (pair_choice * (n_tok + DUMMY_TOKENS) + pair_tok) * ROW_CHUNKS
    tiles_e = (counts + tm - 1) // tm
    tile_end = jnp.cumsum(tiles_e)
    tile_begin = tile_end - tiles_e
    dense_begin = jnp.cumsum(counts) - counts
    tj = jnp.arange(n_tiles, dtype=I32)
    total_tiles = tile_end[-1]
    used = tj < total_tiles
    tj_used = jnp.minimum(tj, total_tiles - 1)[:, None]
    member = jnp.logical_and(tile_begin[None, :] <= tj_used, tj_used < tile_end[None, :])
    of_tile = lambda per_expert: jnp.sum(jnp.where(member, per_expert[None, :], 0), axis=1)
    e_ids = jnp.arange(N_EXPERTS, dtype=I32)
    te = of_tile(e_ids)
    local = tj - of_tile(tile_begin)
    ts = jnp.where(used, of_tile(dense_begin) + local * tm, 0).astype(I32)
    tn = jnp.where(used, jnp.clip(of_tile(counts) - local * tm, 0, tm), 0).astype(I32)
    first = jnp.logical_and(used, local == 0)
    busy = counts > 0
    wslot = of_tile((jnp.cumsum(busy.astype(I32)) - 1) % 2)
    later = jnp.logical_and(e_ids[None, :] > e_ids[:, None], busy[None, :])
    succ = jnp.min(jnp.where(later, e_ids[None, :], N_EXPERTS), axis=1)
    succ = jnp.where(succ == N_EXPERTS, -1, succ)
    ne = jnp.where(first, of_tile(succ), -1).astype(I32)
    prev_used = jnp.concatenate([jnp.zeros((1,), jnp.bool_), used[:-1]])
    tf = (used.astype(I32) * TILE_USED + first.astype(I32) * TILE_FIRST + wslot * TILE_WSLOT
          + prev_used.astype(I32) * TILE_PREV_USED)
    y_all = _experts(te, ts, tn, tf, ne, gather_rows, scatter_rows, h2_all, w_gate, w_up, w_down, n_tok)
    return y_all.reshape(TOP_K, (n_tok + DUMMY_TOKENS) * ROW_CHUNKS, LANES), wts.T


def kernel(x_prompt, x_sample, cache_k_win, cache_v_win, state_pool, c_prompt, c_sample, w_ada, b_ada, g_mix, w_in, q_norm, k_norm, w_pool, pool_scale, w_out, g_ffn, w_router, b_router, w_gate, w_up, w_down, ws_gate, ws_up, ws_down):
    depth = w_ada.shape[0]
    assert depth == 1
    l = 0
    nb, seq, _ = x_prompt.shape
    bd, dec_seq, _ = x_sample.shape
    assert dec_seq == 1 and seq % (TOKEN_TILE * 16) == 0
    n_prompt = nb * seq
    n_tok = n_prompt + bd
    n_keep = min(MAX_WINDOW, seq)
    tiles_per_batch = seq // TOKEN_TILE

    n_mod_rows = -(-(nb + bd) // 8) * 8
    c_all = jnp.concatenate([c_prompt, c_sample, jnp.zeros((n_mod_rows - nb - bd, D_MODEL), F32)], axis=0)
    mods = _ada(c_all, w_ada[l], b_ada[l])
    mods_p = [mods[:nb, j * D_MODEL:(j + 1) * D_MODEL].reshape(nb, 1, D_MODEL) for j in range(6)]
    mods_s = [mods[nb:nb + bd, j * D_MODEL:(j + 1) * D_MODEL].reshape(1, bd, D_MODEL) for j in range(6)]

    g_mix_l = g_mix[l].reshape(1, D_MODEL)
    g_ffn_l = g_ffn[l].reshape(1, D_MODEL)
    qn = q_norm[l].reshape(1, HEAD_DIM)
    kn = k_norm[l].reshape(1, HEAD_DIM)
    w_in_bf = w_in[l].astype(BF16)
    w_out_bf = w_out[l].astype(BF16)
    w_pool_bf = w_pool[l].astype(BF16)
    ps = pool_scale[l].reshape(1, D_POOL)
    w_router_pad = jnp.pad(w_router[l], ((0, 0), (0, LANES - N_EXPERTS)))
    w_router_hi = w_router_pad.astype(BF16)
    w_router_lo = (w_router_pad - w_router_hi.astype(F32)).astype(BF16)
    wsg_bf, wsu_bf, wsd_bf = ws_gate[l].astype(BF16), ws_up[l].astype(BF16), ws_down[l].astype(BF16)

    rope_p = _rope_tables(jnp.arange(seq, dtype=I32))
    rope_s = _rope_tables(jnp.full((bd,), PAST_LEN, I32))
    xp2 = x_prompt.reshape(n_prompt, D_MODEL)
    xs2 = x_sample.reshape(bd, D_MODEL)
    assert n_keep % TOKEN_TILE == 0
    qp, kp, vp, up, kwin_p, vwin_p = _in_proj(xp2, mods_p[0], mods_p[1], g_mix_l, w_in_bf, qn, kn, rope_p,
                                              TOKEN_TILE, tiles_per_batch, tiles_per_batch, n_keep // TOKEN_TILE)
    qs, _, _, us, k_new, v_new = _in_proj(xs2, mods_s[0], mods_s[1], g_mix_l, w_in_bf, qn, kn, rope_s, bd, 1, 1, 1)

    to_seq = lambda t: t.reshape(nb, seq, -1)
    heads = lambda t: t.reshape(bd, N_HEADS, HEAD_DIM)
    ck, cv = cache_k_win[l], cache_v_win[l]
    k_new, v_new = heads(k_new), heads(v_new)
    attn_p, k_win_s, v_win_s = _attn_prompt(to_seq(qp), to_seq(kp), to_seq(vp), ck, cv, k_new, v_new)
    attn_p = attn_p.reshape(n_prompt, D_ATTN)
    pool_p = _pool_prompt(to_seq(up), w_pool_bf, ps).reshape(n_prompt, D_POOL)
    attn_s = _attn_decode(heads(qs), k_new, v_new, ck, cv).reshape(bd, D_ATTN)
    pool_s = _pool_decode(us, jnp.swapaxes(state_pool[l], 0, 1), w_pool_bf, ps)

    base_p, h2_all, lg_p = _out_proj(xp2, attn_p, pool_p, mods_p[2], mods_p[3], mods_p[4], mods_p[5],
                                     w_out_bf, g_ffn_l, w_router_hi, w_router_lo, wsg_bf, wsu_bf, wsd_bf,
                                     TOKEN_TILE, tiles_per_batch, n_tok, None, 0)
    base_s, h2_all, lg_s = _out_proj(xs2, attn_s, pool_s, mods_s[2], mods_s[3], mods_s[4], mods_s[5],
                                     w_out_bf, g_ffn_l, w_router_hi, w_router_lo, wsg_bf, wsu_bf, wsd_bf,
                                     bd, 1, n_tok, h2_all, n_prompt // bd)

    y_planes, wts_t = _moe_routed(h2_all, jnp.concatenate([lg_p, lg_s], axis=1), b_router[l],
                                  w_gate[l], w_up[l], w_down[l])

    yp = _combine(y_planes, wts_t[:n_prompt], base_p, mods_p[5], COMBINE_TILE, seq // COMBINE_TILE, 0)
    ys = _combine(y_planes, wts_t[n_prompt:n_tok], base_s, mods_s[5], bd, 1, n_prompt // bd)

    y_prompt = yp.reshape(nb, seq, D_MODEL)
    y_sample = ys.reshape(bd, dec_seq, D_MODEL)
    win = lambda t: t.reshape(1, nb, n_keep, N_HEADS, HEAD_DIM)
    pool_p_state = to_seq(up)[:, seq - POOL_STATE:][None]
    pool_s_state = jnp.concatenate([state_pool[l][:, 1:], us[:, None, :]], axis=1)[None]
    return (y_prompt, y_sample, win(kwin_p), win(vwin_p), pool_p_state,
            k_win_s[None], v_win_s[None], pool_s_state)
```

```python
import functools

import jax
import jax.numpy as jnp
from jax import lax
from jax.experimental import pallas as pl
from jax.experimental.pallas import tpu as pltpu

F32 = jnp.float32
BF16 = jnp.bfloat16
I32 = jnp.int32

D_MODEL = 2048
N_HEADS = 8
HEAD_DIM = 128
D_ATTN = N_HEADS * HEAD_DIM
D_POOL = D_MODEL - D_ATTN
D_IN = 3 * D_ATTN + D_POOL
ROPE_DIM = HEAD_DIM // 4
ROPE_HALF = ROPE_DIM // 2
ROPE_THETA = 500000.0
DILATED_GROUPS = ((128, 1), (512, 4), (2048, 16))
QBLOCK = 128
ATTN_SCALE = HEAD_DIM ** -0.5
POOL_WINDOWS = (2, 4, 8, 16)
POOL_GROUP_DIM = D_POOL // len(POOL_WINDOWS)
POOL_STATE = max(POOL_WINDOWS) - 1
POOL_HALO = 16
N_EXPERTS = 64
TOP_K = 8
N_EXPERT_GROUPS = 8
GROUP_SIZE = N_EXPERTS // N_EXPERT_GROUPS
TOPK_GROUPS = 4
D_EXPERT = 512
D_SHARED = 512
ROUTED_SCALE = 2.5
EPS = 1e-6
PAST_LEN = 16384
MAX_WINDOW = 2048

LANES = 128
ROW_CHUNKS = D_MODEL // LANES
NEG_BIG = -1e30
LOG2_E = 1.4426950408889634

TOKEN_TILE = 256
EXPERT_TILE = 256
COMBINE_TILE = 128
ROUTE_TILE = 256
ADA_COL_TILE = 1024
POOL_TILE = 512
VMEM_LIMIT = 60 * 1024 * 1024


def _cparams(sem, vmem=VMEM_LIMIT):
    return pltpu.CompilerParams(dimension_semantics=sem, vmem_limit_bytes=vmem)


def _ada_kernel(c_ref, w_ref, b_ref, o_ref):
    c = c_ref[...]
    s = (c * jax.nn.sigmoid(c)).astype(BF16)
    o_ref[...] = jnp.dot(s, w_ref[...].astype(BF16), preferred_element_type=F32) + b_ref[...]


def _ada(c_all, w_ada, b_ada):
    rows = c_all.shape[0]
    n = w_ada.shape[1]
    tn = ADA_COL_TILE
    return pl.pallas_call(
        _ada_kernel,
        grid=(n // tn,),
        in_specs=[
            pl.BlockSpec((rows, D_MODEL), lambda j: (0, 0)),
            pl.BlockSpec((D_MODEL, tn), lambda j: (0, j)),
            pl.BlockSpec((1, tn), lambda j: (0, j)),
        ],
        out_specs=pl.BlockSpec((rows, tn), lambda j: (0, j)),
        out_shape=jax.ShapeDtypeStruct((rows, n), F32),
        compiler_params=_cparams(("parallel",)),
        name="ada_modulation",
    )(c_all, w_ada, b_ada.reshape(1, n))


def _rms(x):
    return x * lax.rsqrt(jnp.mean(x * x, axis=-1, keepdims=True) + EPS)


def _in_kernel(x_ref, sh_ref, sc_ref, g_ref, w_ref, qn_ref, kn_ref, c_ref, s1_ref, s2_ref,
               q_ref, k_ref, v_ref, u_ref, kw_ref, vw_ref):
    tm = x_ref.shape[0]
    x = x_ref[...]
    h = _rms(x) * g_ref[...] * (1.0 + sc_ref[0]) + sh_ref[0]
    hb = h.astype(BF16)
    cos = c_ref[...]
    s1 = s1_ref[...]
    s2 = s2_ref[...]

    def head_rows(win_ref, hd, val):
        win_ref[pl.ds(hd, tm, stride=N_HEADS), :] = val

    def qk(sec, nrm, out_ref, win_ref):
        z = jnp.dot(hb, w_ref[:, sec * D_ATTN:(sec + 1) * D_ATTN], preferred_element_type=F32)
        for hd in range(N_HEADS):
            sl = slice(hd * HEAD_DIM, (hd + 1) * HEAD_DIM)
            r = _rms(z[:, sl]) * nrm
            r = r * cos + pltpu.roll(r, HEAD_DIM - ROPE_HALF, 1) * s1 + pltpu.roll(r, ROPE_HALF, 1) * s2
            out_ref[:, sl] = r
            if win_ref is not None:
                head_rows(win_ref, hd, r)

    qk(0, qn_ref[...], q_ref, None)
    qk(1, kn_ref[...], k_ref, kw_ref)
    v = jnp.dot(hb, w_ref[:, 2 * D_ATTN:3 * D_ATTN], preferred_element_type=F32)
    v_ref[...] = v
    for hd in range(N_HEADS):
        head_rows(vw_ref, hd, v[:, hd * HEAD_DIM:(hd + 1) * HEAD_DIM])
    u_ref[...] = jnp.dot(hb, w_ref[:, 3 * D_ATTN:], preferred_element_type=F32)


def _in_proj(x2d, shift, scale, g_mix, w_in_bf, q_norm, k_norm, rope, tm, rows_per_mod, rope_tiles, keep_tiles):
    m = x2d.shape[0]
    r = shift.shape[1]
    n_groups = m // tm // rows_per_mod
    skip = rows_per_mod - keep_tiles
    mod_spec = pl.BlockSpec((1, r, D_MODEL), lambda i: (i // rows_per_mod, 0, 0))
    rope_spec = pl.BlockSpec((tm, HEAD_DIM), lambda i: (i % rope_tiles, 0))
    out_spec = pl.BlockSpec((tm, D_ATTN), lambda i: (i, 0))
    out_sd = jax.ShapeDtypeStruct((m, D_ATTN), F32)
    win_spec = pl.BlockSpec(
        (tm * N_HEADS, HEAD_DIM),
        lambda i: ((i // rows_per_mod) * keep_tiles + jnp.maximum(i % rows_per_mod - skip, 0), 0))
    win_sd = jax.ShapeDtypeStruct((n_groups * keep_tiles * tm * N_HEADS, HEAD_DIM), F32)
    return pl.pallas_call(
        _in_kernel,
        grid=(m // tm,),
        in_specs=[
            pl.BlockSpec((tm, D_MODEL), lambda i: (i, 0)),
            mod_spec, mod_spec,
            pl.BlockSpec((1, D_MODEL), lambda i: (0, 0)),
            pl.BlockSpec((D_MODEL, D_IN), lambda i: (0, 0)),
            pl.BlockSpec((1, HEAD_DIM), lambda i: (0, 0)),
            pl.BlockSpec((1, HEAD_DIM), lambda i: (0, 0)),
            rope_spec, rope_spec, rope_spec,
        ],
        out_specs=[out_spec] * 4 + [win_spec] * 2,
        out_shape=[out_sd] * 4 + [win_sd] * 2,
        compiler_params=_cparams(("arbitrary",)),
        name="in_proj",
    )(x2d, shift, scale, g_mix, w_in_bf, q_norm, k_norm, *rope)


def _rope_tables(pos):
    inv_freq = jnp.float32(ROPE_THETA) ** (-jnp.arange(ROPE_HALF, dtype=F32) / ROPE_HALF)
    ang = pos.astype(F32)[:, None] * inv_freq[None, :]
    cos, sin = jnp.cos(ang), jnp.sin(ang)
    t = pos.shape[0]
    rest = HEAD_DIM - ROPE_DIM
    c = jnp.concatenate([cos, cos, jnp.ones((t, rest), F32)], axis=1)
    s1 = jnp.concatenate([-sin, jnp.zeros((t, ROPE_HALF + rest), F32)], axis=1)
    s2 = jnp.concatenate([jnp.zeros((t, ROPE_HALF), F32), sin, jnp.zeros((t, rest), F32)], axis=1)
    return c, s1, s2


ATTN_INTERLEAVE = 4
WINDOW_ROWS_PER_STEP = 2
REGROUP_MIN_DILATION = 16


REGROUP_STRIDE = 4


def _to_residue_major(read_rows, dst, tmp, seq, sub_len):
    n, part = REGROUP_STRIDE, seq // REGROUP_STRIDE
    for a in range(n):
        tmp[pl.ds(a * part, part), :] = read_rows(pl.ds(a, part, stride=n))
    for a in range(n):
        for s in range(n):
            dst[pl.ds((s * n + a) * sub_len, sub_len), :] = tmp[pl.ds(a * part + s, sub_len, stride=n), :]


def _from_residue_major(src, dst, tmp, seq, sub_len):
    n, part = REGROUP_STRIDE, seq // REGROUP_STRIDE
    for a in range(n):
        for s in range(n):
            tmp[pl.ds(a * part + s, sub_len, stride=n), :] = src[pl.ds((s * n + a) * sub_len, sub_len), :]
    for a in range(n):
        dst[pl.ds(a, part, stride=n), :] = tmp[pl.ds(a * part, part), :]


def _attn_prompt_kernel(q_ref, k_ref, v_ref, ck_hbm, cv_hbm, kn_ref, vn_ref, o_ref, ok_hbm, ov_hbm,
                        m_scr, l_scr, acc_scr, q_rm, k_rm, v_rm, m_rm, l_rm, a_rm, wbuf, isem, osem):
    seq = q_ref.shape[1]
    half = wbuf.shape[2]
    step = pl.program_id(0) * pl.num_programs(1) + pl.program_id(1)
    last_step = pl.num_programs(0) * pl.num_programs(1) - 1
    windows = ((ck_hbm, kn_ref, ok_hbm), (cv_hbm, vn_ref, ov_hbm))
    row0 = step * WINDOW_ROWS_PER_STEP
    n_bounds = 2 * WINDOW_ROWS_PER_STEP
    assert n_bounds == len(DILATED_GROUPS) + 1

    def chunk_in(row, h, a):
        n_rows = half - h
        return pltpu.make_async_copy(windows[a][0].at[row, pl.ds(h * half + 1, n_rows)],
                                     wbuf.at[h, a, pl.ds(0, n_rows)], isem.at[h, a])

    def chunk_out(row, h, a):
        return pltpu.make_async_copy(wbuf.at[h, a], windows[a][2].at[row, pl.ds(h * half, half)], osem.at[h, a])

    def finish_in_start_out(row, h):
        for a in range(2):
            chunk_in(row, h, a).wait()
            if h == 1:
                wbuf[h, a, half - 1] = windows[a][1][row]
            chunk_out(row, h, a).start()

    def finish_out(row, h):
        for a in range(2):
            chunk_out(row, h, a).wait()

    def window_boundary(b):
        h = b % 2
        if b == 0:
            @pl.when(step > 0)
            def _():
                finish_in_start_out(row0 - 1, 1)
                finish_out(row0 - 1, 0)
        elif b == 1:
            finish_in_start_out(row0, 0)
            pl.when(step > 0)(lambda: finish_out(row0 - 1, 1))
        else:
            finish_in_start_out(row0 + (b - 1) // 2, (b - 1) % 2)
            finish_out(row0 + (b - 2) // 2, h)
        for a in range(2):
            chunk_in(row0 + b // 2, h, a).start()

    window_phases = [functools.partial(window_boundary, b) for b in range(n_bounds)]
    window_phases[0]()
    m_scr[...] = jnp.full(m_scr.shape, NEG_BIG, F32)
    l_scr[...] = jnp.zeros(l_scr.shape, F32)
    acc_scr[...] = jnp.zeros(acc_scr.shape, F32)
    qi = lax.broadcasted_iota(I32, (QBLOCK, 2 * QBLOCK), 0)
    kj = lax.broadcasted_iota(I32, (QBLOCK, 2 * QBLOCK), 1)
    dist = qi + QBLOCK - kj
    band_mask = jnp.logical_and(dist >= 0, dist <= QBLOCK)
    bias_band = jnp.where(band_mask, 0.0, NEG_BIG)
    bias_first = jnp.where(jnp.logical_and(band_mask, kj >= QBLOCK), 0.0, NEG_BIG)
    ones = jnp.ones((2 * QBLOCK, HEAD_DIM), BF16)
    dims = (((1,), (1,)), ((), ()))
    q_scale = ATTN_SCALE * LOG2_E

    regrouped = None
    for window, dil in DILATED_GROUPS:
        assert window // dil == QBLOCK
        n_units = seq // QBLOCK
        assert n_units % ATTN_INTERLEAVE == 0 and (dil % ATTN_INTERLEAVE == 0 or dil == 1)

        def rows(start, dil=dil):
            return pl.ds(start, QBLOCK) if dil == 1 else pl.ds(start, QBLOCK, stride=dil)

        sub_len = seq // dil
        regroup = dil >= REGROUP_MIN_DILATION
        if regroup:
            assert dil == REGROUP_STRIDE ** 2 and regrouped is None
            regrouped = (dil, sub_len)
            for src, dst in ((q_ref, q_rm), (k_ref, k_rm), (v_ref, v_rm)):
                _to_residue_major(lambda rows_, src=src: src[0, rows_, :], dst, m_rm, seq, sub_len)
            m_rm[...] = jnp.full(m_rm.shape, NEG_BIG, F32)
            l_rm[...] = jnp.zeros(l_rm.shape, F32)
            a_rm[...] = jnp.zeros(a_rm.shape, F32)

        def qkv_blocks(blk, res, start, prev, rows=rows, regroup=regroup, sub_len=sub_len):
            if regroup:
                cur = pl.ds(res * sub_len + blk * QBLOCK, QBLOCK)
                prv = pl.ds(res * sub_len + jnp.maximum(blk - 1, 0) * QBLOCK, QBLOCK)
                return (q_rm[cur, :], k_rm[prv, :], k_rm[cur, :], v_rm[prv, :], v_rm[cur, :])
            return (q_ref[0, rows(start), :], k_ref[0, rows(prev), :], k_ref[0, rows(start), :],
                    v_ref[0, rows(prev), :], v_ref[0, rows(start), :])

        state_refs = (m_rm, l_rm, a_rm) if regroup else (m_scr, l_scr, acc_scr)

        def state_rows(blk, res, start, rows=rows, regroup=regroup, sub_len=sub_len):
            return pl.ds(res * sub_len + blk * QBLOCK, QBLOCK) if regroup else rows(start)

        def body(it, carry, dil=dil, rows=rows, qkv_blocks=qkv_blocks, state_refs=state_refs, state_rows=state_rows):
            m_st, l_st, a_st = state_refs
            loaded = []
            for u in range(ATTN_INTERLEAVE):
                f = it * ATTN_INTERLEAVE + u
                blk, res = f // dil, f % dil
                start = blk * (QBLOCK * dil) + res
                prev = jnp.maximum(start - QBLOCK * dil, res)
                q_blk, k_prev, k_cur, v_prev, v_cur = qkv_blocks(blk, res, start, prev)
                q = (q_blk * q_scale).astype(BF16)
                k = jnp.concatenate([k_prev, k_cur], axis=0).astype(BF16)
                v = jnp.concatenate([v_prev, v_cur], axis=0).astype(BF16)
                v_ones = jnp.concatenate([v, ones], axis=1)
                st = state_rows(blk, res, start)
                state = (m_st[st, :], l_st[st, :], a_st[st, :])
                loaded.append((st, blk, q, k, v_ones, state))
            scores = [lax.dot_general(q, k, dims, preferred_element_type=F32)
                      + jnp.where(blk > 0, bias_band, bias_first)
                      for _, blk, q, k, _, _ in loaded]
            probs = []
            for s, (_, _, _, _, _, (m_old, _, _)) in zip(scores, loaded):
                m_new = jnp.maximum(m_old, jnp.max(s, axis=1, keepdims=True))
                p = jnp.exp2(s - jnp.concatenate([m_new, m_new], axis=1)).astype(BF16)
                probs.append((m_new, jnp.exp2(m_old - m_new), p))
            updated = []
            for (m_new, alpha, p), (start, _, _, _, v_ones, (_, l_old, a_old)) in zip(probs, loaded):
                pv = jnp.dot(p, v_ones, preferred_element_type=F32)
                updated.append((start, m_new, alpha * l_old + pv[:, HEAD_DIM:], alpha * a_old + pv[:, :HEAD_DIM]))
            for st, m_new, l_new, a_new in updated:
                m_st[st, :] = m_new
                l_st[st, :] = l_new
                a_st[st, :] = a_new
            return carry

        trips = n_units // ATTN_INTERLEAVE
        phase = window_phases[DILATED_GROUPS.index((window, dil)) + 1]
        if (window, dil) == DILATED_GROUPS[-1]:
            lax.fori_loop(0, trips // 2, body, 0)
            phase()
            lax.fori_loop(trips // 2, trips, body, 0)
        else:
            lax.fori_loop(0, trips, body, 0)
            phase()

    @pl.when(step == last_step)
    def _():
        last_row = row0 + WINDOW_ROWS_PER_STEP - 1
        finish_in_start_out(last_row, 1)
        finish_out(last_row, 0)
        finish_out(last_row, 1)

    if regrouped is None:
        o_ref[0] = acc_scr[...] / l_scr[...]
    else:
        _, sub_len = regrouped
        _from_residue_major(a_rm, q_rm, k_rm, seq, sub_len)
        _from_residue_major(l_rm, k_rm, a_rm, seq, sub_len)
        _from_residue_major(m_rm, v_rm, a_rm, seq, sub_len)
        m_a, m_b = m_scr[...], v_rm[...]
        m_all = jnp.maximum(m_a, m_b)
        w_a, w_b = jnp.exp2(m_a - m_all), jnp.exp2(m_b - m_all)
        o_ref[0] = (acc_scr[...] * w_a + q_rm[...] * w_b) / (l_scr[...] * w_a + k_rm[...] * w_b)


def _attn_prompt(q, k, v, cache_k, cache_v, k_new, v_new):
    b, s, _ = q.shape
    bd, n_buf = cache_k.shape[0], cache_k.shape[1]
    assert bd == b * N_HEADS * WINDOW_ROWS_PER_STEP
    spec = pl.BlockSpec((1, s, HEAD_DIM), lambda bi, hi: (bi, 0, hi))
    any_spec = pl.BlockSpec(memory_space=pl.ANY)
    new_spec = pl.BlockSpec(k_new.shape, lambda bi, hi: (0, 0, 0))
    win_sd = jax.ShapeDtypeStruct(cache_k.shape, cache_k.dtype)
    return pl.pallas_call(
        _attn_prompt_kernel,
        grid=(b, N_HEADS),
        in_specs=[spec, spec, spec, any_spec, any_spec, new_spec, new_spec],
        out_specs=[spec, any_spec, any_spec],
        out_shape=[jax.ShapeDtypeStruct((b, s, D_ATTN), F32), win_sd, win_sd],
        scratch_shapes=[pltpu.VMEM((s, HEAD_DIM), F32)] * 9 + [
            pltpu.VMEM((2, 2, n_buf // 2) + cache_k.shape[2:], F32),
            pltpu.SemaphoreType.DMA((2, 2)),
            pltpu.SemaphoreType.DMA((2, 2)),
        ],
        compiler_params=_cparams(("arbitrary", "arbitrary")),
        name="attn_prompt",
    )(q, k, v, cache_k, cache_v, k_new, v_new)


def _attn_decode_kernel(q_ref, kn_ref, vn_ref, k1_ref, k2_ref, k3_ref, v1_ref, v2_ref, v3_ref, o_ref):
    q = q_ref[0]
    k_new = kn_ref[0]
    v_new = vn_ref[0]
    n_grp = len(DILATED_GROUPS)

    def scores(kb):
        return jnp.sum(kb * q[None], axis=-1, keepdims=True) * ATTN_SCALE

    kcs = [k1_ref[0], k2_ref[0][:, 0], k3_ref[0][:, 0]]
    vcs = [v1_ref[0], v2_ref[0][:, 0], v3_ref[0][:, 0]]
    s_new = jnp.sum(k_new * q, axis=-1, keepdims=True) * ATTN_SCALE
    s_grp = [scores(kb) for kb in kcs]
    m = s_new
    for s in s_grp:
        m = jnp.maximum(m, jnp.max(s, axis=0))
    p_new = jnp.exp(s_new - m)
    den = n_grp * p_new
    num = n_grp * p_new * v_new
    for s, vb in zip(s_grp, vcs):
        p = jnp.exp(s - m[None])
        den = den + jnp.sum(p, axis=0)
        num = num + jnp.sum(p * vb, axis=0)
    o_ref[0] = num / den


def _attn_decode(q, k_new, v_new, cache_k, cache_v):
    bd, n_buf = cache_k.shape[0], cache_k.shape[1]
    assert n_buf == MAX_WINDOW

    def views(cache):
        out, specs = [], []
        for window, dil in DILATED_GROUPS:
            band = window // dil
            assert band == QBLOCK and n_buf % dil == 0 and n_buf // dil >= band
            if dil == 1:
                out.append(cache)
                specs.append(pl.BlockSpec((1, band, N_HEADS, HEAD_DIM),
                                          lambda b, nb=n_buf // band: (b, nb - 1, 0, 0)))
            else:
                out.append(cache.reshape(bd, n_buf // dil, dil, N_HEADS, HEAD_DIM))
                specs.append(pl.BlockSpec((1, band, 1, N_HEADS, HEAD_DIM),
                                          lambda b, nb=n_buf // dil // band: (b, nb - 1, 0, 0, 0)))
        return out, specs

    kv, kspecs = views(cache_k)
    vv, vspecs = views(cache_v)
    tok = pl.BlockSpec((1, N_HEADS, HEAD_DIM), lambda b: (b, 0, 0))
    return pl.pallas_call(
        _attn_decode_kernel,
        grid=(bd,),
        in_specs=[tok, tok, tok] + kspecs + vspecs,
        out_specs=tok,
        out_shape=jax.ShapeDtypeStruct((bd, N_HEADS, HEAD_DIM), F32),
        compiler_params=_cparams(("parallel",)),
        name="attn_decode",
    )(q, k_new, v_new, *kv, *vv)


def _pool_project(d_groups, wp_ref, ps_ref, out_ref):
    for g, d in enumerate(d_groups):
        sl = slice(g * POOL_GROUP_DIM, (g + 1) * POOL_GROUP_DIM)
        out_ref[:, sl] = jnp.dot(d.astype(BF16), wp_ref[g], preferred_element_type=F32) * ps_ref[:, sl]


def _pool_prompt_kernel(u_ref, prev_ref, wp_ref, ps_ref, o_ref):
    i = pl.program_id(1)
    tp = u_ref.shape[1]
    u = u_ref[0]
    prev = jnp.where(i > 0, prev_ref[0], 0.0)
    ext = jnp.concatenate([prev, u], axis=0)
    pos = i * tp + lax.broadcasted_iota(I32, (tp, 1), 0)
    d_groups = []
    for g, w in enumerate(POOL_WINDOWS):
        sl = slice(g * POOL_GROUP_DIM, (g + 1) * POOL_GROUP_DIM)
        a = ext[:, sl]
        span = 1
        while span < w:
            a = a + pltpu.roll(a, span, 0)
            span *= 2
        win = a[POOL_HALO:, :]
        cnt = jnp.minimum(w, pos + 1).astype(F32)
        d_groups.append(win / cnt - u[:, sl])
    _pool_project(d_groups, wp_ref, ps_ref, o_ref.at[0])


def _pool_prompt(u, w_pool_bf, pool_scale):
    b, s, _ = u.shape
    tp = POOL_TILE
    halo_blocks = tp // POOL_HALO
    return pl.pallas_call(
        _pool_prompt_kernel,
        grid=(b, s // tp),
        in_specs=[
            pl.BlockSpec((1, tp, D_POOL), lambda bi, i: (bi, i, 0)),
            pl.BlockSpec((1, POOL_HALO, D_POOL), lambda bi, i: (bi, jnp.maximum(i * halo_blocks - 1, 0), 0)),
            pl.BlockSpec(w_pool_bf.shape, lambda bi, i: (0, 0, 0)),
            pl.BlockSpec((1, D_POOL), lambda bi, i: (0, 0)),
        ],
        out_specs=pl.BlockSpec((1, tp, D_POOL), lambda bi, i: (bi, i, 0)),
        out_shape=jax.ShapeDtypeStruct((b, s, D_POOL), F32),
        compiler_params=_cparams(("parallel", "parallel")),
        name="pool_prompt",
    )(u, u, w_pool_bf, pool_scale)


def _pool_decode_kernel(u_ref, st_ref, wp_ref, ps_ref, o_ref):
    u = u_ref[...]
    d_groups = []
    for g, w in enumerate(POOL_WINDOWS):
        sl = slice(g * POOL_GROUP_DIM, (g + 1) * POOL_GROUP_DIM)
        win = u[:, sl]
        for j in range(1, w):
            win = win + st_ref[POOL_STATE - j][:, sl]
        d_groups.append(win / float(w) - u[:, sl])
    _pool_project(d_groups, wp_ref, ps_ref, o_ref)


def _pool_decode(u, state_t, w_pool_bf, pool_scale):
    assert PAST_LEN + 1 >= max(POOL_WINDOWS)
    return pl.pallas_call(
        _pool_decode_kernel,
        out_shape=jax.ShapeDtypeStruct(u.shape, F32),
        compiler_params=_cparams(None),
        name="pool_decode",
    )(u, state_t, w_pool_bf, pool_scale)


def _split_bf16(x):
    hi = x.astype(BF16)
    lo = (x - hi.astype(F32)).astype(BF16)
    return hi, lo


def _out_kernel(x_ref, a_ref, p_ref, g1_ref, sh_ref, sc_ref, g2_ref, wo_ref, gf_ref, wrh_ref, wrl_ref,
                wsg_ref, wsu_ref, wsd_ref, base_ref, h2_ref, lg_ref):
    tm = x_ref.shape[0]
    mix = (jnp.dot(a_ref[...].astype(BF16), wo_ref[:D_ATTN, :], preferred_element_type=F32)
           + jnp.dot(p_ref[...].astype(BF16), wo_ref[D_ATTN:, :], preferred_element_type=F32))
    x1 = x_ref[...] + g1_ref[0] * mix
    h2 = _rms(x1) * gf_ref[...] * (1.0 + sc_ref[0]) + sh_ref[0]
    for c in range(ROW_CHUNKS):
        h2_ref[pl.ds(c, tm, stride=ROW_CHUNKS), :] = h2[:, c * LANES:(c + 1) * LANES]
    h_hi, h_lo = _split_bf16(h2)
    w_hi, w_lo = wrh_ref[...], wrl_ref[...]
    lg = (jnp.dot(h_hi, w_hi, preferred_element_type=F32) + jnp.dot(h_lo, w_hi, preferred_element_type=F32)
          + jnp.dot(h_hi, w_lo, preferred_element_type=F32))
    if tm < LANES:
        lg = jnp.concatenate([lg, jnp.zeros((LANES - tm, LANES), F32)], axis=0)
    lg_ref[...] = lg.T[:N_EXPERTS, :tm]
    sg = jnp.dot(h_hi, wsg_ref[...], preferred_element_type=F32)
    su = jnp.dot(h_hi, wsu_ref[...], preferred_element_type=F32)
    hs = (sg * jax.nn.sigmoid(sg) * su).astype(BF16)
    base_ref[...] = x1 + g2_ref[0] * jnp.dot(hs, wsd_ref[...], preferred_element_type=F32)


def _out_proj(x2d, attn, pool, gate1, shift2, scale2, gate2, w_out_bf, g_ffn, w_router_hi, w_router_lo,
              ws_gate_bf, ws_up_bf, ws_down_bf, tm, rows_per_mod, h2_tokens, h2_all, h2_block0):
    m = x2d.shape[0]
    r = gate1.shape[1]
    n_main = m // tm
    h2_rows = h2_tokens * ROW_CHUNKS
    n_fill = 0 if h2_all is not None else pl.cdiv(h2_rows - m * ROW_CHUNKS, tm * ROW_CHUNKS)
    last = n_main - 1
    row = lambda i: jnp.minimum(i, last)
    mod_spec = pl.BlockSpec((1, r, D_MODEL), lambda i: (row(i) // rows_per_mod, 0, 0))
    const = lambda shape: pl.BlockSpec(shape, lambda i: (0,) * len(shape))
    in_specs = [
        pl.BlockSpec((tm, D_MODEL), lambda i: (row(i), 0)),
        pl.BlockSpec((tm, D_ATTN), lambda i: (row(i), 0)),
        pl.BlockSpec((tm, D_POOL), lambda i: (row(i), 0)),
        mod_spec, mod_spec, mod_spec, mod_spec,
        const((D_MODEL, D_MODEL)),
        const((1, D_MODEL)),
        const((D_MODEL, LANES)), const((D_MODEL, LANES)),
        const((D_MODEL, D_SHARED)), const((D_MODEL, D_SHARED)), const((D_SHARED, D_MODEL)),
    ]
    args = [x2d, attn, pool, gate1, shift2, scale2, gate2, w_out_bf, g_ffn, w_router_hi, w_router_lo,
            ws_gate_bf, ws_up_bf, ws_down_bf]
    aliases = {}
    n_in = len(args)
    if h2_all is not None:
        in_specs.append(pl.BlockSpec(memory_space=pl.ANY))
        args.append(h2_all)
        aliases = {n_in: 1}

    def kernel(*refs):
        refs = refs[:n_in] + refs[len(args):]
        if n_fill == 0:
            _out_kernel(*refs)
            return
        step = pl.program_id(0)
        pl.when(step < n_main)(lambda: _out_kernel(*refs))

        @pl.when(step >= n_main)
        def _():
            h2_ref = refs[n_in + 1]
            h2_ref[...] = jnp.zeros(h2_ref.shape, F32)

    return pl.pallas_call(
        kernel,
        grid=(n_main + n_fill,),
        in_specs=in_specs,
        out_specs=[
            pl.BlockSpec((tm, D_MODEL), lambda i: (row(i), 0)),
            pl.BlockSpec((tm * ROW_CHUNKS, LANES), lambda i: (i + h2_block0, 0)),
            pl.BlockSpec((N_EXPERTS, tm), lambda i: (0, row(i))),
        ],
        out_shape=[
            jax.ShapeDtypeStruct((m, D_MODEL), F32),
            jax.ShapeDtypeStruct((h2_rows, LANES), F32),
            jax.ShapeDtypeStruct((N_EXPERTS, m), F32),
        ],
        input_output_aliases=aliases,
        compiler_params=_cparams(("arbitrary",)),
        name="out_proj",
    )(*args)


def _first_index_of_max(v, iota, size, axis):
    m = jnp.max(v, axis=axis, keepdims=True)
    idx = jnp.min(jnp.where(v == m, iota, size), axis=axis, keepdims=True)
    return m, idx


def _route_kernel(n_valid, lg_ref, b_ref, eidx_ref, wts_ref, cnt_ref, carry):
    i = pl.program_id(0)
    tn = lg_ref.shape[1]

    @pl.when(i == 0)
    def _():
        carry[...] = jnp.zeros(carry.shape, F32)

    scores = jax.nn.sigmoid(lg_ref[...])
    biased = scores + b_ref[...]
    grp = biased.reshape(N_EXPERT_GROUPS, GROUP_SIZE, tn)
    io_g = lax.broadcasted_iota(I32, grp.shape, 1)
    m1, i1 = _first_index_of_max(grp, io_g, GROUP_SIZE, 1)
    m2 = jnp.max(jnp.where(io_g == i1, -jnp.inf, grp), axis=1, keepdims=True)
    gscore = (m1 + m2)[:, 0, :]
    io_n = lax.broadcasted_iota(I32, gscore.shape, 0)
    gsel = jnp.zeros(gscore.shape, jnp.bool_)
    for _ in range(TOPK_GROUPS):
        _, gi = _first_index_of_max(gscore, io_n, N_EXPERT_GROUPS, 0)
        hit = io_n == gi
        gsel = jnp.logical_or(gsel, hit)
        gscore = jnp.where(hit, -jnp.inf, gscore)
    emask = jnp.broadcast_to(gsel[:, None, :], grp.shape).reshape(N_EXPERTS, tn)
    cand = jnp.where(emask, biased, -jnp.inf)
    io_e = lax.broadcasted_iota(I32, cand.shape, 0)
    picked = jnp.zeros(cand.shape, jnp.bool_)
    eidx, sel = [], []
    for _ in range(TOP_K):
        _, ei = _first_index_of_max(cand, io_e, N_EXPERTS, 0)
        hit = io_e == ei
        eidx.append(ei)
        sel.append(jnp.sum(jnp.where(hit, scores, 0.0), axis=0, keepdims=True))
        picked = jnp.logical_or(picked, hit)
        cand = jnp.where(hit, -jnp.inf, cand)
    sel = jnp.concatenate(sel, axis=0)
    eidx = jnp.concatenate(eidx, axis=0)
    wts_ref[...] = sel / jnp.sum(sel, axis=0, keepdims=True) * ROUTED_SCALE
    eidx_ref[...] = eidx

    tok = i * tn + lax.broadcasted_iota(I32, cand.shape, 1)
    mask = jnp.logical_and(picked, tok < n_valid).astype(F32)
    total = carry[...] + jnp.sum(mask, axis=1, keepdims=True)
    carry[...] = total
    cnt_ref[...] = total.astype(I32)


def _route(logits_t, b_router, n_valid):
    mp = logits_t.shape[1]
    tn = ROUTE_TILE
    tok_spec = pl.BlockSpec((TOP_K, tn), lambda i: (0, i))
    return pl.pallas_call(
        functools.partial(_route_kernel, n_valid),
        grid=(mp // tn,),
        in_specs=[
            pl.BlockSpec((N_EXPERTS, tn), lambda i: (0, i)),
            pl.BlockSpec((N_EXPERTS, 1), lambda i: (0, 0)),
        ],
        out_specs=[tok_spec, tok_spec, pl.BlockSpec((N_EXPERTS, LANES), lambda i: (0, 0))],
        out_shape=[
            jax.ShapeDtypeStruct((TOP_K, mp), I32),
            jax.ShapeDtypeStruct((TOP_K, mp), F32),
            jax.ShapeDtypeStruct((N_EXPERTS, LANES), I32),
        ],
        scratch_shapes=[pltpu.VMEM((N_EXPERTS, LANES), F32)],
        compiler_params=_cparams(("arbitrary",)),
        name="route",
    )(logits_t, b_router.reshape(N_EXPERTS, 1))


TILE_USED = 1
TILE_FIRST = 2
TILE_WSLOT = 4
TILE_PREV_USED = 8
PAIR_BITS = 3
assert 1 << PAIR_BITS == TOP_K
DUMMY_TOKENS = 2 * EXPERT_TILE // TOP_K
GATHER_SLOTS = 3


def _experts_kernel(n_tok, te_ref, ts_ref, tn_ref, tf_ref, ne_ref, g_ref, s_ref,
                    h_hbm, wg_hbm, wu_hbm, wd_hbm, y_hbm,
                    xbuf, ybuf, xs, wg_f, wu_f, wd_f, wg_bf, wu_bf, wd_bf, gsem, ssem, wsem, zsem):
    j = pl.program_id(0)
    last = pl.num_programs(0) - 1
    tm = EXPERT_TILE
    rc = ROW_CHUNKS
    plane_tokens = n_tok + DUMMY_TOKENS
    slot = j % 2
    other = 1 - slot
    gslot = j % GATHER_SLOTS
    flags = tf_ref[j]
    used = (flags & TILE_USED) != 0
    first = (flags & TILE_FIRST) != 0
    prev_used = (flags & TILE_PREV_USED) != 0
    wslot = (flags // TILE_WSLOT) & 1
    prv = jnp.maximum(j - 1, 0)
    ahead = jnp.minimum(j + GATHER_SLOTS - 1, last)

    def buf_rows(i):
        return pl.ds(i * rc, rc) if isinstance(i, int) else pl.ds(pl.multiple_of(i * rc, rc), rc)

    def gather_row(i, start, dst):
        row = g_ref[start + i]
        pltpu.make_async_copy(h_hbm.at[pl.ds(pl.multiple_of(row, rc), rc), :],
                              xbuf.at[dst, buf_rows(i), :], gsem.at[dst]).start(priority=0)

    def scatter_row(i, start, n_valid, src):
        spare = ((i & (TOP_K - 1)) * plane_tokens + n_tok + (i >> PAIR_BITS)) * rc + src * ((tm >> PAIR_BITS) * rc)
        row = jnp.where(i < n_valid, s_ref[start + i], spare)
        pltpu.make_async_copy(ybuf.at[src, buf_rows(i), :],
                              y_hbm.at[pl.ds(pl.multiple_of(row, rc), rc), :],
                              ssem.at[src]).start(priority=i % 2 if isinstance(i, int) else 0)

    def scatter_tile_loop(tile, src):
        start, n_valid = ts_ref[tile], tn_ref[tile]
        lax.fori_loop(0, tm, lambda i, c: (scatter_row(i, start, n_valid, src), c)[1], 0)

    def wait_gather(dst):
        pltpu.make_async_copy(h_hbm.at[pl.ds(0, tm * rc), :], xbuf.at[dst], gsem.at[dst]).wait()

    def wait_scatter(src):
        pltpu.make_async_copy(ybuf.at[src], y_hbm.at[pl.ds(0, tm * rc), :], ssem.at[src]).wait()

    def weight_copies(e, ws):
        return [pltpu.make_async_copy(src.at[e], dst.at[ws], wsem.at[ws])
                for src, dst in ((wg_hbm, wg_f), (wu_hbm, wu_f), (wd_hbm, wd_f))]

    @pl.when(j == 0)
    def _():
        ybuf[1] = jnp.zeros(ybuf.shape[1:], ybuf.dtype)
        fills = [pltpu.make_async_copy(ybuf.at[1, pl.ds(0, DUMMY_TOKENS * rc), :],
                                       y_hbm.at[pl.ds((k * plane_tokens + n_tok) * rc, DUMMY_TOKENS * rc), :],
                                       zsem.at[0]) for k in range(TOP_K)]
        for c in fills:
            c.start()
        for c in fills:
            c.wait()
        for c in weight_copies(te_ref[0], 0):
            c.start(priority=1)
        for t in range(GATHER_SLOTS - 1):
            start_t = ts_ref[jnp.minimum(t, last)]
            lax.fori_loop(0, tm, lambda i, c, start_t=start_t, t=t: (gather_row(i, start_t, t), c)[1], 0)

    @pl.when(first)
    def _():
        for c in weight_copies(te_ref[j], wslot):
            c.wait()
        for ws in range(2):

            @pl.when(wslot == ws)
            def _(ws=ws):
                wg_bf[...] = wg_f[ws].astype(BF16)
                wu_bf[...] = wu_f[ws].astype(BF16)
                wd_bf[...] = wd_f[ws].astype(BF16)
        nxt_e = ne_ref[j]

        @pl.when(nxt_e >= 0)
        def _():
            for c in weight_copies(nxt_e, 1 - wslot):
                c.start(priority=1)

    @pl.when(jnp.logical_and(used, j > 0))
    def _():
        wait_scatter(slot)

    @pl.when(used)
    def _():
        wait_gather(gslot)
        xs[...] = jnp.concatenate(
            [xbuf[gslot, pl.ds(c, tm, stride=rc), :] for c in range(rc)], axis=1).astype(BF16)
        start_n = ts_ref[ahead]
        ahead_slot = (j + GATHER_SLOTS - 1) % GATHER_SLOTS
        for i in range(tm):
            gather_row(i, start_n, ahead_slot)
        start_p = ts_ref[prv]
        n_valid_p = jnp.where(j > 0, tn_ref[prv], 0)
        for i in range(tm):
            scatter_row(i, start_p, n_valid_p, other)
        x = xs[...]
        hg = jnp.dot(x, wg_bf[...], preferred_element_type=F32)
        hu = jnp.dot(x, wu_bf[...], preferred_element_type=F32)
        h = (hg * jax.nn.sigmoid(hg) * hu).astype(BF16)
        y = jnp.dot(h, wd_bf[...], preferred_element_type=F32)
        for c in range(rc):
            ybuf[slot, pl.ds(c, tm, stride=rc), :] = y[:, c * LANES:(c + 1) * LANES]

    @pl.when(jnp.logical_and(jnp.logical_not(used), prev_used))
    def _():
        for t in range(GATHER_SLOTS - 1):
            wait_gather((j + t) % GATHER_SLOTS)
        wait_scatter(slot)
        scatter_tile_loop(prv, other)
        wait_scatter(other)

    @pl.when(jnp.logical_and(used, j == last))
    def _():
        for t in range(1, GATHER_SLOTS):
            wait_gather((j + t) % GATHER_SLOTS)
        wait_scatter(other)
        scatter_tile_loop(j, slot)
        wait_scatter(slot)


def _experts(tile_expert, tile_start, tile_rows, tile_flags, next_expert, gather_rows, scatter_rows, h2_all,
             w_gate, w_up, w_down, n_tok):
    n_tiles = tile_expert.shape[0]
    tm = EXPERT_TILE
    any_spec = pl.BlockSpec(memory_space=pl.ANY)
    grid_spec = pltpu.PrefetchScalarGridSpec(
        num_scalar_prefetch=7,
        grid=(n_tiles,),
        in_specs=[any_spec] * 4,
        out_specs=any_spec,
        scratch_shapes=[
            pltpu.VMEM((GATHER_SLOTS, tm * ROW_CHUNKS, LANES), F32),
            pltpu.VMEM((2, tm * ROW_CHUNKS, LANES), F32),
            pltpu.VMEM((tm, D_MODEL), BF16),
            pltpu.VMEM((2, D_MODEL, D_EXPERT), F32),
            pltpu.VMEM((2, D_MODEL, D_EXPERT), F32),
            pltpu.VMEM((2, D_EXPERT, D_MODEL), F32),
            pltpu.VMEM((D_MODEL, D_EXPERT), BF16),
            pltpu.VMEM((D_MODEL, D_EXPERT), BF16),
            pltpu.VMEM((D_EXPERT, D_MODEL), BF16),
            pltpu.SemaphoreType.DMA((GATHER_SLOTS,)),
            pltpu.SemaphoreType.DMA((2,)),
            pltpu.SemaphoreType.DMA((2,)),
            pltpu.SemaphoreType.DMA((1,)),
        ],
    )
    return pl.pallas_call(
        functools.partial(_experts_kernel, n_tok),
        grid_spec=grid_spec,
        out_shape=jax.ShapeDtypeStruct((TOP_K * (n_tok + DUMMY_TOKENS) * ROW_CHUNKS, LANES), F32),
        compiler_params=_cparams(("arbitrary",)),
        name="experts",
    )(tile_expert, tile_start, tile_rows, tile_flags, next_expert, gather_rows, scatter_rows, h2_all,
      w_gate, w_up, w_down)


COMBINE_UNROLL = 4


def _combine_kernel(w_ref, y_ref, base_ref, g2_ref, o_ref, acc_rows):
    i = pl.program_id(0)
    tc = base_ref.shape[0]
    rc = ROW_CHUNKS

    def body(g, carry):
        for u in range(COMBINE_UNROLL):
            t = g * COMBINE_UNROLL + u
            rows = pl.ds(pl.multiple_of(t * rc, rc), rc)
            w0 = (i * tc + t) * TOP_K
            acc = w_ref[w0] * y_ref[0, rows, :]
            for k in range(1, TOP_K):
                acc = acc + w_ref[w0 + k] * y_ref[k, rows, :]
            acc_rows[rows, :] = acc
        return carry

    lax.fori_loop(0, tc // COMBINE_UNROLL, body, 0)
    g2 = g2_ref[0]
    for c in range(rc):
        sl = slice(c * LANES, (c + 1) * LANES)
        o_ref[:, sl] = base_ref[:, sl] + g2[:, sl] * acc_rows[pl.ds(c, tc, stride=rc), :]


def _combine(y_planes, wts, base, gate2, tc, rows_per_mod, block0):
    m = base.shape[0]
    r = gate2.shape[1]
    assert tc % COMBINE_UNROLL == 0
    grid_spec = pltpu.PrefetchScalarGridSpec(
        num_scalar_prefetch=1,
        grid=(m // tc,),
        in_specs=[
            pl.BlockSpec((TOP_K, tc * ROW_CHUNKS, LANES), lambda i, w: (0, i + block0, 0)),
            pl.BlockSpec((tc, D_MODEL), lambda i, w: (i, 0)),
            pl.BlockSpec((1, r, D_MODEL), lambda i, w: (i // rows_per_mod, 0, 0)),
        ],
        out_specs=pl.BlockSpec((tc, D_MODEL), lambda i, w: (i, 0)),
        scratch_shapes=[pltpu.VMEM((tc * ROW_CHUNKS, LANES), F32)],
    )
    return pl.pallas_call(
        _combine_kernel,
        grid_spec=grid_spec,
        out_shape=jax.ShapeDtypeStruct((m, D_MODEL), F32),
        compiler_params=_cparams(("parallel",)),
        name="combine",
    )(wts.reshape(-1), y_planes, base, gate2)


def _moe_routed(h2_all, logits_t, b_router, w_gate, w_up, w_down):
    n_tok = logits_t.shape[1]
    mp = -(-n_tok // ROUTE_TILE) * ROUTE_TILE
    logits_t = jnp.pad(logits_t, ((0, 0), (0, mp - n_tok)))
    eidx, wts, counts = _route(logits_t, b_router, n_tok)
    eidx, wts, counts = eidx[:, :n_tok], wts[:, :n_tok], counts[:, 0]

    tm = EXPERT_TILE
    n_pairs = n_tok * TOP_K
    n_tiles = n_pairs // tm + N_EXPERTS
    pair_code = jnp.arange(n_tok, dtype=I32)[None, :] * TOP_K + jnp.arange(TOP_K, dtype=I32)[:, None]
    (sorted_pairs,) = lax.sort(((eidx * n_pairs + pair_code).reshape(-1),), is_stable=False)
    sorted_pairs = jnp.concatenate([sorted_pairs % n_pairs, jnp.zeros((tm,), I32)])
    pair_tok, pair_choice = sorted_pairs >> PAIR_BITS, sorted_pairs & (TOP_K - 1)
    gather_rows = pair_tok * ROW_CHUNKS
    scatter_rows = (pair_choice * (n_tok + DUMMY_TOKENS) + pair_tok) * ROW_CHUNKS
    tiles_e = (counts + tm - 1) // tm
    tile_end = jnp.cumsum(tiles_e)
    tile_begin = tile_end - tiles_e
    dense_begin = jnp.cumsum(counts) - counts
    tj = jnp.arange(n_tiles, dtype=I32)
    total_tiles = tile_end[-1]
    used = tj < total_tiles
    tj_used = jnp.minimum(tj, total_tiles - 1)[:, None]
    member = jnp.logical_and(tile_begin[None, :] <= tj_used, tj_used < tile_end[None, :])
    of_tile = lambda per_expert: jnp.sum(jnp.where(member, per_expert[None, :], 0), axis=1)
    e_ids = jnp.arange(N_EXPERTS, dtype=I32)
    te = of_tile(e_ids)
    local = tj - of_tile(tile_begin)
    ts = jnp.where(used, of_tile(dense_begin) + local * tm, 0).astype(I32)
    tn = jnp.where(used, jnp.clip(of_tile(counts) - local * tm, 0, tm), 0).astype(I32)
    first = jnp.logical_and(used, local == 0)
    busy = counts > 0
    wslot = of_tile((jnp.cumsum(busy.astype(I32)) - 1) % 2)
    later = jnp.logical_and(e_ids[None, :] > e_ids[:, None], busy[None, :])
    succ = jnp.min(jnp.where(later, e_ids[None, :], N_EXPERTS), axis=1)
    succ = jnp.where(succ == N_EXPERTS, -1, succ)
    ne = jnp.where(first, of_tile(succ), -1).astype(I32)
    prev_used = jnp.concatenate([jnp.zeros((1,), jnp.bool_), used[:-1]])
    tf = (used.astype(I32) * TILE_USED + first.astype(I32) * TILE_FIRST + wslot * TILE_WSLOT
          + prev_used.astype(I32) * TILE_PREV_USED)
    y_all = _experts(te, ts, tn, tf, ne, gather_rows, scatter_rows, h2_all, w_gate, w_up, w_down, n_tok)
    return y_all.reshape(TOP_K, (n_tok + DUMMY_TOKENS) * ROW_CHUNKS, LANES), wts.T


def kernel(x_prompt, x_sample, cache_k_win, cache_v_win, state_pool, c_prompt, c_sample, w_ada, b_ada, g_mix, w_in, q_norm, k_norm, w_pool, pool_scale, w_out, g_ffn, w_router, b_router, w_gate, w_up, w_down, ws_gate, ws_up, ws_down):
    depth = w_ada.shape[0]
    assert depth == 1
    l = 0
    nb, seq, _ = x_prompt.shape
    bd, dec_seq, _ = x_sample.shape
    assert dec_seq == 1 and seq % (TOKEN_TILE * 16) == 0
    n_prompt = nb * seq
    n_tok = n_prompt + bd
    n_keep = min(MAX_WINDOW, seq)
    tiles_per_batch = seq // TOKEN_TILE

    n_mod_rows = -(-(nb + bd) // 8) * 8
    c_all = jnp.concatenate([c_prompt, c_sample, jnp.zeros((n_mod_rows - nb - bd, D_MODEL), F32)], axis=0)
    mods = _ada(c_all, w_ada[l], b_ada[l])
    mods_p = [mods[:nb, j * D_MODEL:(j + 1) * D_MODEL].reshape(nb, 1, D_MODEL) for j in range(6)]
    mods_s = [mods[nb:nb + bd, j * D_MODEL:(j + 1) * D_MODEL].reshape(1, bd, D_MODEL) for j in range(6)]

    g_mix_l = g_mix[l].reshape(1, D_MODEL)
    g_ffn_l = g_ffn[l].reshape(1, D_MODEL)
    qn = q_norm[l].reshape(1, HEAD_DIM)
    kn = k_norm[l].reshape(1, HEAD_DIM)
    w_in_bf = w_in[l].astype(BF16)
    w_out_bf = w_out[l].astype(BF16)
    w_pool_bf = w_pool[l].astype(BF16)
    ps = pool_scale[l].reshape(1, D_POOL)
    w_router_pad = jnp.pad(w_router[l], ((0, 0), (0, LANES - N_EXPERTS)))
    w_router_hi = w_router_pad.astype(BF16)
    w_router_lo = (w_router_pad - w_router_hi.astype(F32)).astype(BF16)
    wsg_bf, wsu_bf, wsd_bf = ws_gate[l].astype(BF16), ws_up[l].astype(BF16), ws_down[l].astype(BF16)

    rope_p = _rope_tables(jnp.arange(seq, dtype=I32))
    rope_s = _rope_tables(jnp.full((bd,), PAST_LEN, I32))
    xp2 = x_prompt.reshape(n_prompt, D_MODEL)
    xs2 = x_sample.reshape(bd, D_MODEL)
    assert n_keep % TOKEN_TILE == 0
    qp, kp, vp, up, kwin_p, vwin_p = _in_proj(xp2, mods_p[0], mods_p[1], g_mix_l, w_in_bf, qn, kn, rope_p,
                                              TOKEN_TILE, tiles_per_batch, tiles_per_batch, n_keep // TOKEN_TILE)
    qs, _, _, us, k_new, v_new = _in_proj(xs2, mods_s[0], mods_s[1], g_mix_l, w_in_bf, qn, kn, rope_s, bd, 1, 1, 1)

    to_seq = lambda t: t.reshape(nb, seq, -1)
    heads = lambda t: t.reshape(bd, N_HEADS, HEAD_DIM)
    ck, cv = cache_k_win[l], cache_v_win[l]
    k_new, v_new = heads(k_new), heads(v_new)
    attn_p, k_win_s, v_win_s = _attn_prompt(to_seq(qp), to_seq(kp), to_seq(vp), ck, cv, k_new, v_new)
    attn_p = attn_p.reshape(n_prompt, D_ATTN)
    pool_p = _pool_prompt(to_seq(up), w_pool_bf, ps).reshape(n_prompt, D_POOL)
    attn_s = _attn_decode(heads(qs), k_new, v_new, ck, cv).reshape(bd, D_ATTN)
    pool_s = _pool_decode(us, jnp.swapaxes(state_pool[l], 0, 1), w_pool_bf, ps)

    base_p, h2_all, lg_p = _out_proj(xp2, attn_p, pool_p, mods_p[2], mods_p[3], mods_p[4], mods_p[5],
                                     w_out_bf, g_ffn_l, w_router_hi, w_router_lo, wsg_bf, wsu_bf, wsd_bf,
                                     TOKEN_TILE, tiles_per_batch, n_tok, None, 0)
    base_s, h2_all, lg_s = _out_proj(xs2, attn_s, pool_s, mods_s[2], mods_s[3], mods_s[4], mods_s[5],
                                     w_out_bf, g_ffn_l, w_router_hi, w_router_lo, wsg_bf, wsu_bf, wsd_bf,
                                     bd, 1, n_tok, h2_all, n_prompt // bd)

    y_planes, wts_t = _moe_routed(h2_all, jnp.concatenate([lg_p, lg_s], axis=1), b_router[l],
                                  w_gate[l], w_up[l], w_down[l])

    yp = _combine(y_planes, wts_t[:n_prompt], base_p, mods_p[5], COMBINE_TILE, seq // COMBINE_TILE, 0)
    ys = _combine(y_planes, wts_t[n_prompt:n_tok], base_s, mods_s[5], bd, 1, n_prompt // bd)

    y_prompt = yp.reshape(nb, seq, D_MODEL)
    y_sample = ys.reshape(bd, dec_seq, D_MODEL)
    win = lambda t: t.reshape(1, nb, n_keep, N_HEADS, HEAD_DIM)
    pool_p_state = to_seq(up)[:, seq - POOL_STATE:][None]
    pool_s_state = jnp.concatenate([state_pool[l][:, 1:], us[:, None, :]], axis=1)[None]
    return (y_prompt, y_sample, win(kwin_p), win(vwin_p), pool_p_state,
            k_win_s[None], v_win_s[None], pool_s_state)
```

```python
import functools

import jax
import jax.numpy as jnp
from jax import lax
from jax.experimental import pallas as pl
from jax.experimental.pallas import tpu as pltpu

F32 = jnp.float32
BF16 = jnp.bfloat16
I32 = jnp.int32

D_MODEL = 2048
N_HEADS = 8
HEAD_DIM = 128
D_ATTN = N_HEADS * HEAD_DIM
D_POOL = D_MODEL - D_ATTN
D_IN = 3 * D_ATTN + D_POOL
ROPE_DIM = HEAD_DIM // 4
ROPE_HALF = ROPE_DIM // 2
ROPE_THETA = 500000.0
DILATED_GROUPS = ((128, 1), (512, 4), (2048, 16))
QBLOCK = 128
ATTN_SCALE = HEAD_DIM ** -0.5
POOL_WINDOWS = (2, 4, 8, 16)
POOL_GROUP_DIM = D_POOL // len(POOL_WINDOWS)
POOL_STATE = max(POOL_WINDOWS) - 1
POOL_HALO = 16
N_EXPERTS = 64
TOP_K = 8
N_EXPERT_GROUPS = 8
GROUP_SIZE = N_EXPERTS // N_EXPERT_GROUPS
TOPK_GROUPS = 4
D_EXPERT = 512
D_SHARED = 512
ROUTED_SCALE = 2.5
EPS = 1e-6
PAST_LEN = 16384
MAX_WINDOW = 2048

LANES = 128
ROW_CHUNKS = D_MODEL // LANES
NEG_BIG = -1e30
LOG2_E = 1.4426950408889634

TOKEN_TILE = 256
EXPERT_TILE = 256
COMBINE_TILE = 128
ROUTE_TILE = 256
ADA_COL_TILE = 1024
POOL_TILE = 512
VMEM_LIMIT = 60 * 1024 * 1024


def _cparams(sem, vmem=VMEM_LIMIT):
    return pltpu.CompilerParams(dimension_semantics=sem, vmem_limit_bytes=vmem)


def _ada_kernel(c_ref, w_ref, b_ref, o_ref):
    c = c_ref[...]
    s = (c * jax.nn.sigmoid(c)).astype(BF16)
    o_ref[...] = jnp.dot(s, w_ref[...].astype(BF16), preferred_element_type=F32) + b_ref[...]


def _ada(c_all, w_ada, b_ada):
    rows = c_all.shape[0]
    n = w_ada.shape[1]
    tn = ADA_COL_TILE
    return pl.pallas_call(
        _ada_kernel,
        grid=(n // tn,),
        in_specs=[
            pl.BlockSpec((rows, D_MODEL), lambda j: (0, 0)),
            pl.BlockSpec((D_MODEL, tn), lambda j: (0, j)),
            pl.BlockSpec((1, tn), lambda j: (0, j)),
        ],
        out_specs=pl.BlockSpec((rows, tn), lambda j: (0, j)),
        out_shape=jax.ShapeDtypeStruct((rows, n), F32),
        compiler_params=_cparams(("parallel",)),
        name="ada_modulation",
    )(c_all, w_ada, b_ada.reshape(1, n))


def _rms(x):
    return x * lax.rsqrt(jnp.mean(x * x, axis=-1, keepdims=True) + EPS)


def _in_kernel(x_ref, sh_ref, sc_ref, g_ref, w_ref, qn_ref, kn_ref, c_ref, s1_ref, s2_ref,
               q_ref, k_ref, v_ref, u_ref, kw_ref, vw_ref):
    tm = x_ref.shape[0]
    x = x_ref[...]
    h = _rms(x) * g_ref[...] * (1.0 + sc_ref[0]) + sh_ref[0]
    hb = h.astype(BF16)
    cos = c_ref[...]
    s1 = s1_ref[...]
    s2 = s2_ref[...]

    def head_rows(win_ref, hd, val):
        win_ref[pl.ds(hd, tm, stride=N_HEADS), :] = val

    def qk(sec, nrm, out_ref, win_ref):
        z = jnp.dot(hb, w_ref[:, sec * D_ATTN:(sec + 1) * D_ATTN], preferred_element_type=F32)
        for hd in range(N_HEADS):
            sl = slice(hd * HEAD_DIM, (hd + 1) * HEAD_DIM)
            r = _rms(z[:, sl]) * nrm
            r = r * cos + pltpu.roll(r, HEAD_DIM - ROPE_HALF, 1) * s1 + pltpu.roll(r, ROPE_HALF, 1) * s2
            out_ref[:, sl] = r
            if win_ref is not None:
                head_rows(win_ref, hd, r)

    qk(0, qn_ref[...], q_ref, None)
    qk(1, kn_ref[...], k_ref, kw_ref)
    v = jnp.dot(hb, w_ref[:, 2 * D_ATTN:3 * D_ATTN], preferred_element_type=F32)
    v_ref[...] = v
    for hd in range(N_HEADS):
        head_rows(vw_ref, hd, v[:, hd * HEAD_DIM:(hd + 1) * HEAD_DIM])
    u_ref[...] = jnp.dot(hb, w_ref[:, 3 * D_ATTN:], preferred_element_type=F32)


def _in_proj(x2d, shift, scale, g_mix, w_in_bf, q_norm, k_norm, rope, tm, rows_per_mod, rope_tiles, keep_tiles):
    m = x2d.shape[0]
    r = shift.shape[1]
    n_groups = m // tm // rows_per_mod
    skip = rows_per_mod - keep_tiles
    mod_spec = pl.BlockSpec((1, r, D_MODEL), lambda i: (i // rows_per_mod, 0, 0))
    rope_spec = pl.BlockSpec((tm, HEAD_DIM), lambda i: (i % rope_tiles, 0))
    out_spec = pl.BlockSpec((tm, D_ATTN), lambda i: (i, 0))
    out_sd = jax.ShapeDtypeStruct((m, D_ATTN), F32)
    win_spec = pl.BlockSpec(
        (tm * N_HEADS, HEAD_DIM),
        lambda i: ((i // rows_per_mod) * keep_tiles + jnp.maximum(i % rows_per_mod - skip, 0), 0))
    win_sd = jax.ShapeDtypeStruct((n_groups * keep_tiles * tm * N_HEADS, HEAD_DIM), F32)
    return pl.pallas_call(
        _in_kernel,
        grid=(m // tm,),
        in_specs=[
            pl.BlockSpec((tm, D_MODEL), lambda i: (i, 0)),
            mod_spec, mod_spec,
            pl.BlockSpec((1, D_MODEL), lambda i: (0, 0)),
            pl.BlockSpec((D_MODEL, D_IN), lambda i: (0, 0)),
            pl.BlockSpec((1, HEAD_DIM), lambda i: (0, 0)),
            pl.BlockSpec((1, HEAD_DIM), lambda i: (0, 0)),
            rope_spec, rope_spec, rope_spec,
        ],
        out_specs=[out_spec] * 4 + [win_spec] * 2,
        out_shape=[out_sd] * 4 + [win_sd] * 2,
        compiler_params=_cparams(("arbitrary",)),
        name="in_proj",
    )(x2d, shift, scale, g_mix, w_in_bf, q_norm, k_norm, *rope)


def _rope_tables(pos):
    inv_freq = jnp.float32(ROPE_THETA) ** (-jnp.arange(ROPE_HALF, dtype=F32) / ROPE_HALF)
    ang = pos.astype(F32)[:, None] * inv_freq[None, :]
    cos, sin = jnp.cos(ang), jnp.sin(ang)
    t = pos.shape[0]
    rest = HEAD_DIM - ROPE_DIM
    c = jnp.concatenate([cos, cos, jnp.ones((t, rest), F32)], axis=1)
    s1 = jnp.concatenate([-sin, jnp.zeros((t, ROPE_HALF + rest), F32)], axis=1)
    s2 = jnp.concatenate([jnp.zeros((t, ROPE_HALF), F32), sin, jnp.zeros((t, rest), F32)], axis=1)
    return c, s1, s2


ATTN_INTERLEAVE = 4
WINDOW_ROWS_PER_STEP = 2
REGROUP_MIN_DILATION = 16


REGROUP_STRIDE = 4


def _to_residue_major(read_rows, dst, tmp, seq, sub_len):
    n, part = REGROUP_STRIDE, seq // REGROUP_STRIDE
    for a in range(n):
        tmp[pl.ds(a * part, part), :] = read_rows(pl.ds(a, part, stride=n))
    for a in range(n):
        for s in range(n):
            dst[pl.ds((s * n + a) * sub_len, sub_len), :] = tmp[pl.ds(a * part + s, sub_len, stride=n), :]


def _from_residue_major(src, dst, tmp, seq, sub_len):
    n, part = REGROUP_STRIDE, seq // REGROUP_STRIDE
    for a in range(n):
        for s in range(n):
            tmp[pl.ds(a * part + s, sub_len, stride=n), :] = src[pl.ds((s * n + a) * sub_len, sub_len), :]
    for a in range(n):
        dst[pl.ds(a, part, stride=n), :] = tmp[pl.ds(a * part, part), :]


def _attn_prompt_kernel(q_ref, k_ref, v_ref, ck_hbm, cv_hbm, kn_ref, vn_ref, o_ref, ok_hbm, ov_hbm,
                        m_scr, l_scr, acc_scr, q_rm, k_rm, v_rm, m_rm, l_rm, a_rm, wbuf, isem, osem):
    seq = q_ref.shape[1]
    half = wbuf.shape[2]
    step = pl.program_id(0) * pl.num_programs(1) + pl.program_id(1)
    last_step = pl.num_programs(0) * pl.num_programs(1) - 1
    windows = ((ck_hbm, kn_ref, ok_hbm), (cv_hbm, vn_ref, ov_hbm))
    row0 = step * WINDOW_ROWS_PER_STEP
    n_bounds = 2 * WINDOW_ROWS_PER_STEP
    assert n_bounds == len(DILATED_GROUPS) + 1

    def chunk_in(row, h, a):
        n_rows = half - h
        return pltpu.make_async_copy(windows[a][0].at[row, pl.ds(h * half + 1, n_rows)],
                                     wbuf.at[h, a, pl.ds(0, n_rows)], isem.at[h, a])

    def chunk_out(row, h, a):
        return pltpu.make_async_copy(wbuf.at[h, a], windows[a][2].at[row, pl.ds(h * half, half)], osem.at[h, a])

    def finish_in_start_out(row, h):
        for a in range(2):
            chunk_in(row, h, a).wait()
            if h == 1:
                wbuf[h, a, half - 1] = windows[a][1][row]
            chunk_out(row, h, a).start()

    def finish_out(row, h):
        for a in range(2):
            chunk_out(row, h, a).wait()

    def window_boundary(b):
        h = b % 2
        if b == 0:
            @pl.when(step > 0)
            def _():
                finish_in_start_out(row0 - 1, 1)
                finish_out(row0 - 1, 0)
        elif b == 1:
            finish_in_start_out(row0, 0)
            pl.when(step > 0)(lambda: finish_out(row0 - 1, 1))
        else:
            finish_in_start_out(row0 + (b - 1) // 2, (b - 1) % 2)
            finish_out(row0 + (b - 2) // 2, h)
        for a in range(2):
            chunk_in(row0 + b // 2, h, a).start()

    window_phases = [functools.partial(window_boundary, b) for b in range(n_bounds)]
    window_phases[0]()
    m_scr[...] = jnp.full(m_scr.shape, NEG_BIG, F32)
    l_scr[...] = jnp.zeros(l_scr.shape, F32)
    acc_scr[...] = jnp.zeros(acc_scr.shape, F32)
    qi = lax.broadcasted_iota(I32, (QBLOCK, 2 * QBLOCK), 0)
    kj = lax.broadcasted_iota(I32, (QBLOCK, 2 * QBLOCK), 1)
    dist = qi + QBLOCK - kj
    band_mask = jnp.logical_and(dist >= 0, dist <= QBLOCK)
    bias_band = jnp.where(band_mask, 0.0, NEG_BIG)
    bias_first = jnp.where(jnp.logical_and(band_mask, kj >= QBLOCK), 0.0, NEG_BIG)
    ones = jnp.ones((2 * QBLOCK, HEAD_DIM), BF16)
    dims = (((1,), (1,)), ((), ()))
    q_scale = ATTN_SCALE * LOG2_E

    regrouped = None
    for window, dil in DILATED_GROUPS:
        assert window // dil == QBLOCK
        n_units = seq // QBLOCK
        assert n_units % ATTN_INTERLEAVE == 0 and (dil % ATTN_INTERLEAVE == 0 or dil == 1)

        def rows(start, dil=dil):
            return pl.ds(start, QBLOCK) if dil == 1 else pl.ds(start, QBLOCK, stride=dil)

        sub_len = seq // dil
        regroup = dil >= REGROUP_MIN_DILATION
        if regroup:
            assert dil == REGROUP_STRIDE ** 2 and regrouped is None
            regrouped = (dil, sub_len)
            for src, dst in ((q_ref, q_rm), (k_ref, k_rm), (v_ref, v_rm)):
                _to_residue_major(lambda rows_, src=src: src[0, rows_, :], dst, m_rm, seq, sub_len)
            m_rm[...] = jnp.full(m_rm.shape, NEG_BIG, F32)
            l_rm[...] = jnp.zeros(l_rm.shape, F32)
            a_rm[...] = jnp.zeros(a_rm.shape, F32)

        def qkv_blocks(blk, res, start, prev, rows=rows, regroup=regroup, sub_len=sub_len):
            if regroup:
                cur = pl.ds(res * sub_len + blk * QBLOCK, QBLOCK)
                prv = pl.ds(res * sub_len + jnp.maximum(blk - 1, 0) * QBLOCK, QBLOCK)
                return (q_rm[cur, :], k_rm[prv, :], k_rm[cur, :], v_rm[prv, :], v_rm[cur, :])
            return (q_ref[0, rows(start), :], k_ref[0, rows(prev), :], k_ref[0, rows(start), :],
                    v_ref[0, rows(prev), :], v_ref[0, rows(start), :])

        state_refs = (m_rm, l_rm, a_rm) if regroup else (m_scr, l_scr, acc_scr)

        def state_rows(blk, res, start, rows=rows, regroup=regroup, sub_len=sub_len):
            return pl.ds(res * sub_len + blk * QBLOCK, QBLOCK) if regroup else rows(start)

        def body(it, carry, dil=dil, rows=rows, qkv_blocks=qkv_blocks, state_refs=state_refs, state_rows=state_rows):
            m_st, l_st, a_st = state_refs
            loaded = []
            for u in range(ATTN_INTERLEAVE):
                f = it * ATTN_INTERLEAVE + u
                blk, res = f // dil, f % dil
                start = blk * (QBLOCK * dil) + res
                prev = jnp.maximum(start - QBLOCK * dil, res)
                q_blk, k_prev, k_cur, v_prev, v_cur = qkv_blocks(blk, res, start, prev)
                q = (q_blk * q_scale).astype(BF16)
                k = jnp.concatenate([k_prev, k_cur], axis=0).astype(BF16)
                v = jnp.concatenate([v_prev, v_cur], axis=0).astype(BF16)
                v_ones = jnp.concatenate([v, ones], axis=1)
                st = state_rows(blk, res, start)
                state = (m_st[st, :], l_st[st, :], a_st[st, :])
                loaded.append((st, blk, q, k, v_ones, state))
            scores = [lax.dot_general(q, k, dims, preferred_element_type=F32)
                      + jnp.where(blk > 0, bias_band, bias_first)
                      for _, blk, q, k, _, _ in loaded]
            probs = []
            for s, (_, _, _, _, _, (m_old, _, _)) in zip(scores, loaded):
                m_new = jnp.maximum(m_old, jnp.max(s, axis=1, keepdims=True))
                p = jnp.exp2(s - jnp.concatenate([m_new, m_new], axis=1)).astype(BF16)
                probs.append((m_new, jnp.exp2(m_old - m_new), p))
            updated = []
            for (m_new, alpha, p), (start, _, _, _, v_ones, (_, l_old, a_old)) in zip(probs, loaded):
                pv = jnp.dot(p, v_ones, preferred_element_type=F32)
                updated.append((start, m_new, alpha * l_old + pv[:, HEAD_DIM:], alpha * a_old + pv[:, :HEAD_DIM]))
            for st, m_new, l_new, a_new in updated:
                m_st[st, :] = m_new
                l_st[st, :] = l_new
                a_st[st, :] = a_new
            return carry

        trips = n_units // ATTN_INTERLEAVE
        phase = window_phases[DILATED_GROUPS.index((window, dil)) + 1]
        if (window, dil) == DILATED_GROUPS[-1]:
            lax.fori_loop(0, trips // 2, body, 0)
            phase()
            lax.fori_loop(trips // 2, trips, body, 0)
        else:
            lax.fori_loop(0, trips, body, 0)
            phase()

    @pl.when(step == last_step)
    def _():
        last_row = row0 + WINDOW_ROWS_PER_STEP - 1
        finish_in_start_out(last_row, 1)
        finish_out(last_row, 0)
        finish_out(last_row, 1)

    if regrouped is None:
        o_ref[0] = acc_scr[...] / l_scr[...]
    else:
        _, sub_len = regrouped
        _from_residue_major(a_rm, q_rm, k_rm, seq, sub_len)
        _from_residue_major(l_rm, k_rm, a_rm, seq, sub_len)
        _from_residue_major(m_rm, v_rm, a_rm, seq, sub_len)
        m_a, m_b = m_scr[...], v_rm[...]
        m_all = jnp.maximum(m_a, m_b)
        w_a, w_b = jnp.exp2(m_a - m_all), jnp.exp2(m_b - m_all)
        o_ref[0] = (acc_scr[...] * w_a + q_rm[...] * w_b) / (l_scr[...] * w_a + k_rm[...] * w_b)


def _attn_prompt(q, k, v, cache_k, cache_v, k_new, v_new):
    b, s, _ = q.shape
    bd, n_buf = cache_k.shape[0], cache_k.shape[1]
    assert bd == b * N_HEADS * WINDOW_ROWS_PER_STEP
    spec = pl.BlockSpec((1, s, HEAD_DIM), lambda bi, hi: (bi, 0, hi))
    any_spec = pl.BlockSpec(memory_space=pl.ANY)
    new_spec = pl.BlockSpec(k_new.shape, lambda bi, hi: (0, 0, 0))
    win_sd = jax.ShapeDtypeStruct(cache_k.shape, cache_k.dtype)
    return pl.pallas_call(
        _attn_prompt_kernel,
        grid=(b, N_HEADS),
        in_specs=[spec, spec, spec, any_spec, any_spec, new_spec, new_spec],
        out_specs=[spec, any_spec, any_spec],
        out_shape=[jax.ShapeDtypeStruct((b, s, D_ATTN), F32), win_sd, win_sd],
        scratch_shapes=[pltpu.VMEM((s, HEAD_DIM), F32)] * 9 + [
            pltpu.VMEM((2, 2, n_buf // 2) + cache_k.shape[2:], F32),
            pltpu.SemaphoreType.DMA((2, 2)),
            pltpu.SemaphoreType.DMA((2, 2)),
        ],
        compiler_params=_cparams(("arbitrary", "arbitrary")),
        name="attn_prompt",
    )(q, k, v, cache_k, cache_v, k_new, v_new)


def _attn_decode_kernel(q_ref, kn_ref, vn_ref, k1_ref, k2_ref, k3_ref, v1_ref, v2_ref, v3_ref, o_ref):
    q = q_ref[0]
    k_new = kn_ref[0]
    v_new = vn_ref[0]
    n_grp = len(DILATED_GROUPS)

    def scores(kb):
        return jnp.sum(kb * q[None], axis=-1, keepdims=True) * ATTN_SCALE

    kcs = [k1_ref[0], k2_ref[0][:, 0], k3_ref[0][:, 0]]
    vcs = [v1_ref[0], v2_ref[0][:, 0], v3_ref[0][:, 0]]
    s_new = jnp.sum(k_new * q, axis=-1, keepdims=True) * ATTN_SCALE
    s_grp = [scores(kb) for kb in kcs]
    m = s_new
    for s in s_grp:
        m = jnp.maximum(m, jnp.max(s, axis=0))
    p_new = jnp.exp(s_new - m)
    den = n_grp * p_new
    num = n_grp * p_new * v_new
    for s, vb in zip(s_grp, vcs):
        p = jnp.exp(s - m[None])
        den = den + jnp.sum(p, axis=0)
        num = num + jnp.sum(p * vb, axis=0)
    o_ref[0] = num / den


def _attn_decode(q, k_new, v_new, cache_k, cache_v):
    bd, n_buf = cache_k.shape[0], cache_k.shape[1]
    assert n_buf == MAX_WINDOW

    def views(cache):
        out, specs = [], []
        for window, dil in DILATED_GROUPS:
            band = window // dil
            assert band == QBLOCK and n_buf % dil == 0 and n_buf // dil >= band
            if dil == 1:
                out.append(cache)
                specs.append(pl.BlockSpec((1, band, N_HEADS, HEAD_DIM),
                                          lambda b, nb=n_buf // band: (b, nb - 1, 0, 0)))
            else:
                out.append(cache.reshape(bd, n_buf // dil, dil, N_HEADS, HEAD_DIM))
                specs.append(pl.BlockSpec((1, band, 1, N_HEADS, HEAD_DIM),
                                          lambda b, nb=n_buf // dil // band: (b, nb - 1, 0, 0, 0)))
        return out, specs

    kv, kspecs = views(cache_k)
    vv, vspecs = views(cache_v)
    tok = pl.BlockSpec((1, N_HEADS, HEAD_DIM), lambda b: (b, 0, 0))
    return pl.pallas_call(
        _attn_decode_kernel,
        grid=(bd,),
        in_specs=[tok, tok, tok] + kspecs + vspecs,
        out_specs=tok,
        out_shape=jax.ShapeDtypeStruct((bd, N_HEADS, HEAD_DIM), F32),
        compiler_params=_cparams(("parallel",)),
        name="attn_decode",
    )(q, k_new, v_new, *kv, *vv)


def _pool_project(d_groups, wp_ref, ps_ref, out_ref):
    for g, d in enumerate(d_groups):
        sl = slice(g * POOL_GROUP_DIM, (g + 1) * POOL_GROUP_DIM)
        out_ref[:, sl] = jnp.dot(d.astype(BF16), wp_ref[g], preferred_element_type=F32) * ps_ref[:, sl]


def _pool_prompt_kernel(u_ref, prev_ref, wp_ref, ps_ref, o_ref):
    i = pl.program_id(1)
    tp = u_ref.shape[1]
    u = u_ref[0]
    prev = jnp.where(i > 0, prev_ref[0], 0.0)
    ext = jnp.concatenate([prev, u], axis=0)
    pos = i * tp + lax.broadcasted_iota(I32, (tp, 1), 0)
    d_groups = []
    for g, w in enumerate(POOL_WINDOWS):
        sl = slice(g * POOL_GROUP_DIM, (g + 1) * POOL_GROUP_DIM)
        a = ext[:, sl]
        span = 1
        while span < w:
            a = a + pltpu.roll(a, span, 0)
            span *= 2
        win = a[POOL_HALO:, :]
        cnt = jnp.minimum(w, pos + 1).astype(F32)
        d_groups.append(win / cnt - u[:, sl])
    _pool_project(d_groups, wp_ref, ps_ref, o_ref.at[0])


def _pool_prompt(u, w_pool_bf, pool_scale):
    b, s, _ = u.shape
    tp = POOL_TILE
    halo_blocks = tp // POOL_HALO
    return pl.pallas_call(
        _pool_prompt_kernel,
        grid=(b, s // tp),
        in_specs=[
            pl.BlockSpec((1, tp, D_POOL), lambda bi, i: (bi, i, 0)),
            pl.BlockSpec((1, POOL_HALO, D_POOL), lambda bi, i: (bi, jnp.maximum(i * halo_blocks - 1, 0), 0)),
            pl.BlockSpec(w_pool_bf.shape, lambda bi, i: (0, 0, 0)),
            pl.BlockSpec((1, D_POOL), lambda bi, i: (0, 0)),
        ],
        out_specs=pl.BlockSpec((1, tp, D_POOL), lambda bi, i: (bi, i, 0)),
        out_shape=jax.ShapeDtypeStruct((b, s, D_POOL), F32),
        compiler_params=_cparams(("parallel", "parallel")),
        name="pool_prompt",
    )(u, u, w_pool_bf, pool_scale)


def _pool_decode_kernel(u_ref, st_ref, wp_ref, ps_ref, o_ref):
    u = u_ref[...]
    d_groups = []
    for g, w in enumerate(POOL_WINDOWS):
        sl = slice(g * POOL_GROUP_DIM, (g + 1) * POOL_GROUP_DIM)
        win = u[:, sl]
        for j in range(1, w):
            win = win + st_ref[POOL_STATE - j][:, sl]
        d_groups.append(win / float(w) - u[:, sl])
    _pool_project(d_groups, wp_ref, ps_ref, o_ref)


def _pool_decode(u, state_t, w_pool_bf, pool_scale):
    assert PAST_LEN + 1 >= max(POOL_WINDOWS)
    return pl.pallas_call(
        _pool_decode_kernel,
        out_shape=jax.ShapeDtypeStruct(u.shape, F32),
        compiler_params=_cparams(None),
        name="pool_decode",
    )(u, state_t, w_pool_bf, pool_scale)


def _split_bf16(x):
    hi = x.astype(BF16)
    lo = (x - hi.astype(F32)).astype(BF16)
    return hi, lo


def _out_kernel(x_ref, a_ref, p_ref, g1_ref, sh_ref, sc_ref, g2_ref, wo_ref, gf_ref, wrh_ref, wrl_ref,
                wsg_ref, wsu_ref, wsd_ref, base_ref, h2_ref, lg_ref):
    tm = x_ref.shape[0]
    mix = (jnp.dot(a_ref[...].astype(BF16), wo_ref[:D_ATTN, :], preferred_element_type=F32)
           + jnp.dot(p_ref[...].astype(BF16), wo_ref[D_ATTN:, :], preferred_element_type=F32))
    x1 = x_ref[...] + g1_ref[0] * mix
    h2 = _rms(x1) * gf_ref[...] * (1.0 + sc_ref[0]) + sh_ref[0]
    for c in range(ROW_CHUNKS):
        h2_ref[pl.ds(c, tm, stride=ROW_CHUNKS), :] = h2[:, c * LANES:(c + 1) * LANES]
    h_hi, h_lo = _split_bf16(h2)
    w_hi, w_lo = wrh_ref[...], wrl_ref[...]
    lg = (jnp.dot(h_hi, w_hi, preferred_element_type=F32) + jnp.dot(h_lo, w_hi, preferred_element_type=F32)
          + jnp.dot(h_hi, w_lo, preferred_element_type=F32))
    if tm < LANES:
        lg = jnp.concatenate([lg, jnp.zeros((LANES - tm, LANES), F32)], axis=0)
    lg_ref[...] = lg.T[:N_EXPERTS, :tm]
    sg = jnp.dot(h_hi, wsg_ref[...], preferred_element_type=F32)
    su = jnp.dot(h_hi, wsu_ref[...], preferred_element_type=F32)
    hs = (sg * jax.nn.sigmoid(sg) * su).astype(BF16)
    base_ref[...] = x1 + g2_ref[0] * jnp.dot(hs, wsd_ref[...], preferred_element_type=F32)


def _out_proj(x2d, attn, pool, gate1, shift2, scale2, gate2, w_out_bf, g_ffn, w_router_hi, w_router_lo,
              ws_gate_bf, ws_up_bf, ws_down_bf, tm, rows_per_mod, h2_tokens, h2_all, h2_block0):
    m = x2d.shape[0]
    r = gate1.shape[1]
    n_main = m // tm
    h2_rows = h2_tokens * ROW_CHUNKS
    n_fill = 0 if h2_all is not None else pl.cdiv(h2_rows - m * ROW_CHUNKS, tm * ROW_CHUNKS)
    last = n_main - 1
    row = lambda i: jnp.minimum(i, last)
    mod_spec = pl.BlockSpec((1, r, D_MODEL), lambda i: (row(i) // rows_per_mod, 0, 0))
    const = lambda shape: pl.BlockSpec(shape, lambda i: (0,) * len(shape))
    in_specs = [
        pl.BlockSpec((tm, D_MODEL), lambda i: (row(i), 0)),
        pl.BlockSpec((tm, D_ATTN), lambda i: (row(i), 0)),
        pl.BlockSpec((tm, D_POOL), lambda i: (row(i), 0)),
        mod_spec, mod_spec, mod_spec, mod_spec,
        const((D_MODEL, D_MODEL)),
        const((1, D_MODEL)),
        const((D_MODEL, LANES)), const((D_MODEL, LANES)),
        const((D_MODEL, D_SHARED)), const((D_MODEL, D_SHARED)), const((D_SHARED, D_MODEL)),
    ]
    args = [x2d, attn, pool, gate1, shift2, scale2, gate2, w_out_bf, g_ffn, w_router_hi, w_router_lo,
            ws_gate_bf, ws_up_bf, ws_down_bf]
    aliases = {}
    n_in = len(args)
    if h2_all is not None:
        in_specs.append(pl.BlockSpec(memory_space=pl.ANY))
        args.append(h2_all)
        aliases = {n_in: 1}

    def kernel(*refs):
        refs = refs[:n_in] + refs[len(args):]
        if n_fill == 0:
            _out_kernel(*refs)
            return
        step = pl.program_id(0)
        pl.when(step < n_main)(lambda: _out_kernel(*refs))

        @pl.when(step >= n_main)
        def _():
            h2_ref = refs[n_in + 1]
            h2_ref[...] = jnp.zeros(h2_ref.shape, F32)

    return pl.pallas_call(
        kernel,
        grid=(n_main + n_fill,),
        in_specs=in_specs,
        out_specs=[
            pl.BlockSpec((tm, D_MODEL), lambda i: (row(i), 0)),
            pl.BlockSpec((tm * ROW_CHUNKS, LANES), lambda i: (i + h2_block0, 0)),
            pl.BlockSpec((N_EXPERTS, tm), lambda i: (0, row(i))),
        ],
        out_shape=[
            jax.ShapeDtypeStruct((m, D_MODEL), F32),
            jax.ShapeDtypeStruct((h2_rows, LANES), F32),
            jax.ShapeDtypeStruct((N_EXPERTS, m), F32),
        ],
        input_output_aliases=aliases,
        compiler_params=_cparams(("arbitrary",)),
        name="out_proj",
    )(*args)


def _first_index_of_max(v, iota, size, axis):
    m = jnp.max(v, axis=axis, keepdims=True)
    idx = jnp.min(jnp.where(v == m, iota, size), axis=axis, keepdims=True)
    return m, idx


def _route_kernel(n_valid, lg_ref, b_ref, eidx_ref, wts_ref, cnt_ref, carry):
    i = pl.program_id(0)
    tn = lg_ref.shape[1]

    @pl.when(i == 0)
    def _():
        carry[...] = jnp.zeros(carry.shape, F32)

    scores = jax.nn.sigmoid(lg_ref[...])
    biased = scores + b_ref[...]
    grp = biased.reshape(N_EXPERT_GROUPS, GROUP_SIZE, tn)
    io_g = lax.broadcasted_iota(I32, grp.shape, 1)
    m1, i1 = _first_index_of_max(grp, io_g, GROUP_SIZE, 1)
    m2 = jnp.max(jnp.where(io_g == i1, -jnp.inf, grp), axis=1, keepdims=True)
    gscore = (m1 + m2)[:, 0, :]
    io_n = lax.broadcasted_iota(I32, gscore.shape, 0)
    gsel = jnp.zeros(gscore.shape, jnp.bool_)
    for _ in range(TOPK_GROUPS):
        _, gi = _first_index_of_max(gscore, io_n, N_EXPERT_GROUPS, 0)
        hit = io_n == gi
        gsel = jnp.logical_or(gsel, hit)
        gscore = jnp.where(hit, -jnp.inf, gscore)
    emask = jnp.broadcast_to(gsel[:, None, :], grp.shape).reshape(N_EXPERTS, tn)
    cand = jnp.where(emask, biased, -jnp.inf)
    io_e = lax.broadcasted_iota(I32, cand.shape, 0)
    picked = jnp.zeros(cand.shape, jnp.bool_)
    eidx, sel = [], []
    for _ in range(TOP_K):
        _, ei = _first_index_of_max(cand, io_e, N_EXPERTS, 0)
        hit = io_e == ei
        eidx.append(ei)
        sel.append(jnp.sum(jnp.where(hit, scores, 0.0), axis=0, keepdims=True))
        picked = jnp.logical_or(picked, hit)
        cand = jnp.where(hit, -jnp.inf, cand)
    sel = jnp.concatenate(sel, axis=0)
    eidx = jnp.concatenate(eidx, axis=0)
    wts_ref[...] = sel / jnp.sum(sel, axis=0, keepdims=True) * ROUTED_SCALE
    eidx_ref[...] = eidx

    tok = i * tn + lax.broadcasted_iota(I32, cand.shape, 1)
    mask = jnp.logical_and(picked, tok < n_valid).astype(F32)
    total = carry[...] + jnp.sum(mask, axis=1, keepdims=True)
    carry[...] = total
    cnt_ref[...] = total.astype(I32)


def _route(logits_t, b_router, n_valid):
    mp = logits_t.shape[1]
    tn = ROUTE_TILE
    tok_spec = pl.BlockSpec((TOP_K, tn), lambda i: (0, i))
    return pl.pallas_call(
        functools.partial(_route_kernel, n_valid),
        grid=(mp // tn,),
        in_specs=[
            pl.BlockSpec((N_EXPERTS, tn), lambda i: (0, i)),
            pl.BlockSpec((N_EXPERTS, 1), lambda i: (0, 0)),
        ],
        out_specs=[tok_spec, tok_spec, pl.BlockSpec((N_EXPERTS, LANES), lambda i: (0, 0))],
        out_shape=[
            jax.ShapeDtypeStruct((TOP_K, mp), I32),
            jax.ShapeDtypeStruct((TOP_K, mp), F32),
            jax.ShapeDtypeStruct((N_EXPERTS, LANES), I32),
        ],
        scratch_shapes=[pltpu.VMEM((N_EXPERTS, LANES), F32)],
        compiler_params=_cparams(("arbitrary",)),
        name="route",
    )(logits_t, b_router.reshape(N_EXPERTS, 1))


TILE_USED = 1
TILE_FIRST = 2
TILE_WSLOT = 4
TILE_PREV_USED = 8
PAIR_BITS = 3
assert 1 << PAIR_BITS == TOP_K
DUMMY_TOKENS = 2 * EXPERT_TILE // TOP_K
GATHER_SLOTS = 3


def _experts_kernel(n_tok, te_ref, ts_ref, tn_ref, tf_ref, ne_ref, g_ref, s_ref,
                    h_hbm, wg_hbm, wu_hbm, wd_hbm, y_hbm,
                    xbuf, ybuf, xs, wg_f, wu_f, wd_f, wg_bf, wu_bf, wd_bf, gsem, ssem, wsem, zsem):
    j = pl.program_id(0)
    last = pl.num_programs(0) - 1
    tm = EXPERT_TILE
    rc = ROW_CHUNKS
    plane_tokens = n_tok + DUMMY_TOKENS
    slot = j % 2
    other = 1 - slot
    gslot = j % GATHER_SLOTS
    flags = tf_ref[j]
    used = (flags & TILE_USED) != 0
    first = (flags & TILE_FIRST) != 0
    prev_used = (flags & TILE_PREV_USED) != 0
    wslot = (flags // TILE_WSLOT) & 1
    prv = jnp.maximum(j - 1, 0)
    ahead = jnp.minimum(j + GATHER_SLOTS - 1, last)

    def buf_rows(i):
        return pl.ds(i * rc, rc) if isinstance(i, int) else pl.ds(pl.multiple_of(i * rc, rc), rc)

    def gather_row(i, start, dst):
        row = g_ref[start + i]
        pltpu.make_async_copy(h_hbm.at[pl.ds(pl.multiple_of(row, rc), rc), :],
                              xbuf.at[dst, buf_rows(i), :], gsem.at[dst]).start(priority=0)

    def scatter_row(i, start, n_valid, src):
        spare = ((i & (TOP_K - 1)) * plane_tokens + n_tok + (i >> PAIR_BITS)) * rc + src * ((tm >> PAIR_BITS) * rc)
        row = jnp.where(i < n_valid, s_ref[start + i], spare)
        pltpu.make_async_copy(ybuf.at[src, buf_rows(i), :],
                              y_hbm.at[pl.ds(pl.multiple_of(row, rc), rc), :],
                              ssem.at[src]).start(priority=1 if isinstance(i, int) else 0)

    def scatter_tile_loop(tile, src):
        start, n_valid = ts_ref[tile], tn_ref[tile]
        lax.fori_loop(0, tm, lambda i, c: (scatter_row(i, start, n_valid, src), c)[1], 0)

    def wait_gather(dst):
        pltpu.make_async_copy(h_hbm.at[pl.ds(0, tm * rc), :], xbuf.at[dst], gsem.at[dst]).wait()

    def wait_scatter(src):
        pltpu.make_async_copy(ybuf.at[src], y_hbm.at[pl.ds(0, tm * rc), :], ssem.at[src]).wait()

    def weight_copies(e, ws):
        return [pltpu.make_async_copy(src.at[e], dst.at[ws], wsem.at[ws])
                for src, dst in ((wg_hbm, wg_f), (wu_hbm, wu_f), (wd_hbm, wd_f))]

    @pl.when(j == 0)
    def _():
        ybuf[1] = jnp.zeros(ybuf.shape[1:], ybuf.dtype)
        fills = [pltpu.make_async_copy(ybuf.at[1, pl.ds(0, DUMMY_TOKENS * rc), :],
                                       y_hbm.at[pl.ds((k * plane_tokens + n_tok) * rc, DUMMY_TOKENS * rc), :],
                                       zsem.at[0]) for k in range(TOP_K)]
        for c in fills:
            c.start()
        for c in fills:
            c.wait()
        for c in weight_copies(te_ref[0], 0):
            c.start(priority=1)
        for t in range(GATHER_SLOTS - 1):
            start_t = ts_ref[jnp.minimum(t, last)]
            lax.fori_loop(0, tm, lambda i, c, start_t=start_t, t=t: (gather_row(i, start_t, t), c)[1], 0)

    @pl.when(first)
    def _():
        for c in weight_copies(te_ref[j], wslot):
            c.wait()
        wg_bf[...] = wg_f[wslot].astype(BF16)
        wu_bf[...] = wu_f[wslot].astype(BF16)
        wd_bf[...] = wd_f[wslot].astype(BF16)
        nxt_e = ne_ref[j]

        @pl.when(nxt_e >= 0)
        def _():
            for c in weight_copies(nxt_e, 1 - wslot):
                c.start(priority=1)

    @pl.when(jnp.logical_and(used, j > 0))
    def _():
        wait_scatter(slot)

    @pl.when(used)
    def _():
        wait_gather(gslot)
        xs[...] = jnp.concatenate(
            [xbuf[gslot, pl.ds(c, tm, stride=rc), :] for c in range(rc)], axis=1).astype(BF16)
        start_n = ts_ref[ahead]
        ahead_slot = (j + GATHER_SLOTS - 1) % GATHER_SLOTS
        for i in range(tm):
            gather_row(i, start_n, ahead_slot)
        start_p = ts_ref[prv]
        n_valid_p = jnp.where(j > 0, tn_ref[prv], 0)
        for i in range(tm):
            scatter_row(i, start_p, n_valid_p, other)
        x = xs[...]
        hg = jnp.dot(x, wg_bf[...], preferred_element_type=F32)
        hu = jnp.dot(x, wu_bf[...], preferred_element_type=F32)
        h = (hg * jax.nn.sigmoid(hg) * hu).astype(BF16)
        y = jnp.dot(h, wd_bf[...], preferred_element_type=F32)
        for c in range(rc):
            ybuf[slot, pl.ds(c, tm, stride=rc), :] = y[:, c * LANES:(c + 1) * LANES]

    @pl.when(jnp.logical_and(jnp.logical_not(used), prev_used))
    def _():
        for t in range(GATHER_SLOTS - 1):
            wait_gather((j + t) % GATHER_SLOTS)
        wait_scatter(slot)
        scatter_tile_loop(prv, other)
        wait_scatter(other)

    @pl.when(jnp.logical_and(used, j == last))
    def _():
        for t in range(1, GATHER_SLOTS):
            wait_gather((j + t) % GATHER_SLOTS)
        wait_scatter(other)
        scatter_tile_loop(j, slot)
        wait_scatter(slot)


def _experts(tile_expert, tile_start, tile_rows, tile_flags, next_expert, gather_rows, scatter_rows, h2_all,
             w_gate, w_up, w_down, n_tok):
    n_tiles = tile_expert.shape[0]
    tm = EXPERT_TILE
    any_spec = pl.BlockSpec(memory_space=pl.ANY)
    grid_spec = pltpu.PrefetchScalarGridSpec(
        num_scalar_prefetch=7,
        grid=(n_tiles,),
        in_specs=[any_spec] * 4,
        out_specs=any_spec,
        scratch_shapes=[
            pltpu.VMEM((GATHER_SLOTS, tm * ROW_CHUNKS, LANES), F32),
            pltpu.VMEM((2, tm * ROW_CHUNKS, LANES), F32),
            pltpu.VMEM((tm, D_MODEL), BF16),
            pltpu.VMEM((2, D_MODEL, D_EXPERT), F32),
            pltpu.VMEM((2, D_MODEL, D_EXPERT), F32),
            pltpu.VMEM((2, D_EXPERT, D_MODEL), F32),
            pltpu.VMEM((D_MODEL, D_EXPERT), BF16),
            pltpu.VMEM((D_MODEL, D_EXPERT), BF16),
            pltpu.VMEM((D_EXPERT, D_MODEL), BF16),
            pltpu.SemaphoreType.DMA((GATHER_SLOTS,)),
            pltpu.SemaphoreType.DMA((2,)),
            pltpu.SemaphoreType.DMA((2,)),
            pltpu.SemaphoreType.DMA((1,)),
        ],
    )
    return pl.pallas_call(
        functools.partial(_experts_kernel, n_tok),
        grid_spec=grid_spec,
        out_shape=jax.ShapeDtypeStruct((TOP_K * (n_tok + DUMMY_TOKENS) * ROW_CHUNKS, LANES), F32),
        compiler_params=_cparams(("arbitrary",)),
        name="experts",
    )(tile_expert, tile_start, tile_rows, tile_flags, next_expert, gather_rows, scatter_rows, h2_all,
      w_gate, w_up, w_down)


COMBINE_UNROLL = 4


def _combine_kernel(w_ref, y_ref, base_ref, g2_ref, o_ref, acc_rows):
    i = pl.program_id(0)
    tc = base_ref.shape[0]
    rc = ROW_CHUNKS

    def body(g, carry):
        for u in range(COMBINE_UNROLL):
            t = g * COMBINE_UNROLL + u
            rows = pl.ds(pl.multiple_of(t * rc, rc), rc)
            w0 = (i * tc + t) * TOP_K
            acc = w_ref[w0] * y_ref[0, rows, :]
            for k in range(1, TOP_K):
                acc = acc + w_ref[w0 + k] * y_ref[k, rows, :]
            acc_rows[rows, :] = acc
        return carry

    lax.fori_loop(0, tc // COMBINE_UNROLL, body, 0)
    g2 = g2_ref[0]
    for c in range(rc):
        sl = slice(c * LANES, (c + 1) * LANES)
        o_ref[:, sl] = base_ref[:, sl] + g2[:, sl] * acc_rows[pl.ds(c, tc, stride=rc), :]


def _combine(y_planes, wts, base, gate2, tc, rows_per_mod, block0):
    m = base.shape[0]
    r = gate2.shape[1]
    assert tc % COMBINE_UNROLL == 0
    grid_spec = pltpu.PrefetchScalarGridSpec(
        num_scalar_prefetch=1,
        grid=(m // tc,),
        in_specs=[
            pl.BlockSpec((TOP_K, tc * ROW_CHUNKS, LANES), lambda i, w: (0, i + block0, 0)),
            pl.BlockSpec((tc, D_MODEL), lambda i, w: (i, 0)),
            pl.BlockSpec((1, r, D_MODEL), lambda i, w: (i // rows_per_mod, 0, 0)),
        ],
        out_specs=pl.BlockSpec((tc, D_MODEL), lambda i, w: (i, 0)),
        scratch_shapes=[pltpu.VMEM((tc * ROW_CHUNKS, LANES), F32)],
    )
    return pl.pallas_call(
        _combine_kernel,
        grid_spec=grid_spec,
        out_shape=jax.ShapeDtypeStruct((m, D_MODEL), F32),
        compiler_params=_cparams(("parallel",)),
        name="combine",
    )(wts.reshape(-1), y_planes, base, gate2)


def _moe_routed(h2_all, logits_t, b_router, w_gate, w_up, w_down):
    n_tok = logits_t.shape[1]
    mp = -(-n_tok // ROUTE_TILE) * ROUTE_TILE
    logits_t = jnp.pad(logits_t, ((0, 0), (0, mp - n_tok)))
    eidx, wts, counts = _route(logits_t, b_router, n_tok)
    eidx, wts, counts = eidx[:, :n_tok], wts[:, :n_tok], counts[:, 0]

    tm = EXPERT_TILE
    n_pairs = n_tok * TOP_K
    n_tiles = n_pairs // tm + N_EXPERTS
    pair_code = jnp.arange(n_tok, dtype=I32)[None, :] * TOP_K + jnp.arange(TOP_K, dtype=I32)[:, None]
    (sorted_pairs,) = lax.sort(((eidx * n_pairs + pair_code).reshape(-1),), is_stable=False)
    sorted_pairs = jnp.concatenate([sorted_pairs % n_pairs, jnp.zeros((tm,), I32)])
    pair_tok, pair_choice = sorted_pairs >> PAIR_BITS, sorted_pairs & (TOP_K - 1)
    gather_rows = pair_tok * ROW_CHUNKS
    scatter_rows = (pair_choice * (n_tok + DUMMY_TOKENS) + pair_tok) * ROW_CHUNKS
    tiles_e = (counts + tm - 1) // tm
    tile_end = jnp.cumsum(tiles_e)
    tile_begin = tile_end - tiles_e
    dense_begin = jnp.cumsum(counts) - counts
    tj = jnp.arange(n_tiles, dtype=I32)
    total_tiles = tile_end[-1]
    used = tj < total_tiles
    tj_used = jnp.minimum(tj, total_tiles - 1)[:, None]
    member = jnp.logical_and(tile_begin[None, :] <= tj_used, tj_used < tile_end[None, :])
    of_tile = lambda per_expert: jnp.sum(jnp.where(member, per_expert[None, :], 0), axis=1)
    e_ids = jnp.arange(N_EXPERTS, dtype=I32)
    te = of_tile(e_ids)
    local = tj - of_tile(tile_begin)
    ts = jnp.where(used, of_tile(dense_begin) + local * tm, 0).astype(I32)
    tn = jnp.where(used, jnp.clip(of_tile(counts) - local * tm, 0, tm), 0).astype(I32)
    first = jnp.logical_and(used, local == 0)
    busy = counts > 0
    wslot = of_tile((jnp.cumsum(busy.astype(I32)) - 1) % 2)
    later = jnp.logical_and(e_ids[None, :] > e_ids[:, None], busy[None, :])
    succ = jnp.min(jnp.where(later, e_ids[None, :], N_EXPERTS), axis=1)
    succ = jnp.where(succ == N_EXPERTS, -1, succ)
    ne = jnp.where(first, of_tile(succ), -1).astype(I32)
    prev_used = jnp.concatenate([jnp.zeros((1,), jnp.bool_), used[:-1]])
    tf = (used.astype(I32) * TILE_USED + first.astype(I32) * TILE_FIRST + wslot * TILE_WSLOT
          + prev_used.astype(I32) * TILE_PREV_USED)
    y_all = _experts(te, ts, tn, tf, ne, gather_rows, scatter_rows, h2_all, w_gate, w_up, w_down, n_tok)
    return y_all.reshape(TOP_K, (n_tok + DUMMY_TOKENS) * ROW_CHUNKS, LANES), wts.T


def kernel(x_prompt, x_sample, cache_k_win, cache_v_win, state_pool, c_prompt, c_sample, w_ada, b_ada, g_mix, w_in, q_norm, k_norm, w_pool, pool_scale, w_out, g_ffn, w_router, b_router, w_gate, w_up, w_down, ws_gate, ws_up, ws_down):
    depth = w_ada.shape[0]
    assert depth == 1
    l = 0
    nb, seq, _ = x_prompt.shape
    bd, dec_seq, _ = x_sample.shape
    assert dec_seq == 1 and seq % (TOKEN_TILE * 16) == 0
    n_prompt = nb * seq
    n_tok = n_prompt + bd
    n_keep = min(MAX_WINDOW, seq)
    tiles_per_batch = seq // TOKEN_TILE

    n_mod_rows = -(-(nb + bd) // 8) * 8
    c_all = jnp.concatenate([c_prompt, c_sample, jnp.zeros((n_mod_rows - nb - bd, D_MODEL), F32)], axis=0)
    mods = _ada(c_all, w_ada[l], b_ada[l])
    mods_p = [mods[:nb, j * D_MODEL:(j + 1) * D_MODEL].reshape(nb, 1, D_MODEL) for j in range(6)]
    mods_s = [mods[nb:nb + bd, j * D_MODEL:(j + 1) * D_MODEL].reshape(1, bd, D_MODEL) for j in range(6)]

    g_mix_l = g_mix[l].reshape(1, D_MODEL)
    g_ffn_l = g_ffn[l].reshape(1, D_MODEL)
    qn = q_norm[l].reshape(1, HEAD_DIM)
    kn = k_norm[l].reshape(1, HEAD_DIM)
    w_in_bf = w_in[l].astype(BF16)
    w_out_bf = w_out[l].astype(BF16)
    w_pool_bf = w_pool[l].astype(BF16)
    ps = pool_scale[l].reshape(1, D_POOL)
    w_router_pad = jnp.pad(w_router[l], ((0, 0), (0, LANES - N_EXPERTS)))
    w_router_hi = w_router_pad.astype(BF16)
    w_router_lo = (w_router_pad - w_router_hi.astype(F32)).astype(BF16)
    wsg_bf, wsu_bf, wsd_bf = ws_gate[l].astype(BF16), ws_up[l].astype(BF16), ws_down[l].astype(BF16)

    rope_p = _rope_tables(jnp.arange(seq, dtype=I32))
    rope_s = _rope_tables(jnp.full((bd,), PAST_LEN, I32))
    xp2 = x_prompt.reshape(n_prompt, D_MODEL)
    xs2 = x_sample.reshape(bd, D_MODEL)
    assert n_keep % TOKEN_TILE == 0
    qp, kp, vp, up, kwin_p, vwin_p = _in_proj(xp2, mods_p[0], mods_p[1], g_mix_l, w_in_bf, qn, kn, rope_p,
                                              TOKEN_TILE, tiles_per_batch, tiles_per_batch, n_keep // TOKEN_TILE)
    qs, _, _, us, k_new, v_new = _in_proj(xs2, mods_s[0], mods_s[1], g_mix_l, w_in_bf, qn, kn, rope_s, bd, 1, 1, 1)

    to_seq = lambda t: t.reshape(nb, seq, -1)
    heads = lambda t: t.reshape(bd, N_HEADS, HEAD_DIM)
    ck, cv = cache_k_win[l], cache_v_win[l]
    k_new, v_new = heads(k_new), heads(v_new)
    attn_p, k_win_s, v_win_s = _attn_prompt(to_seq(qp), to_seq(kp), to_seq(vp), ck, cv, k_new, v_new)
    attn_p = attn_p.reshape(n_prompt, D_ATTN)
    pool_p = _pool_prompt(to_seq(up), w_pool_bf, ps).reshape(n_prompt, D_POOL)
    attn_s = _attn_decode(heads(qs), k_new, v_new, ck, cv).reshape(bd, D_ATTN)
    pool_s = _pool_decode(us, jnp.swapaxes(state_pool[l], 0, 1), w_pool_bf, ps)

    base_p, h2_all, lg_p = _out_proj(xp2, attn_p, pool_p, mods_p[2], mods_p[3], mods_p[4], mods_p[5],
                                     w_out_bf, g_ffn_l, w_router_hi, w_router_lo, wsg_bf, wsu_bf, wsd_bf,
                                     TOKEN_TILE, tiles_per_batch, n_tok, None, 0)
    base_s, h2_all, lg_s = _out_proj(xs2, attn_s, pool_s, mods_s[2], mods_s[3], mods_s[4], mods_s[5],
                                     w_out_bf, g_ffn_l, w_router_hi, w_router_lo, wsg_bf, wsu_bf, wsd_bf,
                                     bd, 1, n_tok, h2_all, n_prompt // bd)

    y_planes, wts_t = _moe_routed(h2_all, jnp.concatenate([lg_p, lg_s], axis=1), b_router[l],
                                  w_gate[l], w_up[l], w_down[l])

    yp = _combine(y_planes, wts_t[:n_prompt], base_p, mods_p[5], COMBINE_TILE, seq // COMBINE_TILE, 0)
    ys = _combine(y_planes, wts_t[n_prompt:n_tok], base_s, mods_s[5], bd, 1, n_prompt // bd)

    y_prompt = yp.reshape(nb, seq, D_MODEL)
    y_sample = ys.reshape(bd, dec_seq, D_MODEL)
    win = lambda t: t.reshape(1, nb, n_keep, N_HEADS, HEAD_DIM)
    pool_p_state = to_seq(up)[:, seq - POOL_STATE:][None]
    pool_s_state = jnp.concatenate([state_pool[l][:, 1:], us[:, None, :]], axis=1)[None]
    return (y_prompt, y_sample, win(kwin_p), win(vwin_p), pool_p_state,
            k_win_s[None], v_win_s[None], pool_s_state)
```

```python
import functools

import jax
import jax.numpy as jnp
from jax import lax
from jax.experimental import pallas as pl
from jax.experimental.pallas import tpu as pltpu

F32 = jnp.float32
BF16 = jnp.bfloat16
I32 = jnp.int32

D_MODEL = 2048
N_HEADS = 8
HEAD_DIM = 128
D_ATTN = N_HEADS * HEAD_DIM
D_POOL = D_MODEL - D_ATTN
D_IN = 3 * D_ATTN + D_POOL
ROPE_DIM = HEAD_DIM // 4
ROPE_HALF = ROPE_DIM // 2
ROPE_THETA = 500000.0
DILATED_GROUPS = ((128, 1), (512, 4), (2048, 16))
QBLOCK = 128
ATTN_SCALE = HEAD_DIM ** -0.5
POOL_WINDOWS = (2, 4, 8, 16)
POOL_GROUP_DIM = D_POOL // len(POOL_WINDOWS)
POOL_STATE = max(POOL_WINDOWS) - 1
POOL_HALO = 16
N_EXPERTS = 64
TOP_K = 8
N_EXPERT_GROUPS = 8
GROUP_SIZE = N_EXPERTS // N_EXPERT_GROUPS
TOPK_GROUPS = 4
D_EXPERT = 512
D_SHARED = 512
ROUTED_SCALE = 2.5
EPS = 1e-6
PAST_LEN = 16384
MAX_WINDOW = 2048

LANES = 128
ROW_CHUNKS = D_MODEL // LANES
NEG_BIG = -1e30
LOG2_E = 1.4426950408889634

TOKEN_TILE = 256
EXPERT_TILE = 256
COMBINE_TILE = 128
ROUTE_TILE = 256
ADA_COL_TILE = 1024
POOL_TILE = 512
VMEM_LIMIT = 60 * 1024 * 1024


def _cparams(sem, vmem=VMEM_LIMIT):
    return pltpu.CompilerParams(dimension_semantics=sem, vmem_limit_bytes=vmem)


def _ada_kernel(c_ref, w_ref, b_ref, o_ref):
    c = c_ref[...]
    s = (c * jax.nn.sigmoid(c)).astype(BF16)
    o_ref[...] = jnp.dot(s, w_ref[...].astype(BF16), preferred_element_type=F32) + b_ref[...]


def _ada(c_all, w_ada, b_ada):
    rows = c_all.shape[0]
    n = w_ada.shape[1]
    tn = ADA_COL_TILE
    return pl.pallas_call(
        _ada_kernel,
        grid=(n // tn,),
        in_specs=[
            pl.BlockSpec((rows, D_MODEL), lambda j: (0, 0)),
            pl.BlockSpec((D_MODEL, tn), lambda j: (0, j)),
            pl.BlockSpec((1, tn), lambda j: (0, j)),
        ],
        out_specs=pl.BlockSpec((rows, tn), lambda j: (0, j)),
        out_shape=jax.ShapeDtypeStruct((rows, n), F32),
        compiler_params=_cparams(("parallel",)),
        name="ada_modulation",
    )(c_all, w_ada, b_ada.reshape(1, n))


def _rms(x):
    return x * lax.rsqrt(jnp.mean(x * x, axis=-1, keepdims=True) + EPS)


def _in_kernel(x_ref, sh_ref, sc_ref, g_ref, w_ref, qn_ref, kn_ref, c_ref, s1_ref, s2_ref,
               q_ref, k_ref, v_ref, u_ref, kw_ref, vw_ref):
    tm = x_ref.shape[0]
    x = x_ref[...]
    h = _rms(x) * g_ref[...] * (1.0 + sc_ref[0]) + sh_ref[0]
    hb = h.astype(BF16)
    cos = c_ref[...]
    s1 = s1_ref[...]
    s2 = s2_ref[...]

    def head_rows(win_ref, hd, val):
        win_ref[pl.ds(hd, tm, stride=N_HEADS), :] = val

    def qk(sec, nrm, out_ref, win_ref):
        z = jnp.dot(hb, w_ref[:, sec * D_ATTN:(sec + 1) * D_ATTN], preferred_element_type=F32)
        for hd in range(N_HEADS):
            sl = slice(hd * HEAD_DIM, (hd + 1) * HEAD_DIM)
            r = _rms(z[:, sl]) * nrm
            r = r * cos + pltpu.roll(r, HEAD_DIM - ROPE_HALF, 1) * s1 + pltpu.roll(r, ROPE_HALF, 1) * s2
            out_ref[:, sl] = r
            if win_ref is not None:
                head_rows(win_ref, hd, r)

    qk(0, qn_ref[...], q_ref, None)
    qk(1, kn_ref[...], k_ref, kw_ref)
    v = jnp.dot(hb, w_ref[:, 2 * D_ATTN:3 * D_ATTN], preferred_element_type=F32)
    v_ref[...] = v
    for hd in range(N_HEADS):
        head_rows(vw_ref, hd, v[:, hd * HEAD_DIM:(hd + 1) * HEAD_DIM])
    u_ref[...] = jnp.dot(hb, w_ref[:, 3 * D_ATTN:], preferred_element_type=F32)


def _in_proj(x2d, shift, scale, g_mix, w_in_bf, q_norm, k_norm, rope, tm, rows_per_mod, rope_tiles, keep_tiles):
    m = x2d.shape[0]
    r = shift.shape[1]
    n_groups = m // tm // rows_per_mod
    skip = rows_per_mod - keep_tiles
    mod_spec = pl.BlockSpec((1, r, D_MODEL), lambda i: (i // rows_per_mod, 0, 0))
    rope_spec = pl.BlockSpec((tm, HEAD_DIM), lambda i: (i % rope_tiles, 0))
    out_spec = pl.BlockSpec((tm, D_ATTN), lambda i: (i, 0))
    out_sd = jax.ShapeDtypeStruct((m, D_ATTN), F32)
    win_spec = pl.BlockSpec(
        (tm * N_HEADS, HEAD_DIM),
        lambda i: ((i // rows_per_mod) * keep_tiles + jnp.maximum(i % rows_per_mod - skip, 0), 0))
    win_sd = jax.ShapeDtypeStruct((n_groups * keep_tiles * tm * N_HEADS, HEAD_DIM), F32)
    return pl.pallas_call(
        _in_kernel,
        grid=(m // tm,),
        in_specs=[
            pl.BlockSpec((tm, D_MODEL), lambda i: (i, 0)),
            mod_spec, mod_spec,
            pl.BlockSpec((1, D_MODEL), lambda i: (0, 0)),
            pl.BlockSpec((D_MODEL, D_IN), lambda i: (0, 0)),
            pl.BlockSpec((1, HEAD_DIM), lambda i: (0, 0)),
            pl.BlockSpec((1, HEAD_DIM), lambda i: (0, 0)),
            rope_spec, rope_spec, rope_spec,
        ],
        out_specs=[out_spec] * 4 + [win_spec] * 2,
        out_shape=[out_sd] * 4 + [win_sd] * 2,
        compiler_params=_cparams(("arbitrary",)),
        name="in_proj",
    )(x2d, shift, scale, g_mix, w_in_bf, q_norm, k_norm, *rope)


def _rope_tables(pos):
    inv_freq = jnp.float32(ROPE_THETA) ** (-jnp.arange(ROPE_HALF, dtype=F32) / ROPE_HALF)
    ang = pos.astype(F32)[:, None] * inv_freq[None, :]
    cos, sin = jnp.cos(ang), jnp.sin(ang)
    t = pos.shape[0]
    rest = HEAD_DIM - ROPE_DIM
    c = jnp.concatenate([cos, cos, jnp.ones((t, rest), F32)], axis=1)
    s1 = jnp.concatenate([-sin, jnp.zeros((t, ROPE_HALF + rest), F32)], axis=1)
    s2 = jnp.concatenate([jnp.zeros((t, ROPE_HALF), F32), sin, jnp.zeros((t, rest), F32)], axis=1)
    return c, s1, s2


ATTN_INTERLEAVE = 4
WINDOW_ROWS_PER_STEP = 2
REGROUP_MIN_DILATION = 16


REGROUP_STRIDE = 4


def _to_residue_major(read_rows, dst, tmp, seq, sub_len):
    n, part = REGROUP_STRIDE, seq // REGROUP_STRIDE
    for a in range(n):
        tmp[pl.ds(a * part, part), :] = read_rows(pl.ds(a, part, stride=n))
    for a in range(n):
        for s in range(n):
            dst[pl.ds((s * n + a) * sub_len, sub_len), :] = tmp[pl.ds(a * part + s, sub_len, stride=n), :]


def _from_residue_major(src, dst, tmp, seq, sub_len):
    n, part = REGROUP_STRIDE, seq // REGROUP_STRIDE
    for a in range(n):
        for s in range(n):
            tmp[pl.ds(a * part + s, sub_len, stride=n), :] = src[pl.ds((s * n + a) * sub_len, sub_len), :]
    for a in range(n):
        dst[pl.ds(a, part, stride=n), :] = tmp[pl.ds(a * part, part), :]


def _attn_prompt_kernel(q_ref, k_ref, v_ref, ck_hbm, cv_hbm, kn_ref, vn_ref, o_ref, ok_hbm, ov_hbm,
                        m_scr, l_scr, acc_scr, q_rm, k_rm, v_rm, m_rm, l_rm, a_rm, wbuf, isem, osem):
    seq = q_ref.shape[1]
    half = wbuf.shape[2]
    step = pl.program_id(0) * pl.num_programs(1) + pl.program_id(1)
    last_step = pl.num_programs(0) * pl.num_programs(1) - 1
    windows = ((ck_hbm, kn_ref, ok_hbm), (cv_hbm, vn_ref, ov_hbm))
    row0 = step * WINDOW_ROWS_PER_STEP
    n_bounds = 2 * WINDOW_ROWS_PER_STEP
    assert n_bounds == len(DILATED_GROUPS) + 1

    def chunk_in(row, h, a):
        n_rows = half - h
        return pltpu.make_async_copy(windows[a][0].at[row, pl.ds(h * half + 1, n_rows)],
                                     wbuf.at[h, a, pl.ds(0, n_rows)], isem.at[h, a])

    def chunk_out(row, h, a):
        return pltpu.make_async_copy(wbuf.at[h, a], windows[a][2].at[row, pl.ds(h * half, half)], osem.at[h, a])

    def finish_in_start_out(row, h):
        for a in range(2):
            chunk_in(row, h, a).wait()
            if h == 1:
                wbuf[h, a, half - 1] = windows[a][1][row]
            chunk_out(row, h, a).start()

    def finish_out(row, h):
        for a in range(2):
            chunk_out(row, h, a).wait()

    def window_boundary(b):
        h = b % 2
        if b == 0:
            @pl.when(step > 0)
            def _():
                finish_in_start_out(row0 - 1, 1)
                finish_out(row0 - 1, 0)
        elif b == 1:
            finish_in_start_out(row0, 0)
            pl.when(step > 0)(lambda: finish_out(row0 - 1, 1))
        else:
            finish_in_start_out(row0 + (b - 1) // 2, (b - 1) % 2)
            finish_out(row0 + (b - 2) // 2, h)
        for a in range(2):
            chunk_in(row0 + b // 2, h, a).start()

    window_phases = [functools.partial(window_boundary, b) for b in range(n_bounds)]
    window_phases[0]()
    m_scr[...] = jnp.full(m_scr.shape, NEG_BIG, F32)
    l_scr[...] = jnp.zeros(l_scr.shape, F32)
    acc_scr[...] = jnp.zeros(acc_scr.shape, F32)
    qi = lax.broadcasted_iota(I32, (QBLOCK, 2 * QBLOCK), 0)
    kj = lax.broadcasted_iota(I32, (QBLOCK, 2 * QBLOCK), 1)
    dist = qi + QBLOCK - kj
    band_mask = jnp.logical_and(dist >= 0, dist <= QBLOCK)
    bias_band = jnp.where(band_mask, 0.0, NEG_BIG)
    bias_first = jnp.where(jnp.logical_and(band_mask, kj >= QBLOCK), 0.0, NEG_BIG)
    ones = jnp.ones((2 * QBLOCK, HEAD_DIM), BF16)
    dims = (((1,), (1,)), ((), ()))
    q_scale = ATTN_SCALE * LOG2_E

    regrouped = None
    for window, dil in DILATED_GROUPS:
        assert window // dil == QBLOCK
        n_units = seq // QBLOCK
        assert n_units % ATTN_INTERLEAVE == 0 and (dil % ATTN_INTERLEAVE == 0 or dil == 1)

        def rows(start, dil=dil):
            return pl.ds(start, QBLOCK) if dil == 1 else pl.ds(start, QBLOCK, stride=dil)

        sub_len = seq // dil
        regroup = dil >= REGROUP_MIN_DILATION
        if regroup:
            assert dil == REGROUP_STRIDE ** 2 and regrouped is None
            regrouped = (dil, sub_len)
            for src, dst in ((q_ref, q_rm), (k_ref, k_rm), (v_ref, v_rm)):
                _to_residue_major(lambda rows_, src=src: src[0, rows_, :], dst, m_rm, seq, sub_len)
            m_rm[...] = jnp.full(m_rm.shape, NEG_BIG, F32)
            l_rm[...] = jnp.zeros(l_rm.shape, F32)
            a_rm[...] = jnp.zeros(a_rm.shape, F32)

        def qkv_blocks(blk, res, start, prev, rows=rows, regroup=regroup, sub_len=sub_len):
            if regroup:
                cur = pl.ds(res * sub_len + blk * QBLOCK, QBLOCK)
                prv = pl.ds(res * sub_len + jnp.maximum(blk - 1, 0) * QBLOCK, QBLOCK)
                return (q_rm[cur, :], k_rm[prv, :], k_rm[cur, :], v_rm[prv, :], v_rm[cur, :])
            return (q_ref[0, rows(start), :], k_ref[0, rows(prev), :], k_ref[0, rows(start), :],
                    v_ref[0, rows(prev), :], v_ref[0, rows(start), :])

        state_refs = (m_rm, l_rm, a_rm) if regroup else (m_scr, l_scr, acc_scr)

        def state_rows(blk, res, start, rows=rows, regroup=regroup, sub_len=sub_len):
            return pl.ds(res * sub_len + blk * QBLOCK, QBLOCK) if regroup else rows(start)

        def body(it, carry, dil=dil, rows=rows, qkv_blocks=qkv_blocks, state_refs=state_refs, state_rows=state_rows):
            m_st, l_st, a_st = state_refs
            loaded = []
            for u in range(ATTN_INTERLEAVE):
                f = it * ATTN_INTERLEAVE + u
                blk, res = f // dil, f % dil
                start = blk * (QBLOCK * dil) + res
                prev = jnp.maximum(start - QBLOCK * dil, res)
                q_blk, k_prev, k_cur, v_prev, v_cur = qkv_blocks(blk, res, start, prev)
                q = (q_blk * q_scale).astype(BF16)
                k = jnp.concatenate([k_prev, k_cur], axis=0).astype(BF16)
                v = jnp.concatenate([v_prev, v_cur], axis=0).astype(BF16)
                v_ones = jnp.concatenate([v, ones], axis=1)
                st = state_rows(blk, res, start)
                state = (m_st[st, :], l_st[st, :], a_st[st, :])
                loaded.append((st, blk, q, k, v_ones, state))
            scores = [lax.dot_general(q, k, dims, preferred_element_type=F32)
                      + jnp.where(blk > 0, bias_band, bias_first)
                      for _, blk, q, k, _, _ in loaded]
            probs = []
            for s, (_, _, _, _, _, (m_old, _, _)) in zip(scores, loaded):
                m_new = jnp.maximum(m_old, jnp.max(s, axis=1, keepdims=True))
                p = jnp.exp2(s - jnp.concatenate([m_new, m_new], axis=1)).astype(BF16)
                probs.append((m_new, jnp.exp2(m_old - m_new), p))
            updated = []
            for (m_new, alpha, p), (start, _, _, _, v_ones, (_, l_old, a_old)) in zip(probs, loaded):
                pv = jnp.dot(p, v_ones, preferred_element_type=F32)
                updated.append((start, m_new, alpha * l_old + pv[:, HEAD_DIM:], alpha * a_old + pv[:, :HEAD_DIM]))
            for st, m_new, l_new, a_new in updated:
                m_st[st, :] = m_new
                l_st[st, :] = l_new
                a_st[st, :] = a_new
            return carry

        trips = n_units // ATTN_INTERLEAVE
        phase = window_phases[DILATED_GROUPS.index((window, dil)) + 1]
        if (window, dil) == DILATED_GROUPS[-1]:
            lax.fori_loop(0, trips // 2, body, 0)
            phase()
            lax.fori_loop(trips // 2, trips, body, 0)
        else:
            lax.fori_loop(0, trips, body, 0)
            phase()

    @pl.when(step == last_step)
    def _():
        last_row = row0 + WINDOW_ROWS_PER_STEP - 1
        finish_in_start_out(last_row, 1)
        finish_out(last_row, 0)
        finish_out(last_row, 1)

    if regrouped is None:
        o_ref[0] = acc_scr[...] / l_scr[...]
    else:
        _, sub_len = regrouped
        _from_residue_major(a_rm, q_rm, k_rm, seq, sub_len)
        _from_residue_major(l_rm, k_rm, a_rm, seq, sub_len)
        _from_residue_major(m_rm, v_rm, a_rm, seq, sub_len)
        m_a, m_b = m_scr[...], v_rm[...]
        m_all = jnp.maximum(m_a, m_b)
        w_a, w_b = jnp.exp2(m_a - m_all), jnp.exp2(m_b - m_all)
        o_ref[0] = (acc_scr[...] * w_a + q_rm[...] * w_b) / (l_scr[...] * w_a + k_rm[...] * w_b)


def _attn_prompt(q, k, v, cache_k, cache_v, k_new, v_new):
    b, s, _ = q.shape
    bd, n_buf = cache_k.shape[0], cache_k.shape[1]
    assert bd == b * N_HEADS * WINDOW_ROWS_PER_STEP
    spec = pl.BlockSpec((1, s, HEAD_DIM), lambda bi, hi: (bi, 0, hi))
    any_spec = pl.BlockSpec(memory_space=pl.ANY)
    new_spec = pl.BlockSpec(k_new.shape, lambda bi, hi: (0, 0, 0))
    win_sd = jax.ShapeDtypeStruct(cache_k.shape, cache_k.dtype)
    return pl.pallas_call(
        _attn_prompt_kernel,
        grid=(b, N_HEADS),
        in_specs=[spec, spec, spec, any_spec, any_spec, new_spec, new_spec],
        out_specs=[spec, any_spec, any_spec],
        out_shape=[jax.ShapeDtypeStruct((b, s, D_ATTN), F32), win_sd, win_sd],
        scratch_shapes=[pltpu.VMEM((s, HEAD_DIM), F32)] * 9 + [
            pltpu.VMEM((2, 2, n_buf // 2) + cache_k.shape[2:], F32),
            pltpu.SemaphoreType.DMA((2, 2)),
            pltpu.SemaphoreType.DMA((2, 2)),
        ],
        compiler_params=_cparams(("arbitrary", "arbitrary")),
        name="attn_prompt",
    )(q, k, v, cache_k, cache_v, k_new, v_new)


DECODE_ROWS_PER_STEP = 4


def _attn_decode_kernel(q_ref, kn_ref, vn_ref, k1_ref, k2_ref, k3_ref, v1_ref, v2_ref, v3_ref, o_ref):
    n_grp = len(DILATED_GROUPS)
    for r in range(q_ref.shape[0]):
        q = q_ref[r]
        k_new = kn_ref[r]
        v_new = vn_ref[r]

        def scores(kb, q=q):
            return jnp.sum(kb * q[None], axis=-1, keepdims=True) * ATTN_SCALE

        kcs = [k1_ref[r], k2_ref[r][:, 0], k3_ref[r][:, 0]]
        vcs = [v1_ref[r], v2_ref[r][:, 0], v3_ref[r][:, 0]]
        s_new = jnp.sum(k_new * q, axis=-1, keepdims=True) * ATTN_SCALE
        s_grp = [scores(kb) for kb in kcs]
        m = s_new
        for s in s_grp:
            m = jnp.maximum(m, jnp.max(s, axis=0))
        p_new = jnp.exp(s_new - m)
        den = n_grp * p_new
        num = n_grp * p_new * v_new
        for s, vb in zip(s_grp, vcs):
            p = jnp.exp(s - m[None])
            den = den + jnp.sum(p, axis=0)
            num = num + jnp.sum(p * vb, axis=0)
        o_ref[r] = num / den


def _attn_decode(q, k_new, v_new, cache_k, cache_v):
    bd, n_buf = cache_k.shape[0], cache_k.shape[1]
    rows = DECODE_ROWS_PER_STEP
    assert n_buf == MAX_WINDOW and bd % rows == 0

    def views(cache):
        out, specs = [], []
        for window, dil in DILATED_GROUPS:
            band = window // dil
            assert band == QBLOCK and n_buf % dil == 0 and n_buf // dil >= band
            if dil == 1:
                out.append(cache)
                specs.append(pl.BlockSpec((rows, band, N_HEADS, HEAD_DIM),
                                          lambda b, nb=n_buf // band: (b, nb - 1, 0, 0)))
            else:
                out.append(cache.reshape(bd, n_buf // dil, dil, N_HEADS, HEAD_DIM))
                specs.append(pl.BlockSpec((rows, band, 1, N_HEADS, HEAD_DIM),
                                          lambda b, nb=n_buf // dil // band: (b, nb - 1, 0, 0, 0)))
        return out, specs

    kv, kspecs = views(cache_k)
    vv, vspecs = views(cache_v)
    tok = pl.BlockSpec((rows, N_HEADS, HEAD_DIM), lambda b: (b, 0, 0))
    return pl.pallas_call(
        _attn_decode_kernel,
        grid=(bd // rows,),
        in_specs=[tok, tok, tok] + kspecs + vspecs,
        out_specs=tok,
        out_shape=jax.ShapeDtypeStruct((bd, N_HEADS, HEAD_DIM), F32),
        compiler_params=_cparams(("parallel",)),
        name="attn_decode",
    )(q, k_new, v_new, *kv, *vv)


def _pool_project(d_groups, wp_ref, ps_ref, out_ref):
    for g, d in enumerate(d_groups):
        sl = slice(g * POOL_GROUP_DIM, (g + 1) * POOL_GROUP_DIM)
        out_ref[:, sl] = jnp.dot(d.astype(BF16), wp_ref[g], preferred_element_type=F32) * ps_ref[:, sl]


def _pool_prompt_kernel(u_ref, prev_ref, wp_ref, ps_ref, o_ref):
    i = pl.program_id(1)
    tp = u_ref.shape[1]
    u = u_ref[0]
    prev = jnp.where(i > 0, prev_ref[0], 0.0)
    ext = jnp.concatenate([prev, u], axis=0)
    pos = i * tp + lax.broadcasted_iota(I32, (tp, 1), 0)
    d_groups = []
    for g, w in enumerate(POOL_WINDOWS):
        sl = slice(g * POOL_GROUP_DIM, (g + 1) * POOL_GROUP_DIM)
        a = ext[:, sl]
        span = 1
        while span < w:
            a = a + pltpu.roll(a, span, 0)
            span *= 2
        win = a[POOL_HALO:, :]
        cnt = jnp.minimum(w, pos + 1).astype(F32)
        d_groups.append(win / cnt - u[:, sl])
    _pool_project(d_groups, wp_ref, ps_ref, o_ref.at[0])


def _pool_prompt(u, w_pool_bf, pool_scale):
    b, s, _ = u.shape
    tp = POOL_TILE
    halo_blocks = tp // POOL_HALO
    return pl.pallas_call(
        _pool_prompt_kernel,
        grid=(b, s // tp),
        in_specs=[
            pl.BlockSpec((1, tp, D_POOL), lambda bi, i: (bi, i, 0)),
            pl.BlockSpec((1, POOL_HALO, D_POOL), lambda bi, i: (bi, jnp.maximum(i * halo_blocks - 1, 0), 0)),
            pl.BlockSpec(w_pool_bf.shape, lambda bi, i: (0, 0, 0)),
            pl.BlockSpec((1, D_POOL), lambda bi, i: (0, 0)),
        ],
        out_specs=pl.BlockSpec((1, tp, D_POOL), lambda bi, i: (bi, i, 0)),
        out_shape=jax.ShapeDtypeStruct((b, s, D_POOL), F32),
        compiler_params=_cparams(("parallel", "parallel")),
        name="pool_prompt",
    )(u, u, w_pool_bf, pool_scale)


def _pool_decode_kernel(u_ref, st_ref, wp_ref, ps_ref, o_ref):
    u = u_ref[...]
    d_groups = []
    for g, w in enumerate(POOL_WINDOWS):
        sl = slice(g * POOL_GROUP_DIM, (g + 1) * POOL_GROUP_DIM)
        win = u[:, sl]
        for j in range(1, w):
            win = win + st_ref[POOL_STATE - j][:, sl]
        d_groups.append(win / float(w) - u[:, sl])
    _pool_project(d_groups, wp_ref, ps_ref, o_ref)


def _pool_decode(u, state_t, w_pool_bf, pool_scale):
    assert PAST_LEN + 1 >= max(POOL_WINDOWS)
    return pl.pallas_call(
        _pool_decode_kernel,
        out_shape=jax.ShapeDtypeStruct(u.shape, F32),
        compiler_params=_cparams(None),
        name="pool_decode",
    )(u, state_t, w_pool_bf, pool_scale)


def _split_bf16(x):
    hi = x.astype(BF16)
    lo = (x - hi.astype(F32)).astype(BF16)
    return hi, lo


def _out_kernel(x_ref, a_ref, p_ref, g1_ref, sh_ref, sc_ref, g2_ref, wo_ref, gf_ref, wrh_ref, wrl_ref,
                wsg_ref, wsu_ref, wsd_ref, base_ref, h2_ref, lg_ref):
    tm = x_ref.shape[0]
    mix = (jnp.dot(a_ref[...].astype(BF16), wo_ref[:D_ATTN, :], preferred_element_type=F32)
           + jnp.dot(p_ref[...].astype(BF16), wo_ref[D_ATTN:, :], preferred_element_type=F32))
    x1 = x_ref[...] + g1_ref[0] * mix
    h2 = _rms(x1) * gf_ref[...] * (1.0 + sc_ref[0]) + sh_ref[0]
    for c in range(ROW_CHUNKS):
        h2_ref[pl.ds(c, tm, stride=ROW_CHUNKS), :] = h2[:, c * LANES:(c + 1) * LANES]
    h_hi, h_lo = _split_bf16(h2)
    w_hi, w_lo = wrh_ref[...], wrl_ref[...]
    lg = (jnp.dot(h_hi, w_hi, preferred_element_type=F32) + jnp.dot(h_lo, w_hi, preferred_element_type=F32)
          + jnp.dot(h_hi, w_lo, preferred_element_type=F32))
    if tm < LANES:
        lg = jnp.concatenate([lg, jnp.zeros((LANES - tm, LANES), F32)], axis=0)
    lg_ref[...] = lg.T[:N_EXPERTS, :tm]
    sg = jnp.dot(h_hi, wsg_ref[...], preferred_element_type=F32)
    su = jnp.dot(h_hi, wsu_ref[...], preferred_element_type=F32)
    hs = (sg * jax.nn.sigmoid(sg) * su).astype(BF16)
    base_ref[...] = x1 + g2_ref[0] * jnp.dot(hs, wsd_ref[...], preferred_element_type=F32)


def _out_proj(x2d, attn, pool, gate1, shift2, scale2, gate2, w_out_bf, g_ffn, w_router_hi, w_router_lo,
              ws_gate_bf, ws_up_bf, ws_down_bf, tm, rows_per_mod, h2_tokens, h2_all, h2_block0):
    m = x2d.shape[0]
    r = gate1.shape[1]
    n_main = m // tm
    h2_rows = h2_tokens * ROW_CHUNKS
    n_fill = 0 if h2_all is not None else pl.cdiv(h2_rows - m * ROW_CHUNKS, tm * ROW_CHUNKS)
    last = n_main - 1
    row = lambda i: jnp.minimum(i, last)
    mod_spec = pl.BlockSpec((1, r, D_MODEL), lambda i: (row(i) // rows_per_mod, 0, 0))
    const = lambda shape: pl.BlockSpec(shape, lambda i: (0,) * len(shape))
    in_specs = [
        pl.BlockSpec((tm, D_MODEL), lambda i: (row(i), 0)),
        pl.BlockSpec((tm, D_ATTN), lambda i: (row(i), 0)),
        pl.BlockSpec((tm, D_POOL), lambda i: (row(i), 0)),
        mod_spec, mod_spec, mod_spec, mod_spec,
        const((D_MODEL, D_MODEL)),
        const((1, D_MODEL)),
        const((D_MODEL, LANES)), const((D_MODEL, LANES)),
        const((D_MODEL, D_SHARED)), const((D_MODEL, D_SHARED)), const((D_SHARED, D_MODEL)),
    ]
    args = [x2d, attn, pool, gate1, shift2, scale2, gate2, w_out_bf, g_ffn, w_router_hi, w_router_lo,
            ws_gate_bf, ws_up_bf, ws_down_bf]
    aliases = {}
    n_in = len(args)
    if h2_all is not None:
        in_specs.append(pl.BlockSpec(memory_space=pl.ANY))
        args.append(h2_all)
        aliases = {n_in: 1}

    def kernel(*refs):
        refs = refs[:n_in] + refs[len(args):]
        if n_fill == 0:
            _out_kernel(*refs)
            return
        step = pl.program_id(0)
        pl.when(step < n_main)(lambda: _out_kernel(*refs))

        @pl.when(step >= n_main)
        def _():
            h2_ref = refs[n_in + 1]
            h2_ref[...] = jnp.zeros(h2_ref.shape, F32)

    return pl.pallas_call(
        kernel,
        grid=(n_main + n_fill,),
        in_specs=in_specs,
        out_specs=[
            pl.BlockSpec((tm, D_MODEL), lambda i: (row(i), 0)),
            pl.BlockSpec((tm * ROW_CHUNKS, LANES), lambda i: (i + h2_block0, 0)),
            pl.BlockSpec((N_EXPERTS, tm), lambda i: (0, row(i))),
        ],
        out_shape=[
            jax.ShapeDtypeStruct((m, D_MODEL), F32),
            jax.ShapeDtypeStruct((h2_rows, LANES), F32),
            jax.ShapeDtypeStruct((N_EXPERTS, m), F32),
        ],
        input_output_aliases=aliases,
        compiler_params=_cparams(("arbitrary",)),
        name="out_proj",
    )(*args)


def _first_index_of_max(v, iota, size, axis):
    m = jnp.max(v, axis=axis, keepdims=True)
    idx = jnp.min(jnp.where(v == m, iota, size), axis=axis, keepdims=True)
    return m, idx


def _route_kernel(n_valid, lg_ref, b_ref, eidx_ref, wts_ref, cnt_ref, carry):
    i = pl.program_id(0)
    tn = lg_ref.shape[1]

    @pl.when(i == 0)
    def _():
        carry[...] = jnp.zeros(carry.shape, F32)

    scores = jax.nn.sigmoid(lg_ref[...])
    biased = scores + b_ref[...]
    grp = biased.reshape(N_EXPERT_GROUPS, GROUP_SIZE, tn)
    io_g = lax.broadcasted_iota(I32, grp.shape, 1)
    m1, i1 = _first_index_of_max(grp, io_g, GROUP_SIZE, 1)
    m2 = jnp.max(jnp.where(io_g == i1, -jnp.inf, grp), axis=1, keepdims=True)
    gscore = (m1 + m2)[:, 0, :]
    io_n = lax.broadcasted_iota(I32, gscore.shape, 0)
    gsel = jnp.zeros(gscore.shape, jnp.bool_)
    for _ in range(TOPK_GROUPS):
        _, gi = _first_index_of_max(gscore, io_n, N_EXPERT_GROUPS, 0)
        hit = io_n == gi
        gsel = jnp.logical_or(gsel, hit)
        gscore = jnp.where(hit, -jnp.inf, gscore)
    emask = jnp.broadcast_to(gsel[:, None, :], grp.shape).reshape(N_EXPERTS, tn)
    cand = jnp.where(emask, biased, -jnp.inf)
    io_e = lax.broadcasted_iota(I32, cand.shape, 0)
    picked = jnp.zeros(cand.shape, jnp.bool_)
    eidx, sel = [], []
    for _ in range(TOP_K):
        _, ei = _first_index_of_max(cand, io_e, N_EXPERTS, 0)
        hit = io_e == ei
        eidx.append(ei)
        sel.append(jnp.sum(jnp.where(hit, scores, 0.0), axis=0, keepdims=True))
        picked = jnp.logical_or(picked, hit)
        cand = jnp.where(hit, -jnp.inf, cand)
    sel = jnp.concatenate(sel, axis=0)
    eidx = jnp.concatenate(eidx, axis=0)
    wts_ref[...] = sel / jnp.sum(sel, axis=0, keepdims=True) * ROUTED_SCALE
    eidx_ref[...] = eidx

    tok = i * tn + lax.broadcasted_iota(I32, cand.shape, 1)
    mask = jnp.logical_and(picked, tok < n_valid).astype(F32)
    total = carry[...] + jnp.sum(mask, axis=1, keepdims=True)
    carry[...] = total
    cnt_ref[...] = total.astype(I32)


def _route(logits_t, b_router, n_valid):
    mp = logits_t.shape[1]
    tn = ROUTE_TILE
    tok_spec = pl.BlockSpec((TOP_K, tn), lambda i: (0, i))
    return pl.pallas_call(
        functools.partial(_route_kernel, n_valid),
        grid=(mp // tn,),
        in_specs=[
            pl.BlockSpec((N_EXPERTS, tn), lambda i: (0, i)),
            pl.BlockSpec((N_EXPERTS, 1), lambda i: (0, 0)),
        ],
        out_specs=[tok_spec, tok_spec, pl.BlockSpec((N_EXPERTS, LANES), lambda i: (0, 0))],
        out_shape=[
            jax.ShapeDtypeStruct((TOP_K, mp), I32),
            jax.ShapeDtypeStruct((TOP_K, mp), F32),
            jax.ShapeDtypeStruct((N_EXPERTS, LANES), I32),
        ],
        scratch_shapes=[pltpu.VMEM((N_EXPERTS, LANES), F32)],
        compiler_params=_cparams(("arbitrary",)),
        name="route",
    )(logits_t, b_router.reshape(N_EXPERTS, 1))


TILE_USED = 1
TILE_FIRST = 2
TILE_WSLOT = 4
TILE_PREV_USED = 8
PAIR_BITS = 3
assert 1 << PAIR_BITS == TOP_K
DUMMY_TOKENS = 2 * EXPERT_TILE // TOP_K
GATHER_SLOTS = 3


def _experts_kernel(n_tok, te_ref, ts_ref, tn_ref, tf_ref, ne_ref, g_ref, s_ref,
                    h_hbm, wg_hbm, wu_hbm, wd_hbm, y_hbm,
                    xbuf, ybuf, xs, wg_f, wu_f, wd_f, wg_bf, wu_bf, wd_bf, gsem, ssem, wsem, zsem):
    j = pl.program_id(0)
    last = pl.num_programs(0) - 1
    tm = EXPERT_TILE
    rc = ROW_CHUNKS
    plane_tokens = n_tok + DUMMY_TOKENS
    slot = j % 2
    other = 1 - slot
    gslot = j % GATHER_SLOTS
    flags = tf_ref[j]
    used = (flags & TILE_USED) != 0
    first = (flags & TILE_FIRST) != 0
    prev_used = (flags & TILE_PREV_USED) != 0
    wslot = (flags // TILE_WSLOT) & 1
    prv = jnp.maximum(j - 1, 0)
    ahead = jnp.minimum(j + GATHER_SLOTS - 1, last)

    def buf_rows(i):
        return pl.ds(i * rc, rc) if isinstance(i, int) else pl.ds(pl.multiple_of(i * rc, rc), rc)

    def gather_row(i, start, dst):
        row = g_ref[start + i]
        pltpu.make_async_copy(h_hbm.at[pl.ds(pl.multiple_of(row, rc), rc), :],
                              xbuf.at[dst, buf_rows(i), :], gsem.at[dst]).start(priority=0)

    def scatter_row(i, start, n_valid, src):
        spare = ((i & (TOP_K - 1)) * plane_tokens + n_tok + (i >> PAIR_BITS)) * rc + src * ((tm >> PAIR_BITS) * rc)
        row = jnp.where(i < n_valid, s_ref[start + i], spare)
        pltpu.make_async_copy(ybuf.at[src, buf_rows(i), :],
                              y_hbm.at[pl.ds(pl.multiple_of(row, rc), rc), :],
                              ssem.at[src]).start(priority=i % 2 if isinstance(i, int) else 0)

    def scatter_tile_loop(tile, src):
        start, n_valid = ts_ref[tile], tn_ref[tile]
        lax.fori_loop(0, tm, lambda i, c: (scatter_row(i, start, n_valid, src), c)[1], 0)

    def wait_gather(dst):
        pltpu.make_async_copy(h_hbm.at[pl.ds(0, tm * rc), :], xbuf.at[dst], gsem.at[dst]).wait()

    def wait_scatter(src):
        pltpu.make_async_copy(ybuf.at[src], y_hbm.at[pl.ds(0, tm * rc), :], ssem.at[src]).wait()

    def weight_copies(e, ws):
        return [pltpu.make_async_copy(src.at[e], dst.at[ws], wsem.at[ws])
                for src, dst in ((wg_hbm, wg_f), (wu_hbm, wu_f), (wd_hbm, wd_f))]

    @pl.when(j == 0)
    def _():
        ybuf[1] = jnp.zeros(ybuf.shape[1:], ybuf.dtype)
        fills = [pltpu.make_async_copy(ybuf.at[1, pl.ds(0, DUMMY_TOKENS * rc), :],
                                       y_hbm.at[pl.ds((k * plane_tokens + n_tok) * rc, DUMMY_TOKENS * rc), :],
                                       zsem.at[0]) for k in range(TOP_K)]
        for c in fills:
            c.start()
        for c in fills:
            c.wait()
        for c in weight_copies(te_ref[0], 0):
            c.start(priority=1)
        for t in range(GATHER_SLOTS - 1):
            start_t = ts_ref[jnp.minimum(t, last)]
            lax.fori_loop(0, tm, lambda i, c, start_t=start_t, t=t: (gather_row(i, start_t, t), c)[1], 0)

    @pl.when(first)
    def _():
        for c in weight_copies(te_ref[j], wslot):
            c.wait()
        wg_bf[...] = wg_f[wslot].astype(BF16)
        wu_bf[...] = wu_f[wslot].astype(BF16)
        wd_bf[...] = wd_f[wslot].astype(BF16)
        nxt_e = ne_ref[j]

        @pl.when(nxt_e >= 0)
        def _():
            for c in weight_copies(nxt_e, 1 - wslot):
                c.start(priority=1)

    @pl.when(jnp.logical_and(used, j > 0))
    def _():
        wait_scatter(slot)

    @pl.when(used)
    def _():
        wait_gather(gslot)
        xs[...] = jnp.concatenate(
            [xbuf[gslot, pl.ds(c, tm, stride=rc), :] for c in range(rc)], axis=1).astype(BF16)
        start_n = ts_ref[ahead]
        ahead_slot = (j + GATHER_SLOTS - 1) % GATHER_SLOTS
        for i in range(tm):
            gather_row(i, start_n, ahead_slot)
        start_p = ts_ref[prv]
        n_valid_p = jnp.where(j > 0, tn_ref[prv], 0)
        for i in range(tm):
            scatter_row(i, start_p, n_valid_p, other)
        x = xs[...]
        hg = jnp.dot(x, wg_bf[...], preferred_element_type=F32)
        hu = jnp.dot(x, wu_bf[...], preferred_element_type=F32)
        h = (hg * jax.nn.sigmoid(hg) * hu).astype(BF16)
        y = jnp.dot(h, wd_bf[...], preferred_element_type=F32)
        for c in range(rc):
            ybuf[slot, pl.ds(c, tm, stride=rc), :] = y[:, c * LANES:(c + 1) * LANES]

    @pl.when(jnp.logical_and(jnp.logical_not(used), prev_used))
    def _():
        for t in range(GATHER_SLOTS - 1):
            wait_gather((j + t) % GATHER_SLOTS)
        wait_scatter(slot)
        scatter_tile_loop(prv, other)
        wait_scatter(other)

    @pl.when(jnp.logical_and(used, j == last))
    def _():
        for t in range(1, GATHER_SLOTS):
            wait_gather((j + t) % GATHER_SLOTS)
        wait_scatter(other)
        scatter_tile_loop(j, slot)
        wait_scatter(slot)


def _experts(tile_expert, tile_start, tile_rows, tile_flags, next_expert, gather_rows, scatter_rows, h2_all,
             w_gate, w_up, w_down, n_tok):
    n_tiles = tile_expert.shape[0]
    tm = EXPERT_TILE
    any_spec = pl.BlockSpec(memory_space=pl.ANY)
    grid_spec = pltpu.PrefetchScalarGridSpec(
        num_scalar_prefetch=7,
        grid=(n_tiles,),
        in_specs=[any_spec] * 4,
        out_specs=any_spec,
        scratch_shapes=[
            pltpu.VMEM((GATHER_SLOTS, tm * ROW_CHUNKS, LANES), F32),
            pltpu.VMEM((2, tm * ROW_CHUNKS, LANES), F32),
            pltpu.VMEM((tm, D_MODEL), BF16),
            pltpu.VMEM((2, D_MODEL, D_EXPERT), F32),
            pltpu.VMEM((2, D_MODEL, D_EXPERT), F32),
            pltpu.VMEM((2, D_EXPERT, D_MODEL), F32),
            pltpu.VMEM((D_MODEL, D_EXPERT), BF16),
            pltpu.VMEM((D_MODEL, D_EXPERT), BF16),
            pltpu.VMEM((D_EXPERT, D_MODEL), BF16),
            pltpu.SemaphoreType.DMA((GATHER_SLOTS,)),
            pltpu.SemaphoreType.DMA((2,)),
            pltpu.SemaphoreType.DMA((2,)),
            pltpu.SemaphoreType.DMA((1,)),
        ],
    )
    return pl.pallas_call(
        functools.partial(_experts_kernel, n_tok),
        grid_spec=grid_spec,
        out_shape=jax.ShapeDtypeStruct((TOP_K * (n_tok + DUMMY_TOKENS) * ROW_CHUNKS, LANES), F32),
        compiler_params=_cparams(("arbitrary",)),
        name="experts",
    )(tile_expert, tile_start, tile_rows, tile_flags, next_expert, gather_rows, scatter_rows, h2_all,
      w_gate, w_up, w_down)


COMBINE_UNROLL = 4


def _combine_kernel(w_ref, y_ref, base_ref, g2_ref, o_ref, acc_rows):
    i = pl.program_id(0)
    tc = base_ref.shape[0]
    rc = ROW_CHUNKS

    def body(g, carry):
        for u in range(COMBINE_UNROLL):
            t = g * COMBINE_UNROLL + u
            rows = pl.ds(pl.multiple_of(t * rc, rc), rc)
            w0 = (i * tc + t) * TOP_K
            acc = w_ref[w0] * y_ref[0, rows, :]
            for k in range(1, TOP_K):
                acc = acc + w_ref[w0 + k] * y_ref[k, rows, :]
            acc_rows[rows, :] = acc
        return carry

    lax.fori_loop(0, tc // COMBINE_UNROLL, body, 0)
    g2 = g2_ref[0]
    for c in range(rc):
        sl = slice(c * LANES, (c + 1) * LANES)
        o_ref[:, sl] = base_ref[:, sl] + g2[:, sl] * acc_rows[pl.ds(c, tc, stride=rc), :]


def _combine(y_planes, wts, base, gate2, tc, rows_per_mod, block0):
    m = base.shape[0]
    r = gate2.shape[1]
    assert tc % COMBINE_UNROLL == 0
    grid_spec = pltpu.PrefetchScalarGridSpec(
        num_scalar_prefetch=1,
        grid=(m // tc,),
        in_specs=[
            pl.BlockSpec((TOP_K, tc * ROW_CHUNKS, LANES), lambda i, w: (0, i + block0, 0)),
            pl.BlockSpec((tc, D_MODEL), lambda i, w: (i, 0)),
            pl.BlockSpec((1, r, D_MODEL), lambda i, w: (i // rows_per_mod, 0, 0)),
        ],
        out_specs=pl.BlockSpec((tc, D_MODEL), lambda i, w: (i, 0)),
        scratch_shapes=[pltpu.VMEM((tc * ROW_CHUNKS, LANES), F32)],
    )
    return pl.pallas_call(
        _combine_kernel,
        grid_spec=grid_spec,
        out_shape=jax.ShapeDtypeStruct((m, D_MODEL), F32),
        compiler_params=_cparams(("parallel",)),
        name="combine",
    )(wts.reshape(-1), y_planes, base, gate2)


def _moe_routed(h2_all, logits_t, b_router, w_gate, w_up, w_down):
    n_tok = logits_t.shape[1]
    mp = -(-n_tok // ROUTE_TILE) * ROUTE_TILE
    logits_t = jnp.pad(logits_t, ((0, 0), (0, mp - n_tok)))
    eidx, wts, counts = _route(logits_t, b_router, n_tok)
    eidx, wts, counts = eidx[:, :n_tok], wts[:, :n_tok], counts[:, 0]

    tm = EXPERT_TILE
    n_pairs = n_tok * TOP_K
    n_tiles = n_pairs // tm + N_EXPERTS
    pair_code = jnp.arange(n_tok, dtype=I32)[None, :] * TOP_K + jnp.arange(TOP_K, dtype=I32)[:, None]
    (sorted_pairs,) = lax.sort(((eidx * n_pairs + pair_code).reshape(-1),), is_stable=False)
    sorted_pairs = jnp.concatenate([sorted_pairs % n_pairs, jnp.zeros((tm,), I32)])
    pair_tok, pair_choice = sorted_pairs >> PAIR_BITS, sorted_pairs & (TOP_K - 1)
    gather_rows = pair_tok * ROW_CHUNKS
    scatter_rows = (pair_choice * (n_tok + DUMMY_TOKENS) + pair_tok) * ROW_CHUNKS
    tiles_e = (counts + tm - 1) // tm
    tile_end = jnp.cumsum(tiles_e)
    tile_begin = tile_end - tiles_e
    dense_begin = jnp.cumsum(counts) - counts
    tj = jnp.arange(n_tiles, dtype=I32)
    total_tiles = tile_end[-1]
    used = tj < total_tiles
    tj_used = jnp.minimum(tj, total_tiles - 1)[:, None]
    member = jnp.logical_and(tile_begin[None, :] <= tj_used, tj_used < tile_end[None, :])
    of_tile = lambda per_expert: jnp.sum(jnp.where(member, per_expert[None, :], 0), axis=1)
    e_ids = jnp.arange(N_EXPERTS, dtype=I32)
    te = of_tile(e_ids)
    local = tj - of_tile(tile_begin)
    ts = jnp.where(used, of_tile(dense_begin) + local * tm, 0).astype(I32)
    tn = jnp.where(used, jnp.clip(of_tile(counts) - local * tm, 0, tm), 0).astype(I32)
    first = jnp.logical_and(used, local == 0)
    busy = counts > 0
    wslot = of_tile((jnp.cumsum(busy.astype(I32)) - 1) % 2)
    later = jnp.logical_and(e_ids[None, :] > e_ids[:, None], busy[None, :])
    succ = jnp.min(jnp.where(later, e_ids[None, :], N_EXPERTS), axis=1)
    succ = jnp.where(succ == N_EXPERTS, -1, succ)
    ne = jnp.where(first, of_tile(succ), -1).astype(I32)
    prev_used = jnp.concatenate([jnp.zeros((1,), jnp.bool_), used[:-1]])
    tf = (used.astype(I32) * TILE_USED + first.astype(I32) * TILE_FIRST + wslot * TILE_WSLOT
          + prev_used.astype(I32) * TILE_PREV_USED)
    y_all = _experts(te, ts, tn, tf, ne, gather_rows, scatter_rows, h2_all, w_gate, w_up, w_down, n_tok)
    return y_all.reshape(TOP_K, (n_tok + DUMMY_TOKENS) * ROW_CHUNKS, LANES), wts.T


def kernel(x_prompt, x_sample, cache_k_win, cache_v_win, state_pool, c_prompt, c_sample, w_ada, b_ada, g_mix, w_in, q_norm, k_norm, w_pool, pool_scale, w_out, g_ffn, w_router, b_router, w_gate, w_up, w_down, ws_gate, ws_up, ws_down):
    depth = w_ada.shape[0]
    assert depth == 1
    l = 0
    nb, seq, _ = x_prompt.shape
    bd, dec_seq, _ = x_sample.shape
    assert dec_seq == 1 and seq % (TOKEN_TILE * 16) == 0
    n_prompt = nb * seq
    n_tok = n_prompt + bd
    n_keep = min(MAX_WINDOW, seq)
    tiles_per_batch = seq // TOKEN_TILE

    n_mod_rows = -(-(nb + bd) // 8) * 8
    c_all = jnp.concatenate([c_prompt, c_sample, jnp.zeros((n_mod_rows - nb - bd, D_MODEL), F32)], axis=0)
    mods = _ada(c_all, w_ada[l], b_ada[l])
    mods_p = [mods[:nb, j * D_MODEL:(j + 1) * D_MODEL].reshape(nb, 1, D_MODEL) for j in range(6)]
    mods_s = [mods[nb:nb + bd, j * D_MODEL:(j + 1) * D_MODEL].reshape(1, bd, D_MODEL) for j in range(6)]

    g_mix_l = g_mix[l].reshape(1, D_MODEL)
    g_ffn_l = g_ffn[l].reshape(1, D_MODEL)
    qn = q_norm[l].reshape(1, HEAD_DIM)
    kn = k_norm[l].reshape(1, HEAD_DIM)
    w_in_bf = w_in[l].astype(BF16)
    w_out_bf = w_out[l].astype(BF16)
    w_pool_bf = w_pool[l].astype(BF16)
    ps = pool_scale[l].reshape(1, D_POOL)
    w_router_pad = jnp.pad(w_router[l], ((0, 0), (0, LANES - N_EXPERTS)))
    w_router_hi = w_router_pad.astype(BF16)
    w_router_lo = (w_router_pad - w_router_hi.astype(F32)).astype(BF16)
    wsg_bf, wsu_bf, wsd_bf = ws_gate[l].astype(BF16), ws_up[l].astype(BF16), ws_down[l].astype(BF16)

    rope_p = _rope_tables(jnp.arange(seq, dtype=I32))
    rope_s = _rope_tables(jnp.full((bd,), PAST_LEN, I32))
    xp2 = x_prompt.reshape(n_prompt, D_MODEL)
    xs2 = x_sample.reshape(bd, D_MODEL)
    assert n_keep % TOKEN_TILE == 0
    qp, kp, vp, up, kwin_p, vwin_p = _in_proj(xp2, mods_p[0], mods_p[1], g_mix_l, w_in_bf, qn, kn, rope_p,
                                              TOKEN_TILE, tiles_per_batch, tiles_per_batch, n_keep // TOKEN_TILE)
    qs, _, _, us, k_new, v_new = _in_proj(xs2, mods_s[0], mods_s[1], g_mix_l, w_in_bf, qn, kn, rope_s, bd, 1, 1, 1)

    to_seq = lambda t: t.reshape(nb, seq, -1)
    heads = lambda t: t.reshape(bd, N_HEADS, HEAD_DIM)
    ck, cv = cache_k_win[l], cache_v_win[l]
    k_new, v_new = heads(k_new), heads(v_new)
    attn_p, k_win_s, v_win_s = _attn_prompt(to_seq(qp), to_seq(kp), to_seq(vp), ck, cv, k_new, v_new)
    attn_p = attn_p.reshape(n_prompt, D_ATTN)
    pool_p = _pool_prompt(to_seq(up), w_pool_bf, ps).reshape(n_prompt, D_POOL)
    attn_s = _attn_decode(heads(qs), k_new, v_new, ck, cv).reshape(bd, D_ATTN)
    pool_s = _pool_decode(us, jnp.swapaxes(state_pool[l], 0, 1), w_pool_bf, ps)

    base_p, h2_all, lg_p = _out_proj(xp2, attn_p, pool_p, mods_p[2], mods_p[3], mods_p[4], mods_p[5],
                                     w_out_bf, g_ffn_l, w_router_hi, w_router_lo, wsg_bf, wsu_bf, wsd_bf,
                                     TOKEN_TILE, tiles_per_batch, n_tok, None, 0)
    base_s, h2_all, lg_s = _out_proj(xs2, attn_s, pool_s, mods_s[2], mods_s[3], mods_s[4], mods_s[5],
                                     w_out_bf, g_ffn_l, w_router_hi, w_router_lo, wsg_bf, wsu_bf, wsd_bf,
                                     bd, 1, n_tok, h2_all, n_prompt // bd)

    y_planes, wts_t = _moe_routed(h2_all, jnp.concatenate([lg_p, lg_s], axis=1), b_router[l],
                                  w_gate[l], w_up[l], w_down[l])

    yp = _combine(y_planes, wts_t[:n_prompt], base_p, mods_p[5], COMBINE_TILE, seq // COMBINE_TILE, 0)
    ys = _combine(y_planes, wts_t[n_prompt:n_tok], base_s, mods_s[5], bd, 1, n_prompt // bd)

    y_prompt = yp.reshape(nb, seq, D_MODEL)
    y_sample = ys.reshape(bd, dec_seq, D_MODEL)
    win = lambda t: t.reshape(1, nb, n_keep, N_HEADS, HEAD_DIM)
    pool_p_state = to_seq(up)[:, seq - POOL_STATE:][None]
    pool_s_state = jnp.concatenate([state_pool[l][:, 1:], us[:, None, :]], axis=1)[None]
    return (y_prompt, y_sample, win(kwin_p), win(vwin_p), pool_p_state,
            k_win_s[None], v_win_s[None], pool_s_state)
```
